```python
import jax, jax.numpy as jnp
from jax import lax
import numpy as np

D_MODEL = 2048
BATCH = 2
SEQ = 8192
DEPTH = 4

FOX_HEADS = 8
FOX_HEAD_DIM = 128
FOX_BLOCK = 128
GDN_HEADS = 8
GDN_HEAD_DIM = 128
GDN_CONV = 4
GDN_CHUNK = 64
MLSTM_HEADS = 8
MLSTM_QK_DIM = 64
MLSTM_V_DIM = 128
MLSTM_CHUNK = 64
D_FF = 5632
FFN_CONV = 3
N_BRANCH = 3
EPS = 1e-6

FOX_W = FOX_HEADS * FOX_HEAD_DIM
GDN_W = GDN_HEADS * GDN_HEAD_DIM
MLSTM_QK_W = MLSTM_HEADS * MLSTM_QK_DIM
MLSTM_V_W = MLSTM_HEADS * MLSTM_V_DIM

IN_SPLITS = (
    ("fox_q", FOX_W), ("fox_k", FOX_W), ("fox_v", FOX_W), ("fox_f", FOX_HEADS),
    ("gdn_qkv", 3 * GDN_W), ("gdn_z", GDN_W), ("gdn_b", GDN_HEADS), ("gdn_a", GDN_HEADS),
    ("ml_q", MLSTM_QK_W), ("ml_k", MLSTM_QK_W), ("ml_v", MLSTM_V_W),
    ("ml_i", MLSTM_HEADS), ("ml_f", MLSTM_HEADS), ("ml_o", MLSTM_V_W),
    ("gate", N_BRANCH * D_MODEL),
)
D_IN = sum(w for _, w in IN_SPLITS)

kernel_name = "hybrid_fox_gdn_mlstm_convffn"


def rmsnorm(x, g):
    x32 = x.astype(jnp.float32)
    y = x32 * lax.rsqrt(jnp.mean(x32 * x32, axis=-1, keepdims=True) + EPS)
    return (y * g.astype(jnp.float32)).astype(x.dtype)


def l2norm(x):
    return x * lax.rsqrt(jnp.sum(x * x, axis=-1, keepdims=True) + EPS)


def causal_dwconv(x, w):
    K = w.shape[0]
    S = x.shape[1]
    xp = jnp.pad(x, ((0, 0), (K - 1, 0), (0, 0)))
    return sum(xp[:, j:j + S] * w[j] for j in range(K))


def split_cols(h):
    out = {}
    off = 0
    for name, w in IN_SPLITS:
        out[name] = h[..., off:off + w]
        off += w
    return out


def to_chunks(x, L):
    B, S, H = x.shape[:3]
    rest = x.shape[3:]
    x = x.reshape((B, S // L, L, H) + rest)
    return x.transpose((1, 0, 3, 2) + tuple(range(4, x.ndim)))


def from_chunks(x):
    NC, B, H, L, d = x.shape
    return x.transpose(1, 0, 3, 2, 4).reshape(B, NC * L, H, d)


def fox_attention(q, k, v, log_f):
    B, S, H, Dh = q.shape
    F = jnp.cumsum(log_f, axis=1).transpose(0, 2, 1)
    kh = k.transpose(0, 2, 1, 3)
    vh = v.transpose(0, 2, 1, 3)
    NB = S // FOX_BLOCK
    qb = q.reshape(B, NB, FOX_BLOCK, H, Dh).transpose(1, 0, 3, 2, 4)
    Fb = F.reshape(B, H, NB, FOX_BLOCK).transpose(2, 0, 1, 3)
    kpos = jnp.arange(S)
    scale = Dh ** -0.5

    def block(args):
        qi, Fi, bi = args
        s = jnp.einsum('bhqd,bhkd->bhqk', qi, kh).astype(jnp.float32) * scale
        s = s + Fi[..., :, None] - F[:, :, None, :]
        qpos = bi * FOX_BLOCK + jnp.arange(FOX_BLOCK)
        s = jnp.where(kpos[None, :] <= qpos[:, None], s, -jnp.inf)
        p = jax.nn.softmax(s, axis=-1)
        return jnp.einsum('bhqk,bhkd->bhqd', p.astype(vh.dtype), vh)

    o = lax.map(block, (qb, Fb, jnp.arange(NB)))
    return o.transpose(1, 0, 3, 2, 4).reshape(B, S, H * Dh)


def gated_deltanet(q, k, v, beta, g):
    B, S, H, dk = q.shape
    dv = v.shape[-1]
    L = GDN_CHUNK
    q = l2norm(q) * dk ** -0.5
    k = l2norm(k)
    qc, kc, vc = to_chunks(q, L), to_chunks(k, L), to_chunks(v, L)
    bc = to_chunks(beta, L)
    gam = jnp.cumsum(to_chunks(g, L), axis=-1)
    tri = jnp.tril(jnp.ones((L, L), dtype=bool))
    decay = jnp.exp(jnp.where(tri, gam[..., :, None] - gam[..., None, :], -jnp.inf))
    kb = kc * bc[..., None]
    M = jnp.tril(jnp.einsum('nbhrd,nbhsd->nbhrs', kb, kc) * decay, -1)
    A = M + jnp.eye(L, dtype=M.dtype)
    rhs = jnp.concatenate([vc * bc[..., None], kb * jnp.exp(gam)[..., None]], axis=-1)
    X = lax.linalg.triangular_solve(A, rhs, left_side=True, lower=True, unit_diagonal=True)
    U, W = X[..., :dv], X[..., dv:]
    Aqk = jnp.einsum('nbhrd,nbhsd->nbhrs', qc, kc) * decay
    qg = qc * jnp.exp(gam)[..., None]
    kg = kc * jnp.exp(gam[..., -1:] - gam)[..., None]
    gL = jnp.exp(gam[..., -1])

    def step(St, xs):
        U_i, W_i, qg_i, kg_i, Aqk_i, gL_i = xs
        v_new = U_i - jnp.einsum('bhld,bhde->bhle', W_i, St)
        o = jnp.einsum('bhld,bhde->bhle', qg_i, St) + jnp.einsum('bhls,bhse->bhle', Aqk_i, v_new)
        St = St * gL_i[..., None, None] + jnp.einsum('bhld,bhle->bhde', kg_i, v_new)
        return St, o

    S0 = jnp.zeros((B, H, dk, dv), jnp.float32)
    _, o = lax.scan(step, S0, (U, W, qg, kg, Aqk, gL))
    return from_chunks(o)


def mlstm(q, k, v, i_pre, logf):
    B, S, H, dqk = q.shape
    dv = v.shape[-1]
    L = MLSTM_CHUNK
    q = q * dqk ** -0.5
    qc, kc, vc = to_chunks(q, L), to_chunks(k, L), to_chunks(v, L)
    ic = to_chunks(i_pre, L)
    bcum = jnp.cumsum(to_chunks(logf, L), axis=-1)
    tri = jnp.tril(jnp.ones((L, L), dtype=bool))
    Dm = jnp.where(tri, bcum[..., :, None] - bcum[..., None, :] + ic[..., None, :], -jnp.inf)
    Dmax = jnp.max(Dm, axis=-1)
    qk = jnp.einsum('nbhrd,nbhsd->nbhrs', qc, kc)

    def step(carry, xs):
        C, n, m = carry
        q_i, v_i, k_i, b_i, D_i, Dmax_i, qk_i = xs
        inter = b_i + m[..., None]
        m_r = jnp.maximum(inter, Dmax_i)
        w_inter = jnp.exp(inter - m_r)
        Wd = jnp.exp(D_i - m_r[..., None])
        Sm = Wd * qk_i
        num = w_inter[..., None] * jnp.einsum('bhld,bhde->bhle', q_i, C) + jnp.einsum('bhls,bhse->bhle', Sm, v_i)
        den = w_inter * jnp.einsum('bhld,bhd->bhl', q_i, n) + jnp.sum(Sm, axis=-1)
        h = num / jnp.maximum(jnp.abs(den), jnp.exp(-m_r))[..., None]
        wL = Wd[..., -1, :]
        C = w_inter[..., -1, None, None] * C + jnp.einsum('bhl,bhld,bhle->bhde', wL, k_i, v_i)
        n = w_inter[..., -1, None] * n + jnp.einsum('bhl,bhld->bhd', wL, k_i)
        return (C, n, m_r[..., -1]), h

    init = (jnp.zeros((B, H, dqk, dv), jnp.float32), jnp.zeros((B, H, dqk), jnp.float32),
            jnp.zeros((B, H), jnp.float32))
    _, h = lax.scan(step, init, (qc, vc, kc, bcum, Dm, Dmax, qk))
    return from_chunks(h)


def hybrid_layer(x, norm_mix_g, w_in, fox_f_bias, gdn_conv_w, gdn_a_log, gdn_dt_bias, gdn_norm_g,
                 ml_i_bias, ml_f_bias, ml_norm_g, gate_bias, w_fox_proj, w_gdn_proj, w_ml_proj,
                 w_out, norm_ffn_g, w_up, ffn_conv_w, w_down):
    B, S, D = x.shape
    f32 = jnp.float32
    xn = rmsnorm(x, norm_mix_g)
    p = split_cols(xn @ w_in)

    fq = p['fox_q'].reshape(B, S, FOX_HEADS, FOX_HEAD_DIM)
    fk = p['fox_k'].reshape(B, S, FOX_HEADS, FOX_HEAD_DIM)
    fv = p['fox_v'].reshape(B, S, FOX_HEADS, FOX_HEAD_DIM)
    fox_logf = jax.nn.log_sigmoid(p['fox_f'].astype(f32) + fox_f_bias.astype(f32))
    y_fox = fox_attention(fq, fk, fv, fox_logf)

    qkv = jax.nn.silu(causal_dwconv(p['gdn_qkv'], gdn_conv_w)).astype(f32)
    gq = qkv[..., :GDN_W].reshape(B, S, GDN_HEADS, GDN_HEAD_DIM)
    gk = qkv[..., GDN_W:2 * GDN_W].reshape(B, S, GDN_HEADS, GDN_HEAD_DIM)
    gv = qkv[..., 2 * GDN_W:].reshape(B, S, GDN_HEADS, GDN_HEAD_DIM)
    beta = jax.nn.sigmoid(p['gdn_b'].astype(f32))
    g = -jnp.exp(gdn_a_log.astype(f32)) * jax.nn.softplus(p['gdn_a'].astype(f32) + gdn_dt_bias.astype(f32))
    o_gdn = gated_deltanet(gq, gk, gv, beta, g)
    z = p['gdn_z'].astype(f32).reshape(B, S, GDN_HEADS, GDN_HEAD_DIM)
    y_gdn = (rmsnorm(o_gdn, gdn_norm_g) * jax.nn.silu(z)).reshape(B, S, GDN_W).astype(x.dtype)

    mq = p['ml_q'].astype(f32).reshape(B, S, MLSTM_HEADS, MLSTM_QK_DIM)
    mk = p['ml_k'].astype(f32).reshape(B, S, MLSTM_HEADS, MLSTM_QK_DIM)
    mv = p['ml_v'].astype(f32).reshape(B, S, MLSTM_HEADS, MLSTM_V_DIM)
    i_pre = p['ml_i'].astype(f32) + ml_i_bias.astype(f32)
    ml_logf = jax.nn.log_sigmoid(p['ml_f'].astype(f32) + ml_f_bias.astype(f32))
    h_tilde = mlstm(mq, mk, mv, i_pre, ml_logf)
    o_gate = jax.nn.sigmoid(p['ml_o'].astype(f32)).reshape(B, S, MLSTM_HEADS, MLSTM_V_DIM)
    y_ml = rmsnorm(o_gate * h_tilde, ml_norm_g).reshape(B, S, MLSTM_V_W).astype(x.dtype)

    gates = jax.nn.sigmoid(p['gate'].astype(f32) + gate_bias.astype(f32)).reshape(B, S, N_BRANCH, D)
    y = (gates[:, :, 0] * (y_fox @ w_fox_proj) + gates[:, :, 1] * (y_gdn @ w_gdn_proj)
         + gates[:, :, 2] * (y_ml @ w_ml_proj))
    x = x + y.astype(x.dtype) @ w_out

    xn = rmsnorm(x, norm_ffn_g)
    hid = causal_dwconv(xn @ w_up, ffn_conv_w)
    u, gt = hid[..., :D_FF], hid[..., D_FF:]
    return x + (jax.nn.silu(gt) * u) @ w_down


def setup_inputs(seed: int = 0) -> dict:
    key = jax.random.key(seed)
    ks = jax.random.split(key, 24)
    f32 = jnp.float32
    resid = (2 * DEPTH) ** -0.5

    def nrm(k, shape, scale):
        return jax.random.normal(k, shape, f32) * scale

    dt = jnp.exp(jax.random.uniform(ks[6], (DEPTH, GDN_HEADS), f32, np.log(1e-3), np.log(1e-1)))
    return {
        "x": nrm(ks[0], (BATCH, SEQ, D_MODEL), 1.0),
        "norm_mix_g": 1.0 + nrm(ks[1], (DEPTH, D_MODEL), 0.02),
        "w_in": nrm(ks[2], (DEPTH, D_MODEL, D_IN), D_MODEL ** -0.5),
        "fox_f_bias": 3.0 + nrm(ks[3], (DEPTH, FOX_HEADS), 0.5),
        "gdn_conv_w": nrm(ks[4], (DEPTH, GDN_CONV, 3 * GDN_W), GDN_CONV ** -0.5),
        "gdn_a_log": jnp.log(jax.random.uniform(ks[5], (DEPTH, GDN_HEADS), f32, 1.0, 16.0)),
        "gdn_dt_bias": dt + jnp.log(-jnp.expm1(-dt)),
        "gdn_norm_g": 1.0 + nrm(ks[7], (DEPTH, GDN_HEAD_DIM), 0.02),
        "ml_i_bias": nrm(ks[8], (DEPTH, MLSTM_HEADS), 0.1),
        "ml_f_bias": jax.random.uniform(ks[9], (DEPTH, MLSTM_HEADS), f32, 3.0, 6.0),
        "ml_norm_g": 1.0 + nrm(ks[10], (DEPTH, MLSTM_V_DIM), 0.02),
        "gate_bias": nrm(ks[11], (DEPTH, N_BRANCH * D_MODEL), 0.01),
        "w_fox_proj": nrm(ks[12], (DEPTH, FOX_W, D_MODEL), FOX_W ** -0.5),
        "w_gdn_proj": nrm(ks[13], (DEPTH, GDN_W, D_MODEL), GDN_W ** -0.5),
        "w_ml_proj": nrm(ks[14], (DEPTH, MLSTM_V_W, D_MODEL), MLSTM_V_W ** -0.5),
        "w_out": nrm(ks[15], (DEPTH, D_MODEL, D_MODEL), D_MODEL ** -0.5 * resid),
        "norm_ffn_g": 1.0 + nrm(ks[16], (DEPTH, D_MODEL), 0.02),
        "w_up": nrm(ks[17], (DEPTH, D_MODEL, 2 * D_FF), D_MODEL ** -0.5),
        "ffn_conv_w": nrm(ks[18], (DEPTH, FFN_CONV, 2 * D_FF), FFN_CONV ** -0.5),
        "w_down": nrm(ks[19], (DEPTH, D_FF, D_MODEL), D_FF ** -0.5 * resid),
        "norm_final_g": 1.0 + nrm(ks[20], (D_MODEL,), 0.02),
    }


def reference(x, norm_mix_g, w_in, fox_f_bias, gdn_conv_w, gdn_a_log, gdn_dt_bias, gdn_norm_g,
              ml_i_bias, ml_f_bias, ml_norm_g, gate_bias, w_fox_proj, w_gdn_proj, w_ml_proj,
              w_out, norm_ffn_g, w_up, ffn_conv_w, w_down, norm_final_g):
    for l in range(DEPTH):
        x = hybrid_layer(x, norm_mix_g[l], w_in[l], fox_f_bias[l], gdn_conv_w[l], gdn_a_log[l],
                         gdn_dt_bias[l], gdn_norm_g[l], ml_i_bias[l], ml_f_bias[l], ml_norm_g[l],
                         gate_bias[l], w_fox_proj[l], w_gdn_proj[l], w_ml_proj[l], w_out[l],
                         norm_ffn_g[l], w_up[l], ffn_conv_w[l], w_down[l])
    return rmsnorm(x, norm_final_g)
```

```python
import functools

import jax
import jax.numpy as jnp
from jax import lax
from jax.experimental import pallas as pl
from jax.experimental.pallas import tpu as pltpu

F32 = jnp.float32
BF16 = jnp.bfloat16
HIGHEST = lax.Precision.HIGHEST

EPS = 1e-6
HEADS = 8
HEAD_DIM = 128
ML_QK_DIM = 64
CHUNK = 64
CHUNK_SHIFT = CHUNK.bit_length() - 1
GDN_CONV = 4
FFN_CONV = 3
N_BRANCH = 3
LANES = 128
HALO = 16

SM_FOX_F, SM_GDN_B, SM_GDN_A, SM_ML_I, SM_ML_F = 0, 8, 16, 24, 32

V7X_VMEM_LIMIT_BYTES = 48 * 1024 * 1024


def _params(*semantics):
    return pltpu.CompilerParams(dimension_semantics=semantics,
                                vmem_limit_bytes=V7X_VMEM_LIMIT_BYTES)


def _rmsnorm(x, g):
    return x * lax.rsqrt(jnp.mean(x * x, axis=-1, keepdims=True) + EPS) * g


def _sigmoid(x):
    return 1.0 / (1.0 + jnp.exp(-x))


def _mm(a, b):
    return jnp.dot(a.astype(BF16), b.astype(BF16), preferred_element_type=F32)


def _mm_nt(a, b):
    return lax.dot_general(a.astype(BF16), b.astype(BF16), (((1,), (1,)), ((), ())),
                           preferred_element_type=F32)


def _mm_tn(a, b):
    return lax.dot_general(a.astype(BF16), b.astype(BF16), (((0,), (0,)), ((), ())),
                           preferred_element_type=F32)


def _mm_hi(a, b):
    return jnp.dot(a, b, precision=HIGHEST, preferred_element_type=F32)


def _norm_matmul_kernel(x_ref, g_ref, w_ref, cs_ref, o_ref, xn_ref):
    @pl.when(pl.program_id(1) == 0)
    def _():
        xn_ref[...] = _rmsnorm(x_ref[...], g_ref[...]).astype(BF16)

    acc = jnp.dot(xn_ref[...], w_ref[...], preferred_element_type=F32)
    o_ref[...] = (acc * cs_ref[...]).astype(o_ref.dtype)


def _norm_matmul(x, g, w, colscale, out_dtype, tm=512, tn=512):
    T, D = x.shape
    N = w.shape[1]
    return pl.pallas_call(
        _norm_matmul_kernel,
        grid=(T // tm, N // tn),
        in_specs=[
            pl.BlockSpec((tm, D), lambda i, j: (i, 0)),
            pl.BlockSpec((1, D), lambda i, j: (0, 0)),
            pl.BlockSpec((D, tn), lambda i, j: (0, j)),
            pl.BlockSpec((1, tn), lambda i, j: (0, j)),
        ],
        out_specs=pl.BlockSpec((tm, tn), lambda i, j: (i, j)),
        out_shape=jax.ShapeDtypeStruct((T, N), out_dtype),
        scratch_shapes=[pltpu.VMEM((tm, D), BF16)],
        compiler_params=_params("parallel", "arbitrary"),
    )(x, g, w, colscale)


def _small_kernel(x_ref, g_ref, w_ref, bias_ref, alog_ref, o_ref, carry_ref, *, blocks_per_seq):
    i = pl.program_id(0)
    tm = x_ref.shape[0]

    @pl.when(i % blocks_per_seq == 0)
    def _():
        carry_ref[...] = jnp.zeros_like(carry_ref)

    xn = _rmsnorm(x_ref[...], g_ref[...]).astype(BF16)
    t = jnp.dot(xn, w_ref[...], preferred_element_type=F32) + bias_ref[...]
    lane = lax.broadcasted_iota(jnp.int32, (tm, LANES), 1)
    e = jnp.log1p(jnp.exp(-jnp.abs(t)))
    logsig = jnp.minimum(t, 0.0) - e
    softplus = jnp.maximum(t, 0.0) + e
    g_decay = -jnp.exp(alog_ref[...]) * softplus
    is_logsig = (lane < SM_GDN_B) | ((lane >= SM_ML_F) & (lane < SM_ML_F + HEADS))
    val = jnp.where(is_logsig, logsig,
                    jnp.where(lane < SM_GDN_A, _sigmoid(t),
                              jnp.where(lane < SM_ML_I, g_decay, t)))

    r = lax.broadcasted_iota(jnp.int32, (tm, tm), 0)
    c = lax.broadcasted_iota(jnp.int32, (tm, tm), 1)
    tri = c <= r
    same_chunk = (r >> CHUNK_SHIFT) == (c >> CHUNK_SHIFT)
    tri_full = jnp.where(tri, 1.0, 0.0).astype(F32)
    tri_chunk = jnp.where(tri & same_chunk, 1.0, 0.0).astype(F32)
    cs_full = _mm_hi(tri_full, val) + carry_ref[...]
    cs_chunk = _mm_hi(tri_chunk, val)
    carry_ref[...] = cs_full[tm - 1:tm, :]

    is_chunk_cs = ((lane >= SM_GDN_A) & (lane < SM_ML_I)) | ((lane >= SM_ML_F) & (lane < SM_ML_F + HEADS))
    o_ref[...] = jnp.where(lane < SM_GDN_B, cs_full, jnp.where(is_chunk_cs, cs_chunk, val))


def _small_gates(x, g, w, bias, alog, seq, tm=512):
    T, D = x.shape
    return pl.pallas_call(
        functools.partial(_small_kernel, blocks_per_seq=seq // tm),
        grid=(T // tm,),
        in_specs=[
            pl.BlockSpec((tm, D), lambda i: (i, 0)),
            pl.BlockSpec((1, D), lambda i: (0, 0)),
            pl.BlockSpec((D, LANES), lambda i: (0, 0)),
            pl.BlockSpec((1, LANES), lambda i: (0, 0)),
            pl.BlockSpec((1, LANES), lambda i: (0, 0)),
        ],
        out_specs=pl.BlockSpec((tm, LANES), lambda i: (i, 0)),
        out_shape=jax.ShapeDtypeStruct((T, LANES), F32),
        scratch_shapes=[pltpu.VMEM((1, LANES), F32)],
        compiler_params=_params("arbitrary"),
    )(x, g, w, bias, alog)


def _proj_conv_kernel(x_ref, xh_ref, g_ref, *rest, taps, gated, blocks_per_seq):
    if gated:
        wu_ref, wg_ref, cu_ref, cg_ref, o_ref, xn_ref, hu_ref, hg_ref = rest
    else:
        wu_ref, cu_ref, o_ref, xn_ref, hu_ref = rest
    tm = x_ref.shape[0]
    i = pl.program_id(0)

    @pl.when(pl.program_id(1) == 0)
    def _():
        xn_ref[HALO:, :] = _rmsnorm(x_ref[...], g_ref[...]).astype(BF16)
        keep = jnp.where(i % blocks_per_seq == 0, 0.0, 1.0)
        xn_ref[:HALO, :] = (_rmsnorm(xh_ref[...], g_ref[...]) * keep).astype(BF16)

    def conv(w_ref, c_ref, h_ref):
        h_ref[...] = jnp.dot(xn_ref[...], w_ref[...], preferred_element_type=F32)
        acc = None
        for t in range(taps):
            term = h_ref[pl.ds(HALO - (taps - 1) + t, tm), :] * c_ref[t:t + 1, :]
            acc = term if acc is None else acc + term
        return acc

    if gated:
        u = conv(wu_ref, cu_ref, hu_ref)
        gt = conv(wg_ref, cg_ref, hg_ref)
        o_ref[...] = (gt * _sigmoid(gt) * u).astype(o_ref.dtype)
    else:
        u = conv(wu_ref, cu_ref, hu_ref)
        o_ref[...] = (u * _sigmoid(u)).astype(o_ref.dtype)


def _proj_conv(x, g, w, conv_w, seq, *, gated, tm=512, tn=512):
    T, D = x.shape
    taps = conv_w.shape[0]
    n_out = w.shape[1] // 2 if gated else w.shape[1]
    nj = n_out // tn
    halo_blocks = tm // HALO
    x_specs = [
        pl.BlockSpec((tm, D), lambda i, j: (i, 0)),
        pl.BlockSpec((HALO, D), lambda i, j: (jnp.maximum(i * halo_blocks - 1, 0), 0)),
        pl.BlockSpec((1, D), lambda i, j: (0, 0)),
    ]
    if gated:
        w_specs = [
            pl.BlockSpec((D, tn), lambda i, j: (0, j)),
            pl.BlockSpec((D, tn), lambda i, j: (0, nj + j)),
            pl.BlockSpec((taps, tn), lambda i, j: (0, j)),
            pl.BlockSpec((taps, tn), lambda i, j: (0, nj + j)),
        ]
        args = (x, x, g, w, w, conv_w, conv_w)
        scratch = [pltpu.VMEM((tm + HALO, D), BF16), pltpu.VMEM((tm + HALO, tn), F32),
                   pltpu.VMEM((tm + HALO, tn), F32)]
    else:
        w_specs = [
            pl.BlockSpec((D, tn), lambda i, j: (0, j)),
            pl.BlockSpec((taps, tn), lambda i, j: (0, j)),
        ]
        args = (x, x, g, w, conv_w)
        scratch = [pltpu.VMEM((tm + HALO, D), BF16), pltpu.VMEM((tm + HALO, tn), F32)]
    return pl.pallas_call(
        functools.partial(_proj_conv_kernel, taps=taps, gated=gated, blocks_per_seq=seq // tm),
        grid=(T // tm, nj),
        in_specs=x_specs + w_specs,
        out_specs=pl.BlockSpec((tm, tn), lambda i, j: (i, j)),
        out_shape=jax.ShapeDtypeStruct((T, n_out), BF16),
        scratch_shapes=scratch,
        compiler_params=_params("parallel", "arbitrary"),
    )(*args)


NEG_BIG = -1e30


def _fox_kernel(q_ref, k_ref, v_ref, f_ref, o_ref, m_ref, l_ref, acc_ref, *, tq, tk):
    qi = pl.program_id(2)
    ki = pl.program_id(3)
    last = ((qi + 1) * tq - 1) // tk

    @pl.when(ki == 0)
    def _():
        m_ref[...] = jnp.full_like(m_ref, NEG_BIG)
        l_ref[...] = jnp.zeros_like(l_ref)
        acc_ref[...] = jnp.zeros_like(acc_ref)

    @pl.when(ki <= last)
    def _():
        s = lax.dot_general(q_ref[...], k_ref[...], (((1,), (1,)), ((), ())),
                            preferred_element_type=F32)
        s = s - f_ref[...]
        qpos = qi * tq + lax.broadcasted_iota(jnp.int32, (tq, tk), 0)
        kpos = ki * tk + lax.broadcasted_iota(jnp.int32, (tq, tk), 1)
        s = jnp.where(kpos <= qpos, s, NEG_BIG)
        m_prev = m_ref[...]
        m_new = jnp.maximum(m_prev, jnp.max(s, axis=-1, keepdims=True))
        alpha = jnp.exp(m_prev - m_new)
        p = jnp.exp(s - m_new)
        l_ref[...] = alpha * l_ref[...] + jnp.sum(p, axis=-1, keepdims=True)
        acc_ref[...] = alpha * acc_ref[...] + jnp.dot(p.astype(BF16), v_ref[...],
                                                      preferred_element_type=F32)
        m_ref[...] = m_new

    @pl.when(ki == last)
    def _():
        o_ref[...] = (acc_ref[...] / l_ref[...]).astype(o_ref.dtype)


def _fox_attention(qkv, row_f, batch, seq, *, q_col, k_col, v_col, tq=512, tk=512):
    T = qkv.shape[0]
    nq, nk = seq // tq, seq // tk

    def kv_row(b, qi, ki):
        return b * nk + jnp.minimum(ki, ((qi + 1) * tq - 1) // tk)

    return pl.pallas_call(
        functools.partial(_fox_kernel, tq=tq, tk=tk),
        grid=(batch, HEADS, nq, nk),
        in_specs=[
            pl.BlockSpec((tq, HEAD_DIM), lambda b, h, qi, ki: (b * nq + qi, q_col + h)),
            pl.BlockSpec((tk, HEAD_DIM), lambda b, h, qi, ki: (kv_row(b, qi, ki), k_col + h)),
            pl.BlockSpec((tk, HEAD_DIM), lambda b, h, qi, ki: (kv_row(b, qi, ki), v_col + h)),
            pl.BlockSpec((None, None, 1, tk),
                         lambda b, h, qi, ki: (b, h, 0, jnp.minimum(ki, ((qi + 1) * tq - 1) // tk))),
        ],
        out_specs=pl.BlockSpec((tq, HEAD_DIM), lambda b, h, qi, ki: (b * nq + qi, h)),
        out_shape=jax.ShapeDtypeStruct((T, HEADS * HEAD_DIM), BF16),
        scratch_shapes=[pltpu.VMEM((tq, 1), F32), pltpu.VMEM((tq, 1), F32),
                        pltpu.VMEM((tq, HEAD_DIM), F32)],
        compiler_params=_params("parallel", "parallel", "parallel", "arbitrary"),
    )(qkv, qkv, qkv, row_f)


def _unit_lower_inverse(m, eye):
    n = m.shape[0]
    p = -m
    inv = eye + p
    k = 2
    while k < n:
        p = _mm_hi(p, p)
        inv = inv + _mm_hi(inv, p)
        k *= 2
    return inv


def _gdn_kernel(q_ref, k_ref, v_ref, z_ref, col_ref, rowg_ref, ng_ref, o_ref, s_ref):
    @pl.when(pl.program_id(1) == 0)
    def _():
        s_ref[...] = jnp.zeros_like(s_ref)

    L = CHUNK
    r = lax.broadcasted_iota(jnp.int32, (L, L), 0)
    c = lax.broadcasted_iota(jnp.int32, (L, L), 1)
    tri = c <= r
    strict = c < r
    eye = jnp.where(c == r, 1.0, 0.0).astype(F32)
    col = col_ref[...]
    rowg = rowg_ref[...]
    ng = ng_ref[...]
    for h in range(HEADS):
        sl = slice(h * HEAD_DIM, (h + 1) * HEAD_DIM)
        q = q_ref[:, sl].astype(F32)
        k = k_ref[:, sl].astype(F32)
        v = v_ref[:, sl].astype(F32)
        qn = q * lax.rsqrt(jnp.sum(q * q, axis=-1, keepdims=True) + EPS) * (HEAD_DIM ** -0.5)
        kn = k * lax.rsqrt(jnp.sum(k * k, axis=-1, keepdims=True) + EPS)
        beta = col[:, SM_GDN_B + h:SM_GDN_B + h + 1]
        gc = col[:, SM_GDN_A + h:SM_GDN_A + h + 1]
        gr = rowg[h:h + 1, :]
        g_last = gc[L - 1:L, :]
        decay = jnp.where(tri, jnp.exp(jnp.where(tri, gc - gr, 0.0)), 0.0)
        eg = jnp.exp(gc)
        kb = kn * beta
        m = jnp.where(strict, _mm_nt(kb, kn) * decay, 0.0)
        t_inv = _unit_lower_inverse(m, eye)
        x = _mm_hi(t_inv, jnp.concatenate([v * beta, kb * eg], axis=-1))
        u = x[:, :HEAD_DIM]
        w = x[:, HEAD_DIM:]
        a_qk = _mm_nt(qn, kn) * decay
        state = s_ref[h]
        v_new = u - _mm(w, state)
        o = _mm(qn * eg, state) + _mm(a_qk, v_new)
        s_ref[h] = state * jnp.exp(g_last) + _mm_tn(kn * jnp.exp(g_last - gc), v_new)
        z = z_ref[:, sl]
        o_ref[:, sl] = (_rmsnorm(o, ng) * (z * _sigmoid(z))).astype(o_ref.dtype)


def _gdn(qkv, z_src, col_small, row_g, norm_g, batch, seq, *, z_col):
    T = qkv.shape[0]
    nc = seq // CHUNK
    width = HEADS * HEAD_DIM
    row = lambda b, c: b * nc + c
    return pl.pallas_call(
        _gdn_kernel,
        grid=(batch, nc),
        in_specs=[
            pl.BlockSpec((CHUNK, width), lambda b, c: (row(b, c), 0)),
            pl.BlockSpec((CHUNK, width), lambda b, c: (row(b, c), 1)),
            pl.BlockSpec((CHUNK, width), lambda b, c: (row(b, c), 2)),
            pl.BlockSpec((CHUNK, width), lambda b, c: (row(b, c), z_col)),
            pl.BlockSpec((CHUNK, LANES), lambda b, c: (row(b, c), 0)),
            pl.BlockSpec((None, HEADS, CHUNK), lambda b, c: (row(b, c), 0, 0)),
            pl.BlockSpec((1, HEAD_DIM), lambda b, c: (0, 0)),
        ],
        out_specs=pl.BlockSpec((CHUNK, width), lambda b, c: (row(b, c), 0)),
        out_shape=jax.ShapeDtypeStruct((T, width), BF16),
        scratch_shapes=[pltpu.VMEM((HEADS, HEAD_DIM, HEAD_DIM), F32)],
        compiler_params=_params("parallel", "arbitrary"),
    )(qkv, qkv, qkv, z_src, col_small, row_g, norm_g)


def _mlstm_kernel(q_ref, k_ref, v_ref, og_ref, col_ref, rowi_ref, rowb_ref, ng_ref, o_ref,
                  c_ref, n_ref, m_ref):
    @pl.when(pl.program_id(1) == 0)
    def _():
        c_ref[...] = jnp.zeros_like(c_ref)
        n_ref[...] = jnp.zeros_like(n_ref)
        m_ref[...] = jnp.zeros_like(m_ref)

    L = CHUNK
    r = lax.broadcasted_iota(jnp.int32, (L, L), 0)
    c = lax.broadcasted_iota(jnp.int32, (L, L), 1)
    tri = c <= r
    col = col_ref[...]
    rowi = rowi_ref[...]
    rowb = rowb_ref[...]
    ng = ng_ref[...]
    for h in range(HEADS):
        qs = slice(h * ML_QK_DIM, (h + 1) * ML_QK_DIM)
        vs = slice(h * HEAD_DIM, (h + 1) * HEAD_DIM)
        q = q_ref[:, qs].astype(F32)
        k = k_ref[:, qs].astype(F32)
        v = v_ref[:, vs]
        ic = col[:, SM_ML_I + h:SM_ML_I + h + 1]
        bc = col[:, SM_ML_F + h:SM_ML_F + h + 1]
        ir = rowi[h:h + 1, :]
        br = rowb[h:h + 1, :]
        dm = jnp.where(tri, bc - br + ir, -jnp.inf)
        dmax = jnp.max(dm, axis=-1, keepdims=True)
        qk = _mm_nt(q, k)
        m_prev = m_ref[h]
        inter = bc + m_prev
        m_r = jnp.maximum(inter, dmax)
        w_inter = jnp.exp(inter - m_r)
        wd = jnp.exp(dm - m_r)
        sm = wd * qk
        c_state = c_ref[h]
        n_state = n_ref[h]
        num = w_inter * _mm(q, c_state) + _mm(sm, v)
        den = (w_inter * jnp.sum(q * n_state, axis=-1, keepdims=True)
               + jnp.sum(sm, axis=-1, keepdims=True))
        h_tilde = num / jnp.maximum(jnp.abs(den), jnp.exp(-m_r))
        b_last = bc[L - 1:L, :]
        m_last = m_r[L - 1:L, :]
        w_last = w_inter[L - 1:L, :]
        wk = jnp.exp(b_last - bc + ic - m_last) * k
        c_ref[h] = w_last * c_state + _mm_tn(wk, v)
        n_ref[h] = w_last * n_state + jnp.sum(wk, axis=0, keepdims=True)
        m_ref[h] = m_last
        o_ref[:, vs] = _rmsnorm(_sigmoid(og_ref[:, vs]) * h_tilde, ng).astype(o_ref.dtype)


def _mlstm(qkv, og_src, col_small, row_i, row_b, norm_g, batch, seq, *, q_col, k_col, v_col, og_col):
    T = qkv.shape[0]
    nc = seq // CHUNK
    qk_width = HEADS * ML_QK_DIM
    width = HEADS * HEAD_DIM
    row = lambda b, c: b * nc + c
    return pl.pallas_call(
        _mlstm_kernel,
        grid=(batch, nc),
        in_specs=[
            pl.BlockSpec((CHUNK, qk_width), lambda b, c: (row(b, c), q_col)),
            pl.BlockSpec((CHUNK, qk_width), lambda b, c: (row(b, c), k_col)),
            pl.BlockSpec((CHUNK, width), lambda b, c: (row(b, c), v_col)),
            pl.BlockSpec((CHUNK, width), lambda b, c: (row(b, c), og_col)),
            pl.BlockSpec((CHUNK, LANES), lambda b, c: (row(b, c), 0)),
            pl.BlockSpec((None, HEADS, CHUNK), lambda b, c: (row(b, c), 0, 0)),
            pl.BlockSpec((None, HEADS, CHUNK), lambda b, c: (row(b, c), 0, 0)),
            pl.BlockSpec((1, HEAD_DIM), lambda b, c: (0, 0)),
        ],
        out_specs=pl.BlockSpec((CHUNK, width), lambda b, c: (row(b, c), 0)),
        out_shape=jax.ShapeDtypeStruct((T, width), BF16),
        scratch_shapes=[pltpu.VMEM((HEADS, ML_QK_DIM, HEAD_DIM), F32),
                        pltpu.VMEM((HEADS, 1, ML_QK_DIM), F32),
                        pltpu.VMEM((HEADS, 1, 1), F32)],
        compiler_params=_params("parallel", "arbitrary"),
    )(qkv, qkv, qkv, og_src, col_small, row_i, row_b, norm_g)


def _merge_kernel(yf_ref, yg_ref, ym_ref, wf_ref, wg_ref, wm_ref, g0_ref, g1_ref, g2_ref,
                  b0_ref, b1_ref, b2_ref, o_ref):
    def branch(y_ref, w_ref, g_ref, b_ref):
        return _sigmoid(g_ref[...] + b_ref[...]) * jnp.dot(y_ref[...], w_ref[...],
                                                          preferred_element_type=F32)

    y = (branch(yf_ref, wf_ref, g0_ref, b0_ref) + branch(yg_ref, wg_ref, g1_ref, b1_ref)
         + branch(ym_ref, wm_ref, g2_ref, b2_ref))
    o_ref[...] = y.astype(o_ref.dtype)


def _merge(y_fox, y_gdn, y_ml, w_fox, w_gdn, w_ml, gate_src, gate_bias, d_model, *, gate_col,
           tm=512, tn=512):
    T, K = y_fox.shape
    nj = d_model // tn
    y_spec = pl.BlockSpec((tm, K), lambda i, j: (i, 0))
    w_spec = pl.BlockSpec((K, tn), lambda i, j: (0, j))
    gate_specs = [pl.BlockSpec((tm, tn), lambda i, j, n=n: (i, gate_col + n * nj + j))
                  for n in range(N_BRANCH)]
    bias_specs = [pl.BlockSpec((1, tn), lambda i, j, n=n: (0, n * nj + j)) for n in range(N_BRANCH)]
    return pl.pallas_call(
        _merge_kernel,
        grid=(T // tm, nj),
        in_specs=[y_spec, y_spec, y_spec, w_spec, w_spec, w_spec] + gate_specs + bias_specs,
        out_specs=pl.BlockSpec((tm, tn), lambda i, j: (i, j)),
        out_shape=jax.ShapeDtypeStruct((T, d_model), BF16),
        compiler_params=_params("parallel", "parallel"),
    )(y_fox, y_gdn, y_ml, w_fox, w_gdn, w_ml, gate_src, gate_src, gate_src,
      gate_bias, gate_bias, gate_bias)


def _matmul_residual_kernel(a_ref, w_ref, x_ref, o_ref):
    o_ref[...] = x_ref[...] + jnp.dot(a_ref[...], w_ref[...], preferred_element_type=F32)


def _matmul_residual(a, w, x, tm=512, tn=512):
    T, K = a.shape
    N = w.shape[1]
    return pl.pallas_call(
        _matmul_residual_kernel,
        grid=(T // tm, N // tn),
        in_specs=[
            pl.BlockSpec((tm, K), lambda i, j: (i, 0)),
            pl.BlockSpec((K, tn), lambda i, j: (0, j)),
            pl.BlockSpec((tm, tn), lambda i, j: (i, j)),
        ],
        out_specs=pl.BlockSpec((tm, tn), lambda i, j: (i, j)),
        out_shape=jax.ShapeDtypeStruct((T, N), F32),
        compiler_params=_params("parallel", "parallel"),
    )(a, w, x)


def _final_norm_kernel(x_ref, g_ref, o_ref):
    o_ref[...] = _rmsnorm(x_ref[...], g_ref[...])


def _final_norm(x, g, tm=512):
    T, D = x.shape
    return pl.pallas_call(
        _final_norm_kernel,
        grid=(T // tm,),
        in_specs=[pl.BlockSpec((tm, D), lambda i: (i, 0)), pl.BlockSpec((1, D), lambda i: (0, 0))],
        out_specs=pl.BlockSpec((tm, D), lambda i: (i, 0)),
        out_shape=jax.ShapeDtypeStruct((T, D), F32),
        compiler_params=_params("parallel"),
    )(x, g)


def _pad_lanes(v):
    return jnp.pad(v, (0, LANES - v.shape[0])).reshape(1, LANES)


def _chunk_rows(col_small, start, batch, seq):
    v = col_small[:, start:start + HEADS].reshape(batch * (seq // CHUNK), CHUNK, HEADS)
    return v.transpose(0, 2, 1)


def _layer(x, p, batch, seq):
    T, D = x.shape
    fox_w = HEADS * HEAD_DIM
    gdn_w = HEADS * HEAD_DIM
    ml_qk_w = HEADS * ML_QK_DIM
    ml_v_w = HEADS * HEAD_DIM
    widths = [fox_w, fox_w, fox_w, HEADS, 3 * gdn_w, gdn_w, HEADS, HEADS,
              ml_qk_w, ml_qk_w, ml_v_w, HEADS, HEADS, ml_v_w, N_BRANCH * D]
    names = ["fox_q", "fox_k", "fox_v", "fox_f", "gdn_qkv", "gdn_z", "gdn_b", "gdn_a",
             "ml_q", "ml_k", "ml_v", "ml_i", "ml_f", "ml_o", "gate"]
    off = {}
    o = 0
    for n, w in zip(names, widths):
        off[n] = (o, o + w)
        o += w
    w_in = p["w_in"].astype(BF16)
    cols = lambda a, b: w_in[:, off[a][0]:off[b][1]]
    g_mix = p["norm_mix_g"].reshape(1, D)

    w_a = jnp.concatenate([cols("fox_q", "fox_v"), cols("ml_q", "ml_v")], axis=1)
    scale_a = jnp.concatenate([
        jnp.full((fox_w,), HEAD_DIM ** -0.5, F32), jnp.ones((2 * fox_w,), F32),
        jnp.full((ml_qk_w,), ML_QK_DIM ** -0.5, F32), jnp.ones((ml_qk_w + ml_v_w,), F32)]).reshape(1, -1)
    proj_a = _norm_matmul(x, g_mix, w_a, scale_a, BF16)
    w_b = jnp.concatenate([cols("gdn_z", "gdn_z"), cols("ml_o", "ml_o"), cols("gate", "gate")], axis=1)
    proj_b = _norm_matmul(x, g_mix, w_b, jnp.ones((1, w_b.shape[1]), F32), F32)
    w_s = jnp.concatenate([cols("fox_f", "fox_f"), cols("gdn_b", "gdn_a"), cols("ml_i", "ml_f")], axis=1)
    w_s = jnp.pad(w_s, ((0, 0), (0, LANES - w_s.shape[1])))
    zeros8 = jnp.zeros((HEADS,), F32)
    bias_s = _pad_lanes(jnp.concatenate([p["fox_f_bias"], zeros8, p["gdn_dt_bias"],
                                         p["ml_i_bias"], p["ml_f_bias"]]))
    alog_s = _pad_lanes(jnp.concatenate([zeros8, zeros8, p["gdn_a_log"]]))
    col_small = _small_gates(x, g_mix, w_s, bias_s, alog_s, seq)
    row_f = col_small[:, SM_FOX_F:SM_FOX_F + HEADS].reshape(batch, seq, HEADS)
    row_f = row_f.transpose(0, 2, 1).reshape(batch, HEADS, 1, seq)
    row_g = _chunk_rows(col_small, SM_GDN_A, batch, seq)
    row_i = _chunk_rows(col_small, SM_ML_I, batch, seq)
    row_b = _chunk_rows(col_small, SM_ML_F, batch, seq)
    gdn_qkv = _proj_conv(x, g_mix, cols("gdn_qkv", "gdn_qkv"), p["gdn_conv_w"], seq, gated=False)

    y_fox = _fox_attention(proj_a, row_f, batch, seq, q_col=0, k_col=HEADS, v_col=2 * HEADS)
    y_gdn = _gdn(gdn_qkv, proj_b, col_small, row_g, p["gdn_norm_g"].reshape(1, HEAD_DIM),
                 batch, seq, z_col=0)
    ml_base = 3 * fox_w
    y_ml = _mlstm(proj_a, proj_b, col_small, row_i, row_b, p["ml_norm_g"].reshape(1, HEAD_DIM),
                  batch, seq, q_col=ml_base // ml_qk_w, k_col=ml_base // ml_qk_w + 1,
                  v_col=(ml_base + 2 * ml_qk_w) // ml_v_w, og_col=1)

    tn = 512
    y = _merge(y_fox, y_gdn, y_ml, p["w_fox_proj"].astype(BF16), p["w_gdn_proj"].astype(BF16),
               p["w_ml_proj"].astype(BF16), proj_b, p["gate_bias"].reshape(1, -1), D,
               gate_col=(gdn_w + ml_v_w) // tn, tn=tn)
    x = _matmul_residual(y, p["w_out"].astype(BF16), x)

    act = _proj_conv(x, p["norm_ffn_g"].reshape(1, D), p["w_up"].astype(BF16), p["ffn_conv_w"], seq,
                     gated=True)
    return _matmul_residual(act, p["w_down"].astype(BF16), x)


def kernel(x, norm_mix_g, w_in, fox_f_bias, gdn_conv_w, gdn_a_log, gdn_dt_bias, gdn_norm_g,
           ml_i_bias, ml_f_bias, ml_norm_g, gate_bias, w_fox_proj, w_gdn_proj, w_ml_proj,
           w_out, norm_ffn_g, w_up, ffn_conv_w, w_down, norm_final_g):
    batch, seq, d_model = x.shape
    stacked = dict(norm_mix_g=norm_mix_g, w_in=w_in, fox_f_bias=fox_f_bias, gdn_conv_w=gdn_conv_w,
                   gdn_a_log=gdn_a_log, gdn_dt_bias=gdn_dt_bias, gdn_norm_g=gdn_norm_g,
                   ml_i_bias=ml_i_bias, ml_f_bias=ml_f_bias, ml_norm_g=ml_norm_g,
                   gate_bias=gate_bias, w_fox_proj=w_fox_proj, w_gdn_proj=w_gdn_proj,
                   w_ml_proj=w_ml_proj, w_out=w_out, norm_ffn_g=norm_ffn_g, w_up=w_up,
                   ffn_conv_w=ffn_conv_w, w_down=w_down)

    def body(h, p):
        return _layer(h, p, batch, seq), None

    h, _ = lax.scan(body, x.reshape(batch * seq, d_model), stacked)
    return _final_norm(h, norm_final_g.reshape(1, d_model)).reshape(batch, seq, d_model)
```

```python
import functools

import jax
import jax.numpy as jnp
from jax import lax
from jax.experimental import pallas as pl
from jax.experimental.pallas import tpu as pltpu

F32 = jnp.float32
BF16 = jnp.bfloat16
HIGHEST = lax.Precision.HIGHEST

EPS = 1e-6
HEADS = 8
HEAD_DIM = 128
ML_QK_DIM = 64
CHUNK = 64
CHUNK_SHIFT = CHUNK.bit_length() - 1
GDN_CONV = 4
FFN_CONV = 3
N_BRANCH = 3
LANES = 128
HALO = 16

SM_FOX_F, SM_GDN_B, SM_GDN_A, SM_ML_I, SM_ML_F = 0, 8, 16, 24, 32

V7X_VMEM_LIMIT_BYTES = 48 * 1024 * 1024


def _params(*semantics):
    return pltpu.CompilerParams(dimension_semantics=semantics,
                                vmem_limit_bytes=V7X_VMEM_LIMIT_BYTES)


def _rmsnorm(x, g):
    return x * lax.rsqrt(jnp.mean(x * x, axis=-1, keepdims=True) + EPS) * g


def _sigmoid(x):
    return 1.0 / (1.0 + jnp.exp(-x))


def _mm(a, b):
    return jnp.dot(a.astype(BF16), b.astype(BF16), preferred_element_type=F32)


def _mm_nt(a, b):
    return lax.dot_general(a.astype(BF16), b.astype(BF16), (((1,), (1,)), ((), ())),
                           preferred_element_type=F32)


def _mm_tn(a, b):
    return lax.dot_general(a.astype(BF16), b.astype(BF16), (((0,), (0,)), ((), ())),
                           preferred_element_type=F32)


def _mm_hi(a, b):
    return jnp.dot(a, b, precision=HIGHEST, preferred_element_type=F32)


def _norm_matmul_kernel(x_ref, g_ref, w_ref, cs_ref, o_ref, xn_ref):
    @pl.when(pl.program_id(1) == 0)
    def _():
        xn_ref[...] = _rmsnorm(x_ref[...], g_ref[...]).astype(BF16)

    acc = jnp.dot(xn_ref[...], w_ref[...], preferred_element_type=F32)
    o_ref[...] = (acc * cs_ref[...]).astype(o_ref.dtype)


def _norm_matmul(x, g, w, colscale, out_dtype, tm=1024, tn=512):
    T, D = x.shape
    N = w.shape[1]
    return pl.pallas_call(
        _norm_matmul_kernel,
        grid=(T // tm, N // tn),
        in_specs=[
            pl.BlockSpec((tm, D), lambda i, j: (i, 0)),
            pl.BlockSpec((1, D), lambda i, j: (0, 0)),
            pl.BlockSpec((D, tn), lambda i, j: (0, j)),
            pl.BlockSpec((1, tn), lambda i, j: (0, j)),
        ],
        out_specs=pl.BlockSpec((tm, tn), lambda i, j: (i, j)),
        out_shape=jax.ShapeDtypeStruct((T, N), out_dtype),
        scratch_shapes=[pltpu.VMEM((tm, D), BF16)],
        compiler_params=_params("parallel", "arbitrary"),
    )(x, g, w, colscale)


def _small_kernel(x_ref, g_ref, w_ref, bias_ref, alog_ref, o_ref, carry_ref, *, blocks_per_seq):
    i = pl.program_id(0)
    tm = x_ref.shape[0]

    @pl.when(i % blocks_per_seq == 0)
    def _():
        carry_ref[...] = jnp.zeros_like(carry_ref)

    xn = _rmsnorm(x_ref[...], g_ref[...]).astype(BF16)
    t = jnp.dot(xn, w_ref[...], preferred_element_type=F32) + bias_ref[...]
    lane = lax.broadcasted_iota(jnp.int32, (tm, LANES), 1)
    e = jnp.log1p(jnp.exp(-jnp.abs(t)))
    logsig = jnp.minimum(t, 0.0) - e
    softplus = jnp.maximum(t, 0.0) + e
    g_decay = -jnp.exp(alog_ref[...]) * softplus
    is_logsig = (lane < SM_GDN_B) | ((lane >= SM_ML_F) & (lane < SM_ML_F + HEADS))
    val = jnp.where(is_logsig, logsig,
                    jnp.where(lane < SM_GDN_A, _sigmoid(t),
                              jnp.where(lane < SM_ML_I, g_decay, t)))

    r = lax.broadcasted_iota(jnp.int32, (tm, tm), 0)
    c = lax.broadcasted_iota(jnp.int32, (tm, tm), 1)
    tri = c <= r
    same_chunk = (r >> CHUNK_SHIFT) == (c >> CHUNK_SHIFT)
    tri_full = jnp.where(tri, 1.0, 0.0).astype(F32)
    tri_chunk = jnp.where(tri & same_chunk, 1.0, 0.0).astype(F32)
    cs_full = _mm_hi(tri_full, val) + carry_ref[...]
    cs_chunk = _mm_hi(tri_chunk, val)
    carry_ref[...] = cs_full[tm - 1:tm, :]

    is_chunk_cs = ((lane >= SM_GDN_A) & (lane < SM_ML_I)) | ((lane >= SM_ML_F) & (lane < SM_ML_F + HEADS))
    o_ref[...] = jnp.where(lane < SM_GDN_B, cs_full, jnp.where(is_chunk_cs, cs_chunk, val))


def _small_gates(x, g, w, bias, alog, seq, tm=512):
    T, D = x.shape
    return pl.pallas_call(
        functools.partial(_small_kernel, blocks_per_seq=seq // tm),
        grid=(T // tm,),
        in_specs=[
            pl.BlockSpec((tm, D), lambda i: (i, 0)),
            pl.BlockSpec((1, D), lambda i: (0, 0)),
            pl.BlockSpec((D, LANES), lambda i: (0, 0)),
            pl.BlockSpec((1, LANES), lambda i: (0, 0)),
            pl.BlockSpec((1, LANES), lambda i: (0, 0)),
        ],
        out_specs=pl.BlockSpec((tm, LANES), lambda i: (i, 0)),
        out_shape=jax.ShapeDtypeStruct((T, LANES), F32),
        scratch_shapes=[pltpu.VMEM((1, LANES), F32)],
        compiler_params=_params("arbitrary"),
    )(x, g, w, bias, alog)


def _proj_conv_kernel(x_ref, xh_ref, g_ref, *rest, taps, gated, blocks_per_seq):
    if gated:
        wu_ref, wg_ref, cu_ref, cg_ref, o_ref, xn_ref, hu_ref, hg_ref = rest
    else:
        wu_ref, cu_ref, o_ref, xn_ref, hu_ref = rest
    tm = x_ref.shape[0]
    i = pl.program_id(0)

    @pl.when(pl.program_id(1) == 0)
    def _():
        xn_ref[HALO:, :] = _rmsnorm(x_ref[...], g_ref[...]).astype(BF16)
        keep = jnp.where(i % blocks_per_seq == 0, 0.0, 1.0)
        xn_ref[:HALO, :] = (_rmsnorm(xh_ref[...], g_ref[...]) * keep).astype(BF16)

    def conv(w_ref, c_ref, h_ref):
        h_ref[...] = jnp.dot(xn_ref[...], w_ref[...], preferred_element_type=F32)
        acc = None
        for t in range(taps):
            term = h_ref[pl.ds(HALO - (taps - 1) + t, tm), :] * c_ref[t:t + 1, :]
            acc = term if acc is None else acc + term
        return acc

    if gated:
        u = conv(wu_ref, cu_ref, hu_ref)
        gt = conv(wg_ref, cg_ref, hg_ref)
        o_ref[...] = (gt * _sigmoid(gt) * u).astype(o_ref.dtype)
    else:
        u = conv(wu_ref, cu_ref, hu_ref)
        o_ref[...] = (u * _sigmoid(u)).astype(o_ref.dtype)


def _proj_conv(x, g, w, conv_w, seq, *, gated, tm=1024, tn=512):
    T, D = x.shape
    taps = conv_w.shape[0]
    n_out = w.shape[1] // 2 if gated else w.shape[1]
    nj = n_out // tn
    halo_blocks = tm // HALO
    x_specs = [
        pl.BlockSpec((tm, D), lambda i, j: (i, 0)),
        pl.BlockSpec((HALO, D), lambda i, j: (jnp.maximum(i * halo_blocks - 1, 0), 0)),
        pl.BlockSpec((1, D), lambda i, j: (0, 0)),
    ]
    if gated:
        w_specs = [
            pl.BlockSpec((D, tn), lambda i, j: (0, j)),
            pl.BlockSpec((D, tn), lambda i, j: (0, nj + j)),
            pl.BlockSpec((taps, tn), lambda i, j: (0, j)),
            pl.BlockSpec((taps, tn), lambda i, j: (0, nj + j)),
        ]
        args = (x, x, g, w, w, conv_w, conv_w)
        scratch = [pltpu.VMEM((tm + HALO, D), BF16), pltpu.VMEM((tm + HALO, tn), F32),
                   pltpu.VMEM((tm + HALO, tn), F32)]
    else:
        w_specs = [
            pl.BlockSpec((D, tn), lambda i, j: (0, j)),
            pl.BlockSpec((taps, tn), lambda i, j: (0, j)),
        ]
        args = (x, x, g, w, conv_w)
        scratch = [pltpu.VMEM((tm + HALO, D), BF16), pltpu.VMEM((tm + HALO, tn), F32)]
    return pl.pallas_call(
        functools.partial(_proj_conv_kernel, taps=taps, gated=gated, blocks_per_seq=seq // tm),
        grid=(T // tm, nj),
        in_specs=x_specs + w_specs,
        out_specs=pl.BlockSpec((tm, tn), lambda i, j: (i, j)),
        out_shape=jax.ShapeDtypeStruct((T, n_out), BF16),
        scratch_shapes=scratch,
        compiler_params=_params("parallel", "arbitrary"),
    )(*args)


NEG_BIG = -1e30


LOG2E = 1.4426950408889634


def _fox_kernel(q_ref, k_ref, v_ref, f_ref, o_ref, m_ref, acc_ref, *, tq, tk, hps):
    qi = pl.program_id(2)
    ones = jnp.ones((tk, HEAD_DIM), BF16)
    lane_tiles = tk // LANES
    m_ref[...] = jnp.full_like(m_ref, NEG_BIG)
    acc_ref[...] = jnp.zeros_like(acc_ref)

    def block(j, masked):
        start = pl.multiple_of(j * tk, tk)
        heads = [slice(h * HEAD_DIM, (h + 1) * HEAD_DIM) for h in range(hps)]
        scores = []
        for h, hs in enumerate(heads):
            s = lax.dot_general(q_ref[:, hs], k_ref[pl.ds(start, tk), hs],
                                (((1,), (1,)), ((), ())), preferred_element_type=F32)
            s = s - f_ref[h, pl.ds(j, 1), :] * LOG2E
            if masked:
                row = lax.broadcasted_iota(jnp.int32, (tq, tk), 0)
                col = lax.broadcasted_iota(jnp.int32, (tq, tk), 1)
                s = jnp.where(col <= row, s, NEG_BIG)
            scores.append([s[:, c * LANES:(c + 1) * LANES] for c in range(lane_tiles)])
        probs, alphas = [], []
        for h, tiles in enumerate(scores):
            m_prev = m_ref[h]
            m_tile = functools.reduce(jnp.maximum, tiles)
            m_new = jnp.maximum(m_prev, jnp.max(m_tile, axis=-1, keepdims=True))
            m_ref[h] = m_new
            alphas.append(jnp.exp2(m_prev - m_new))
            probs.append(jnp.concatenate([jnp.exp2(t - m_new).astype(BF16) for t in tiles], axis=1))
        for h, hs in enumerate(heads):
            v_ext = jnp.concatenate([v_ref[pl.ds(start, tk), hs], ones], axis=1)
            pv = jnp.dot(probs[h], v_ext, preferred_element_type=F32)
            acc_ref[h] = jnp.concatenate([alphas[h], alphas[h]], axis=1) * acc_ref[h] + pv

    def full_block(j, carry):
        block(j, masked=False)
        return carry

    lax.fori_loop(0, qi, full_block, 0)
    block(qi, masked=True)
    for h in range(hps):
        acc = acc_ref[h]
        o_ref[:, h * HEAD_DIM:(h + 1) * HEAD_DIM] = (
            acc[:, :HEAD_DIM] / acc[:, HEAD_DIM:]).astype(o_ref.dtype)


def _fox_attention(qkv, f_cum, batch, seq, *, q_col, k_col, v_col, tq=512, hps=2):
    T = qkv.shape[0]
    nq = seq // tq
    width = hps * HEAD_DIM
    f_blocks = f_cum.reshape(batch, HEADS, nq, tq)
    return pl.pallas_call(
        functools.partial(_fox_kernel, tq=tq, tk=tq, hps=hps),
        grid=(batch, HEADS // hps, nq),
        in_specs=[
            pl.BlockSpec((tq, width), lambda b, h, qi: (b * nq + qi, q_col + h)),
            pl.BlockSpec((seq, width), lambda b, h, qi: (b, k_col + h)),
            pl.BlockSpec((seq, width), lambda b, h, qi: (b, v_col + h)),
            pl.BlockSpec((None, hps, nq, tq), lambda b, h, qi: (b, h, 0, 0)),
        ],
        out_specs=pl.BlockSpec((tq, width), lambda b, h, qi: (b * nq + qi, h)),
        out_shape=jax.ShapeDtypeStruct((T, HEADS * HEAD_DIM), BF16),
        scratch_shapes=[pltpu.VMEM((hps, tq, LANES), F32),
                        pltpu.VMEM((hps, tq, 2 * HEAD_DIM), F32)],
        compiler_params=_params("parallel", "parallel", "arbitrary"),
    )(qkv, qkv, qkv, f_blocks)


def _bmm(a, b):
    return jnp.einsum("gmk,gkn->gmn", a.astype(BF16), b.astype(BF16), preferred_element_type=F32)


def _bmm_nt(a, b):
    return jnp.einsum("gmk,gnk->gmn", a.astype(BF16), b.astype(BF16), preferred_element_type=F32)


def _bmm_tn(a, b):
    return jnp.einsum("gkm,gkn->gmn", a.astype(BF16), b.astype(BF16), preferred_element_type=F32)


def _unit_lower_inverse(m, eye, r, c):
    n = m.shape[-1]
    same_pair = (r >> 1) == (c >> 1)
    x = eye - jnp.where(same_pair, m, 0.0)
    b = 2
    while b < n:
        shift = b.bit_length() - 1
        joins = ((r >> (shift + 1)) == (c >> (shift + 1))) & ((r >> shift) != (c >> shift))
        x = x - _bmm(_bmm(x, jnp.where(joins, m, 0.0)), x)
        b *= 2
    return x


def _head_stack(ref):
    return jnp.stack([ref[:, h * HEAD_DIM:(h + 1) * HEAD_DIM] for h in range(HEADS)])


def _head_cols(col, start):
    return jnp.stack([col[:, start + h:start + h + 1] for h in range(HEADS)])


def _gdn_kernel(q_ref, k_ref, v_ref, z_ref, col_ref, rowg_ref, ng_ref, o_ref, s_ref):
    @pl.when(pl.program_id(1) == 0)
    def _():
        s_ref[...] = jnp.zeros_like(s_ref)

    L = CHUNK
    r = lax.broadcasted_iota(jnp.int32, (1, L, L), 1)
    c = lax.broadcasted_iota(jnp.int32, (1, L, L), 2)
    tri = c <= r
    eye = jnp.where(c == r, 1.0, 0.0).astype(F32)
    col = col_ref[...]
    q = _head_stack(q_ref).astype(F32)
    k = _head_stack(k_ref).astype(F32)
    v = _head_stack(v_ref).astype(F32)
    qn = q * lax.rsqrt(jnp.sum(q * q, axis=-1, keepdims=True) + EPS) * (HEAD_DIM ** -0.5)
    kn = k * lax.rsqrt(jnp.sum(k * k, axis=-1, keepdims=True) + EPS)
    beta = _head_cols(col, SM_GDN_B)
    gc = _head_cols(col, SM_GDN_A)
    gr = rowg_ref[...][:, None, :]
    g_last = gc[:, L - 1:L, :]
    decay = jnp.where(tri, jnp.exp(jnp.where(tri, gc - gr, 0.0)), 0.0)
    eg = jnp.exp(gc)
    kb = kn * beta
    m = jnp.where(c < r, _bmm_nt(kb, kn) * decay, 0.0)
    t_inv = _unit_lower_inverse(m, eye, r, c)
    x = _bmm(t_inv, jnp.concatenate([v * beta, kb * eg], axis=-1))
    u = x[:, :, :HEAD_DIM]
    w = x[:, :, HEAD_DIM:]
    a_qk = _bmm_nt(qn, kn) * decay
    state = s_ref[...]
    v_new = u - _bmm(w, state)
    o = _bmm(qn * eg, state) + _bmm(a_qk, v_new)
    s_ref[...] = state * jnp.exp(g_last) + _bmm_tn(kn * jnp.exp(g_last - gc), v_new)
    z = _head_stack(z_ref)
    y = (_rmsnorm(o, ng_ref[...]) * (z * _sigmoid(z))).astype(o_ref.dtype)
    for h in range(HEADS):
        o_ref[:, h * HEAD_DIM:(h + 1) * HEAD_DIM] = y[h]


def _gdn(qkv, z_src, col_small, row_g, norm_g, batch, seq, *, z_col):
    T = qkv.shape[0]
    nc = seq // CHUNK
    width = HEADS * HEAD_DIM
    row = lambda b, c: b * nc + c
    return pl.pallas_call(
        _gdn_kernel,
        grid=(batch, nc),
        in_specs=[
            pl.BlockSpec((CHUNK, width), lambda b, c: (row(b, c), 0)),
            pl.BlockSpec((CHUNK, width), lambda b, c: (row(b, c), 1)),
            pl.BlockSpec((CHUNK, width), lambda b, c: (row(b, c), 2)),
            pl.BlockSpec((CHUNK, width), lambda b, c: (row(b, c), z_col)),
            pl.BlockSpec((CHUNK, LANES), lambda b, c: (row(b, c), 0)),
            pl.BlockSpec((None, HEADS, CHUNK), lambda b, c: (row(b, c), 0, 0)),
            pl.BlockSpec((1, HEAD_DIM), lambda b, c: (0, 0)),
        ],
        out_specs=pl.BlockSpec((CHUNK, width), lambda b, c: (row(b, c), 0)),
        out_shape=jax.ShapeDtypeStruct((T, width), BF16),
        scratch_shapes=[pltpu.VMEM((HEADS, HEAD_DIM, HEAD_DIM), F32)],
        compiler_params=_params("parallel", "arbitrary"),
    )(qkv, qkv, qkv, z_src, col_small, row_g, norm_g)


def _mlstm_kernel(q_ref, k_ref, v_ref, og_ref, col_ref, rowi_ref, rowb_ref, ng_ref, o_ref,
                  c_ref, n_ref, m_ref):
    @pl.when(pl.program_id(1) == 0)
    def _():
        c_ref[...] = jnp.zeros_like(c_ref)
        n_ref[...] = jnp.zeros_like(n_ref)
        m_ref[...] = jnp.zeros_like(m_ref)

    L = CHUNK
    r = lax.broadcasted_iota(jnp.int32, (1, L, L), 1)
    c = lax.broadcasted_iota(jnp.int32, (1, L, L), 2)
    tri = c <= r
    col = col_ref[...]
    qk_slices = [slice(h * ML_QK_DIM, (h + 1) * ML_QK_DIM) for h in range(HEADS)]
    q = jnp.stack([q_ref[:, s] for s in qk_slices]).astype(F32)
    k = jnp.stack([k_ref[:, s] for s in qk_slices]).astype(F32)
    v = _head_stack(v_ref)
    ic = _head_cols(col, SM_ML_I)
    bc = _head_cols(col, SM_ML_F)
    ir = rowi_ref[...][:, None, :]
    br = rowb_ref[...][:, None, :]
    dm = jnp.where(tri, bc - br + ir, -jnp.inf)
    dmax = jnp.max(dm, axis=-1, keepdims=True)
    qk = _bmm_nt(q, k)
    inter = bc + m_ref[...]
    m_r = jnp.maximum(inter, dmax)
    w_inter = jnp.exp(inter - m_r)
    sm = jnp.exp(dm - m_r) * qk
    c_state = c_ref[...]
    n_state = n_ref[...]
    num = w_inter * _bmm(q, c_state) + _bmm(sm, v)
    den = (w_inter * jnp.sum(q * n_state, axis=-1, keepdims=True)
           + jnp.sum(sm, axis=-1, keepdims=True))
    h_tilde = num / jnp.maximum(jnp.abs(den), jnp.exp(-m_r))
    b_last = bc[:, L - 1:L, :]
    m_last = m_r[:, L - 1:L, :]
    w_last = w_inter[:, L - 1:L, :]
    wk = jnp.exp(b_last - bc + ic - m_last) * k
    c_ref[...] = w_last * c_state + _bmm_tn(wk, v)
    n_ref[...] = w_last * n_state + jnp.sum(wk, axis=1, keepdims=True)
    m_ref[...] = m_last
    y = _rmsnorm(_sigmoid(_head_stack(og_ref)) * h_tilde, ng_ref[...]).astype(o_ref.dtype)
    for h in range(HEADS):
        o_ref[:, h * HEAD_DIM:(h + 1) * HEAD_DIM] = y[h]


def _mlstm(qkv, og_src, col_small, row_i, row_b, norm_g, batch, seq, *, q_col, k_col, v_col, og_col):
    T = qkv.shape[0]
    nc = seq // CHUNK
    qk_width = HEADS * ML_QK_DIM
    width = HEADS * HEAD_DIM
    row = lambda b, c: b * nc + c
    return pl.pallas_call(
        _mlstm_kernel,
        grid=(batch, nc),
        in_specs=[
            pl.BlockSpec((CHUNK, qk_width), lambda b, c: (row(b, c), q_col)),
            pl.BlockSpec((CHUNK, qk_width), lambda b, c: (row(b, c), k_col)),
            pl.BlockSpec((CHUNK, width), lambda b, c: (row(b, c), v_col)),
            pl.BlockSpec((CHUNK, width), lambda b, c: (row(b, c), og_col)),
            pl.BlockSpec((CHUNK, LANES), lambda b, c: (row(b, c), 0)),
            pl.BlockSpec((None, HEADS, CHUNK), lambda b, c: (row(b, c), 0, 0)),
            pl.BlockSpec((None, HEADS, CHUNK), lambda b, c: (row(b, c), 0, 0)),
            pl.BlockSpec((1, HEAD_DIM), lambda b, c: (0, 0)),
        ],
        out_specs=pl.BlockSpec((CHUNK, width), lambda b, c: (row(b, c), 0)),
        out_shape=jax.ShapeDtypeStruct((T, width), BF16),
        scratch_shapes=[pltpu.VMEM((HEADS, ML_QK_DIM, HEAD_DIM), F32),
                        pltpu.VMEM((HEADS, 1, ML_QK_DIM), F32),
                        pltpu.VMEM((HEADS, 1, 1), F32)],
        compiler_params=_params("parallel", "arbitrary"),
    )(qkv, qkv, qkv, og_src, col_small, row_i, row_b, norm_g)


def _merge_kernel(yf_ref, yg_ref, ym_ref, wf_ref, wg_ref, wm_ref, g0_ref, g1_ref, g2_ref,
                  b0_ref, b1_ref, b2_ref, o_ref):
    def branch(y_ref, w_ref, g_ref, b_ref):
        return _sigmoid(g_ref[...] + b_ref[...]) * jnp.dot(y_ref[...], w_ref[...],
                                                          preferred_element_type=F32)

    y = (branch(yf_ref, wf_ref, g0_ref, b0_ref) + branch(yg_ref, wg_ref, g1_ref, b1_ref)
         + branch(ym_ref, wm_ref, g2_ref, b2_ref))
    o_ref[...] = y.astype(o_ref.dtype)


def _merge(y_fox, y_gdn, y_ml, w_fox, w_gdn, w_ml, gate_src, gate_bias, d_model, *, gate_col,
           tm=1024, tn=512):
    T, K = y_fox.shape
    nj = d_model // tn
    y_spec = pl.BlockSpec((tm, K), lambda i, j: (i, 0))
    w_spec = pl.BlockSpec((K, tn), lambda i, j: (0, j))
    gate_specs = [pl.BlockSpec((tm, tn), lambda i, j, n=n: (i, gate_col + n * nj + j))
                  for n in range(N_BRANCH)]
    bias_specs = [pl.BlockSpec((1, tn), lambda i, j, n=n: (0, n * nj + j)) for n in range(N_BRANCH)]
    return pl.pallas_call(
        _merge_kernel,
        grid=(T // tm, nj),
        in_specs=[y_spec, y_spec, y_spec, w_spec, w_spec, w_spec] + gate_specs + bias_specs,
        out_specs=pl.BlockSpec((tm, tn), lambda i, j: (i, j)),
        out_shape=jax.ShapeDtypeStruct((T, d_model), BF16),
        compiler_params=_params("parallel", "parallel"),
    )(y_fox, y_gdn, y_ml, w_fox, w_gdn, w_ml, gate_src, gate_src, gate_src,
      gate_bias, gate_bias, gate_bias)


def _matmul_residual_kernel(a_ref, w_ref, x_ref, o_ref):
    o_ref[...] = x_ref[...] + jnp.dot(a_ref[...], w_ref[...], preferred_element_type=F32)


def _matmul_residual(a, w, x, tm=512, tn=512):
    T, K = a.shape
    N = w.shape[1]
    return pl.pallas_call(
        _matmul_residual_kernel,
        grid=(T // tm, N // tn),
        in_specs=[
            pl.BlockSpec((tm, K), lambda i, j: (i, 0)),
            pl.BlockSpec((K, tn), lambda i, j: (0, j)),
            pl.BlockSpec((tm, tn), lambda i, j: (i, j)),
        ],
        out_specs=pl.BlockSpec((tm, tn), lambda i, j: (i, j)),
        out_shape=jax.ShapeDtypeStruct((T, N), F32),
        compiler_params=_params("parallel", "parallel"),
    )(a, w, x)


def _final_norm_kernel(x_ref, g_ref, o_ref):
    o_ref[...] = _rmsnorm(x_ref[...], g_ref[...])


def _final_norm(x, g, tm=512):
    T, D = x.shape
    return pl.pallas_call(
        _final_norm_kernel,
        grid=(T // tm,),
        in_specs=[pl.BlockSpec((tm, D), lambda i: (i, 0)), pl.BlockSpec((1, D), lambda i: (0, 0))],
        out_specs=pl.BlockSpec((tm, D), lambda i: (i, 0)),
        out_shape=jax.ShapeDtypeStruct((T, D), F32),
        compiler_params=_params("parallel"),
    )(x, g)


def _pad_lanes(v):
    return jnp.pad(v, (0, LANES - v.shape[0])).reshape(1, LANES)


def _chunk_rows(col_small, start, batch, seq):
    v = col_small[:, start:start + HEADS].reshape(batch * (seq // CHUNK), CHUNK, HEADS)
    return v.transpose(0, 2, 1)


def _layer(x, p, batch, seq):
    T, D = x.shape
    fox_w = HEADS * HEAD_DIM
    gdn_w = HEADS * HEAD_DIM
    ml_qk_w = HEADS * ML_QK_DIM
    ml_v_w = HEADS * HEAD_DIM
    widths = [fox_w, fox_w, fox_w, HEADS, 3 * gdn_w, gdn_w, HEADS, HEADS,
              ml_qk_w, ml_qk_w, ml_v_w, HEADS, HEADS, ml_v_w, N_BRANCH * D]
    names = ["fox_q", "fox_k", "fox_v", "fox_f", "gdn_qkv", "gdn_z", "gdn_b", "gdn_a",
             "ml_q", "ml_k", "ml_v", "ml_i", "ml_f", "ml_o", "gate"]
    off = {}
    o = 0
    for n, w in zip(names, widths):
        off[n] = (o, o + w)
        o += w
    w_in = p["w_in"].astype(BF16)
    cols = lambda a, b: w_in[:, off[a][0]:off[b][1]]
    g_mix = p["norm_mix_g"].reshape(1, D)

    w_a = jnp.concatenate([cols("fox_q", "fox_v"), cols("ml_q", "ml_v")], axis=1)
    scale_a = jnp.concatenate([
        jnp.full((fox_w,), HEAD_DIM ** -0.5 * LOG2E, F32), jnp.ones((2 * fox_w,), F32),
        jnp.full((ml_qk_w,), ML_QK_DIM ** -0.5, F32), jnp.ones((ml_qk_w + ml_v_w,), F32)]).reshape(1, -1)
    proj_a = _norm_matmul(x, g_mix, w_a, scale_a, BF16)
    w_b = jnp.concatenate([cols("gdn_z", "gdn_z"), cols("ml_o", "ml_o"), cols("gate", "gate")], axis=1)
    proj_b = _norm_matmul(x, g_mix, w_b, jnp.ones((1, w_b.shape[1]), F32), F32)
    w_s = jnp.concatenate([cols("fox_f", "fox_f"), cols("gdn_b", "gdn_a"), cols("ml_i", "ml_f")], axis=1)
    w_s = jnp.pad(w_s, ((0, 0), (0, LANES - w_s.shape[1])))
    zeros8 = jnp.zeros((HEADS,), F32)
    bias_s = _pad_lanes(jnp.concatenate([p["fox_f_bias"], zeros8, p["gdn_dt_bias"],
                                         p["ml_i_bias"], p["ml_f_bias"]]))
    alog_s = _pad_lanes(jnp.concatenate([zeros8, zeros8, p["gdn_a_log"]]))
    col_small = _small_gates(x, g_mix, w_s, bias_s, alog_s, seq)
    row_f = col_small[:, SM_FOX_F:SM_FOX_F + HEADS].reshape(batch, seq, HEADS)
    row_f = row_f.transpose(0, 2, 1)
    row_g = _chunk_rows(col_small, SM_GDN_A, batch, seq)
    row_i = _chunk_rows(col_small, SM_ML_I, batch, seq)
    row_b = _chunk_rows(col_small, SM_ML_F, batch, seq)
    gdn_qkv = _proj_conv(x, g_mix, cols("gdn_qkv", "gdn_qkv"), p["gdn_conv_w"], seq, gated=False)

    hps = 4
    y_fox = _fox_attention(proj_a, row_f, batch, seq, q_col=0, k_col=HEADS // hps,
                           v_col=2 * HEADS // hps, hps=hps)
    y_gdn = _gdn(gdn_qkv, proj_b, col_small, row_g, p["gdn_norm_g"].reshape(1, HEAD_DIM),
                 batch, seq, z_col=0)
    ml_base = 3 * fox_w
    y_ml = _mlstm(proj_a, proj_b, col_small, row_i, row_b, p["ml_norm_g"].reshape(1, HEAD_DIM),
                  batch, seq, q_col=ml_base // ml_qk_w, k_col=ml_base // ml_qk_w + 1,
                  v_col=(ml_base + 2 * ml_qk_w) // ml_v_w, og_col=1)

    tn = 512
    y = _merge(y_fox, y_gdn, y_ml, p["w_fox_proj"].astype(BF16), p["w_gdn_proj"].astype(BF16),
               p["w_ml_proj"].astype(BF16), proj_b, p["gate_bias"].reshape(1, -1), D,
               gate_col=(gdn_w + ml_v_w) // tn, tn=tn)
    x = _matmul_residual(y, p["w_out"].astype(BF16), x, tm=1024)

    act = _proj_conv(x, p["norm_ffn_g"].reshape(1, D), p["w_up"].astype(BF16), p["ffn_conv_w"], seq,
                     gated=True)
    return _matmul_residual(act, p["w_down"].astype(BF16), x)


def kernel(x, norm_mix_g, w_in, fox_f_bias, gdn_conv_w, gdn_a_log, gdn_dt_bias, gdn_norm_g,
           ml_i_bias, ml_f_bias, ml_norm_g, gate_bias, w_fox_proj, w_gdn_proj, w_ml_proj,
           w_out, norm_ffn_g, w_up, ffn_conv_w, w_down, norm_final_g):
    batch, seq, d_model = x.shape
    stacked = dict(norm_mix_g=norm_mix_g, w_in=w_in, fox_f_bias=fox_f_bias, gdn_conv_w=gdn_conv_w,
                   gdn_a_log=gdn_a_log, gdn_dt_bias=gdn_dt_bias, gdn_norm_g=gdn_norm_g,
                   ml_i_bias=ml_i_bias, ml_f_bias=ml_f_bias, ml_norm_g=ml_norm_g,
                   gate_bias=gate_bias, w_fox_proj=w_fox_proj, w_gdn_proj=w_gdn_proj,
                   w_ml_proj=w_ml_proj, w_out=w_out, norm_ffn_g=norm_ffn_g, w_up=w_up,
                   ffn_conv_w=ffn_conv_w, w_down=w_down)

    def body(h, p):
        return _layer(h, p, batch, seq), None

    h, _ = lax.scan(body, x.reshape(batch * seq, d_model), stacked)
    return _final_norm(h, norm_final_g.reshape(1, d_model)).reshape(batch, seq, d_model)
```

```python
import functools

import jax
import jax.numpy as jnp
from jax import lax
from jax.experimental import pallas as pl
from jax.experimental.pallas import tpu as pltpu

F32 = jnp.float32
BF16 = jnp.bfloat16
HIGHEST = lax.Precision.HIGHEST

EPS = 1e-6
HEADS = 8
HEAD_DIM = 128
ML_QK_DIM = 64
CHUNK = 64
CHUNK_SHIFT = CHUNK.bit_length() - 1
GDN_CONV = 4
FFN_CONV = 3
N_BRANCH = 3
LANES = 128
MXU_WIDTH = 256
HALO = 16

SM_FOX_F, SM_GDN_B, SM_GDN_A, SM_ML_I, SM_ML_F = 0, 8, 16, 24, 32

V7X_VMEM_LIMIT_BYTES = 48 * 1024 * 1024


def _params(*semantics):
    return pltpu.CompilerParams(dimension_semantics=semantics,
                                vmem_limit_bytes=V7X_VMEM_LIMIT_BYTES)


def _rmsnorm(x, g):
    return x * lax.rsqrt(jnp.mean(x * x, axis=-1, keepdims=True) + EPS) * g


def _sigmoid(x):
    return 1.0 / (1.0 + jnp.exp(-x))


def _mm_hi(a, b):
    return jnp.dot(a, b, precision=HIGHEST, preferred_element_type=F32)


def _norm_matmul_kernel(x_ref, g_ref, w_ref, cs_ref, o_ref, xn_ref):
    @pl.when(pl.program_id(1) == 0)
    def _():
        xn_ref[...] = _rmsnorm(x_ref[...], g_ref[...]).astype(BF16)

    acc = jnp.dot(xn_ref[...], w_ref[...], preferred_element_type=F32)
    o_ref[...] = (acc * cs_ref[...]).astype(o_ref.dtype)


def _norm_matmul(x, g, w, colscale, out_dtype, tm=1024, tn=512):
    T, D = x.shape
    N = w.shape[1]
    return pl.pallas_call(
        _norm_matmul_kernel,
        grid=(T // tm, N // tn),
        in_specs=[
            pl.BlockSpec((tm, D), lambda i, j: (i, 0)),
            pl.BlockSpec((1, D), lambda i, j: (0, 0)),
            pl.BlockSpec((D, tn), lambda i, j: (0, j)),
            pl.BlockSpec((1, tn), lambda i, j: (0, j)),
        ],
        out_specs=pl.BlockSpec((tm, tn), lambda i, j: (i, j)),
        out_shape=jax.ShapeDtypeStruct((T, N), out_dtype),
        scratch_shapes=[pltpu.VMEM((tm, D), BF16)],
        compiler_params=_params("parallel", "arbitrary"),
    )(x, g, w, colscale)


def _small_kernel(x_ref, g_ref, w_ref, bias_ref, alog_ref, o_ref, carry_ref, *, blocks_per_seq):
    i = pl.program_id(0)
    tm = x_ref.shape[0]

    @pl.when(i % blocks_per_seq == 0)
    def _():
        carry_ref[...] = jnp.zeros_like(carry_ref)

    xn = _rmsnorm(x_ref[...], g_ref[...]).astype(BF16)
    t = jnp.dot(xn, w_ref[...], preferred_element_type=F32) + bias_ref[...]
    lane = lax.broadcasted_iota(jnp.int32, (tm, LANES), 1)
    e = jnp.log1p(jnp.exp(-jnp.abs(t)))
    logsig = jnp.minimum(t, 0.0) - e
    softplus = jnp.maximum(t, 0.0) + e
    g_decay = -jnp.exp(alog_ref[...]) * softplus
    is_logsig = (lane < SM_GDN_B) | ((lane >= SM_ML_F) & (lane < SM_ML_F + HEADS))
    val = jnp.where(is_logsig, logsig,
                    jnp.where(lane < SM_GDN_A, _sigmoid(t),
                              jnp.where(lane < SM_ML_I, g_decay, t)))

    r = lax.broadcasted_iota(jnp.int32, (tm, tm), 0)
    c = lax.broadcasted_iota(jnp.int32, (tm, tm), 1)
    tri = c <= r
    same_chunk = (r >> CHUNK_SHIFT) == (c >> CHUNK_SHIFT)
    tri_full = jnp.where(tri, 1.0, 0.0).astype(F32)
    tri_chunk = jnp.where(tri & same_chunk, 1.0, 0.0).astype(F32)
    cs_full = _mm_hi(tri_full, val) + carry_ref[...]
    cs_chunk = _mm_hi(tri_chunk, val)
    carry_ref[...] = cs_full[tm - 1:tm, :]

    is_chunk_cs = ((lane >= SM_GDN_A) & (lane < SM_ML_I)) | ((lane >= SM_ML_F) & (lane < SM_ML_F + HEADS))
    o_ref[...] = jnp.where(lane < SM_GDN_B, cs_full, jnp.where(is_chunk_cs, cs_chunk, val))


def _small_gates(x, g, w, bias, alog, seq, tm=512):
    T, D = x.shape
    return pl.pallas_call(
        functools.partial(_small_kernel, blocks_per_seq=seq // tm),
        grid=(T // tm,),
        in_specs=[
            pl.BlockSpec((tm, D), lambda i: (i, 0)),
            pl.BlockSpec((1, D), lambda i: (0, 0)),
            pl.BlockSpec((D, LANES), lambda i: (0, 0)),
            pl.BlockSpec((1, LANES), lambda i: (0, 0)),
            pl.BlockSpec((1, LANES), lambda i: (0, 0)),
        ],
        out_specs=pl.BlockSpec((tm, LANES), lambda i: (i, 0)),
        out_shape=jax.ShapeDtypeStruct((T, LANES), F32),
        scratch_shapes=[pltpu.VMEM((1, LANES), F32)],
        compiler_params=_params("arbitrary"),
    )(x, g, w, bias, alog)


def _conv_taps(h_ref, c_ref, taps, tm):
    acc = None
    for t in range(taps):
        term = h_ref[pl.ds(HALO - (taps - 1) + t, tm), :] * c_ref[t:t + 1, :]
        acc = term if acc is None else acc + term
    return acc


def _norm_with_halo(x_ref, xh_ref, g_ref, xn_ref, at_seq_start):
    xn_ref[HALO:, :] = _rmsnorm(x_ref[...], g_ref[...]).astype(BF16)
    keep = jnp.where(at_seq_start, 0.0, 1.0)
    xn_ref[:HALO, :] = (_rmsnorm(xh_ref[...], g_ref[...]) * keep).astype(BF16)


def _proj_conv_kernel(x_ref, xh_ref, g_ref, w_ref, c_ref, o_ref, xn_ref, h_ref, *, taps,
                      blocks_per_seq):
    @pl.when(pl.program_id(1) == 0)
    def _():
        _norm_with_halo(x_ref, xh_ref, g_ref, xn_ref, pl.program_id(0) % blocks_per_seq == 0)

    h_ref[...] = jnp.dot(xn_ref[...], w_ref[...], preferred_element_type=F32)
    u = _conv_taps(h_ref, c_ref, taps, x_ref.shape[0])
    o_ref[...] = (u * _sigmoid(u)).astype(o_ref.dtype)


def _proj_conv(x, g, w, conv_w, seq, tm=1024, tn=512):
    T, D = x.shape
    taps = conv_w.shape[0]
    N = w.shape[1]
    halo_blocks = tm // HALO
    return pl.pallas_call(
        functools.partial(_proj_conv_kernel, taps=taps, blocks_per_seq=seq // tm),
        grid=(T // tm, N // tn),
        in_specs=[
            pl.BlockSpec((tm, D), lambda i, j: (i, 0)),
            pl.BlockSpec((HALO, D), lambda i, j: (jnp.maximum(i * halo_blocks - 1, 0), 0)),
            pl.BlockSpec((1, D), lambda i, j: (0, 0)),
            pl.BlockSpec((D, tn), lambda i, j: (0, j)),
            pl.BlockSpec((taps, tn), lambda i, j: (0, j)),
        ],
        out_specs=pl.BlockSpec((tm, tn), lambda i, j: (i, j)),
        out_shape=jax.ShapeDtypeStruct((T, N), BF16),
        scratch_shapes=[pltpu.VMEM((tm + HALO, D), BF16), pltpu.VMEM((tm + HALO, tn), F32)],
        compiler_params=_params("parallel", "arbitrary"),
    )(x, x, g, w, conv_w)


NEG_BIG = -1e30


LOG2E = 1.4426950408889634


def _fox_kernel(q_ref, k_ref, v_ref, f_ref, o_ref, m_ref, acc_ref, *, tq, tk, hps):
    qi = pl.program_id(2)
    ones = jnp.ones((tk, HEAD_DIM), BF16)
    lane_tiles = tk // LANES
    m_ref[...] = jnp.full_like(m_ref, NEG_BIG)
    acc_ref[...] = jnp.zeros_like(acc_ref)

    def block(j, masked):
        start = pl.multiple_of(j * tk, tk)
        heads = [slice(h * HEAD_DIM, (h + 1) * HEAD_DIM) for h in range(hps)]
        scores = []
        for h, hs in enumerate(heads):
            s = lax.dot_general(q_ref[:, hs], k_ref[pl.ds(start, tk), hs],
                                (((1,), (1,)), ((), ())), preferred_element_type=F32)
            s = s - f_ref[h, pl.ds(j, 1), :] * LOG2E
            if masked:
                row = lax.broadcasted_iota(jnp.int32, (tq, tk), 0)
                col = lax.broadcasted_iota(jnp.int32, (tq, tk), 1)
                s = jnp.where(col <= row, s, NEG_BIG)
            scores.append([s[:, c * LANES:(c + 1) * LANES] for c in range(lane_tiles)])
        probs, alphas = [], []
        for h, tiles in enumerate(scores):
            m_prev = m_ref[h]
            m_tile = functools.reduce(jnp.maximum, tiles)
            m_new = jnp.maximum(m_prev, jnp.max(m_tile, axis=-1, keepdims=True))
            m_ref[h] = m_new
            alphas.append(jnp.exp2(m_prev - m_new))
            probs.append(jnp.concatenate([jnp.exp2(t - m_new).astype(BF16) for t in tiles], axis=1))
        for h, hs in enumerate(heads):
            v_ext = jnp.concatenate([v_ref[pl.ds(start, tk), hs], ones], axis=1)
            pv = jnp.dot(probs[h], v_ext, preferred_element_type=F32)
            acc_ref[h] = jnp.concatenate([alphas[h], alphas[h]], axis=1) * acc_ref[h] + pv

    def full_block(j, carry):
        block(j, masked=False)
        return carry

    lax.fori_loop(0, qi, full_block, 0)
    block(qi, masked=True)
    for h in range(hps):
        acc = acc_ref[h]
        o_ref[:, h * HEAD_DIM:(h + 1) * HEAD_DIM] = (
            acc[:, :HEAD_DIM] / acc[:, HEAD_DIM:]).astype(o_ref.dtype)


def _fox_attention(qkv, f_cum, batch, seq, *, q_col, k_col, v_col, tq=512, hps=2):
    T = qkv.shape[0]
    nq = seq // tq
    width = hps * HEAD_DIM
    f_blocks = f_cum.reshape(batch, HEADS, nq, tq)
    return pl.pallas_call(
        functools.partial(_fox_kernel, tq=tq, tk=tq, hps=hps),
        grid=(batch, HEADS // hps, nq),
        in_specs=[
            pl.BlockSpec((tq, width), lambda b, h, qi: (b * nq + qi, q_col + h)),
            pl.BlockSpec((seq, width), lambda b, h, qi: (b, k_col + h)),
            pl.BlockSpec((seq, width), lambda b, h, qi: (b, v_col + h)),
            pl.BlockSpec((None, hps, nq, tq), lambda b, h, qi: (b, h, 0, 0)),
        ],
        out_specs=pl.BlockSpec((tq, width), lambda b, h, qi: (b * nq + qi, h)),
        out_shape=jax.ShapeDtypeStruct((T, HEADS * HEAD_DIM), BF16),
        scratch_shapes=[pltpu.VMEM((hps, tq, LANES), F32),
                        pltpu.VMEM((hps, tq, 2 * HEAD_DIM), F32)],
        compiler_params=_params("parallel", "parallel", "arbitrary"),
    )(qkv, qkv, qkv, f_blocks)


def _bmm(a, b):
    return jnp.einsum("gmk,gkn->gmn", a.astype(BF16), b.astype(BF16), preferred_element_type=F32)


def _bmm_nt(a, b):
    return jnp.einsum("gmk,gnk->gmn", a.astype(BF16), b.astype(BF16), preferred_element_type=F32)


def _bmm_tn(a, b):
    return jnp.einsum("gkm,gkn->gmn", a.astype(BF16), b.astype(BF16), preferred_element_type=F32)


def _unit_lower_inverse(m, eye, r, c):
    n = m.shape[-1]
    same_pair = (r >> 1) == (c >> 1)
    x = eye - jnp.where(same_pair, m, 0.0)
    b = 2
    while b < n:
        shift = b.bit_length() - 1
        joins = ((r >> (shift + 1)) == (c >> (shift + 1))) & ((r >> shift) != (c >> shift))
        x = x - _bmm(_bmm(x, jnp.where(joins, m, 0.0)), x)
        b *= 2
    return x


def _head_stack(ref):
    return jnp.stack([ref[:, h * HEAD_DIM:(h + 1) * HEAD_DIM] for h in range(HEADS)])


def _head_cols(col, start):
    return jnp.stack([col[:, start + h:start + h + 1] for h in range(HEADS)])


def _gdn_kernel(q_ref, k_ref, v_ref, z_ref, col_ref, rowg_ref, ng_ref, o_ref, s_ref):
    @pl.when(pl.program_id(1) == 0)
    def _():
        s_ref[...] = jnp.zeros_like(s_ref)

    L = CHUNK
    r = lax.broadcasted_iota(jnp.int32, (1, L, L), 1)
    c = lax.broadcasted_iota(jnp.int32, (1, L, L), 2)
    tri = c <= r
    eye = jnp.where(c == r, 1.0, 0.0).astype(F32)
    col = col_ref[...]
    q = _head_stack(q_ref).astype(F32)
    k = _head_stack(k_ref).astype(F32)
    v = _head_stack(v_ref).astype(F32)
    qn = q * lax.rsqrt(jnp.sum(q * q, axis=-1, keepdims=True) + EPS) * (HEAD_DIM ** -0.5)
    kn = k * lax.rsqrt(jnp.sum(k * k, axis=-1, keepdims=True) + EPS)
    beta = _head_cols(col, SM_GDN_B)
    gc = _head_cols(col, SM_GDN_A)
    gr = rowg_ref[...][:, None, :]
    g_last = gc[:, L - 1:L, :]
    decay = jnp.where(tri, jnp.exp(jnp.where(tri, gc - gr, 0.0)), 0.0)
    eg = jnp.exp(gc)
    kb = kn * beta
    m = jnp.where(c < r, _bmm_nt(kb, kn) * decay, 0.0)
    t_inv = _unit_lower_inverse(m, eye, r, c)
    x = _bmm(t_inv, jnp.concatenate([v * beta, kb * eg], axis=-1))
    u = x[:, :, :HEAD_DIM]
    w = x[:, :, HEAD_DIM:]
    a_qk = _bmm_nt(qn, kn) * decay
    state = s_ref[...]
    v_new = u - _bmm(w, state)
    o = _bmm(qn * eg, state) + _bmm(a_qk, v_new)
    s_ref[...] = state * jnp.exp(g_last) + _bmm_tn(kn * jnp.exp(g_last - gc), v_new)
    z = _head_stack(z_ref).astype(F32)
    y = (_rmsnorm(o, ng_ref[...]) * (z * _sigmoid(z))).astype(o_ref.dtype)
    for h in range(HEADS):
        o_ref[:, h * HEAD_DIM:(h + 1) * HEAD_DIM] = y[h]


def _gdn(qkv, z_src, col_small, row_g, norm_g, batch, seq, *, z_col):
    T = qkv.shape[0]
    nc = seq // CHUNK
    width = HEADS * HEAD_DIM
    row = lambda b, c: b * nc + c
    return pl.pallas_call(
        _gdn_kernel,
        grid=(batch, nc),
        in_specs=[
            pl.BlockSpec((CHUNK, width), lambda b, c: (row(b, c), 0)),
            pl.BlockSpec((CHUNK, width), lambda b, c: (row(b, c), 1)),
            pl.BlockSpec((CHUNK, width), lambda b, c: (row(b, c), 2)),
            pl.BlockSpec((CHUNK, width), lambda b, c: (row(b, c), z_col)),
            pl.BlockSpec((CHUNK, LANES), lambda b, c: (row(b, c), 0)),
            pl.BlockSpec((None, HEADS, CHUNK), lambda b, c: (row(b, c), 0, 0)),
            pl.BlockSpec((1, HEAD_DIM), lambda b, c: (0, 0)),
        ],
        out_specs=pl.BlockSpec((CHUNK, width), lambda b, c: (row(b, c), 0)),
        out_shape=jax.ShapeDtypeStruct((T, width), BF16),
        scratch_shapes=[pltpu.VMEM((HEADS, HEAD_DIM, HEAD_DIM), F32)],
        compiler_params=_params("parallel", "arbitrary"),
    )(qkv, qkv, qkv, z_src, col_small, row_g, norm_g)


def _mlstm_kernel(q_ref, k_ref, v_ref, og_ref, col_ref, rowi_ref, rowb_ref, ng_ref, o_ref,
                  c_ref, n_ref, m_ref):
    @pl.when(pl.program_id(1) == 0)
    def _():
        c_ref[...] = jnp.zeros_like(c_ref)
        n_ref[...] = jnp.zeros_like(n_ref)
        m_ref[...] = jnp.zeros_like(m_ref)

    L = CHUNK
    r = lax.broadcasted_iota(jnp.int32, (1, L, L), 1)
    c = lax.broadcasted_iota(jnp.int32, (1, L, L), 2)
    tri = c <= r
    col = col_ref[...]
    qk_slices = [slice(h * ML_QK_DIM, (h + 1) * ML_QK_DIM) for h in range(HEADS)]
    q = jnp.stack([q_ref[:, s] for s in qk_slices]).astype(F32)
    k = jnp.stack([k_ref[:, s] for s in qk_slices]).astype(F32)
    v = _head_stack(v_ref)
    ic = _head_cols(col, SM_ML_I)
    bc = _head_cols(col, SM_ML_F)
    ir = rowi_ref[...][:, None, :]
    br = rowb_ref[...][:, None, :]
    dm = jnp.where(tri, bc - br + ir, -jnp.inf)
    dmax = jnp.max(dm, axis=-1, keepdims=True)
    qk = _bmm_nt(q, k)
    inter = bc + m_ref[...]
    m_r = jnp.maximum(inter, dmax)
    w_inter = jnp.exp(inter - m_r)
    sm = jnp.exp(dm - m_r) * qk
    c_state = c_ref[...]
    n_state = n_ref[...]
    num = w_inter * _bmm(q, c_state) + _bmm(sm, v)
    den = (w_inter * jnp.sum(q * n_state, axis=-1, keepdims=True)
           + jnp.sum(sm, axis=-1, keepdims=True))
    h_tilde = num / jnp.maximum(jnp.abs(den), jnp.exp(-m_r))
    b_last = bc[:, L - 1:L, :]
    m_last = m_r[:, L - 1:L, :]
    w_last = w_inter[:, L - 1:L, :]
    wk = jnp.exp(b_last - bc + ic - m_last) * k
    c_ref[...] = w_last * c_state + _bmm_tn(wk, v)
    n_ref[...] = w_last * n_state + jnp.sum(wk, axis=1, keepdims=True)
    m_ref[...] = m_last
    og = _head_stack(og_ref).astype(F32)
    y = _rmsnorm(_sigmoid(og) * h_tilde, ng_ref[...]).astype(o_ref.dtype)
    for h in range(HEADS):
        o_ref[:, h * HEAD_DIM:(h + 1) * HEAD_DIM] = y[h]


def _mlstm(qkv, og_src, col_small, row_i, row_b, norm_g, batch, seq, *, q_col, k_col, v_col, og_col):
    T = qkv.shape[0]
    nc = seq // CHUNK
    qk_width = HEADS * ML_QK_DIM
    width = HEADS * HEAD_DIM
    row = lambda b, c: b * nc + c
    return pl.pallas_call(
        _mlstm_kernel,
        grid=(batch, nc),
        in_specs=[
            pl.BlockSpec((CHUNK, qk_width), lambda b, c: (row(b, c), q_col)),
            pl.BlockSpec((CHUNK, qk_width), lambda b, c: (row(b, c), k_col)),
            pl.BlockSpec((CHUNK, width), lambda b, c: (row(b, c), v_col)),
            pl.BlockSpec((CHUNK, width), lambda b, c: (row(b, c), og_col)),
            pl.BlockSpec((CHUNK, LANES), lambda b, c: (row(b, c), 0)),
            pl.BlockSpec((None, HEADS, CHUNK), lambda b, c: (row(b, c), 0, 0)),
            pl.BlockSpec((None, HEADS, CHUNK), lambda b, c: (row(b, c), 0, 0)),
            pl.BlockSpec((1, HEAD_DIM), lambda b, c: (0, 0)),
        ],
        out_specs=pl.BlockSpec((CHUNK, width), lambda b, c: (row(b, c), 0)),
        out_shape=jax.ShapeDtypeStruct((T, width), BF16),
        scratch_shapes=[pltpu.VMEM((HEADS, ML_QK_DIM, HEAD_DIM), F32),
                        pltpu.VMEM((HEADS, 1, ML_QK_DIM), F32),
                        pltpu.VMEM((HEADS, 1, 1), F32)],
        compiler_params=_params("parallel", "arbitrary"),
    )(qkv, qkv, qkv, og_src, col_small, row_i, row_b, norm_g)


def _merge_kernel(yf_ref, yg_ref, ym_ref, wf_ref, wg_ref, wm_ref, g0_ref, g1_ref, g2_ref,
                  b0_ref, b1_ref, b2_ref, o_ref):
    def branch(y_ref, w_ref, g_ref, b_ref):
        return _sigmoid(g_ref[...] + b_ref[...]) * jnp.dot(y_ref[...], w_ref[...],
                                                          preferred_element_type=F32)

    y = (branch(yf_ref, wf_ref, g0_ref, b0_ref) + branch(yg_ref, wg_ref, g1_ref, b1_ref)
         + branch(ym_ref, wm_ref, g2_ref, b2_ref))
    o_ref[...] = y.astype(o_ref.dtype)


def _merge(y_fox, y_gdn, y_ml, w_fox, w_gdn, w_ml, gate_src, gate_bias, d_model, *, gate_col,
           tm=1024, tn=512):
    T, K = y_fox.shape
    nj = d_model // tn
    y_spec = pl.BlockSpec((tm, K), lambda i, j: (i, 0))
    w_spec = pl.BlockSpec((K, tn), lambda i, j: (0, j))
    gate_specs = [pl.BlockSpec((tm, tn), lambda i, j, n=n: (i, gate_col + n * nj + j))
                  for n in range(N_BRANCH)]
    bias_specs = [pl.BlockSpec((1, tn), lambda i, j, n=n: (0, n * nj + j)) for n in range(N_BRANCH)]
    return pl.pallas_call(
        _merge_kernel,
        grid=(T // tm, nj),
        in_specs=[y_spec, y_spec, y_spec, w_spec, w_spec, w_spec] + gate_specs + bias_specs,
        out_specs=pl.BlockSpec((tm, tn), lambda i, j: (i, j)),
        out_shape=jax.ShapeDtypeStruct((T, d_model), BF16),
        compiler_params=_params("parallel", "parallel"),
    )(y_fox, y_gdn, y_ml, w_fox, w_gdn, w_ml, gate_src, gate_src, gate_src,
      gate_bias, gate_bias, gate_bias)


def _matmul_residual_kernel(a_ref, w_ref, x_ref, o_ref):
    o_ref[...] = x_ref[...] + jnp.dot(a_ref[...], w_ref[...], preferred_element_type=F32)


def _matmul_residual(a, w, x, tm=512, tn=512):
    T, K = a.shape
    N = w.shape[1]
    return pl.pallas_call(
        _matmul_residual_kernel,
        grid=(T // tm, N // tn),
        in_specs=[
            pl.BlockSpec((tm, K), lambda i, j: (i, 0)),
            pl.BlockSpec((K, tn), lambda i, j: (0, j)),
            pl.BlockSpec((tm, tn), lambda i, j: (i, j)),
        ],
        out_specs=pl.BlockSpec((tm, tn), lambda i, j: (i, j)),
        out_shape=jax.ShapeDtypeStruct((T, N), F32),
        compiler_params=_params("parallel", "parallel"),
    )(a, w, x)


def _final_norm_kernel(x_ref, g_ref, o_ref):
    o_ref[...] = _rmsnorm(x_ref[...], g_ref[...])


def _final_norm(x, g, tm=512):
    T, D = x.shape
    return pl.pallas_call(
        _final_norm_kernel,
        grid=(T // tm,),
        in_specs=[pl.BlockSpec((tm, D), lambda i: (i, 0)), pl.BlockSpec((1, D), lambda i: (0, 0))],
        out_specs=pl.BlockSpec((tm, D), lambda i: (i, 0)),
        out_shape=jax.ShapeDtypeStruct((T, D), F32),
        compiler_params=_params("parallel"),
    )(x, g)


def _ffn_kernel(x_ref, xh_ref, g_ref, wu_ref, wg_ref, cu_ref, cg_ref, wd_ref, o_ref, xn_ref, *,
                taps, blocks_per_seq):
    tm = x_ref.shape[0]
    tn = wu_ref.shape[1]

    @pl.when(pl.program_id(1) == 0)
    def _():
        _norm_with_halo(x_ref, xh_ref, g_ref, xn_ref, pl.program_id(0) % blocks_per_seq == 0)
        o_ref[...] = x_ref[...]

    def conv(h, c_ref, cs):
        acc = None
        for t in range(taps):
            start = HALO - (taps - 1) + t
            term = h[start:start + tm, :] * c_ref[t:t + 1, cs]
            acc = term if acc is None else acc + term
        return acc

    xn = xn_ref[...]
    groups = [slice(c * MXU_WIDTH, (c + 1) * MXU_WIDTH) for c in range(tn // MXU_WIDTH)]
    hidden = [(jnp.dot(xn, wu_ref[:, cs], preferred_element_type=F32),
               jnp.dot(xn, wg_ref[:, cs], preferred_element_type=F32)) for cs in groups]
    acts = []
    for cs, (hu, hg) in zip(groups, hidden):
        gt = conv(hg, cg_ref, cs)
        acts.append((gt * _sigmoid(gt) * conv(hu, cu_ref, cs)).astype(BF16))
    o_ref[...] += jnp.dot(jnp.concatenate(acts, axis=1), wd_ref[...], preferred_element_type=F32)


def _ffn(x, g, w_up, conv_w, w_down, seq, tm=512, tn=512):
    T, D = x.shape
    taps = conv_w.shape[0]
    nj = w_down.shape[0] // tn
    halo_blocks = tm // HALO
    return pl.pallas_call(
        functools.partial(_ffn_kernel, taps=taps, blocks_per_seq=seq // tm),
        grid=(T // tm, nj),
        in_specs=[
            pl.BlockSpec((tm, D), lambda i, j: (i, 0)),
            pl.BlockSpec((HALO, D), lambda i, j: (jnp.maximum(i * halo_blocks - 1, 0), 0)),
            pl.BlockSpec((1, D), lambda i, j: (0, 0)),
            pl.BlockSpec((D, tn), lambda i, j: (0, j)),
            pl.BlockSpec((D, tn), lambda i, j: (0, nj + j)),
            pl.BlockSpec((taps, tn), lambda i, j: (0, j)),
            pl.BlockSpec((taps, tn), lambda i, j: (0, nj + j)),
            pl.BlockSpec((tn, D), lambda i, j: (j, 0)),
        ],
        out_specs=pl.BlockSpec((tm, D), lambda i, j: (i, 0)),
        out_shape=jax.ShapeDtypeStruct((T, D), F32),
        scratch_shapes=[pltpu.VMEM((tm + HALO, D), BF16)],
        compiler_params=_params("parallel", "arbitrary"),
    )(x, x, g, w_up, w_up, conv_w, conv_w, w_down)


def _pad_lanes(v):
    return jnp.pad(v, (0, LANES - v.shape[0])).reshape(1, LANES)


def _chunk_rows(col_small, start, batch, seq):
    v = col_small[:, start:start + HEADS].reshape(batch * (seq // CHUNK), CHUNK, HEADS)
    return v.transpose(0, 2, 1)


def _layer(x, p, batch, seq):
    T, D = x.shape
    fox_w = HEADS * HEAD_DIM
    gdn_w = HEADS * HEAD_DIM
    ml_qk_w = HEADS * ML_QK_DIM
    ml_v_w = HEADS * HEAD_DIM
    widths = [fox_w, fox_w, fox_w, HEADS, 3 * gdn_w, gdn_w, HEADS, HEADS,
              ml_qk_w, ml_qk_w, ml_v_w, HEADS, HEADS, ml_v_w, N_BRANCH * D]
    names = ["fox_q", "fox_k", "fox_v", "fox_f", "gdn_qkv", "gdn_z", "gdn_b", "gdn_a",
             "ml_q", "ml_k", "ml_v", "ml_i", "ml_f", "ml_o", "gate"]
    off = {}
    o = 0
    for n, w in zip(names, widths):
        off[n] = (o, o + w)
        o += w
    w_in = p["w_in"].astype(BF16)
    cols = lambda a, b: w_in[:, off[a][0]:off[b][1]]
    g_mix = p["norm_mix_g"].reshape(1, D)

    w_p = jnp.concatenate([cols("fox_q", "fox_v"), cols("ml_q", "ml_v"), cols("gdn_z", "gdn_z"),
                           cols("ml_o", "ml_o"), cols("gate", "gate")], axis=1)
    z_base = 3 * fox_w + 2 * ml_qk_w + ml_v_w
    gate_base = z_base + gdn_w + ml_v_w
    scale_p = jnp.concatenate([
        jnp.full((fox_w,), HEAD_DIM ** -0.5 * LOG2E, F32), jnp.ones((2 * fox_w,), F32),
        jnp.full((ml_qk_w,), ML_QK_DIM ** -0.5, F32),
        jnp.ones((w_p.shape[1] - 3 * fox_w - ml_qk_w,), F32)]).reshape(1, -1)
    proj = _norm_matmul(x, g_mix, w_p, scale_p, BF16, tn=1024)
    w_s = jnp.concatenate([cols("fox_f", "fox_f"), cols("gdn_b", "gdn_a"), cols("ml_i", "ml_f")], axis=1)
    w_s = jnp.pad(w_s, ((0, 0), (0, LANES - w_s.shape[1])))
    zeros8 = jnp.zeros((HEADS,), F32)
    bias_s = _pad_lanes(jnp.concatenate([p["fox_f_bias"], zeros8, p["gdn_dt_bias"],
                                         p["ml_i_bias"], p["ml_f_bias"]]))
    alog_s = _pad_lanes(jnp.concatenate([zeros8, zeros8, p["gdn_a_log"]]))
    col_small = _small_gates(x, g_mix, w_s, bias_s, alog_s, seq)
    row_f = col_small[:, SM_FOX_F:SM_FOX_F + HEADS].reshape(batch, seq, HEADS)
    row_f = row_f.transpose(0, 2, 1)
    row_g = _chunk_rows(col_small, SM_GDN_A, batch, seq)
    row_i = _chunk_rows(col_small, SM_ML_I, batch, seq)
    row_b = _chunk_rows(col_small, SM_ML_F, batch, seq)
    gdn_qkv = _proj_conv(x, g_mix, cols("gdn_qkv", "gdn_qkv"), p["gdn_conv_w"], seq)

    hps = 4
    y_fox = _fox_attention(proj, row_f, batch, seq, q_col=0, k_col=HEADS // hps,
                           v_col=2 * HEADS // hps, hps=hps)
    y_gdn = _gdn(gdn_qkv, proj, col_small, row_g, p["gdn_norm_g"].reshape(1, HEAD_DIM),
                 batch, seq, z_col=z_base // gdn_w)
    ml_base = 3 * fox_w
    y_ml = _mlstm(proj, proj, col_small, row_i, row_b, p["ml_norm_g"].reshape(1, HEAD_DIM),
                  batch, seq, q_col=ml_base // ml_qk_w, k_col=ml_base // ml_qk_w + 1,
                  v_col=(ml_base + 2 * ml_qk_w) // ml_v_w, og_col=(z_base + gdn_w) // ml_v_w)

    tn = 512
    y = _merge(y_fox, y_gdn, y_ml, p["w_fox_proj"].astype(BF16), p["w_gdn_proj"].astype(BF16),
               p["w_ml_proj"].astype(BF16), proj, p["gate_bias"].reshape(1, -1), D,
               gate_col=gate_base // tn, tn=tn)
    x = _matmul_residual(y, p["w_out"].astype(BF16), x, tm=1024)

    return _ffn(x, p["norm_ffn_g"].reshape(1, D), p["w_up"].astype(BF16), p["ffn_conv_w"],
                p["w_down"].astype(BF16), seq)


def kernel(x, norm_mix_g, w_in, fox_f_bias, gdn_conv_w, gdn_a_log, gdn_dt_bias, gdn_norm_g,
           ml_i_bias, ml_f_bias, ml_norm_g, gate_bias, w_fox_proj, w_gdn_proj, w_ml_proj,
           w_out, norm_ffn_g, w_up, ffn_conv_w, w_down, norm_final_g):
    batch, seq, d_model = x.shape
    stacked = dict(norm_mix_g=norm_mix_g, w_in=w_in, fox_f_bias=fox_f_bias, gdn_conv_w=gdn_conv_w,
                   gdn_a_log=gdn_a_log, gdn_dt_bias=gdn_dt_bias, gdn_norm_g=gdn_norm_g,
                   ml_i_bias=ml_i_bias, ml_f_bias=ml_f_bias, ml_norm_g=ml_norm_g,
                   gate_bias=gate_bias, w_fox_proj=w_fox_proj, w_gdn_proj=w_gdn_proj,
                   w_ml_proj=w_ml_proj, w_out=w_out, norm_ffn_g=norm_ffn_g, w_up=w_up,
                   ffn_conv_w=ffn_conv_w, w_down=w_down)

    def body(h, p):
        return _layer(h, p, batch, seq), None

    h, _ = lax.scan(body, x.reshape(batch * seq, d_model), stacked)
    return _final_norm(h, norm_final_g.reshape(1, d_model)).reshape(batch, seq, d_model)
```

```python
import functools

import jax
import jax.numpy as jnp
from jax import lax
from jax.experimental import pallas as pl
from jax.experimental.pallas import tpu as pltpu

F32 = jnp.float32
BF16 = jnp.bfloat16
HIGHEST = lax.Precision.HIGHEST

EPS = 1e-6
HEADS = 8
HEAD_DIM = 128
ML_QK_DIM = 64
CHUNK = 64
CHUNK_SHIFT = CHUNK.bit_length() - 1
GDN_CONV = 4
FFN_CONV = 3
N_BRANCH = 3
LANES = 128
MXU_WIDTH = 256
HALO = 16

SM_FOX_F, SM_GDN_B, SM_GDN_A, SM_ML_I, SM_ML_F = 0, 8, 16, 24, 32

V7X_VMEM_LIMIT_BYTES = 48 * 1024 * 1024


def _params(*semantics):
    return pltpu.CompilerParams(dimension_semantics=semantics,
                                vmem_limit_bytes=V7X_VMEM_LIMIT_BYTES)


def _rmsnorm(x, g):
    return x * lax.rsqrt(jnp.mean(x * x, axis=-1, keepdims=True) + EPS) * g


def _sigmoid(x):
    return 1.0 / (1.0 + jnp.exp(-x))


def _mm_hi(a, b):
    return jnp.dot(a, b, precision=HIGHEST, preferred_element_type=F32)


def _norm_matmul_kernel(x_ref, g_ref, w_ref, cs_ref, o_ref, xn_ref):
    @pl.when(pl.program_id(1) == 0)
    def _():
        xn_ref[...] = _rmsnorm(x_ref[...], g_ref[...]).astype(BF16)

    acc = jnp.dot(xn_ref[...], w_ref[...], preferred_element_type=F32)
    o_ref[...] = (acc * cs_ref[...]).astype(o_ref.dtype)


def _norm_matmul(x, g, w, colscale, out_dtype, tm=1024, tn=512):
    T, D = x.shape
    N = w.shape[1]
    return pl.pallas_call(
        _norm_matmul_kernel,
        grid=(T // tm, N // tn),
        in_specs=[
            pl.BlockSpec((tm, D), lambda i, j: (i, 0)),
            pl.BlockSpec((1, D), lambda i, j: (0, 0)),
            pl.BlockSpec((D, tn), lambda i, j: (0, j)),
            pl.BlockSpec((1, tn), lambda i, j: (0, j)),
        ],
        out_specs=pl.BlockSpec((tm, tn), lambda i, j: (i, j)),
        out_shape=jax.ShapeDtypeStruct((T, N), out_dtype),
        scratch_shapes=[pltpu.VMEM((tm, D), BF16)],
        compiler_params=_params("parallel", "arbitrary"),
    )(x, g, w, colscale)


def _small_kernel(x_ref, g_ref, w_ref, bias_ref, alog_ref, o_ref, carry_ref, *, blocks_per_seq):
    i = pl.program_id(0)
    tm = x_ref.shape[0]

    @pl.when(i % blocks_per_seq == 0)
    def _():
        carry_ref[...] = jnp.zeros_like(carry_ref)

    xn = _rmsnorm(x_ref[...], g_ref[...]).astype(BF16)
    t = jnp.dot(xn, w_ref[...], preferred_element_type=F32) + bias_ref[...]
    lane = lax.broadcasted_iota(jnp.int32, (tm, LANES), 1)
    e = jnp.log1p(jnp.exp(-jnp.abs(t)))
    logsig = jnp.minimum(t, 0.0) - e
    softplus = jnp.maximum(t, 0.0) + e
    g_decay = -jnp.exp(alog_ref[...]) * softplus
    is_logsig = (lane < SM_GDN_B) | ((lane >= SM_ML_F) & (lane < SM_ML_F + HEADS))
    val = jnp.where(is_logsig, logsig,
                    jnp.where(lane < SM_GDN_A, _sigmoid(t),
                              jnp.where(lane < SM_ML_I, g_decay, t)))

    r = lax.broadcasted_iota(jnp.int32, (tm, tm), 0)
    c = lax.broadcasted_iota(jnp.int32, (tm, tm), 1)
    tri = c <= r
    same_chunk = (r >> CHUNK_SHIFT) == (c >> CHUNK_SHIFT)
    tri_full = jnp.where(tri, 1.0, 0.0).astype(F32)
    tri_chunk = jnp.where(tri & same_chunk, 1.0, 0.0).astype(F32)
    cs_full = _mm_hi(tri_full, val) + carry_ref[...]
    cs_chunk = _mm_hi(tri_chunk, val)
    carry_ref[...] = cs_full[tm - 1:tm, :]

    is_chunk_cs = ((lane >= SM_GDN_A) & (lane < SM_ML_I)) | ((lane >= SM_ML_F) & (lane < SM_ML_F + HEADS))
    o_ref[...] = jnp.where(lane < SM_GDN_B, cs_full, jnp.where(is_chunk_cs, cs_chunk, val))


def _small_gates(x, g, w, bias, alog, seq, tm=512):
    T, D = x.shape
    return pl.pallas_call(
        functools.partial(_small_kernel, blocks_per_seq=seq // tm),
        grid=(T // tm,),
        in_specs=[
            pl.BlockSpec((tm, D), lambda i: (i, 0)),
            pl.BlockSpec((1, D), lambda i: (0, 0)),
            pl.BlockSpec((D, LANES), lambda i: (0, 0)),
            pl.BlockSpec((1, LANES), lambda i: (0, 0)),
            pl.BlockSpec((1, LANES), lambda i: (0, 0)),
        ],
        out_specs=pl.BlockSpec((tm, LANES), lambda i: (i, 0)),
        out_shape=jax.ShapeDtypeStruct((T, LANES), F32),
        scratch_shapes=[pltpu.VMEM((1, LANES), F32)],
        compiler_params=_params("arbitrary"),
    )(x, g, w, bias, alog)


def _conv_taps(h, c_ref, cols):
    taps = c_ref.shape[0]
    tm = h.shape[0] - HALO
    acc = None
    for t in range(taps):
        start = HALO - (taps - 1) + t
        term = h[start:start + tm, :] * c_ref[t:t + 1, cols]
        acc = term if acc is None else acc + term
    return acc


def _column_groups(width):
    return [slice(c * MXU_WIDTH, (c + 1) * MXU_WIDTH) for c in range(width // MXU_WIDTH)]


def _norm_with_halo(x_ref, xh_ref, g_ref, xn_ref, at_seq_start):
    xn_ref[HALO:, :] = _rmsnorm(x_ref[...], g_ref[...]).astype(BF16)
    keep = jnp.where(at_seq_start, 0.0, 1.0)
    xn_ref[:HALO, :] = (_rmsnorm(xh_ref[...], g_ref[...]) * keep).astype(BF16)


def _proj_conv_kernel(x_ref, xh_ref, g_ref, w_ref, c_ref, o_ref, xn_ref, *, blocks_per_seq):
    @pl.when(pl.program_id(1) == 0)
    def _():
        _norm_with_halo(x_ref, xh_ref, g_ref, xn_ref, pl.program_id(0) % blocks_per_seq == 0)

    xn = xn_ref[...]
    groups = _column_groups(w_ref.shape[1])
    hidden = [jnp.dot(xn, w_ref[:, cs], preferred_element_type=F32) for cs in groups]
    for cs, h in zip(groups, hidden):
        u = _conv_taps(h, c_ref, cs)
        o_ref[:, cs] = (u * _sigmoid(u)).astype(o_ref.dtype)


def _proj_conv(x, g, w, conv_w, seq, tm=1024, tn=1024):
    T, D = x.shape
    taps = conv_w.shape[0]
    N = w.shape[1]
    halo_blocks = tm // HALO
    return pl.pallas_call(
        functools.partial(_proj_conv_kernel, blocks_per_seq=seq // tm),
        grid=(T // tm, N // tn),
        in_specs=[
            pl.BlockSpec((tm, D), lambda i, j: (i, 0)),
            pl.BlockSpec((HALO, D), lambda i, j: (jnp.maximum(i * halo_blocks - 1, 0), 0)),
            pl.BlockSpec((1, D), lambda i, j: (0, 0)),
            pl.BlockSpec((D, tn), lambda i, j: (0, j)),
            pl.BlockSpec((taps, tn), lambda i, j: (0, j)),
        ],
        out_specs=pl.BlockSpec((tm, tn), lambda i, j: (i, j)),
        out_shape=jax.ShapeDtypeStruct((T, N), BF16),
        scratch_shapes=[pltpu.VMEM((tm + HALO, D), BF16)],
        compiler_params=_params("parallel", "arbitrary"),
    )(x, x, g, w, conv_w)


NEG_BIG = -1e30


LOG2E = 1.4426950408889634


def _fox_kernel(q_ref, k_ref, v_ref, f_ref, o_ref, m_ref, acc_ref, *, tq, tk, hps):
    qi = pl.program_id(2)
    ones = jnp.ones((tk, HEAD_DIM), BF16)
    lane_tiles = tk // LANES
    m_ref[...] = jnp.full_like(m_ref, NEG_BIG)
    acc_ref[...] = jnp.zeros_like(acc_ref)

    def block(j, masked):
        start = pl.multiple_of(j * tk, tk)
        heads = [slice(h * HEAD_DIM, (h + 1) * HEAD_DIM) for h in range(hps)]
        scores = []
        for h, hs in enumerate(heads):
            s = lax.dot_general(q_ref[:, hs], k_ref[pl.ds(start, tk), hs],
                                (((1,), (1,)), ((), ())), preferred_element_type=F32)
            s = s - f_ref[h, pl.ds(j, 1), :] * LOG2E
            if masked:
                row = lax.broadcasted_iota(jnp.int32, (tq, tk), 0)
                col = lax.broadcasted_iota(jnp.int32, (tq, tk), 1)
                s = jnp.where(col <= row, s, NEG_BIG)
            scores.append([s[:, c * LANES:(c + 1) * LANES] for c in range(lane_tiles)])
        probs, alphas = [], []
        for h, tiles in enumerate(scores):
            m_prev = m_ref[h]
            m_tile = functools.reduce(jnp.maximum, tiles)
            m_new = jnp.maximum(m_prev, jnp.max(m_tile, axis=-1, keepdims=True))
            m_ref[h] = m_new
            alphas.append(jnp.exp2(m_prev - m_new))
            probs.append(jnp.concatenate([jnp.exp2(t - m_new).astype(BF16) for t in tiles], axis=1))
        for h, hs in enumerate(heads):
            v_ext = jnp.concatenate([v_ref[pl.ds(start, tk), hs], ones], axis=1)
            pv = jnp.dot(probs[h], v_ext, preferred_element_type=F32)
            acc_ref[h] = jnp.concatenate([alphas[h], alphas[h]], axis=1) * acc_ref[h] + pv

    def full_block(j, carry):
        block(j, masked=False)
        return carry

    lax.fori_loop(0, qi, full_block, 0)
    block(qi, masked=True)
    for h in range(hps):
        acc = acc_ref[h]
        o_ref[:, h * HEAD_DIM:(h + 1) * HEAD_DIM] = (
            acc[:, :HEAD_DIM] / acc[:, HEAD_DIM:]).astype(o_ref.dtype)


def _fox_attention(qkv, f_cum, batch, seq, *, q_col, k_col, v_col, tq=512, hps=2):
    T = qkv.shape[0]
    nq = seq // tq
    width = hps * HEAD_DIM
    f_blocks = f_cum.reshape(batch, HEADS, nq, tq)
    return pl.pallas_call(
        functools.partial(_fox_kernel, tq=tq, tk=tq, hps=hps),
        grid=(batch, HEADS // hps, nq),
        in_specs=[
            pl.BlockSpec((tq, width), lambda b, h, qi: (b * nq + qi, q_col + h)),
            pl.BlockSpec((seq, width), lambda b, h, qi: (b, k_col + h)),
            pl.BlockSpec((seq, width), lambda b, h, qi: (b, v_col + h)),
            pl.BlockSpec((None, hps, nq, tq), lambda b, h, qi: (b, h, 0, 0)),
        ],
        out_specs=pl.BlockSpec((tq, width), lambda b, h, qi: (b * nq + qi, h)),
        out_shape=jax.ShapeDtypeStruct((T, HEADS * HEAD_DIM), BF16),
        scratch_shapes=[pltpu.VMEM((hps, tq, LANES), F32),
                        pltpu.VMEM((hps, tq, 2 * HEAD_DIM), F32)],
        compiler_params=_params("parallel", "parallel", "arbitrary"),
    )(qkv, qkv, qkv, f_blocks)


def _bmm(a, b):
    return jnp.einsum("gmk,gkn->gmn", a.astype(BF16), b.astype(BF16), preferred_element_type=F32)


def _bmm_nt(a, b):
    return jnp.einsum("gmk,gnk->gmn", a.astype(BF16), b.astype(BF16), preferred_element_type=F32)


def _bmm_tn(a, b):
    return jnp.einsum("gkm,gkn->gmn", a.astype(BF16), b.astype(BF16), preferred_element_type=F32)


def _unit_lower_inverse(m, eye, r, c):
    n = m.shape[-1]
    same_pair = (r >> 1) == (c >> 1)
    x = eye - jnp.where(same_pair, m, 0.0)
    b = 2
    while b < n:
        shift = b.bit_length() - 1
        joins = ((r >> (shift + 1)) == (c >> (shift + 1))) & ((r >> shift) != (c >> shift))
        x = x - _bmm(_bmm(x, jnp.where(joins, m, 0.0)), x)
        b *= 2
    return x


STEP_CHUNKS = 4


def _chunk_head_stack(ref, width=HEAD_DIM):
    return jnp.stack([ref[ci * CHUNK:(ci + 1) * CHUNK, h * width:(h + 1) * width]
                      for ci in range(STEP_CHUNKS) for h in range(HEADS)])


def _chunk_head_cols(col, start):
    return jnp.stack([col[ci * CHUNK:(ci + 1) * CHUNK, start + h:start + h + 1]
                      for ci in range(STEP_CHUNKS) for h in range(HEADS)])


def _chunk_head_rows(row_ref):
    return jnp.concatenate([row_ref[ci] for ci in range(STEP_CHUNKS)], axis=0)[:, None, :]


def _store_heads(o_ref, y):
    for ci in range(STEP_CHUNKS):
        for h in range(HEADS):
            o_ref[ci * CHUNK:(ci + 1) * CHUNK, h * HEAD_DIM:(h + 1) * HEAD_DIM] = y[ci * HEADS + h]


def _gdn_kernel(q_ref, k_ref, v_ref, z_ref, col_ref, rowg_ref, ng_ref, o_ref, s_ref):
    @pl.when(pl.program_id(1) == 0)
    def _():
        s_ref[...] = jnp.zeros_like(s_ref)

    L = CHUNK
    r = lax.broadcasted_iota(jnp.int32, (1, L, L), 1)
    c = lax.broadcasted_iota(jnp.int32, (1, L, L), 2)
    tri = c <= r
    eye = jnp.where(c == r, 1.0, 0.0).astype(F32)
    col = col_ref[...]
    q = _chunk_head_stack(q_ref).astype(F32)
    k = _chunk_head_stack(k_ref).astype(F32)
    v = _chunk_head_stack(v_ref).astype(F32)
    qn = q * lax.rsqrt(jnp.sum(q * q, axis=-1, keepdims=True) + EPS) * (HEAD_DIM ** -0.5)
    kn = k * lax.rsqrt(jnp.sum(k * k, axis=-1, keepdims=True) + EPS)
    beta = _chunk_head_cols(col, SM_GDN_B)
    gc = _chunk_head_cols(col, SM_GDN_A)
    gr = _chunk_head_rows(rowg_ref)
    g_last = gc[:, L - 1:L, :]
    decay = jnp.where(tri, jnp.exp(jnp.where(tri, gc - gr, 0.0)), 0.0)
    eg = jnp.exp(gc)
    kb = kn * beta
    m = jnp.where(c < r, _bmm_nt(kb, kn) * decay, 0.0)
    t_inv = _unit_lower_inverse(m, eye, r, c)
    x = _bmm(t_inv, jnp.concatenate([v * beta, kb * eg], axis=-1))
    u = x[:, :, :HEAD_DIM]
    w = x[:, :, HEAD_DIM:]
    a_qk = _bmm_nt(qn, kn) * decay
    qg = qn * eg
    kg = kn * jnp.exp(g_last - gc)
    carry = jnp.exp(g_last)
    state = s_ref[...]
    outs = []
    for ci in range(STEP_CHUNKS):
        hs = slice(ci * HEADS, (ci + 1) * HEADS)
        v_new = u[hs] - _bmm(w[hs], state)
        outs.append(_bmm(qg[hs], state) + _bmm(a_qk[hs], v_new))
        state = state * carry[hs] + _bmm_tn(kg[hs], v_new)
    s_ref[...] = state
    o = jnp.concatenate(outs, axis=0)
    z = _chunk_head_stack(z_ref).astype(F32)
    _store_heads(o_ref, (_rmsnorm(o, ng_ref[...]) * (z * _sigmoid(z))).astype(o_ref.dtype))


def _gdn(qkv, z_src, col_small, row_g, norm_g, batch, seq, *, z_col):
    T = qkv.shape[0]
    rows = STEP_CHUNKS * CHUNK
    ns = seq // rows
    width = HEADS * HEAD_DIM
    row = lambda b, s: b * ns + s
    return pl.pallas_call(
        _gdn_kernel,
        grid=(batch, ns),
        in_specs=[
            pl.BlockSpec((rows, width), lambda b, s: (row(b, s), 0)),
            pl.BlockSpec((rows, width), lambda b, s: (row(b, s), 1)),
            pl.BlockSpec((rows, width), lambda b, s: (row(b, s), 2)),
            pl.BlockSpec((rows, width), lambda b, s: (row(b, s), z_col)),
            pl.BlockSpec((rows, LANES), lambda b, s: (row(b, s), 0)),
            pl.BlockSpec((STEP_CHUNKS, HEADS, CHUNK), lambda b, s: (row(b, s), 0, 0)),
            pl.BlockSpec((1, HEAD_DIM), lambda b, s: (0, 0)),
        ],
        out_specs=pl.BlockSpec((rows, width), lambda b, s: (row(b, s), 0)),
        out_shape=jax.ShapeDtypeStruct((T, width), BF16),
        scratch_shapes=[pltpu.VMEM((HEADS, HEAD_DIM, HEAD_DIM), F32)],
        compiler_params=_params("parallel", "arbitrary"),
    )(qkv, qkv, qkv, z_src, col_small, row_g, norm_g)


def _mlstm_kernel(q_ref, k_ref, v_ref, og_ref, col_ref, rowi_ref, rowb_ref, ng_ref, o_ref,
                  c_ref, n_ref, m_ref):
    @pl.when(pl.program_id(1) == 0)
    def _():
        c_ref[...] = jnp.zeros_like(c_ref)
        n_ref[...] = jnp.zeros_like(n_ref)
        m_ref[...] = jnp.zeros_like(m_ref)

    L = CHUNK
    r = lax.broadcasted_iota(jnp.int32, (1, L, L), 1)
    c = lax.broadcasted_iota(jnp.int32, (1, L, L), 2)
    tri = c <= r
    col = col_ref[...]
    q = _chunk_head_stack(q_ref, ML_QK_DIM).astype(F32)
    k = _chunk_head_stack(k_ref, ML_QK_DIM).astype(F32)
    v = _chunk_head_stack(v_ref)
    ic = _chunk_head_cols(col, SM_ML_I)
    bc = _chunk_head_cols(col, SM_ML_F)
    ir = _chunk_head_rows(rowi_ref)
    br = _chunk_head_rows(rowb_ref)
    dm = jnp.where(tri, bc - br + ir, -jnp.inf)
    dmax = jnp.max(dm, axis=-1, keepdims=True)
    qk = _bmm_nt(q, k)
    m_state = m_ref[...]
    m_rows, m_lasts = [], []
    for ci in range(STEP_CHUNKS):
        hs = slice(ci * HEADS, (ci + 1) * HEADS)
        m_r = jnp.maximum(bc[hs] + m_state, dmax[hs])
        m_rows.append(m_r)
        m_state = m_r[:, L - 1:L, :]
        m_lasts.append(m_state)
    m_r = jnp.concatenate(m_rows, axis=0)
    m_last = jnp.concatenate(m_lasts, axis=0)
    m_prev = jnp.concatenate([m_ref[...]] + m_lasts[:-1], axis=0)
    w_inter = jnp.exp(bc + m_prev - m_r)
    sm = jnp.exp(dm - m_r) * qk
    intra = _bmm(sm, v)
    intra_sum = jnp.sum(sm, axis=-1, keepdims=True)
    wk = jnp.exp(bc[:, L - 1:L, :] - bc + ic - m_last) * k
    kv = _bmm_tn(wk, v)
    k_sum = jnp.sum(wk, axis=1, keepdims=True)
    w_last = w_inter[:, L - 1:L, :]
    floor = jnp.exp(-m_r)
    c_state = c_ref[...]
    n_state = n_ref[...]
    outs = []
    for ci in range(STEP_CHUNKS):
        hs = slice(ci * HEADS, (ci + 1) * HEADS)
        num = w_inter[hs] * _bmm(q[hs], c_state) + intra[hs]
        den = w_inter[hs] * jnp.sum(q[hs] * n_state, axis=-1, keepdims=True) + intra_sum[hs]
        outs.append(num / jnp.maximum(jnp.abs(den), floor[hs]))
        c_state = w_last[hs] * c_state + kv[hs]
        n_state = w_last[hs] * n_state + k_sum[hs]
    c_ref[...] = c_state
    n_ref[...] = n_state
    m_ref[...] = m_state
    h_tilde = jnp.concatenate(outs, axis=0)
    og = _chunk_head_stack(og_ref).astype(F32)
    _store_heads(o_ref, _rmsnorm(_sigmoid(og) * h_tilde, ng_ref[...]).astype(o_ref.dtype))


def _mlstm(qkv, og_src, col_small, row_i, row_b, norm_g, batch, seq, *, q_col, k_col, v_col, og_col):
    T = qkv.shape[0]
    rows = STEP_CHUNKS * CHUNK
    ns = seq // rows
    qk_width = HEADS * ML_QK_DIM
    width = HEADS * HEAD_DIM
    row = lambda b, s: b * ns + s
    return pl.pallas_call(
        _mlstm_kernel,
        grid=(batch, ns),
        in_specs=[
            pl.BlockSpec((rows, qk_width), lambda b, s: (row(b, s), q_col)),
            pl.BlockSpec((rows, qk_width), lambda b, s: (row(b, s), k_col)),
            pl.BlockSpec((rows, width), lambda b, s: (row(b, s), v_col)),
            pl.BlockSpec((rows, width), lambda b, s: (row(b, s), og_col)),
            pl.BlockSpec((rows, LANES), lambda b, s: (row(b, s), 0)),
            pl.BlockSpec((STEP_CHUNKS, HEADS, CHUNK), lambda b, s: (row(b, s), 0, 0)),
            pl.BlockSpec((STEP_CHUNKS, HEADS, CHUNK), lambda b, s: (row(b, s), 0, 0)),
            pl.BlockSpec((1, HEAD_DIM), lambda b, s: (0, 0)),
        ],
        out_specs=pl.BlockSpec((rows, width), lambda b, s: (row(b, s), 0)),
        out_shape=jax.ShapeDtypeStruct((T, width), BF16),
        scratch_shapes=[pltpu.VMEM((HEADS, ML_QK_DIM, HEAD_DIM), F32),
                        pltpu.VMEM((HEADS, 1, ML_QK_DIM), F32),
                        pltpu.VMEM((HEADS, 1, 1), F32)],
        compiler_params=_params("parallel", "arbitrary"),
    )(qkv, qkv, qkv, og_src, col_small, row_i, row_b, norm_g)


def _merge_kernel(yf_ref, yg_ref, ym_ref, wf_ref, wg_ref, wm_ref, g0_ref, g1_ref, g2_ref,
                  b0_ref, b1_ref, b2_ref, o_ref):
    def branch(y_ref, w_ref, g_ref, b_ref):
        return _sigmoid(g_ref[...] + b_ref[...]) * jnp.dot(y_ref[...], w_ref[...],
                                                          preferred_element_type=F32)

    y = (branch(yf_ref, wf_ref, g0_ref, b0_ref) + branch(yg_ref, wg_ref, g1_ref, b1_ref)
         + branch(ym_ref, wm_ref, g2_ref, b2_ref))
    o_ref[...] = y.astype(o_ref.dtype)


def _merge(y_fox, y_gdn, y_ml, w_fox, w_gdn, w_ml, gate_src, gate_bias, d_model, *, gate_col,
           tm=1024, tn=512):
    T, K = y_fox.shape
    nj = d_model // tn
    y_spec = pl.BlockSpec((tm, K), lambda i, j: (i, 0))
    w_spec = pl.BlockSpec((K, tn), lambda i, j: (0, j))
    gate_specs = [pl.BlockSpec((tm, tn), lambda i, j, n=n: (i, gate_col + n * nj + j))
                  for n in range(N_BRANCH)]
    bias_specs = [pl.BlockSpec((1, tn), lambda i, j, n=n: (0, n * nj + j)) for n in range(N_BRANCH)]
    return pl.pallas_call(
        _merge_kernel,
        grid=(T // tm, nj),
        in_specs=[y_spec, y_spec, y_spec, w_spec, w_spec, w_spec] + gate_specs + bias_specs,
        out_specs=pl.BlockSpec((tm, tn), lambda i, j: (i, j)),
        out_shape=jax.ShapeDtypeStruct((T, d_model), BF16),
        compiler_params=_params("parallel", "parallel"),
    )(y_fox, y_gdn, y_ml, w_fox, w_gdn, w_ml, gate_src, gate_src, gate_src,
      gate_bias, gate_bias, gate_bias)


def _matmul_residual_kernel(a_ref, w_ref, x_ref, o_ref):
    o_ref[...] = x_ref[...] + jnp.dot(a_ref[...], w_ref[...], preferred_element_type=F32)


def _matmul_residual(a, w, x, tm=512, tn=512):
    T, K = a.shape
    N = w.shape[1]
    return pl.pallas_call(
        _matmul_residual_kernel,
        grid=(T // tm, N // tn),
        in_specs=[
            pl.BlockSpec((tm, K), lambda i, j: (i, 0)),
            pl.BlockSpec((K, tn), lambda i, j: (0, j)),
            pl.BlockSpec((tm, tn), lambda i, j: (i, j)),
        ],
        out_specs=pl.BlockSpec((tm, tn), lambda i, j: (i, j)),
        out_shape=jax.ShapeDtypeStruct((T, N), F32),
        compiler_params=_params("parallel", "parallel"),
    )(a, w, x)


def _final_norm_kernel(x_ref, g_ref, o_ref):
    o_ref[...] = _rmsnorm(x_ref[...], g_ref[...])


def _final_norm(x, g, tm=512):
    T, D = x.shape
    return pl.pallas_call(
        _final_norm_kernel,
        grid=(T // tm,),
        in_specs=[pl.BlockSpec((tm, D), lambda i: (i, 0)), pl.BlockSpec((1, D), lambda i: (0, 0))],
        out_specs=pl.BlockSpec((tm, D), lambda i: (i, 0)),
        out_shape=jax.ShapeDtypeStruct((T, D), F32),
        compiler_params=_params("parallel"),
    )(x, g)


def _ffn_kernel(x_ref, xh_ref, g_ref, wu_ref, wg_ref, cu_ref, cg_ref, wd_ref, o_ref, xn_ref, *,
                blocks_per_seq):
    @pl.when(pl.program_id(1) == 0)
    def _():
        _norm_with_halo(x_ref, xh_ref, g_ref, xn_ref, pl.program_id(0) % blocks_per_seq == 0)
        o_ref[...] = x_ref[...]

    xn = xn_ref[...]
    groups = _column_groups(wu_ref.shape[1])
    hidden = [(jnp.dot(xn, wu_ref[:, cs], preferred_element_type=F32),
               jnp.dot(xn, wg_ref[:, cs], preferred_element_type=F32)) for cs in groups]
    acts = []
    for cs, (hu, hg) in zip(groups, hidden):
        gt = _conv_taps(hg, cg_ref, cs)
        acts.append((gt * _sigmoid(gt) * _conv_taps(hu, cu_ref, cs)).astype(BF16))
    o_ref[...] += jnp.dot(jnp.concatenate(acts, axis=1), wd_ref[...], preferred_element_type=F32)


def _ffn(x, g, w_up, conv_w, w_down, seq, tm=1024, tn=512):
    T, D = x.shape
    taps = conv_w.shape[0]
    nj = w_down.shape[0] // tn
    halo_blocks = tm // HALO
    once = pl.Buffered(1)
    return pl.pallas_call(
        functools.partial(_ffn_kernel, blocks_per_seq=seq // tm),
        grid=(T // tm, nj),
        in_specs=[
            pl.BlockSpec((tm, D), lambda i, j: (i, 0), pipeline_mode=once),
            pl.BlockSpec((HALO, D), lambda i, j: (jnp.maximum(i * halo_blocks - 1, 0), 0)),
            pl.BlockSpec((1, D), lambda i, j: (0, 0)),
            pl.BlockSpec((D, tn), lambda i, j: (0, j)),
            pl.BlockSpec((D, tn), lambda i, j: (0, nj + j)),
            pl.BlockSpec((taps, tn), lambda i, j: (0, j)),
            pl.BlockSpec((taps, tn), lambda i, j: (0, nj + j)),
            pl.BlockSpec((tn, D), lambda i, j: (j, 0)),
        ],
        out_specs=pl.BlockSpec((tm, D), lambda i, j: (i, 0), pipeline_mode=once),
        out_shape=jax.ShapeDtypeStruct((T, D), F32),
        scratch_shapes=[pltpu.VMEM((tm + HALO, D), BF16)],
        compiler_params=_params("parallel", "arbitrary"),
    )(x, x, g, w_up, w_up, conv_w, conv_w, w_down)


def _pad_lanes(v):
    return jnp.pad(v, (0, LANES - v.shape[0])).reshape(1, LANES)


def _chunk_rows(col_small, start, batch, seq):
    v = col_small[:, start:start + HEADS].reshape(batch * (seq // CHUNK), CHUNK, HEADS)
    return v.transpose(0, 2, 1)


def _layer(x, p, batch, seq):
    T, D = x.shape
    fox_w = HEADS * HEAD_DIM
    gdn_w = HEADS * HEAD_DIM
    ml_qk_w = HEADS * ML_QK_DIM
    ml_v_w = HEADS * HEAD_DIM
    widths = [fox_w, fox_w, fox_w, HEADS, 3 * gdn_w, gdn_w, HEADS, HEADS,
              ml_qk_w, ml_qk_w, ml_v_w, HEADS, HEADS, ml_v_w, N_BRANCH * D]
    names = ["fox_q", "fox_k", "fox_v", "fox_f", "gdn_qkv", "gdn_z", "gdn_b", "gdn_a",
             "ml_q", "ml_k", "ml_v", "ml_i", "ml_f", "ml_o", "gate"]
    off = {}
    o = 0
    for n, w in zip(names, widths):
        off[n] = (o, o + w)
        o += w
    w_in = p["w_in"].astype(BF16)
    cols = lambda a, b: w_in[:, off[a][0]:off[b][1]]
    g_mix = p["norm_mix_g"].reshape(1, D)

    w_p = jnp.concatenate([cols("fox_q", "fox_v"), cols("ml_q", "ml_v"), cols("gdn_z", "gdn_z"),
                           cols("ml_o", "ml_o"), cols("gate", "gate")], axis=1)
    z_base = 3 * fox_w + 2 * ml_qk_w + ml_v_w
    gate_base = z_base + gdn_w + ml_v_w
    scale_p = jnp.concatenate([
        jnp.full((fox_w,), HEAD_DIM ** -0.5 * LOG2E, F32), jnp.ones((2 * fox_w,), F32),
        jnp.full((ml_qk_w,), ML_QK_DIM ** -0.5, F32),
        jnp.ones((w_p.shape[1] - 3 * fox_w - ml_qk_w,), F32)]).reshape(1, -1)
    proj = _norm_matmul(x, g_mix, w_p, scale_p, BF16, tn=1024)
    w_s = jnp.concatenate([cols("fox_f", "fox_f"), cols("gdn_b", "gdn_a"), cols("ml_i", "ml_f")], axis=1)
    w_s = jnp.pad(w_s, ((0, 0), (0, LANES - w_s.shape[1])))
    zeros8 = jnp.zeros((HEADS,), F32)
    bias_s = _pad_lanes(jnp.concatenate([p["fox_f_bias"], zeros8, p["gdn_dt_bias"],
                                         p["ml_i_bias"], p["ml_f_bias"]]))
    alog_s = _pad_lanes(jnp.concatenate([zeros8, zeros8, p["gdn_a_log"]]))
    col_small = _small_gates(x, g_mix, w_s, bias_s, alog_s, seq)
    row_f = col_small[:, SM_FOX_F:SM_FOX_F + HEADS].reshape(batch, seq, HEADS)
    row_f = row_f.transpose(0, 2, 1)
    row_g = _chunk_rows(col_small, SM_GDN_A, batch, seq)
    row_i = _chunk_rows(col_small, SM_ML_I, batch, seq)
    row_b = _chunk_rows(col_small, SM_ML_F, batch, seq)
    gdn_qkv = _proj_conv(x, g_mix, cols("gdn_qkv", "gdn_qkv"), p["gdn_conv_w"], seq)

    hps = 4
    y_fox = _fox_attention(proj, row_f, batch, seq, q_col=0, k_col=HEADS // hps,
                           v_col=2 * HEADS // hps, hps=hps)
    y_gdn = _gdn(gdn_qkv, proj, col_small, row_g, p["gdn_norm_g"].reshape(1, HEAD_DIM),
                 batch, seq, z_col=z_base // gdn_w)
    ml_base = 3 * fox_w
    y_ml = _mlstm(proj, proj, col_small, row_i, row_b, p["ml_norm_g"].reshape(1, HEAD_DIM),
                  batch, seq, q_col=ml_base // ml_qk_w, k_col=ml_base // ml_qk_w + 1,
                  v_col=(ml_base + 2 * ml_qk_w) // ml_v_w, og_col=(z_base + gdn_w) // ml_v_w)

    tn = 512
    y = _merge(y_fox, y_gdn, y_ml, p["w_fox_proj"].astype(BF16), p["w_gdn_proj"].astype(BF16),
               p["w_ml_proj"].astype(BF16), proj, p["gate_bias"].reshape(1, -1), D,
               gate_col=gate_base // tn, tn=tn)
    x = _matmul_residual(y, p["w_out"].astype(BF16), x, tm=1024)

    return _ffn(x, p["norm_ffn_g"].reshape(1, D), p["w_up"].astype(BF16), p["ffn_conv_w"],
                p["w_down"].astype(BF16), seq)


def kernel(x, norm_mix_g, w_in, fox_f_bias, gdn_conv_w, gdn_a_log, gdn_dt_bias, gdn_norm_g,
           ml_i_bias, ml_f_bias, ml_norm_g, gate_bias, w_fox_proj, w_gdn_proj, w_ml_proj,
           w_out, norm_ffn_g, w_up, ffn_conv_w, w_down, norm_final_g):
    batch, seq, d_model = x.shape
    stacked = dict(norm_mix_g=norm_mix_g, w_in=w_in, fox_f_bias=fox_f_bias, gdn_conv_w=gdn_conv_w,
                   gdn_a_log=gdn_a_log, gdn_dt_bias=gdn_dt_bias, gdn_norm_g=gdn_norm_g,
                   ml_i_bias=ml_i_bias, ml_f_bias=ml_f_bias, ml_norm_g=ml_norm_g,
                   gate_bias=gate_bias, w_fox_proj=w_fox_proj, w_gdn_proj=w_gdn_proj,
                   w_ml_proj=w_ml_proj, w_out=w_out, norm_ffn_g=norm_ffn_g, w_up=w_up,
                   ffn_conv_w=ffn_conv_w, w_down=w_down)

    def body(h, p):
        return _layer(h, p, batch, seq), None

    h, _ = lax.scan(body, x.reshape(batch * seq, d_model), stacked)
    return _final_norm(h, norm_final_g.reshape(1, d_model)).reshape(batch, seq, d_model)
```

```python
import functools

import jax
import jax.numpy as jnp
from jax import lax
from jax.experimental import pallas as pl
from jax.experimental.pallas import tpu as pltpu

F32 = jnp.float32
BF16 = jnp.bfloat16
HIGHEST = lax.Precision.HIGHEST

EPS = 1e-6
HEADS = 8
HEAD_DIM = 128
ML_QK_DIM = 64
CHUNK = 64
CHUNK_SHIFT = CHUNK.bit_length() - 1
GDN_CONV = 4
FFN_CONV = 3
N_BRANCH = 3
LANES = 128
MXU_WIDTH = 256
HALO = 16

SM_FOX_F, SM_GDN_B, SM_GDN_A, SM_ML_I, SM_ML_F = 0, 8, 16, 24, 32

V7X_VMEM_BYTES = 64 * 1024 * 1024
VMEM_LIMIT_BYTES = V7X_VMEM_BYTES * 3 // 4
VMEM_LIMIT_LARGE_BYTES = V7X_VMEM_BYTES * 7 // 8


def _params(*semantics, vmem_limit_bytes=VMEM_LIMIT_BYTES):
    return pltpu.CompilerParams(dimension_semantics=semantics, vmem_limit_bytes=vmem_limit_bytes)


def _rmsnorm(x, g):
    return x * lax.rsqrt(jnp.mean(x * x, axis=-1, keepdims=True) + EPS) * g


def _sigmoid(x):
    return 1.0 / (1.0 + jnp.exp(-x))


def _mm_hi(a, b):
    return jnp.dot(a, b, precision=HIGHEST, preferred_element_type=F32)


def _norm_matmul_kernel(x_ref, g_ref, w_ref, cs_ref, o_ref, xn_ref):
    @pl.when(pl.program_id(1) == 0)
    def _():
        xn_ref[...] = _rmsnorm(x_ref[...], g_ref[...]).astype(BF16)

    acc = jnp.dot(xn_ref[...], w_ref[...], preferred_element_type=F32)
    o_ref[...] = (acc * cs_ref[...]).astype(o_ref.dtype)


def _norm_matmul(x, g, w_all, layer, colscale, out_dtype, tm=1024, tn=512):
    T, D = x.shape
    N = w_all.shape[2]
    return pl.pallas_call(
        _norm_matmul_kernel,
        grid=(T // tm, N // tn),
        in_specs=[
            pl.BlockSpec((tm, D), lambda i, j: (i, 0)),
            pl.BlockSpec((1, D), lambda i, j: (0, 0)),
            pl.BlockSpec((None, D, tn), lambda i, j: (layer, 0, j)),
            pl.BlockSpec((1, tn), lambda i, j: (0, j)),
        ],
        out_specs=pl.BlockSpec((tm, tn), lambda i, j: (i, j)),
        out_shape=jax.ShapeDtypeStruct((T, N), out_dtype),
        scratch_shapes=[pltpu.VMEM((tm, D), BF16)],
        compiler_params=_params("parallel", "arbitrary"),
    )(x, g, w_all, colscale)


def _small_kernel(x_ref, g_ref, w_ref, bias_ref, alog_ref, o_ref, rows_ref, carry_ref, *,
                  blocks_per_seq):
    i = pl.program_id(0)
    tm = x_ref.shape[0]

    @pl.when(i % blocks_per_seq == 0)
    def _():
        carry_ref[...] = jnp.zeros_like(carry_ref)

    xn = _rmsnorm(x_ref[...], g_ref[...]).astype(BF16)
    t = jnp.dot(xn, w_ref[...], preferred_element_type=F32) + bias_ref[...]
    lane = lax.broadcasted_iota(jnp.int32, (tm, LANES), 1)
    e = jnp.log1p(jnp.exp(-jnp.abs(t)))
    logsig = jnp.minimum(t, 0.0) - e
    softplus = jnp.maximum(t, 0.0) + e
    g_decay = -jnp.exp(alog_ref[...]) * softplus
    is_logsig = (lane < SM_GDN_B) | ((lane >= SM_ML_F) & (lane < SM_ML_F + HEADS))
    val = jnp.where(is_logsig, logsig,
                    jnp.where(lane < SM_GDN_A, _sigmoid(t),
                              jnp.where(lane < SM_ML_I, g_decay, t)))

    r = lax.broadcasted_iota(jnp.int32, (tm, tm), 0)
    c = lax.broadcasted_iota(jnp.int32, (tm, tm), 1)
    tri = c <= r
    same_chunk = (r >> CHUNK_SHIFT) == (c >> CHUNK_SHIFT)
    tri_full = jnp.where(tri, 1.0, 0.0).astype(F32)
    tri_chunk = jnp.where(tri & same_chunk, 1.0, 0.0).astype(F32)
    cs_full = _mm_hi(tri_full, val) + carry_ref[...]
    cs_chunk = _mm_hi(tri_chunk, val)
    carry_ref[...] = cs_full[tm - 1:tm, :]

    is_chunk_cs = ((lane >= SM_GDN_A) & (lane < SM_ML_I)) | ((lane >= SM_ML_F) & (lane < SM_ML_F + HEADS))
    out = jnp.where(lane < SM_GDN_B, cs_full, jnp.where(is_chunk_cs, cs_chunk, val))
    o_ref[...] = out
    for ci in range(tm // CHUNK):
        rows_ref[ci] = out[ci * CHUNK:(ci + 1) * CHUNK, :].T


def _small_gates(x, g, w_all, layer, bias, alog, seq, tm=512):
    T, D = x.shape
    return pl.pallas_call(
        functools.partial(_small_kernel, blocks_per_seq=seq // tm),
        grid=(T // tm,),
        in_specs=[
            pl.BlockSpec((tm, D), lambda i: (i, 0)),
            pl.BlockSpec((1, D), lambda i: (0, 0)),
            pl.BlockSpec((None, D, LANES), lambda i: (layer, 0, 0)),
            pl.BlockSpec((1, LANES), lambda i: (0, 0)),
            pl.BlockSpec((1, LANES), lambda i: (0, 0)),
        ],
        out_specs=[pl.BlockSpec((tm, LANES), lambda i: (i, 0)),
                   pl.BlockSpec((tm // CHUNK, LANES, CHUNK), lambda i: (i, 0, 0))],
        out_shape=[jax.ShapeDtypeStruct((T, LANES), F32),
                   jax.ShapeDtypeStruct((T // CHUNK, LANES, CHUNK), F32)],
        scratch_shapes=[pltpu.VMEM((1, LANES), F32)],
        compiler_params=_params("arbitrary"),
    )(x, g, w_all, bias, alog)


def _conv_taps(h, c_ref, cols):
    taps = c_ref.shape[0]
    tm = h.shape[0] - HALO
    acc = None
    for t in range(taps):
        start = HALO - (taps - 1) + t
        term = h[start:start + tm, :] * c_ref[t:t + 1, cols]
        acc = term if acc is None else acc + term
    return acc


def _column_groups(width):
    return [slice(c * MXU_WIDTH, (c + 1) * MXU_WIDTH) for c in range(width // MXU_WIDTH)]


def _norm_with_halo(x_ref, xh_ref, g_ref, xn_ref, at_seq_start):
    xn_ref[HALO:, :] = _rmsnorm(x_ref[...], g_ref[...]).astype(BF16)
    keep = jnp.where(at_seq_start, 0.0, 1.0)
    xn_ref[:HALO, :] = (_rmsnorm(xh_ref[...], g_ref[...]) * keep).astype(BF16)


def _proj_conv_kernel(x_ref, xh_ref, g_ref, w_ref, c_ref, o_ref, xn_ref, *, blocks_per_seq):
    @pl.when(pl.program_id(1) == 0)
    def _():
        _norm_with_halo(x_ref, xh_ref, g_ref, xn_ref, pl.program_id(0) % blocks_per_seq == 0)

    xn = xn_ref[...]
    groups = _column_groups(w_ref.shape[1])
    hidden = [jnp.dot(xn, w_ref[:, cs], preferred_element_type=F32) for cs in groups]
    for cs, h in zip(groups, hidden):
        u = _conv_taps(h, c_ref, cs)
        o_ref[:, cs] = (u * _sigmoid(u)).astype(o_ref.dtype)


def _proj_conv(x, g, w_all, layer, conv_w, seq, tm=1024, tn=1024):
    T, D = x.shape
    taps = conv_w.shape[0]
    N = w_all.shape[2]
    halo_blocks = tm // HALO
    return pl.pallas_call(
        functools.partial(_proj_conv_kernel, blocks_per_seq=seq // tm),
        grid=(T // tm, N // tn),
        in_specs=[
            pl.BlockSpec((tm, D), lambda i, j: (i, 0)),
            pl.BlockSpec((HALO, D), lambda i, j: (jnp.maximum(i * halo_blocks - 1, 0), 0)),
            pl.BlockSpec((1, D), lambda i, j: (0, 0)),
            pl.BlockSpec((None, D, tn), lambda i, j: (layer, 0, j)),
            pl.BlockSpec((taps, tn), lambda i, j: (0, j)),
        ],
        out_specs=pl.BlockSpec((tm, tn), lambda i, j: (i, j)),
        out_shape=jax.ShapeDtypeStruct((T, N), BF16),
        scratch_shapes=[pltpu.VMEM((tm + HALO, D), BF16)],
        compiler_params=_params("parallel", "arbitrary"),
    )(x, x, g, w_all, conv_w)


NEG_BIG = -1e30


LOG2E = 1.4426950408889634


def _fox_kernel(q_ref, k_ref, v_ref, f_ref, o_ref, m_ref, acc_ref, *, tq, tk, hps):
    qi = pl.program_id(2)
    ones = jnp.ones((tk, HEAD_DIM), BF16)
    lane_tiles = tk // LANES
    m_ref[...] = jnp.full_like(m_ref, NEG_BIG)
    acc_ref[...] = jnp.zeros_like(acc_ref)

    def block(j, masked):
        start = pl.multiple_of(j * tk, tk)
        heads = [slice(h * HEAD_DIM, (h + 1) * HEAD_DIM) for h in range(hps)]
        scores = []
        for h, hs in enumerate(heads):
            s = lax.dot_general(q_ref[:, hs], k_ref[pl.ds(start, tk), hs],
                                (((1,), (1,)), ((), ())), preferred_element_type=F32)
            s = s - f_ref[h, pl.ds(j, 1), :] * LOG2E
            if masked:
                row = lax.broadcasted_iota(jnp.int32, (tq, tk), 0)
                col = lax.broadcasted_iota(jnp.int32, (tq, tk), 1)
                s = jnp.where(col <= row, s, NEG_BIG)
            scores.append([s[:, c * LANES:(c + 1) * LANES] for c in range(lane_tiles)])
        probs, alphas = [], []
        for h, tiles in enumerate(scores):
            m_prev = m_ref[h]
            m_tile = functools.reduce(jnp.maximum, tiles)
            m_new = jnp.maximum(m_prev, jnp.max(m_tile, axis=-1, keepdims=True))
            m_ref[h] = m_new
            alphas.append(jnp.exp2(m_prev - m_new))
            probs.append(jnp.concatenate([jnp.exp2(t - m_new).astype(BF16) for t in tiles], axis=1))
        for h, hs in enumerate(heads):
            v_ext = jnp.concatenate([v_ref[pl.ds(start, tk), hs], ones], axis=1)
            pv = jnp.dot(probs[h], v_ext, preferred_element_type=F32)
            acc_ref[h] = jnp.concatenate([alphas[h], alphas[h]], axis=1) * acc_ref[h] + pv

    def full_block(j, carry):
        block(j, masked=False)
        return carry

    lax.fori_loop(0, qi, full_block, 0)
    block(qi, masked=True)
    for h in range(hps):
        acc = acc_ref[h]
        o_ref[:, h * HEAD_DIM:(h + 1) * HEAD_DIM] = (
            acc[:, :HEAD_DIM] / acc[:, HEAD_DIM:]).astype(o_ref.dtype)


def _fox_attention(qkv, f_cum, batch, seq, *, q_col, k_col, v_col, tq=512, hps=2):
    T = qkv.shape[0]
    nq = seq // tq
    width = hps * HEAD_DIM
    f_blocks = f_cum.reshape(batch, HEADS, nq, tq)
    return pl.pallas_call(
        functools.partial(_fox_kernel, tq=tq, tk=tq, hps=hps),
        grid=(batch, HEADS // hps, nq),
        in_specs=[
            pl.BlockSpec((tq, width), lambda b, h, qi: (b * nq + qi, q_col + h)),
            pl.BlockSpec((seq, width), lambda b, h, qi: (b, k_col + h)),
            pl.BlockSpec((seq, width), lambda b, h, qi: (b, v_col + h)),
            pl.BlockSpec((None, hps, nq, tq), lambda b, h, qi: (b, h, 0, 0)),
        ],
        out_specs=pl.BlockSpec((tq, width), lambda b, h, qi: (b * nq + qi, h)),
        out_shape=jax.ShapeDtypeStruct((T, HEADS * HEAD_DIM), BF16),
        scratch_shapes=[pltpu.VMEM((hps, tq, LANES), F32),
                        pltpu.VMEM((hps, tq, 2 * HEAD_DIM), F32)],
        compiler_params=_params("parallel", "parallel", "arbitrary"),
    )(qkv, qkv, qkv, f_blocks)


def _bmm(a, b):
    return jnp.einsum("gmk,gkn->gmn", a.astype(BF16), b.astype(BF16), preferred_element_type=F32)


def _bmm_nt(a, b):
    return jnp.einsum("gmk,gnk->gmn", a.astype(BF16), b.astype(BF16), preferred_element_type=F32)


def _bmm_tn(a, b):
    return jnp.einsum("gkm,gkn->gmn", a.astype(BF16), b.astype(BF16), preferred_element_type=F32)


def _unit_lower_inverse(m, eye, r, c):
    n = m.shape[-1]
    same_pair = (r >> 1) == (c >> 1)
    x = eye - jnp.where(same_pair, m, 0.0)
    b = 2
    while b < n:
        shift = b.bit_length() - 1
        joins = ((r >> (shift + 1)) == (c >> (shift + 1))) & ((r >> shift) != (c >> shift))
        x = x - _bmm(_bmm(x, jnp.where(joins, m, 0.0)), x)
        b *= 2
    return x


STEP_CHUNKS = 4


def _chunk_head_stack(ref, width=HEAD_DIM):
    return jnp.stack([ref[ci * CHUNK:(ci + 1) * CHUNK, h * width:(h + 1) * width]
                      for ci in range(STEP_CHUNKS) for h in range(HEADS)])


def _chunk_head_cols(col, start):
    return jnp.stack([col[ci * CHUNK:(ci + 1) * CHUNK, start + h:start + h + 1]
                      for ci in range(STEP_CHUNKS) for h in range(HEADS)])


def _chunk_head_rows(row_ref):
    return jnp.concatenate([row_ref[ci] for ci in range(STEP_CHUNKS)], axis=0)[:, None, :]


def _store_heads(o_ref, y):
    for ci in range(STEP_CHUNKS):
        for h in range(HEADS):
            o_ref[ci * CHUNK:(ci + 1) * CHUNK, h * HEAD_DIM:(h + 1) * HEAD_DIM] = y[ci * HEADS + h]


def _gdn_kernel(q_ref, k_ref, v_ref, z_ref, col_ref, rowg_ref, ng_ref, o_ref, s_ref):
    @pl.when(pl.program_id(1) == 0)
    def _():
        s_ref[...] = jnp.zeros_like(s_ref)

    L = CHUNK
    r = lax.broadcasted_iota(jnp.int32, (1, L, L), 1)
    c = lax.broadcasted_iota(jnp.int32, (1, L, L), 2)
    tri = c <= r
    eye = jnp.where(c == r, 1.0, 0.0).astype(F32)
    col = col_ref[...]
    q = _chunk_head_stack(q_ref).astype(F32)
    k = _chunk_head_stack(k_ref).astype(F32)
    v = _chunk_head_stack(v_ref).astype(F32)
    qn = q * lax.rsqrt(jnp.sum(q * q, axis=-1, keepdims=True) + EPS) * (HEAD_DIM ** -0.5)
    kn = k * lax.rsqrt(jnp.sum(k * k, axis=-1, keepdims=True) + EPS)
    beta = _chunk_head_cols(col, SM_GDN_B)
    gc = _chunk_head_cols(col, SM_GDN_A)
    gr = _chunk_head_rows(rowg_ref)
    g_last = gc[:, L - 1:L, :]
    decay = jnp.where(tri, jnp.exp(jnp.where(tri, gc - gr, 0.0)), 0.0)
    eg = jnp.exp(gc)
    kb = kn * beta
    m = jnp.where(c < r, _bmm_nt(kb, kn) * decay, 0.0)
    t_inv = _unit_lower_inverse(m, eye, r, c)
    x = _bmm(t_inv, jnp.concatenate([v * beta, kb * eg], axis=-1))
    u = x[:, :, :HEAD_DIM]
    w = x[:, :, HEAD_DIM:]
    a_qk = _bmm_nt(qn, kn) * decay
    qg = qn * eg
    kg = kn * jnp.exp(g_last - gc)
    carry = jnp.exp(g_last)
    state = s_ref[...]
    outs = []
    for ci in range(STEP_CHUNKS):
        hs = slice(ci * HEADS, (ci + 1) * HEADS)
        v_new = u[hs] - _bmm(w[hs], state)
        outs.append(_bmm(qg[hs], state) + _bmm(a_qk[hs], v_new))
        state = state * carry[hs] + _bmm_tn(kg[hs], v_new)
    s_ref[...] = state
    o = jnp.concatenate(outs, axis=0)
    z = _chunk_head_stack(z_ref).astype(F32)
    _store_heads(o_ref, (_rmsnorm(o, ng_ref[...]) * (z * _sigmoid(z))).astype(o_ref.dtype))


def _gdn(qkv, z_src, col_small, row_small, norm_g, batch, seq, *, z_col):
    T = qkv.shape[0]
    rows = STEP_CHUNKS * CHUNK
    ns = seq // rows
    width = HEADS * HEAD_DIM
    row = lambda b, s: b * ns + s
    return pl.pallas_call(
        _gdn_kernel,
        grid=(batch, ns),
        in_specs=[
            pl.BlockSpec((rows, width), lambda b, s: (row(b, s), 0)),
            pl.BlockSpec((rows, width), lambda b, s: (row(b, s), 1)),
            pl.BlockSpec((rows, width), lambda b, s: (row(b, s), 2)),
            pl.BlockSpec((rows, width), lambda b, s: (row(b, s), z_col)),
            pl.BlockSpec((rows, LANES), lambda b, s: (row(b, s), 0)),
            pl.BlockSpec((STEP_CHUNKS, HEADS, CHUNK), lambda b, s: (row(b, s), SM_GDN_A // HEADS, 0)),
            pl.BlockSpec((1, HEAD_DIM), lambda b, s: (0, 0)),
        ],
        out_specs=pl.BlockSpec((rows, width), lambda b, s: (row(b, s), 0)),
        out_shape=jax.ShapeDtypeStruct((T, width), BF16),
        scratch_shapes=[pltpu.VMEM((HEADS, HEAD_DIM, HEAD_DIM), F32)],
        compiler_params=_params("parallel", "arbitrary"),
    )(qkv, qkv, qkv, z_src, col_small, row_small, norm_g)


def _mlstm_kernel(q_ref, k_ref, v_ref, og_ref, col_ref, rowi_ref, rowb_ref, ng_ref, o_ref,
                  c_ref, n_ref, m_ref):
    @pl.when(pl.program_id(1) == 0)
    def _():
        c_ref[...] = jnp.zeros_like(c_ref)
        n_ref[...] = jnp.zeros_like(n_ref)
        m_ref[...] = jnp.zeros_like(m_ref)

    L = CHUNK
    r = lax.broadcasted_iota(jnp.int32, (1, L, L), 1)
    c = lax.broadcasted_iota(jnp.int32, (1, L, L), 2)
    tri = c <= r
    col = col_ref[...]
    q = _chunk_head_stack(q_ref, ML_QK_DIM).astype(F32)
    k = _chunk_head_stack(k_ref, ML_QK_DIM).astype(F32)
    v = _chunk_head_stack(v_ref)
    ic = _chunk_head_cols(col, SM_ML_I)
    bc = _chunk_head_cols(col, SM_ML_F)
    ir = _chunk_head_rows(rowi_ref)
    br = _chunk_head_rows(rowb_ref)
    dm = jnp.where(tri, bc - br + ir, -jnp.inf)
    dmax = jnp.max(dm, axis=-1, keepdims=True)
    qk = _bmm_nt(q, k)
    m_state = m_ref[...]
    m_rows, m_lasts = [], []
    for ci in range(STEP_CHUNKS):
        hs = slice(ci * HEADS, (ci + 1) * HEADS)
        m_r = jnp.maximum(bc[hs] + m_state, dmax[hs])
        m_rows.append(m_r)
        m_state = m_r[:, L - 1:L, :]
        m_lasts.append(m_state)
    m_r = jnp.concatenate(m_rows, axis=0)
    m_last = jnp.concatenate(m_lasts, axis=0)
    m_prev = jnp.concatenate([m_ref[...]] + m_lasts[:-1], axis=0)
    w_inter = jnp.exp(bc + m_prev - m_r)
    sm = jnp.exp(dm - m_r) * qk
    intra = _bmm(sm, v)
    intra_sum = jnp.sum(sm, axis=-1, keepdims=True)
    wk = jnp.exp(bc[:, L - 1:L, :] - bc + ic - m_last) * k
    kv = _bmm_tn(wk, v)
    k_sum = jnp.sum(wk, axis=1, keepdims=True)
    w_last = w_inter[:, L - 1:L, :]
    floor = jnp.exp(-m_r)
    c_state = c_ref[...]
    n_state = n_ref[...]
    outs = []
    for ci in range(STEP_CHUNKS):
        hs = slice(ci * HEADS, (ci + 1) * HEADS)
        num = w_inter[hs] * _bmm(q[hs], c_state) + intra[hs]
        den = w_inter[hs] * jnp.sum(q[hs] * n_state, axis=-1, keepdims=True) + intra_sum[hs]
        outs.append(num / jnp.maximum(jnp.abs(den), floor[hs]))
        c_state = w_last[hs] * c_state + kv[hs]
        n_state = w_last[hs] * n_state + k_sum[hs]
    c_ref[...] = c_state
    n_ref[...] = n_state
    m_ref[...] = m_state
    h_tilde = jnp.concatenate(outs, axis=0)
    og = _chunk_head_stack(og_ref).astype(F32)
    _store_heads(o_ref, _rmsnorm(_sigmoid(og) * h_tilde, ng_ref[...]).astype(o_ref.dtype))


def _mlstm(qkv, og_src, col_small, row_small, norm_g, batch, seq, *, q_col, k_col, v_col, og_col):
    T = qkv.shape[0]
    rows = STEP_CHUNKS * CHUNK
    ns = seq // rows
    qk_width = HEADS * ML_QK_DIM
    width = HEADS * HEAD_DIM
    row = lambda b, s: b * ns + s
    return pl.pallas_call(
        _mlstm_kernel,
        grid=(batch, ns),
        in_specs=[
            pl.BlockSpec((rows, qk_width), lambda b, s: (row(b, s), q_col)),
            pl.BlockSpec((rows, qk_width), lambda b, s: (row(b, s), k_col)),
            pl.BlockSpec((rows, width), lambda b, s: (row(b, s), v_col)),
            pl.BlockSpec((rows, width), lambda b, s: (row(b, s), og_col)),
            pl.BlockSpec((rows, LANES), lambda b, s: (row(b, s), 0)),
            pl.BlockSpec((STEP_CHUNKS, HEADS, CHUNK), lambda b, s: (row(b, s), SM_ML_I // HEADS, 0)),
            pl.BlockSpec((STEP_CHUNKS, HEADS, CHUNK), lambda b, s: (row(b, s), SM_ML_F // HEADS, 0)),
            pl.BlockSpec((1, HEAD_DIM), lambda b, s: (0, 0)),
        ],
        out_specs=pl.BlockSpec((rows, width), lambda b, s: (row(b, s), 0)),
        out_shape=jax.ShapeDtypeStruct((T, width), BF16),
        scratch_shapes=[pltpu.VMEM((HEADS, ML_QK_DIM, HEAD_DIM), F32),
                        pltpu.VMEM((HEADS, 1, ML_QK_DIM), F32),
                        pltpu.VMEM((HEADS, 1, 1), F32)],
        compiler_params=_params("parallel", "arbitrary"),
    )(qkv, qkv, qkv, og_src, col_small, row_small, row_small, norm_g)


def _merge_kernel(yf_ref, yg_ref, ym_ref, wf_ref, wg_ref, wm_ref, g0_ref, g1_ref, g2_ref,
                  b0_ref, b1_ref, b2_ref, o_ref):
    def branch(y_ref, w_ref, g_ref, b_ref):
        return _sigmoid(g_ref[...] + b_ref[...]) * jnp.dot(y_ref[...], w_ref[...],
                                                          preferred_element_type=F32)

    y = (branch(yf_ref, wf_ref, g0_ref, b0_ref) + branch(yg_ref, wg_ref, g1_ref, b1_ref)
         + branch(ym_ref, wm_ref, g2_ref, b2_ref))
    o_ref[...] = y.astype(o_ref.dtype)


def _merge(y_fox, y_gdn, y_ml, w_fox, w_gdn, w_ml, layer, gate_src, gate_bias, d_model, *,
           gate_col, tm=1024, tn=512):
    T, K = y_fox.shape
    nj = d_model // tn
    y_spec = pl.BlockSpec((tm, K), lambda i, j: (i, 0))
    w_spec = pl.BlockSpec((None, K, tn), lambda i, j: (layer, 0, j))
    gate_specs = [pl.BlockSpec((tm, tn), lambda i, j, n=n: (i, gate_col + n * nj + j))
                  for n in range(N_BRANCH)]
    bias_specs = [pl.BlockSpec((1, tn), lambda i, j, n=n: (0, n * nj + j)) for n in range(N_BRANCH)]
    return pl.pallas_call(
        _merge_kernel,
        grid=(T // tm, nj),
        in_specs=[y_spec, y_spec, y_spec, w_spec, w_spec, w_spec] + gate_specs + bias_specs,
        out_specs=pl.BlockSpec((tm, tn), lambda i, j: (i, j)),
        out_shape=jax.ShapeDtypeStruct((T, d_model), BF16),
        compiler_params=_params("parallel", "parallel"),
    )(y_fox, y_gdn, y_ml, w_fox, w_gdn, w_ml, gate_src, gate_src, gate_src,
      gate_bias, gate_bias, gate_bias)


def _matmul_residual_kernel(a_ref, w_ref, x_ref, o_ref):
    o_ref[...] = x_ref[...] + jnp.dot(a_ref[...], w_ref[...], preferred_element_type=F32)


def _matmul_residual(a, w_all, layer, x, tm=1024, tn=512):
    T, K = a.shape
    N = w_all.shape[2]
    return pl.pallas_call(
        _matmul_residual_kernel,
        grid=(T // tm, N // tn),
        in_specs=[
            pl.BlockSpec((tm, K), lambda i, j: (i, 0)),
            pl.BlockSpec((None, K, tn), lambda i, j: (layer, 0, j)),
            pl.BlockSpec((tm, tn), lambda i, j: (i, j)),
        ],
        out_specs=pl.BlockSpec((tm, tn), lambda i, j: (i, j)),
        out_shape=jax.ShapeDtypeStruct((T, N), F32),
        compiler_params=_params("parallel", "parallel"),
    )(a, w_all, x)


def _final_norm_kernel(x_ref, g_ref, o_ref):
    o_ref[...] = _rmsnorm(x_ref[...], g_ref[...])


def _final_norm(x, g, tm=512):
    T, D = x.shape
    return pl.pallas_call(
        _final_norm_kernel,
        grid=(T // tm,),
        in_specs=[pl.BlockSpec((tm, D), lambda i: (i, 0)), pl.BlockSpec((1, D), lambda i: (0, 0))],
        out_specs=pl.BlockSpec((tm, D), lambda i: (i, 0)),
        out_shape=jax.ShapeDtypeStruct((T, D), F32),
        compiler_params=_params("parallel"),
    )(x, g)


def _ffn_kernel(x_ref, xh_ref, g_ref, wu_ref, wg_ref, cu_ref, cg_ref, wd_ref, o_ref, xn_ref, *,
                blocks_per_seq):
    @pl.when(pl.program_id(1) == 0)
    def _():
        _norm_with_halo(x_ref, xh_ref, g_ref, xn_ref, pl.program_id(0) % blocks_per_seq == 0)
        o_ref[...] = x_ref[...]

    xn = xn_ref[...]
    groups = _column_groups(wu_ref.shape[1])
    hidden = [(jnp.dot(xn, wu_ref[:, cs], preferred_element_type=F32),
               jnp.dot(xn, wg_ref[:, cs], preferred_element_type=F32)) for cs in groups]
    acts = []
    for cs, (hu, hg) in zip(groups, hidden):
        gt = _conv_taps(hg, cg_ref, cs)
        acts.append((gt * _sigmoid(gt) * _conv_taps(hu, cu_ref, cs)).astype(BF16))
    o_ref[...] += jnp.dot(jnp.concatenate(acts, axis=1), wd_ref[...], preferred_element_type=F32)


def _ffn(x, g, w_up, conv_w, w_down, layer, seq, tm=1024, tn=512):
    T, D = x.shape
    taps = conv_w.shape[0]
    nj = w_down.shape[1] // tn
    halo_blocks = tm // HALO
    once = pl.Buffered(1)
    return pl.pallas_call(
        functools.partial(_ffn_kernel, blocks_per_seq=seq // tm),
        grid=(T // tm, nj),
        in_specs=[
            pl.BlockSpec((tm, D), lambda i, j: (i, 0)),
            pl.BlockSpec((HALO, D), lambda i, j: (jnp.maximum(i * halo_blocks - 1, 0), 0)),
            pl.BlockSpec((1, D), lambda i, j: (0, 0)),
            pl.BlockSpec((None, D, tn), lambda i, j: (layer, 0, j)),
            pl.BlockSpec((None, D, tn), lambda i, j: (layer, 0, nj + j)),
            pl.BlockSpec((taps, tn), lambda i, j: (0, j)),
            pl.BlockSpec((taps, tn), lambda i, j: (0, nj + j)),
            pl.BlockSpec((None, tn, D), lambda i, j: (layer, j, 0)),
        ],
        out_specs=pl.BlockSpec((tm, D), lambda i, j: (i, 0), pipeline_mode=once),
        out_shape=jax.ShapeDtypeStruct((T, D), F32),
        scratch_shapes=[pltpu.VMEM((tm + HALO, D), BF16)],
        compiler_params=_params("parallel", "arbitrary", vmem_limit_bytes=VMEM_LIMIT_LARGE_BYTES),
    )(x, x, g, w_up, w_up, conv_w, conv_w, w_down)


FOX_W = HEADS * HEAD_DIM
GDN_W = HEADS * HEAD_DIM
ML_QK_W = HEADS * ML_QK_DIM
ML_V_W = HEADS * HEAD_DIM
ML_BASE = 3 * FOX_W
Z_BASE = ML_BASE + 2 * ML_QK_W + ML_V_W
GATE_BASE = Z_BASE + GDN_W + ML_V_W
FOX_HEADS_PER_STEP = 4


def _in_projection_columns(d_model):
    widths = [("fox_q", FOX_W), ("fox_k", FOX_W), ("fox_v", FOX_W), ("fox_f", HEADS),
              ("gdn_qkv", 3 * GDN_W), ("gdn_z", GDN_W), ("gdn_b", HEADS), ("gdn_a", HEADS),
              ("ml_q", ML_QK_W), ("ml_k", ML_QK_W), ("ml_v", ML_V_W), ("ml_i", HEADS),
              ("ml_f", HEADS), ("ml_o", ML_V_W), ("gate", N_BRANCH * d_model)]
    off, start = {}, 0
    for name, width in widths:
        off[name] = (start, start + width)
        start += width
    return off


def _pad_lanes(v):
    return jnp.pad(v, (0, LANES - v.shape[0])).reshape(1, LANES)


def _layer(x, layer, w, p, batch, seq):
    T, D = x.shape
    g_mix = p["norm_mix_g"].reshape(1, D)
    scale_p = jnp.concatenate([
        jnp.full((FOX_W,), HEAD_DIM ** -0.5 * LOG2E, F32), jnp.ones((2 * FOX_W,), F32),
        jnp.full((ML_QK_W,), ML_QK_DIM ** -0.5, F32),
        jnp.ones((w["proj"].shape[2] - 3 * FOX_W - ML_QK_W,), F32)]).reshape(1, -1)
    proj = _norm_matmul(x, g_mix, w["proj"], layer, scale_p, BF16, tn=1024)
    zeros8 = jnp.zeros((HEADS,), F32)
    bias_s = _pad_lanes(jnp.concatenate([p["fox_f_bias"], zeros8, p["gdn_dt_bias"],
                                         p["ml_i_bias"], p["ml_f_bias"]]))
    alog_s = _pad_lanes(jnp.concatenate([zeros8, zeros8, p["gdn_a_log"]]))
    col_small, row_small = _small_gates(x, g_mix, w["small"], layer, bias_s, alog_s, seq)
    f_cum = row_small[:, SM_FOX_F:SM_FOX_F + HEADS, :].reshape(batch, seq // CHUNK, HEADS, CHUNK)
    f_cum = f_cum.transpose(0, 2, 1, 3).reshape(batch, HEADS, seq)
    gdn_qkv = _proj_conv(x, g_mix, w["gdn_qkv"], layer, p["gdn_conv_w"], seq)

    hps = FOX_HEADS_PER_STEP
    y_fox = _fox_attention(proj, f_cum, batch, seq, q_col=0, k_col=HEADS // hps,
                           v_col=2 * HEADS // hps, hps=hps)
    y_gdn = _gdn(gdn_qkv, proj, col_small, row_small, p["gdn_norm_g"].reshape(1, HEAD_DIM),
                 batch, seq, z_col=Z_BASE // GDN_W)
    y_ml = _mlstm(proj, proj, col_small, row_small, p["ml_norm_g"].reshape(1, HEAD_DIM),
                  batch, seq, q_col=ML_BASE // ML_QK_W, k_col=ML_BASE // ML_QK_W + 1,
                  v_col=(ML_BASE + 2 * ML_QK_W) // ML_V_W, og_col=(Z_BASE + GDN_W) // ML_V_W)

    tn = 512
    y = _merge(y_fox, y_gdn, y_ml, w["fox_proj"], w["gdn_proj"], w["ml_proj"], layer, proj,
               p["gate_bias"].reshape(1, -1), D, gate_col=GATE_BASE // tn, tn=tn)
    x = _matmul_residual(y, w["out"], layer, x)
    return _ffn(x, p["norm_ffn_g"].reshape(1, D), w["up"], p["ffn_conv_w"], w["down"], layer, seq)


def kernel(x, norm_mix_g, w_in, fox_f_bias, gdn_conv_w, gdn_a_log, gdn_dt_bias, gdn_norm_g,
           ml_i_bias, ml_f_bias, ml_norm_g, gate_bias, w_fox_proj, w_gdn_proj, w_ml_proj,
           w_out, norm_ffn_g, w_up, ffn_conv_w, w_down, norm_final_g):
    batch, seq, d_model = x.shape
    depth = w_in.shape[0]
    off = _in_projection_columns(d_model)
    cols = lambda a, b: w_in[:, :, off[a][0]:off[b][1]].astype(BF16)
    w_small = jnp.concatenate([cols("fox_f", "fox_f"), cols("gdn_b", "gdn_a"), cols("ml_i", "ml_f")],
                              axis=2)
    w = dict(
        proj=jnp.concatenate([cols("fox_q", "fox_v"), cols("ml_q", "ml_v"), cols("gdn_z", "gdn_z"),
                              cols("ml_o", "ml_o"), cols("gate", "gate")], axis=2),
        small=jnp.pad(w_small, ((0, 0), (0, 0), (0, LANES - w_small.shape[2]))),
        gdn_qkv=cols("gdn_qkv", "gdn_qkv"),
        fox_proj=w_fox_proj.astype(BF16), gdn_proj=w_gdn_proj.astype(BF16),
        ml_proj=w_ml_proj.astype(BF16), out=w_out.astype(BF16), up=w_up.astype(BF16),
        down=w_down.astype(BF16))
    small = dict(norm_mix_g=norm_mix_g, fox_f_bias=fox_f_bias, gdn_conv_w=gdn_conv_w,
                 gdn_a_log=gdn_a_log, gdn_dt_bias=gdn_dt_bias, gdn_norm_g=gdn_norm_g,
                 ml_i_bias=ml_i_bias, ml_f_bias=ml_f_bias, ml_norm_g=ml_norm_g,
                 gate_bias=gate_bias, norm_ffn_g=norm_ffn_g, ffn_conv_w=ffn_conv_w)
    h = x.reshape(batch * seq, d_model)
    for layer in range(depth):
        h = _layer(h, layer, w, {k: v[layer] for k, v in small.items()}, batch, seq)
    return _final_norm(h, norm_final_g.reshape(1, d_model)).reshape(batch, seq, d_model)
```

```python
import functools

import jax
import jax.numpy as jnp
from jax import lax
from jax.experimental import pallas as pl
from jax.experimental.pallas import tpu as pltpu

F32 = jnp.float32
BF16 = jnp.bfloat16
HIGHEST = lax.Precision.HIGHEST

EPS = 1e-6
HEADS = 8
HEAD_DIM = 128
ML_QK_DIM = 64
CHUNK = 64
GDN_CONV = 4
FFN_CONV = 3
N_BRANCH = 3
LANES = 128
MXU_WIDTH = 256
HALO = 16

SM_FOX_F, SM_GDN_B, SM_GDN_A, SM_ML_I, SM_ML_F = 0, 8, 16, 24, 32

V7X_VMEM_BYTES = 64 * 1024 * 1024
VMEM_LIMIT_BYTES = V7X_VMEM_BYTES * 3 // 4
VMEM_LIMIT_LARGE_BYTES = V7X_VMEM_BYTES * 7 // 8


def _params(*semantics, vmem_limit_bytes=VMEM_LIMIT_BYTES):
    return pltpu.CompilerParams(dimension_semantics=semantics, vmem_limit_bytes=vmem_limit_bytes)


def _rmsnorm(x, g):
    return x * lax.rsqrt(jnp.mean(x * x, axis=-1, keepdims=True) + EPS) * g


def _sigmoid(x):
    return 1.0 / (1.0 + jnp.exp(-x))


def _mm_hi(a, b):
    return jnp.dot(a, b, precision=HIGHEST, preferred_element_type=F32)


def _norm_matmul_kernel(x_ref, g_ref, w_ref, cs_ref, o_ref, xn_ref):
    @pl.when(pl.program_id(1) == 0)
    def _():
        xn_ref[...] = _rmsnorm(x_ref[...], g_ref[...]).astype(BF16)

    acc = jnp.dot(xn_ref[...], w_ref[...], preferred_element_type=F32)
    o_ref[...] = (acc * cs_ref[...]).astype(o_ref.dtype)


def _norm_matmul(x, g, w_all, layer, colscale, out_dtype, tm=1024, tn=512):
    T, D = x.shape
    N = w_all.shape[2]
    return pl.pallas_call(
        _norm_matmul_kernel,
        grid=(T // tm, N // tn),
        in_specs=[
            pl.BlockSpec((tm, D), lambda i, j: (i, 0)),
            pl.BlockSpec((1, D), lambda i, j: (0, 0)),
            pl.BlockSpec((None, D, tn), lambda i, j: (layer, 0, j)),
            pl.BlockSpec((1, tn), lambda i, j: (0, j)),
        ],
        out_specs=pl.BlockSpec((tm, tn), lambda i, j: (i, j)),
        out_shape=jax.ShapeDtypeStruct((T, N), out_dtype),
        scratch_shapes=[pltpu.VMEM((tm, D), BF16)],
        compiler_params=_params("parallel", "arbitrary"),
    )(x, g, w_all, colscale)


def _small_kernel(x_ref, g_ref, w_ref, bias_ref, alog_ref, o_ref, rows_ref, carry_ref, *,
                  blocks_per_seq):
    i = pl.program_id(0)
    tm = x_ref.shape[0]

    @pl.when(i % blocks_per_seq == 0)
    def _():
        carry_ref[...] = jnp.zeros_like(carry_ref)

    xn = _rmsnorm(x_ref[...], g_ref[...]).astype(BF16)
    t = jnp.dot(xn, w_ref[...], preferred_element_type=F32) + bias_ref[...]
    lane = lax.broadcasted_iota(jnp.int32, (tm, LANES), 1)
    e = jnp.log1p(jnp.exp(-jnp.abs(t)))
    logsig = jnp.minimum(t, 0.0) - e
    softplus = jnp.maximum(t, 0.0) + e
    g_decay = -jnp.exp(alog_ref[...]) * softplus
    is_logsig = (lane < SM_GDN_B) | ((lane >= SM_ML_F) & (lane < SM_ML_F + HEADS))
    val = jnp.where(is_logsig, logsig,
                    jnp.where(lane < SM_GDN_A, _sigmoid(t),
                              jnp.where(lane < SM_ML_I, g_decay, t)))

    r = lax.broadcasted_iota(jnp.int32, (CHUNK, CHUNK), 0)
    c = lax.broadcasted_iota(jnp.int32, (CHUNK, CHUNK), 1)
    tri = jnp.where(c <= r, 1.0, 0.0).astype(F32)
    chunk_sums = [_mm_hi(tri, val[ci * CHUNK:(ci + 1) * CHUNK, :]) for ci in range(tm // CHUNK)]
    offset = carry_ref[...]
    full_sums = []
    for cs in chunk_sums:
        full_sums.append(cs + offset)
        offset = offset + cs[CHUNK - 1:CHUNK, :]
    carry_ref[...] = offset
    cs_chunk = jnp.concatenate(chunk_sums, axis=0)
    cs_full = jnp.concatenate(full_sums, axis=0)

    is_chunk_cs = ((lane >= SM_GDN_A) & (lane < SM_ML_I)) | ((lane >= SM_ML_F) & (lane < SM_ML_F + HEADS))
    out = jnp.where(lane < SM_GDN_B, cs_full, jnp.where(is_chunk_cs, cs_chunk, val))
    o_ref[...] = out
    for ci in range(tm // CHUNK):
        rows_ref[ci] = out[ci * CHUNK:(ci + 1) * CHUNK, :].T


def _small_gates(x, g, w_all, layer, bias, alog, seq, tm=512):
    T, D = x.shape
    return pl.pallas_call(
        functools.partial(_small_kernel, blocks_per_seq=seq // tm),
        grid=(T // tm,),
        in_specs=[
            pl.BlockSpec((tm, D), lambda i: (i, 0)),
            pl.BlockSpec((1, D), lambda i: (0, 0)),
            pl.BlockSpec((None, D, LANES), lambda i: (layer, 0, 0)),
            pl.BlockSpec((1, LANES), lambda i: (0, 0)),
            pl.BlockSpec((1, LANES), lambda i: (0, 0)),
        ],
        out_specs=[pl.BlockSpec((tm, LANES), lambda i: (i, 0)),
                   pl.BlockSpec((tm // CHUNK, LANES, CHUNK), lambda i: (i, 0, 0))],
        out_shape=[jax.ShapeDtypeStruct((T, LANES), F32),
                   jax.ShapeDtypeStruct((T // CHUNK, LANES, CHUNK), F32)],
        scratch_shapes=[pltpu.VMEM((1, LANES), F32)],
        compiler_params=_params("arbitrary"),
    )(x, g, w_all, bias, alog)


def _conv_taps(h, c_ref, cols):
    taps = c_ref.shape[0]
    tm = h.shape[0] - HALO
    acc = None
    for t in range(taps):
        start = HALO - (taps - 1) + t
        term = h[start:start + tm, :] * c_ref[t:t + 1, cols]
        acc = term if acc is None else acc + term
    return acc


def _column_groups(width):
    return [slice(c * MXU_WIDTH, (c + 1) * MXU_WIDTH) for c in range(width // MXU_WIDTH)]


def _norm_with_halo(x_ref, xh_ref, g_ref, xn_ref, at_seq_start):
    xn_ref[HALO:, :] = _rmsnorm(x_ref[...], g_ref[...]).astype(BF16)
    keep = jnp.where(at_seq_start, 0.0, 1.0)
    xn_ref[:HALO, :] = (_rmsnorm(xh_ref[...], g_ref[...]) * keep).astype(BF16)


def _proj_conv_kernel(x_ref, xh_ref, g_ref, w_ref, c_ref, o_ref, xn_ref, *, blocks_per_seq):
    @pl.when(pl.program_id(1) == 0)
    def _():
        _norm_with_halo(x_ref, xh_ref, g_ref, xn_ref, pl.program_id(0) % blocks_per_seq == 0)

    xn = xn_ref[...]
    groups = _column_groups(w_ref.shape[1])
    hidden = [jnp.dot(xn, w_ref[:, cs], preferred_element_type=F32) for cs in groups]
    for cs, h in zip(groups, hidden):
        u = _conv_taps(h, c_ref, cs)
        o_ref[:, cs] = (u * _sigmoid(u)).astype(o_ref.dtype)


def _proj_conv(x, g, w_all, layer, conv_w, seq, tm=1024, tn=1024):
    T, D = x.shape
    taps = conv_w.shape[0]
    N = w_all.shape[2]
    halo_blocks = tm // HALO
    return pl.pallas_call(
        functools.partial(_proj_conv_kernel, blocks_per_seq=seq // tm),
        grid=(T // tm, N // tn),
        in_specs=[
            pl.BlockSpec((tm, D), lambda i, j: (i, 0)),
            pl.BlockSpec((HALO, D), lambda i, j: (jnp.maximum(i * halo_blocks - 1, 0), 0)),
            pl.BlockSpec((1, D), lambda i, j: (0, 0)),
            pl.BlockSpec((None, D, tn), lambda i, j: (layer, 0, j)),
            pl.BlockSpec((taps, tn), lambda i, j: (0, j)),
        ],
        out_specs=pl.BlockSpec((tm, tn), lambda i, j: (i, j)),
        out_shape=jax.ShapeDtypeStruct((T, N), BF16),
        scratch_shapes=[pltpu.VMEM((tm + HALO, D), BF16)],
        compiler_params=_params("parallel", "arbitrary"),
    )(x, x, g, w_all, conv_w)


NEG_BIG = -1e30


LOG2E = 1.4426950408889634


def _fox_kernel(q_ref, k_ref, v_ref, f_ref, o_ref, m_ref, acc_ref, *, tq, tk, hps):
    qi = pl.program_id(2)
    ones = jnp.ones((tk, HEAD_DIM), BF16)
    lane_tiles = tk // LANES
    m_ref[...] = jnp.full_like(m_ref, NEG_BIG)
    acc_ref[...] = jnp.zeros_like(acc_ref)

    def block(j, masked):
        start = pl.multiple_of(j * tk, tk)
        heads = [slice(h * HEAD_DIM, (h + 1) * HEAD_DIM) for h in range(hps)]
        scores = []
        for h, hs in enumerate(heads):
            s = lax.dot_general(q_ref[:, hs], k_ref[pl.ds(start, tk), hs],
                                (((1,), (1,)), ((), ())), preferred_element_type=F32)
            s = s - f_ref[h, pl.ds(j, 1), :] * LOG2E
            if masked:
                row = lax.broadcasted_iota(jnp.int32, (tq, tk), 0)
                col = lax.broadcasted_iota(jnp.int32, (tq, tk), 1)
                s = jnp.where(col <= row, s, NEG_BIG)
            scores.append([s[:, c * LANES:(c + 1) * LANES] for c in range(lane_tiles)])
        probs, alphas = [], []
        for h, tiles in enumerate(scores):
            m_prev = m_ref[h]
            m_tile = functools.reduce(jnp.maximum, tiles)
            m_new = jnp.maximum(m_prev, jnp.max(m_tile, axis=-1, keepdims=True))
            m_ref[h] = m_new
            alphas.append(jnp.exp2(m_prev - m_new))
            probs.append(jnp.concatenate([jnp.exp2(t - m_new).astype(BF16) for t in tiles], axis=1))
        for h, hs in enumerate(heads):
            v_ext = jnp.concatenate([v_ref[pl.ds(start, tk), hs], ones], axis=1)
            pv = jnp.dot(probs[h], v_ext, preferred_element_type=F32)
            acc_ref[h] = jnp.concatenate([alphas[h], alphas[h]], axis=1) * acc_ref[h] + pv

    def full_block(j, carry):
        block(j, masked=False)
        return carry

    lax.fori_loop(0, qi, full_block, 0)
    block(qi, masked=True)
    for h in range(hps):
        acc = acc_ref[h]
        o_ref[:, h * HEAD_DIM:(h + 1) * HEAD_DIM] = (
            acc[:, :HEAD_DIM] / acc[:, HEAD_DIM:]).astype(o_ref.dtype)


def _fox_attention(qkv, f_cum, batch, seq, *, q_col, k_col, v_col, tq=512, hps=2):
    T = qkv.shape[0]
    nq = seq // tq
    width = hps * HEAD_DIM
    f_blocks = f_cum.reshape(batch, HEADS, nq, tq)
    return pl.pallas_call(
        functools.partial(_fox_kernel, tq=tq, tk=tq, hps=hps),
        grid=(batch, HEADS // hps, nq),
        in_specs=[
            pl.BlockSpec((tq, width), lambda b, h, qi: (b * nq + qi, q_col + h)),
            pl.BlockSpec((seq, width), lambda b, h, qi: (b, k_col + h)),
            pl.BlockSpec((seq, width), lambda b, h, qi: (b, v_col + h)),
            pl.BlockSpec((None, hps, nq, tq), lambda b, h, qi: (b, h, 0, 0)),
        ],
        out_specs=pl.BlockSpec((tq, width), lambda b, h, qi: (b * nq + qi, h)),
        out_shape=jax.ShapeDtypeStruct((T, HEADS * HEAD_DIM), BF16),
        scratch_shapes=[pltpu.VMEM((hps, tq, LANES), F32),
                        pltpu.VMEM((hps, tq, 2 * HEAD_DIM), F32)],
        compiler_params=_params("parallel", "parallel", "arbitrary"),
    )(qkv, qkv, qkv, f_blocks)


def _bmm(a, b):
    return jnp.einsum("gmk,gkn->gmn", a.astype(BF16), b.astype(BF16), preferred_element_type=F32)


def _bmm_nt(a, b):
    return jnp.einsum("gmk,gnk->gmn", a.astype(BF16), b.astype(BF16), preferred_element_type=F32)


def _bmm_tn(a, b):
    return jnp.einsum("gkm,gkn->gmn", a.astype(BF16), b.astype(BF16), preferred_element_type=F32)


def _unit_lower_inverse(m, eye, r, c):
    n = m.shape[-1]
    same_pair = (r >> 1) == (c >> 1)
    x = eye - jnp.where(same_pair, m, 0.0)
    b = 2
    while b < n:
        shift = b.bit_length() - 1
        joins = ((r >> (shift + 1)) == (c >> (shift + 1))) & ((r >> shift) != (c >> shift))
        x = x - _bmm(_bmm(x, jnp.where(joins, m, 0.0)), x)
        b *= 2
    return x


STEP_CHUNKS = 4


def _chunk_head_stack(ref, width=HEAD_DIM):
    return jnp.stack([ref[ci * CHUNK:(ci + 1) * CHUNK, h * width:(h + 1) * width]
                      for ci in range(STEP_CHUNKS) for h in range(HEADS)])


def _chunk_head_cols(col, start):
    return jnp.stack([col[ci * CHUNK:(ci + 1) * CHUNK, start + h:start + h + 1]
                      for ci in range(STEP_CHUNKS) for h in range(HEADS)])


def _chunk_head_rows(row_ref):
    return jnp.concatenate([row_ref[ci] for ci in range(STEP_CHUNKS)], axis=0)[:, None, :]


def _store_heads(o_ref, y):
    for ci in range(STEP_CHUNKS):
        for h in range(HEADS):
            o_ref[ci * CHUNK:(ci + 1) * CHUNK, h * HEAD_DIM:(h + 1) * HEAD_DIM] = y[ci * HEADS + h]


def _gdn_kernel(q_ref, k_ref, v_ref, z_ref, col_ref, rowg_ref, ng_ref, o_ref, s_ref):
    @pl.when(pl.program_id(1) == 0)
    def _():
        s_ref[...] = jnp.zeros_like(s_ref)

    L = CHUNK
    r = lax.broadcasted_iota(jnp.int32, (1, L, L), 1)
    c = lax.broadcasted_iota(jnp.int32, (1, L, L), 2)
    tri = c <= r
    eye = jnp.where(c == r, 1.0, 0.0).astype(F32)
    col = col_ref[...]
    q = _chunk_head_stack(q_ref).astype(F32)
    k = _chunk_head_stack(k_ref).astype(F32)
    v = _chunk_head_stack(v_ref).astype(F32)
    qn = q * lax.rsqrt(jnp.sum(q * q, axis=-1, keepdims=True) + EPS) * (HEAD_DIM ** -0.5)
    kn = k * lax.rsqrt(jnp.sum(k * k, axis=-1, keepdims=True) + EPS)
    beta = _chunk_head_cols(col, SM_GDN_B)
    gc = _chunk_head_cols(col, SM_GDN_A)
    gr = _chunk_head_rows(rowg_ref)
    g_last = gc[:, L - 1:L, :]
    decay = jnp.where(tri, jnp.exp(jnp.where(tri, gc - gr, 0.0)), 0.0)
    eg = jnp.exp(gc)
    kb = kn * beta
    m = jnp.where(c < r, _bmm_nt(kb, kn) * decay, 0.0)
    t_inv = _unit_lower_inverse(m, eye, r, c)
    x = _bmm(t_inv, jnp.concatenate([v * beta, kb * eg], axis=-1))
    u = x[:, :, :HEAD_DIM]
    w = x[:, :, HEAD_DIM:]
    a_qk = _bmm_nt(qn, kn) * decay
    qg = qn * eg
    kg = kn * jnp.exp(g_last - gc)
    carry = jnp.exp(g_last)
    state = s_ref[...]
    outs = []
    for ci in range(STEP_CHUNKS):
        hs = slice(ci * HEADS, (ci + 1) * HEADS)
        v_new = u[hs] - _bmm(w[hs], state)
        outs.append(_bmm(qg[hs], state) + _bmm(a_qk[hs], v_new))
        state = state * carry[hs] + _bmm_tn(kg[hs], v_new)
    s_ref[...] = state
    o = jnp.concatenate(outs, axis=0)
    z = _chunk_head_stack(z_ref).astype(F32)
    _store_heads(o_ref, (_rmsnorm(o, ng_ref[...]) * (z * _sigmoid(z))).astype(o_ref.dtype))


def _gdn(qkv, z_src, col_small, row_small, norm_g, batch, seq, *, z_col):
    T = qkv.shape[0]
    rows = STEP_CHUNKS * CHUNK
    ns = seq // rows
    width = HEADS * HEAD_DIM
    row = lambda b, s: b * ns + s
    return pl.pallas_call(
        _gdn_kernel,
        grid=(batch, ns),
        in_specs=[
            pl.BlockSpec((rows, width), lambda b, s: (row(b, s), 0)),
            pl.BlockSpec((rows, width), lambda b, s: (row(b, s), 1)),
            pl.BlockSpec((rows, width), lambda b, s: (row(b, s), 2)),
            pl.BlockSpec((rows, width), lambda b, s: (row(b, s), z_col)),
            pl.BlockSpec((rows, LANES), lambda b, s: (row(b, s), 0)),
            pl.BlockSpec((STEP_CHUNKS, HEADS, CHUNK), lambda b, s: (row(b, s), SM_GDN_A // HEADS, 0)),
            pl.BlockSpec((1, HEAD_DIM), lambda b, s: (0, 0)),
        ],
        out_specs=pl.BlockSpec((rows, width), lambda b, s: (row(b, s), 0)),
        out_shape=jax.ShapeDtypeStruct((T, width), BF16),
        scratch_shapes=[pltpu.VMEM((HEADS, HEAD_DIM, HEAD_DIM), F32)],
        compiler_params=_params("parallel", "arbitrary"),
    )(qkv, qkv, qkv, z_src, col_small, row_small, norm_g)


def _mlstm_kernel(q_ref, k_ref, v_ref, og_ref, col_ref, rowi_ref, rowb_ref, ng_ref, o_ref,
                  c_ref, m_ref):
    @pl.when(pl.program_id(1) == 0)
    def _():
        c_ref[...] = jnp.zeros_like(c_ref)
        m_ref[...] = jnp.zeros_like(m_ref)

    L = CHUNK
    r = lax.broadcasted_iota(jnp.int32, (1, L, L), 1)
    c = lax.broadcasted_iota(jnp.int32, (1, L, L), 2)
    tri = c <= r
    col = col_ref[...]
    q = _chunk_head_stack(q_ref, ML_QK_DIM).astype(F32)
    k = _chunk_head_stack(k_ref, ML_QK_DIM).astype(F32)
    v = _chunk_head_stack(v_ref)
    v = jnp.concatenate([v, jnp.ones_like(v)], axis=-1)
    ic = _chunk_head_cols(col, SM_ML_I)
    bc = _chunk_head_cols(col, SM_ML_F)
    ir = _chunk_head_rows(rowi_ref)
    br = _chunk_head_rows(rowb_ref)
    dm = jnp.where(tri, bc - br + ir, -jnp.inf)
    dmax = jnp.max(dm, axis=-1, keepdims=True)
    qk = _bmm_nt(q, k)
    m_state = m_ref[...]
    m_rows, m_lasts = [], []
    for ci in range(STEP_CHUNKS):
        hs = slice(ci * HEADS, (ci + 1) * HEADS)
        m_r = jnp.maximum(bc[hs] + m_state, dmax[hs])
        m_rows.append(m_r)
        m_state = m_r[:, L - 1:L, :]
        m_lasts.append(m_state)
    m_r = jnp.concatenate(m_rows, axis=0)
    m_last = jnp.concatenate(m_lasts, axis=0)
    m_prev = jnp.concatenate([m_ref[...]] + m_lasts[:-1], axis=0)
    w_inter = jnp.exp(bc + m_prev - m_r)
    sm = jnp.exp(dm - m_r) * qk
    intra = _bmm(sm, v)
    wk = jnp.exp(bc[:, L - 1:L, :] - bc + ic - m_last) * k
    kv = _bmm_tn(wk, v)
    w_last = w_inter[:, L - 1:L, :]
    floor = jnp.exp(-m_r)
    c_state = c_ref[...]
    outs = []
    for ci in range(STEP_CHUNKS):
        hs = slice(ci * HEADS, (ci + 1) * HEADS)
        num_den = w_inter[hs] * _bmm(q[hs], c_state) + intra[hs]
        den = jnp.maximum(jnp.abs(num_den[:, :, HEAD_DIM:]), floor[hs])
        outs.append(num_den[:, :, :HEAD_DIM] / den)
        c_state = w_last[hs] * c_state + kv[hs]
    c_ref[...] = c_state
    m_ref[...] = m_state
    h_tilde = jnp.concatenate(outs, axis=0)
    og = _chunk_head_stack(og_ref).astype(F32)
    _store_heads(o_ref, _rmsnorm(_sigmoid(og) * h_tilde, ng_ref[...]).astype(o_ref.dtype))


def _mlstm(qkv, og_src, col_small, row_small, norm_g, batch, seq, *, q_col, k_col, v_col, og_col):
    T = qkv.shape[0]
    rows = STEP_CHUNKS * CHUNK
    ns = seq // rows
    qk_width = HEADS * ML_QK_DIM
    width = HEADS * HEAD_DIM
    row = lambda b, s: b * ns + s
    return pl.pallas_call(
        _mlstm_kernel,
        grid=(batch, ns),
        in_specs=[
            pl.BlockSpec((rows, qk_width), lambda b, s: (row(b, s), q_col)),
            pl.BlockSpec((rows, qk_width), lambda b, s: (row(b, s), k_col)),
            pl.BlockSpec((rows, width), lambda b, s: (row(b, s), v_col)),
            pl.BlockSpec((rows, width), lambda b, s: (row(b, s), og_col)),
            pl.BlockSpec((rows, LANES), lambda b, s: (row(b, s), 0)),
            pl.BlockSpec((STEP_CHUNKS, HEADS, CHUNK), lambda b, s: (row(b, s), SM_ML_I // HEADS, 0)),
            pl.BlockSpec((STEP_CHUNKS, HEADS, CHUNK), lambda b, s: (row(b, s), SM_ML_F // HEADS, 0)),
            pl.BlockSpec((1, HEAD_DIM), lambda b, s: (0, 0)),
        ],
        out_specs=pl.BlockSpec((rows, width), lambda b, s: (row(b, s), 0)),
        out_shape=jax.ShapeDtypeStruct((T, width), BF16),
        scratch_shapes=[pltpu.VMEM((HEADS, ML_QK_DIM, 2 * HEAD_DIM), F32),
                        pltpu.VMEM((HEADS, 1, 1), F32)],
        compiler_params=_params("parallel", "arbitrary"),
    )(qkv, qkv, qkv, og_src, col_small, row_small, row_small, norm_g)


def _merge_kernel(yf_ref, yg_ref, ym_ref, wf_ref, wg_ref, wm_ref, g0_ref, g1_ref, g2_ref,
                  b0_ref, b1_ref, b2_ref, o_ref):
    def branch(y_ref, w_ref, g_ref, b_ref):
        return _sigmoid(g_ref[...] + b_ref[...]) * jnp.dot(y_ref[...], w_ref[...],
                                                          preferred_element_type=F32)

    y = (branch(yf_ref, wf_ref, g0_ref, b0_ref) + branch(yg_ref, wg_ref, g1_ref, b1_ref)
         + branch(ym_ref, wm_ref, g2_ref, b2_ref))
    o_ref[...] = y.astype(o_ref.dtype)


def _merge(y_fox, y_gdn, y_ml, w_fox, w_gdn, w_ml, layer, gate_src, gate_bias, d_model, *,
           gate_col, tm=1024, tn=512):
    T, K = y_fox.shape
    nj = d_model // tn
    y_spec = pl.BlockSpec((tm, K), lambda i, j: (i, 0))
    w_spec = pl.BlockSpec((None, K, tn), lambda i, j: (layer, 0, j))
    gate_specs = [pl.BlockSpec((tm, tn), lambda i, j, n=n: (i, gate_col + n * nj + j))
                  for n in range(N_BRANCH)]
    bias_specs = [pl.BlockSpec((1, tn), lambda i, j, n=n: (0, n * nj + j)) for n in range(N_BRANCH)]
    return pl.pallas_call(
        _merge_kernel,
        grid=(T // tm, nj),
        in_specs=[y_spec, y_spec, y_spec, w_spec, w_spec, w_spec] + gate_specs + bias_specs,
        out_specs=pl.BlockSpec((tm, tn), lambda i, j: (i, j)),
        out_shape=jax.ShapeDtypeStruct((T, d_model), BF16),
        compiler_params=_params("parallel", "parallel"),
    )(y_fox, y_gdn, y_ml, w_fox, w_gdn, w_ml, gate_src, gate_src, gate_src,
      gate_bias, gate_bias, gate_bias)


def _matmul_residual_kernel(a_ref, w_ref, x_ref, o_ref):
    o_ref[...] = x_ref[...] + jnp.dot(a_ref[...], w_ref[...], preferred_element_type=F32)


def _matmul_residual(a, w_all, layer, x, tm=1024, tn=512):
    T, K = a.shape
    N = w_all.shape[2]
    return pl.pallas_call(
        _matmul_residual_kernel,
        grid=(T // tm, N // tn),
        in_specs=[
            pl.BlockSpec((tm, K), lambda i, j: (i, 0)),
            pl.BlockSpec((None, K, tn), lambda i, j: (layer, 0, j)),
            pl.BlockSpec((tm, tn), lambda i, j: (i, j)),
        ],
        out_specs=pl.BlockSpec((tm, tn), lambda i, j: (i, j)),
        out_shape=jax.ShapeDtypeStruct((T, N), F32),
        compiler_params=_params("parallel", "parallel"),
    )(a, w_all, x)


def _final_norm_kernel(x_ref, g_ref, o_ref):
    o_ref[...] = _rmsnorm(x_ref[...], g_ref[...])


def _final_norm(x, g, tm=512):
    T, D = x.shape
    return pl.pallas_call(
        _final_norm_kernel,
        grid=(T // tm,),
        in_specs=[pl.BlockSpec((tm, D), lambda i: (i, 0)), pl.BlockSpec((1, D), lambda i: (0, 0))],
        out_specs=pl.BlockSpec((tm, D), lambda i: (i, 0)),
        out_shape=jax.ShapeDtypeStruct((T, D), F32),
        compiler_params=_params("parallel"),
    )(x, g)


def _ffn_kernel(x_ref, xh_ref, g_ref, wu_ref, wg_ref, cu_ref, cg_ref, wd_ref, o_ref, xn_ref, *,
                blocks_per_seq):
    @pl.when(pl.program_id(1) == 0)
    def _():
        _norm_with_halo(x_ref, xh_ref, g_ref, xn_ref, pl.program_id(0) % blocks_per_seq == 0)
        o_ref[...] = x_ref[...]

    xn = xn_ref[...]
    groups = _column_groups(wu_ref.shape[1])
    hidden = [(jnp.dot(xn, wu_ref[:, cs], preferred_element_type=F32),
               jnp.dot(xn, wg_ref[:, cs], preferred_element_type=F32)) for cs in groups]
    acts = []
    for cs, (hu, hg) in zip(groups, hidden):
        gt = _conv_taps(hg, cg_ref, cs)
        acts.append((gt * _sigmoid(gt) * _conv_taps(hu, cu_ref, cs)).astype(BF16))
    o_ref[...] += jnp.dot(jnp.concatenate(acts, axis=1), wd_ref[...], preferred_element_type=F32)


def _ffn(x, g, w_up, conv_w, w_down, layer, seq, tm=1024, tn=512):
    T, D = x.shape
    taps = conv_w.shape[0]
    nj = w_down.shape[1] // tn
    halo_blocks = tm // HALO
    once = pl.Buffered(1)
    return pl.pallas_call(
        functools.partial(_ffn_kernel, blocks_per_seq=seq // tm),
        grid=(T // tm, nj),
        in_specs=[
            pl.BlockSpec((tm, D), lambda i, j: (i, 0)),
            pl.BlockSpec((HALO, D), lambda i, j: (jnp.maximum(i * halo_blocks - 1, 0), 0)),
            pl.BlockSpec((1, D), lambda i, j: (0, 0)),
            pl.BlockSpec((None, D, tn), lambda i, j: (layer, 0, j)),
            pl.BlockSpec((None, D, tn), lambda i, j: (layer, 0, nj + j)),
            pl.BlockSpec((taps, tn), lambda i, j: (0, j)),
            pl.BlockSpec((taps, tn), lambda i, j: (0, nj + j)),
            pl.BlockSpec((None, tn, D), lambda i, j: (layer, j, 0)),
        ],
        out_specs=pl.BlockSpec((tm, D), lambda i, j: (i, 0), pipeline_mode=once),
        out_shape=jax.ShapeDtypeStruct((T, D), F32),
        scratch_shapes=[pltpu.VMEM((tm + HALO, D), BF16)],
        compiler_params=_params("parallel", "arbitrary", vmem_limit_bytes=VMEM_LIMIT_LARGE_BYTES),
    )(x, x, g, w_up, w_up, conv_w, conv_w, w_down)


FOX_W = HEADS * HEAD_DIM
GDN_W = HEADS * HEAD_DIM
ML_QK_W = HEADS * ML_QK_DIM
ML_V_W = HEADS * HEAD_DIM
ML_BASE = 3 * FOX_W
Z_BASE = ML_BASE + 2 * ML_QK_W + ML_V_W
GATE_BASE = Z_BASE + GDN_W + ML_V_W
FOX_HEADS_PER_STEP = 4


def _in_projection_columns(d_model):
    widths = [("fox_q", FOX_W), ("fox_k", FOX_W), ("fox_v", FOX_W), ("fox_f", HEADS),
              ("gdn_qkv", 3 * GDN_W), ("gdn_z", GDN_W), ("gdn_b", HEADS), ("gdn_a", HEADS),
              ("ml_q", ML_QK_W), ("ml_k", ML_QK_W), ("ml_v", ML_V_W), ("ml_i", HEADS),
              ("ml_f", HEADS), ("ml_o", ML_V_W), ("gate", N_BRANCH * d_model)]
    off, start = {}, 0
    for name, width in widths:
        off[name] = (start, start + width)
        start += width
    return off


def _pad_lanes(v):
    return jnp.pad(v, (0, LANES - v.shape[0])).reshape(1, LANES)


def _layer(x, layer, w, p, batch, seq):
    T, D = x.shape
    g_mix = p["norm_mix_g"].reshape(1, D)
    scale_p = jnp.concatenate([
        jnp.full((FOX_W,), HEAD_DIM ** -0.5 * LOG2E, F32), jnp.ones((2 * FOX_W,), F32),
        jnp.full((ML_QK_W,), ML_QK_DIM ** -0.5, F32),
        jnp.ones((w["proj"].shape[2] - 3 * FOX_W - ML_QK_W,), F32)]).reshape(1, -1)
    proj = _norm_matmul(x, g_mix, w["proj"], layer, scale_p, BF16, tn=1024)
    zeros8 = jnp.zeros((HEADS,), F32)
    bias_s = _pad_lanes(jnp.concatenate([p["fox_f_bias"], zeros8, p["gdn_dt_bias"],
                                         p["ml_i_bias"], p["ml_f_bias"]]))
    alog_s = _pad_lanes(jnp.concatenate([zeros8, zeros8, p["gdn_a_log"]]))
    col_small, row_small = _small_gates(x, g_mix, w["small"], layer, bias_s, alog_s, seq)
    f_cum = row_small[:, SM_FOX_F:SM_FOX_F + HEADS, :].reshape(batch, seq // CHUNK, HEADS, CHUNK)
    f_cum = f_cum.transpose(0, 2, 1, 3).reshape(batch, HEADS, seq)
    gdn_qkv = _proj_conv(x, g_mix, w["gdn_qkv"], layer, p["gdn_conv_w"], seq)

    hps = FOX_HEADS_PER_STEP
    y_fox = _fox_attention(proj, f_cum, batch, seq, q_col=0, k_col=HEADS // hps,
                           v_col=2 * HEADS // hps, hps=hps)
    y_gdn = _gdn(gdn_qkv, proj, col_small, row_small, p["gdn_norm_g"].reshape(1, HEAD_DIM),
                 batch, seq, z_col=Z_BASE // GDN_W)
    y_ml = _mlstm(proj, proj, col_small, row_small, p["ml_norm_g"].reshape(1, HEAD_DIM),
                  batch, seq, q_col=ML_BASE // ML_QK_W, k_col=ML_BASE // ML_QK_W + 1,
                  v_col=(ML_BASE + 2 * ML_QK_W) // ML_V_W, og_col=(Z_BASE + GDN_W) // ML_V_W)

    tn = 512
    y = _merge(y_fox, y_gdn, y_ml, w["fox_proj"], w["gdn_proj"], w["ml_proj"], layer, proj,
               p["gate_bias"].reshape(1, -1), D, gate_col=GATE_BASE // tn, tn=tn)
    x = _matmul_residual(y, w["out"], layer, x)
    return _ffn(x, p["norm_ffn_g"].reshape(1, D), w["up"], p["ffn_conv_w"], w["down"], layer, seq)


def kernel(x, norm_mix_g, w_in, fox_f_bias, gdn_conv_w, gdn_a_log, gdn_dt_bias, gdn_norm_g,
           ml_i_bias, ml_f_bias, ml_norm_g, gate_bias, w_fox_proj, w_gdn_proj, w_ml_proj,
           w_out, norm_ffn_g, w_up, ffn_conv_w, w_down, norm_final_g):
    batch, seq, d_model = x.shape
    depth = w_in.shape[0]
    off = _in_projection_columns(d_model)
    cols = lambda a, b: w_in[:, :, off[a][0]:off[b][1]].astype(BF16)
    w_small = jnp.concatenate([cols("fox_f", "fox_f"), cols("gdn_b", "gdn_a"), cols("ml_i", "ml_f")],
                              axis=2)
    w = dict(
        proj=jnp.concatenate([cols("fox_q", "fox_v"), cols("ml_q", "ml_v"), cols("gdn_z", "gdn_z"),
                              cols("ml_o", "ml_o"), cols("gate", "gate")], axis=2),
        small=jnp.pad(w_small, ((0, 0), (0, 0), (0, LANES - w_small.shape[2]))),
        gdn_qkv=cols("gdn_qkv", "gdn_qkv"),
        fox_proj=w_fox_proj.astype(BF16), gdn_proj=w_gdn_proj.astype(BF16),
        ml_proj=w_ml_proj.astype(BF16), out=w_out.astype(BF16), up=w_up.astype(BF16),
        down=w_down.astype(BF16))
    small = dict(norm_mix_g=norm_mix_g, fox_f_bias=fox_f_bias, gdn_conv_w=gdn_conv_w,
                 gdn_a_log=gdn_a_log, gdn_dt_bias=gdn_dt_bias, gdn_norm_g=gdn_norm_g,
                 ml_i_bias=ml_i_bias, ml_f_bias=ml_f_bias, ml_norm_g=ml_norm_g,
                 gate_bias=gate_bias, norm_ffn_g=norm_ffn_g, ffn_conv_w=ffn_conv_w)
    h = x.reshape(batch * seq, d_model)
    for layer in range(depth):
        h = _layer(h, layer, w, {k: v[layer] for k, v in small.items()}, batch, seq)
    return _final_norm(h, norm_final_g.reshape(1, d_model)).reshape(batch, seq, d_model)
```

```python
import functools

import jax
import jax.numpy as jnp
from jax import lax
from jax.experimental import pallas as pl
from jax.experimental.pallas import tpu as pltpu

F32 = jnp.float32
BF16 = jnp.bfloat16
HIGHEST = lax.Precision.HIGHEST

EPS = 1e-6
HEADS = 8
HEAD_DIM = 128
ML_QK_DIM = 64
CHUNK = 64
GDN_CONV = 4
FFN_CONV = 3
N_BRANCH = 3
LANES = 128
MXU_WIDTH = 256
HALO = 16

SM_FOX_F, SM_GDN_B, SM_GDN_A, SM_ML_I, SM_ML_F = 0, 8, 16, 24, 32

V7X_VMEM_BYTES = 64 * 1024 * 1024
VMEM_LIMIT_BYTES = V7X_VMEM_BYTES * 3 // 4
VMEM_LIMIT_LARGE_BYTES = V7X_VMEM_BYTES * 15 // 16


def _params(*semantics, vmem_limit_bytes=VMEM_LIMIT_BYTES):
    return pltpu.CompilerParams(dimension_semantics=semantics, vmem_limit_bytes=vmem_limit_bytes)


def _rmsnorm(x, g):
    return x * lax.rsqrt(jnp.mean(x * x, axis=-1, keepdims=True) + EPS) * g


def _sigmoid(x):
    return 1.0 / (1.0 + jnp.exp(-x))


def _mm_hi(a, b):
    return jnp.dot(a, b, precision=HIGHEST, preferred_element_type=F32)


def _norm_matmul_kernel(x_ref, g_ref, w_ref, cs_ref, o_ref, xn_ref):
    @pl.when(pl.program_id(1) == 0)
    def _():
        xn_ref[...] = _rmsnorm(x_ref[...], g_ref[...]).astype(BF16)

    acc = jnp.dot(xn_ref[...], w_ref[...], preferred_element_type=F32)
    o_ref[...] = (acc * cs_ref[...]).astype(o_ref.dtype)


def _norm_matmul(x, g, w_all, layer, colscale, out_dtype, tm=1024, tn=512):
    T, D = x.shape
    N = w_all.shape[2]
    return pl.pallas_call(
        _norm_matmul_kernel,
        grid=(T // tm, N // tn),
        in_specs=[
            pl.BlockSpec((tm, D), lambda i, j: (i, 0)),
            pl.BlockSpec((1, D), lambda i, j: (0, 0)),
            pl.BlockSpec((None, D, tn), lambda i, j: (layer, 0, j)),
            pl.BlockSpec((1, tn), lambda i, j: (0, j)),
        ],
        out_specs=pl.BlockSpec((tm, tn), lambda i, j: (i, j)),
        out_shape=jax.ShapeDtypeStruct((T, N), out_dtype),
        scratch_shapes=[pltpu.VMEM((tm, D), BF16)],
        compiler_params=_params("parallel", "arbitrary"),
    )(x, g, w_all, colscale)


def _small_kernel(x_ref, g_ref, w_ref, bias_ref, alog_ref, o_ref, rows_ref, carry_ref, *,
                  blocks_per_seq):
    i = pl.program_id(0)
    tm = x_ref.shape[0]

    @pl.when(i % blocks_per_seq == 0)
    def _():
        carry_ref[...] = jnp.zeros_like(carry_ref)

    xn = _rmsnorm(x_ref[...], g_ref[...]).astype(BF16)
    t = jnp.dot(xn, w_ref[...], preferred_element_type=F32) + bias_ref[...]
    lane = lax.broadcasted_iota(jnp.int32, (tm, LANES), 1)
    e = jnp.log1p(jnp.exp(-jnp.abs(t)))
    logsig = jnp.minimum(t, 0.0) - e
    softplus = jnp.maximum(t, 0.0) + e
    g_decay = -jnp.exp(alog_ref[...]) * softplus
    is_logsig = (lane < SM_GDN_B) | ((lane >= SM_ML_F) & (lane < SM_ML_F + HEADS))
    val = jnp.where(is_logsig, logsig,
                    jnp.where(lane < SM_GDN_A, _sigmoid(t),
                              jnp.where(lane < SM_ML_I, g_decay, t)))

    r = lax.broadcasted_iota(jnp.int32, (CHUNK, CHUNK), 0)
    c = lax.broadcasted_iota(jnp.int32, (CHUNK, CHUNK), 1)
    tri = jnp.where(c <= r, 1.0, 0.0).astype(F32)
    chunk_sums = [_mm_hi(tri, val[ci * CHUNK:(ci + 1) * CHUNK, :]) for ci in range(tm // CHUNK)]
    offset = carry_ref[...]
    full_sums = []
    for cs in chunk_sums:
        full_sums.append(cs + offset)
        offset = offset + cs[CHUNK - 1:CHUNK, :]
    carry_ref[...] = offset
    cs_chunk = jnp.concatenate(chunk_sums, axis=0)
    cs_full = jnp.concatenate(full_sums, axis=0)

    is_chunk_cs = ((lane >= SM_GDN_A) & (lane < SM_ML_I)) | ((lane >= SM_ML_F) & (lane < SM_ML_F + HEADS))
    out = jnp.where(lane < SM_GDN_B, cs_full, jnp.where(is_chunk_cs, cs_chunk, val))
    o_ref[...] = out
    for ci in range(tm // CHUNK):
        rows_ref[ci] = out[ci * CHUNK:(ci + 1) * CHUNK, :].T


def _small_gates(x, g, w_all, layer, bias, alog, seq, tm=512):
    T, D = x.shape
    return pl.pallas_call(
        functools.partial(_small_kernel, blocks_per_seq=seq // tm),
        grid=(T // tm,),
        in_specs=[
            pl.BlockSpec((tm, D), lambda i: (i, 0)),
            pl.BlockSpec((1, D), lambda i: (0, 0)),
            pl.BlockSpec((None, D, LANES), lambda i: (layer, 0, 0)),
            pl.BlockSpec((1, LANES), lambda i: (0, 0)),
            pl.BlockSpec((1, LANES), lambda i: (0, 0)),
        ],
        out_specs=[pl.BlockSpec((tm, LANES), lambda i: (i, 0)),
                   pl.BlockSpec((tm // CHUNK, LANES, CHUNK), lambda i: (i, 0, 0))],
        out_shape=[jax.ShapeDtypeStruct((T, LANES), F32),
                   jax.ShapeDtypeStruct((T // CHUNK, LANES, CHUNK), F32)],
        scratch_shapes=[pltpu.VMEM((1, LANES), F32)],
        compiler_params=_params("arbitrary"),
    )(x, g, w_all, bias, alog)


def _conv_taps(h, c_ref, cols):
    taps = c_ref.shape[0]
    tm = h.shape[0] - HALO
    acc = None
    for t in range(taps):
        start = HALO - (taps - 1) + t
        term = h[start:start + tm, :] * c_ref[t:t + 1, cols]
        acc = term if acc is None else acc + term
    return acc


def _column_groups(width):
    return [slice(c * MXU_WIDTH, (c + 1) * MXU_WIDTH) for c in range(width // MXU_WIDTH)]


def _norm_with_halo(x_ref, xh_ref, g_ref, xn_ref, at_seq_start):
    xn_ref[HALO:, :] = _rmsnorm(x_ref[...], g_ref[...]).astype(BF16)
    keep = jnp.where(at_seq_start, 0.0, 1.0)
    xn_ref[:HALO, :] = (_rmsnorm(xh_ref[...], g_ref[...]) * keep).astype(BF16)


def _proj_conv_kernel(x_ref, xh_ref, g_ref, w_ref, c_ref, o_ref, xn_ref, *, blocks_per_seq):
    @pl.when(pl.program_id(1) == 0)
    def _():
        _norm_with_halo(x_ref, xh_ref, g_ref, xn_ref, pl.program_id(0) % blocks_per_seq == 0)

    xn = xn_ref[...]
    groups = _column_groups(w_ref.shape[1])
    hidden = [jnp.dot(xn, w_ref[:, cs], preferred_element_type=F32) for cs in groups]
    for cs, h in zip(groups, hidden):
        u = _conv_taps(h, c_ref, cs)
        o_ref[:, cs] = (u * _sigmoid(u)).astype(o_ref.dtype)


def _proj_conv(x, g, w_all, layer, conv_w, seq, tm=1024, tn=1024):
    T, D = x.shape
    taps = conv_w.shape[0]
    N = w_all.shape[2]
    halo_blocks = tm // HALO
    return pl.pallas_call(
        functools.partial(_proj_conv_kernel, blocks_per_seq=seq // tm),
        grid=(T // tm, N // tn),
        in_specs=[
            pl.BlockSpec((tm, D), lambda i, j: (i, 0)),
            pl.BlockSpec((HALO, D), lambda i, j: (jnp.maximum(i * halo_blocks - 1, 0), 0)),
            pl.BlockSpec((1, D), lambda i, j: (0, 0)),
            pl.BlockSpec((None, D, tn), lambda i, j: (layer, 0, j)),
            pl.BlockSpec((taps, tn), lambda i, j: (0, j)),
        ],
        out_specs=pl.BlockSpec((tm, tn), lambda i, j: (i, j)),
        out_shape=jax.ShapeDtypeStruct((T, N), BF16),
        scratch_shapes=[pltpu.VMEM((tm + HALO, D), BF16)],
        compiler_params=_params("parallel", "arbitrary"),
    )(x, x, g, w_all, conv_w)


NEG_BIG = -1e30


LOG2E = 1.4426950408889634


def _fox_kernel(q_ref, k_ref, v_ref, f_ref, o_ref, m_ref, acc_ref, *, tq, tk, hps):
    qi = pl.program_id(2)
    ones = jnp.ones((tk, HEAD_DIM), BF16)
    lane_tiles = tk // LANES
    m_ref[...] = jnp.full_like(m_ref, NEG_BIG)
    acc_ref[...] = jnp.zeros_like(acc_ref)

    def block(j, masked):
        start = pl.multiple_of(j * tk, tk)
        heads = [slice(h * HEAD_DIM, (h + 1) * HEAD_DIM) for h in range(hps)]
        scores = []
        for h, hs in enumerate(heads):
            s = lax.dot_general(q_ref[:, hs], k_ref[pl.ds(start, tk), hs],
                                (((1,), (1,)), ((), ())), preferred_element_type=F32)
            s = s - f_ref[h, pl.ds(j, 1), :] * LOG2E
            if masked:
                row = lax.broadcasted_iota(jnp.int32, (tq, tk), 0)
                col = lax.broadcasted_iota(jnp.int32, (tq, tk), 1)
                s = jnp.where(col <= row, s, NEG_BIG)
            scores.append([s[:, c * LANES:(c + 1) * LANES] for c in range(lane_tiles)])
        probs, alphas = [], []
        for h, tiles in enumerate(scores):
            m_prev = m_ref[h]
            m_tile = functools.reduce(jnp.maximum, tiles)
            m_new = jnp.maximum(m_prev, jnp.max(m_tile, axis=-1, keepdims=True))
            m_ref[h] = m_new
            alphas.append(jnp.exp2(m_prev - m_new))
            probs.append(jnp.concatenate([jnp.exp2(t - m_new).astype(BF16) for t in tiles], axis=1))
        for h, hs in enumerate(heads):
            v_ext = jnp.concatenate([v_ref[pl.ds(start, tk), hs], ones], axis=1)
            pv = jnp.dot(probs[h], v_ext, preferred_element_type=F32)
            acc_ref[h] = jnp.concatenate([alphas[h], alphas[h]], axis=1) * acc_ref[h] + pv

    def full_block(j, carry):
        block(j, masked=False)
        return carry

    lax.fori_loop(0, qi, full_block, 0)
    block(qi, masked=True)
    for h in range(hps):
        acc = acc_ref[h]
        o_ref[:, h * HEAD_DIM:(h + 1) * HEAD_DIM] = (
            acc[:, :HEAD_DIM] / acc[:, HEAD_DIM:]).astype(o_ref.dtype)


def _fox_attention(qkv, f_cum, batch, seq, *, q_col, k_col, v_col, tq=512, hps=2):
    T = qkv.shape[0]
    nq = seq // tq
    width = hps * HEAD_DIM
    f_blocks = f_cum.reshape(batch, HEADS, nq, tq)
    resident = pl.Buffered(1)
    return pl.pallas_call(
        functools.partial(_fox_kernel, tq=tq, tk=tq, hps=hps),
        grid=(batch, HEADS // hps, nq),
        in_specs=[
            pl.BlockSpec((tq, width), lambda b, h, qi: (b * nq + qi, q_col + h)),
            pl.BlockSpec((seq, width), lambda b, h, qi: (b, k_col + h), pipeline_mode=resident),
            pl.BlockSpec((seq, width), lambda b, h, qi: (b, v_col + h), pipeline_mode=resident),
            pl.BlockSpec((None, hps, nq, tq), lambda b, h, qi: (b, h, 0, 0)),
        ],
        out_specs=pl.BlockSpec((tq, width), lambda b, h, qi: (b * nq + qi, h)),
        out_shape=jax.ShapeDtypeStruct((T, HEADS * HEAD_DIM), BF16),
        scratch_shapes=[pltpu.VMEM((hps, tq, LANES), F32),
                        pltpu.VMEM((hps, tq, 2 * HEAD_DIM), F32)],
        compiler_params=_params("parallel", "parallel", "arbitrary",
                                vmem_limit_bytes=VMEM_LIMIT_LARGE_BYTES),
    )(qkv, qkv, qkv, f_blocks)


def _bmm(a, b):
    return jnp.einsum("gmk,gkn->gmn", a.astype(BF16), b.astype(BF16), preferred_element_type=F32)


def _bmm_nt(a, b):
    return jnp.einsum("gmk,gnk->gmn", a.astype(BF16), b.astype(BF16), preferred_element_type=F32)


def _bmm_tn(a, b):
    return jnp.einsum("gkm,gkn->gmn", a.astype(BF16), b.astype(BF16), preferred_element_type=F32)


def _unit_lower_inverse(m, eye, r, c):
    n = m.shape[-1]
    same_pair = (r >> 1) == (c >> 1)
    x = eye - jnp.where(same_pair, m, 0.0)
    b = 2
    while b < n:
        shift = b.bit_length() - 1
        joins = ((r >> (shift + 1)) == (c >> (shift + 1))) & ((r >> shift) != (c >> shift))
        x = x - _bmm(_bmm(x, jnp.where(joins, m, 0.0)), x)
        b *= 2
    return x


STEP_CHUNKS = 4


def _chunk_head_stack(ref, width=HEAD_DIM):
    return jnp.stack([ref[ci * CHUNK:(ci + 1) * CHUNK, h * width:(h + 1) * width]
                      for ci in range(STEP_CHUNKS) for h in range(HEADS)])


def _chunk_head_cols(col, start):
    return jnp.stack([col[ci * CHUNK:(ci + 1) * CHUNK, start + h:start + h + 1]
                      for ci in range(STEP_CHUNKS) for h in range(HEADS)])


def _chunk_head_rows(row_ref):
    return jnp.concatenate([row_ref[ci] for ci in range(STEP_CHUNKS)], axis=0)[:, None, :]


def _store_heads(o_ref, y):
    for ci in range(STEP_CHUNKS):
        for h in range(HEADS):
            o_ref[ci * CHUNK:(ci + 1) * CHUNK, h * HEAD_DIM:(h + 1) * HEAD_DIM] = y[ci * HEADS + h]


def _gdn_kernel(q_ref, k_ref, v_ref, z_ref, col_ref, rowg_ref, ng_ref, o_ref, s_ref):
    @pl.when(pl.program_id(1) == 0)
    def _():
        s_ref[...] = jnp.zeros_like(s_ref)

    L = CHUNK
    r = lax.broadcasted_iota(jnp.int32, (1, L, L), 1)
    c = lax.broadcasted_iota(jnp.int32, (1, L, L), 2)
    tri = c <= r
    eye = jnp.where(c == r, 1.0, 0.0).astype(F32)
    col = col_ref[...]
    q = _chunk_head_stack(q_ref).astype(F32)
    k = _chunk_head_stack(k_ref).astype(F32)
    v = _chunk_head_stack(v_ref).astype(F32)
    qn = q * lax.rsqrt(jnp.sum(q * q, axis=-1, keepdims=True) + EPS) * (HEAD_DIM ** -0.5)
    kn = k * lax.rsqrt(jnp.sum(k * k, axis=-1, keepdims=True) + EPS)
    beta = _chunk_head_cols(col, SM_GDN_B)
    gc = _chunk_head_cols(col, SM_GDN_A)
    gr = _chunk_head_rows(rowg_ref)
    g_last = gc[:, L - 1:L, :]
    decay = jnp.where(tri, jnp.exp(jnp.where(tri, gc - gr, 0.0)), 0.0)
    eg = jnp.exp(gc)
    kb = kn * beta
    m = jnp.where(c < r, _bmm_nt(kb, kn) * decay, 0.0)
    t_inv = _unit_lower_inverse(m, eye, r, c)
    x = _bmm(t_inv, jnp.concatenate([v * beta, kb * eg], axis=-1))
    u = x[:, :, :HEAD_DIM]
    w = x[:, :, HEAD_DIM:]
    a_qk = _bmm_nt(qn, kn) * decay
    qg = qn * eg
    kg = kn * jnp.exp(g_last - gc)
    carry = jnp.exp(g_last)
    state = s_ref[...]
    outs = []
    for ci in range(STEP_CHUNKS):
        hs = slice(ci * HEADS, (ci + 1) * HEADS)
        v_new = u[hs] - _bmm(w[hs], state)
        outs.append(_bmm(qg[hs], state) + _bmm(a_qk[hs], v_new))
        state = state * carry[hs] + _bmm_tn(kg[hs], v_new)
    s_ref[...] = state
    o = jnp.concatenate(outs, axis=0)
    z = _chunk_head_stack(z_ref).astype(F32)
    _store_heads(o_ref, (_rmsnorm(o, ng_ref[...]) * (z * _sigmoid(z))).astype(o_ref.dtype))


def _gdn(qkv, z_src, col_small, row_small, norm_g, batch, seq, *, z_col):
    T = qkv.shape[0]
    rows = STEP_CHUNKS * CHUNK
    ns = seq // rows
    width = HEADS * HEAD_DIM
    row = lambda b, s: b * ns + s
    return pl.pallas_call(
        _gdn_kernel,
        grid=(batch, ns),
        in_specs=[
            pl.BlockSpec((rows, width), lambda b, s: (row(b, s), 0)),
            pl.BlockSpec((rows, width), lambda b, s: (row(b, s), 1)),
            pl.BlockSpec((rows, width), lambda b, s: (row(b, s), 2)),
            pl.BlockSpec((rows, width), lambda b, s: (row(b, s), z_col)),
            pl.BlockSpec((rows, LANES), lambda b, s: (row(b, s), 0)),
            pl.BlockSpec((STEP_CHUNKS, HEADS, CHUNK), lambda b, s: (row(b, s), SM_GDN_A // HEADS, 0)),
            pl.BlockSpec((1, HEAD_DIM), lambda b, s: (0, 0)),
        ],
        out_specs=pl.BlockSpec((rows, width), lambda b, s: (row(b, s), 0)),
        out_shape=jax.ShapeDtypeStruct((T, width), BF16),
        scratch_shapes=[pltpu.VMEM((HEADS, HEAD_DIM, HEAD_DIM), F32)],
        compiler_params=_params("parallel", "arbitrary"),
    )(qkv, qkv, qkv, z_src, col_small, row_small, norm_g)


def _mlstm_kernel(q_ref, k_ref, v_ref, og_ref, col_ref, rowi_ref, rowb_ref, ng_ref, o_ref,
                  c_ref, m_ref):
    @pl.when(pl.program_id(1) == 0)
    def _():
        c_ref[...] = jnp.zeros_like(c_ref)
        m_ref[...] = jnp.zeros_like(m_ref)

    L = CHUNK
    r = lax.broadcasted_iota(jnp.int32, (1, L, L), 1)
    c = lax.broadcasted_iota(jnp.int32, (1, L, L), 2)
    tri = c <= r
    col = col_ref[...]
    q = _chunk_head_stack(q_ref, ML_QK_DIM).astype(F32)
    k = _chunk_head_stack(k_ref, ML_QK_DIM).astype(F32)
    v = _chunk_head_stack(v_ref)
    v = jnp.concatenate([v, jnp.ones_like(v)], axis=-1)
    ic = _chunk_head_cols(col, SM_ML_I)
    bc = _chunk_head_cols(col, SM_ML_F)
    ir = _chunk_head_rows(rowi_ref)
    br = _chunk_head_rows(rowb_ref)
    dm = jnp.where(tri, bc - br + ir, -jnp.inf)
    dmax = jnp.max(dm, axis=-1, keepdims=True)
    qk = _bmm_nt(q, k)
    m_state = m_ref[...]
    m_rows, m_lasts = [], []
    for ci in range(STEP_CHUNKS):
        hs = slice(ci * HEADS, (ci + 1) * HEADS)
        m_r = jnp.maximum(bc[hs] + m_state, dmax[hs])
        m_rows.append(m_r)
        m_state = m_r[:, L - 1:L, :]
        m_lasts.append(m_state)
    m_r = jnp.concatenate(m_rows, axis=0)
    m_last = jnp.concatenate(m_lasts, axis=0)
    m_prev = jnp.concatenate([m_ref[...]] + m_lasts[:-1], axis=0)
    w_inter = jnp.exp(bc + m_prev - m_r)
    sm = jnp.exp(dm - m_r) * qk
    intra = _bmm(sm, v)
    wk = jnp.exp(bc[:, L - 1:L, :] - bc + ic - m_last) * k
    kv = _bmm_tn(wk, v)
    w_last = w_inter[:, L - 1:L, :]
    floor = jnp.exp(-m_r)
    c_state = c_ref[...]
    outs = []
    for ci in range(STEP_CHUNKS):
        hs = slice(ci * HEADS, (ci + 1) * HEADS)
        num_den = w_inter[hs] * _bmm(q[hs], c_state) + intra[hs]
        den = jnp.maximum(jnp.abs(num_den[:, :, HEAD_DIM:]), floor[hs])
        outs.append(num_den[:, :, :HEAD_DIM] / den)
        c_state = w_last[hs] * c_state + kv[hs]
    c_ref[...] = c_state
    m_ref[...] = m_state
    h_tilde = jnp.concatenate(outs, axis=0)
    og = _chunk_head_stack(og_ref).astype(F32)
    _store_heads(o_ref, _rmsnorm(_sigmoid(og) * h_tilde, ng_ref[...]).astype(o_ref.dtype))


def _mlstm(qkv, og_src, col_small, row_small, norm_g, batch, seq, *, q_col, k_col, v_col, og_col):
    T = qkv.shape[0]
    rows = STEP_CHUNKS * CHUNK
    ns = seq // rows
    qk_width = HEADS * ML_QK_DIM
    width = HEADS * HEAD_DIM
    row = lambda b, s: b * ns + s
    return pl.pallas_call(
        _mlstm_kernel,
        grid=(batch, ns),
        in_specs=[
            pl.BlockSpec((rows, qk_width), lambda b, s: (row(b, s), q_col)),
            pl.BlockSpec((rows, qk_width), lambda b, s: (row(b, s), k_col)),
            pl.BlockSpec((rows, width), lambda b, s: (row(b, s), v_col)),
            pl.BlockSpec((rows, width), lambda b, s: (row(b, s), og_col)),
            pl.BlockSpec((rows, LANES), lambda b, s: (row(b, s), 0)),
            pl.BlockSpec((STEP_CHUNKS, HEADS, CHUNK), lambda b, s: (row(b, s), SM_ML_I // HEADS, 0)),
            pl.BlockSpec((STEP_CHUNKS, HEADS, CHUNK), lambda b, s: (row(b, s), SM_ML_F // HEADS, 0)),
            pl.BlockSpec((1, HEAD_DIM), lambda b, s: (0, 0)),
        ],
        out_specs=pl.BlockSpec((rows, width), lambda b, s: (row(b, s), 0)),
        out_shape=jax.ShapeDtypeStruct((T, width), BF16),
        scratch_shapes=[pltpu.VMEM((HEADS, ML_QK_DIM, 2 * HEAD_DIM), F32),
                        pltpu.VMEM((HEADS, 1, 1), F32)],
        compiler_params=_params("parallel", "arbitrary"),
    )(qkv, qkv, qkv, og_src, col_small, row_small, row_small, norm_g)


def _merge_kernel(yf_ref, yg_ref, ym_ref, wf_ref, wg_ref, wm_ref, g0_ref, g1_ref, g2_ref,
                  b0_ref, b1_ref, b2_ref, o_ref):
    def branch(y_ref, w_ref, g_ref, b_ref):
        return _sigmoid(g_ref[...] + b_ref[...]) * jnp.dot(y_ref[...], w_ref[...],
                                                          preferred_element_type=F32)

    y = (branch(yf_ref, wf_ref, g0_ref, b0_ref) + branch(yg_ref, wg_ref, g1_ref, b1_ref)
         + branch(ym_ref, wm_ref, g2_ref, b2_ref))
    o_ref[...] = y.astype(o_ref.dtype)


def _merge(y_fox, y_gdn, y_ml, w_fox, w_gdn, w_ml, layer, gate_src, gate_bias, d_model, *,
           gate_col, tm=1024, tn=512):
    T, K = y_fox.shape
    nj = d_model // tn
    y_spec = pl.BlockSpec((tm, K), lambda i, j: (i, 0))
    w_spec = pl.BlockSpec((None, K, tn), lambda i, j: (layer, 0, j))
    gate_specs = [pl.BlockSpec((tm, tn), lambda i, j, n=n: (i, gate_col + n * nj + j))
                  for n in range(N_BRANCH)]
    bias_specs = [pl.BlockSpec((1, tn), lambda i, j, n=n: (0, n * nj + j)) for n in range(N_BRANCH)]
    return pl.pallas_call(
        _merge_kernel,
        grid=(T // tm, nj),
        in_specs=[y_spec, y_spec, y_spec, w_spec, w_spec, w_spec] + gate_specs + bias_specs,
        out_specs=pl.BlockSpec((tm, tn), lambda i, j: (i, j)),
        out_shape=jax.ShapeDtypeStruct((T, d_model), BF16),
        compiler_params=_params("parallel", "parallel"),
    )(y_fox, y_gdn, y_ml, w_fox, w_gdn, w_ml, gate_src, gate_src, gate_src,
      gate_bias, gate_bias, gate_bias)


def _matmul_residual_kernel(a_ref, w_ref, x_ref, o_ref):
    o_ref[...] = x_ref[...] + jnp.dot(a_ref[...], w_ref[...], preferred_element_type=F32)


def _matmul_residual(a, w_all, layer, x, tm=1024, tn=512):
    T, K = a.shape
    N = w_all.shape[2]
    return pl.pallas_call(
        _matmul_residual_kernel,
        grid=(T // tm, N // tn),
        in_specs=[
            pl.BlockSpec((tm, K), lambda i, j: (i, 0)),
            pl.BlockSpec((None, K, tn), lambda i, j: (layer, 0, j)),
            pl.BlockSpec((tm, tn), lambda i, j: (i, j)),
        ],
        out_specs=pl.BlockSpec((tm, tn), lambda i, j: (i, j)),
        out_shape=jax.ShapeDtypeStruct((T, N), F32),
        compiler_params=_params("parallel", "parallel"),
    )(a, w_all, x)


def _final_norm_kernel(x_ref, g_ref, o_ref):
    o_ref[...] = _rmsnorm(x_ref[...], g_ref[...])


def _final_norm(x, g, tm=512):
    T, D = x.shape
    return pl.pallas_call(
        _final_norm_kernel,
        grid=(T // tm,),
        in_specs=[pl.BlockSpec((tm, D), lambda i: (i, 0)), pl.BlockSpec((1, D), lambda i: (0, 0))],
        out_specs=pl.BlockSpec((tm, D), lambda i: (i, 0)),
        out_shape=jax.ShapeDtypeStruct((T, D), F32),
        compiler_params=_params("parallel"),
    )(x, g)


def _ffn_kernel(x_ref, xh_ref, g_ref, wu_ref, wg_ref, cu_ref, cg_ref, wd_ref, o_ref, xn_ref, *,
                blocks_per_seq):
    @pl.when(pl.program_id(1) == 0)
    def _():
        _norm_with_halo(x_ref, xh_ref, g_ref, xn_ref, pl.program_id(0) % blocks_per_seq == 0)
        o_ref[...] = x_ref[...]

    xn = xn_ref[...]
    groups = _column_groups(wu_ref.shape[1])
    hidden = [(jnp.dot(xn, wu_ref[:, cs], preferred_element_type=F32),
               jnp.dot(xn, wg_ref[:, cs], preferred_element_type=F32)) for cs in groups]
    acts = []
    for cs, (hu, hg) in zip(groups, hidden):
        gt = _conv_taps(hg, cg_ref, cs)
        acts.append((gt * _sigmoid(gt) * _conv_taps(hu, cu_ref, cs)).astype(BF16))
    o_ref[...] += jnp.dot(jnp.concatenate(acts, axis=1), wd_ref[...], preferred_element_type=F32)


def _ffn(x, g, w_up, conv_w, w_down, layer, seq, tm=1024, tn=512):
    T, D = x.shape
    taps = conv_w.shape[0]
    nj = w_down.shape[1] // tn
    halo_blocks = tm // HALO
    once = pl.Buffered(1)
    return pl.pallas_call(
        functools.partial(_ffn_kernel, blocks_per_seq=seq // tm),
        grid=(T // tm, nj),
        in_specs=[
            pl.BlockSpec((tm, D), lambda i, j: (i, 0)),
            pl.BlockSpec((HALO, D), lambda i, j: (jnp.maximum(i * halo_blocks - 1, 0), 0)),
            pl.BlockSpec((1, D), lambda i, j: (0, 0)),
            pl.BlockSpec((None, D, tn), lambda i, j: (layer, 0, j)),
            pl.BlockSpec((None, D, tn), lambda i, j: (layer, 0, nj + j)),
            pl.BlockSpec((taps, tn), lambda i, j: (0, j)),
            pl.BlockSpec((taps, tn), lambda i, j: (0, nj + j)),
            pl.BlockSpec((None, tn, D), lambda i, j: (layer, j, 0)),
        ],
        out_specs=pl.BlockSpec((tm, D), lambda i, j: (i, 0), pipeline_mode=once),
        out_shape=jax.ShapeDtypeStruct((T, D), F32),
        scratch_shapes=[pltpu.VMEM((tm + HALO, D), BF16)],
        compiler_params=_params("parallel", "arbitrary", vmem_limit_bytes=VMEM_LIMIT_LARGE_BYTES),
    )(x, x, g, w_up, w_up, conv_w, conv_w, w_down)


FOX_W = HEADS * HEAD_DIM
GDN_W = HEADS * HEAD_DIM
ML_QK_W = HEADS * ML_QK_DIM
ML_V_W = HEADS * HEAD_DIM
ML_BASE = 3 * FOX_W
Z_BASE = ML_BASE + 2 * ML_QK_W + ML_V_W
GATE_BASE = Z_BASE + GDN_W + ML_V_W
FOX_HEADS_PER_STEP = 8


def _in_projection_columns(d_model):
    widths = [("fox_q", FOX_W), ("fox_k", FOX_W), ("fox_v", FOX_W), ("fox_f", HEADS),
              ("gdn_qkv", 3 * GDN_W), ("gdn_z", GDN_W), ("gdn_b", HEADS), ("gdn_a", HEADS),
              ("ml_q", ML_QK_W), ("ml_k", ML_QK_W), ("ml_v", ML_V_W), ("ml_i", HEADS),
              ("ml_f", HEADS), ("ml_o", ML_V_W), ("gate", N_BRANCH * d_model)]
    off, start = {}, 0
    for name, width in widths:
        off[name] = (start, start + width)
        start += width
    return off


def _pad_lanes(v):
    return jnp.pad(v, (0, LANES - v.shape[0])).reshape(1, LANES)


def _layer(x, layer, w, p, batch, seq):
    T, D = x.shape
    g_mix = p["norm_mix_g"].reshape(1, D)
    scale_p = jnp.concatenate([
        jnp.full((FOX_W,), HEAD_DIM ** -0.5 * LOG2E, F32), jnp.ones((2 * FOX_W,), F32),
        jnp.full((ML_QK_W,), ML_QK_DIM ** -0.5, F32),
        jnp.ones((w["proj"].shape[2] - 3 * FOX_W - ML_QK_W,), F32)]).reshape(1, -1)
    proj = _norm_matmul(x, g_mix, w["proj"], layer, scale_p, BF16, tn=1024)
    zeros8 = jnp.zeros((HEADS,), F32)
    bias_s = _pad_lanes(jnp.concatenate([p["fox_f_bias"], zeros8, p["gdn_dt_bias"],
                                         p["ml_i_bias"], p["ml_f_bias"]]))
    alog_s = _pad_lanes(jnp.concatenate([zeros8, zeros8, p["gdn_a_log"]]))
    col_small, row_small = _small_gates(x, g_mix, w["small"], layer, bias_s, alog_s, seq)
    f_cum = row_small[:, SM_FOX_F:SM_FOX_F + HEADS, :].reshape(batch, seq // CHUNK, HEADS, CHUNK)
    f_cum = f_cum.transpose(0, 2, 1, 3).reshape(batch, HEADS, seq)
    gdn_qkv = _proj_conv(x, g_mix, w["gdn_qkv"], layer, p["gdn_conv_w"], seq)

    hps = FOX_HEADS_PER_STEP
    y_fox = _fox_attention(proj, f_cum, batch, seq, q_col=0, k_col=HEADS // hps,
                           v_col=2 * HEADS // hps, hps=hps)
    y_gdn = _gdn(gdn_qkv, proj, col_small, row_small, p["gdn_norm_g"].reshape(1, HEAD_DIM),
                 batch, seq, z_col=Z_BASE // GDN_W)
    y_ml = _mlstm(proj, proj, col_small, row_small, p["ml_norm_g"].reshape(1, HEAD_DIM),
                  batch, seq, q_col=ML_BASE // ML_QK_W, k_col=ML_BASE // ML_QK_W + 1,
                  v_col=(ML_BASE + 2 * ML_QK_W) // ML_V_W, og_col=(Z_BASE + GDN_W) // ML_V_W)

    tn = 512
    y = _merge(y_fox, y_gdn, y_ml, w["fox_proj"], w["gdn_proj"], w["ml_proj"], layer, proj,
               p["gate_bias"].reshape(1, -1), D, gate_col=GATE_BASE // tn, tn=tn)
    x = _matmul_residual(y, w["out"], layer, x)
    return _ffn(x, p["norm_ffn_g"].reshape(1, D), w["up"], p["ffn_conv_w"], w["down"], layer, seq)


def kernel(x, norm_mix_g, w_in, fox_f_bias, gdn_conv_w, gdn_a_log, gdn_dt_bias, gdn_norm_g,
           ml_i_bias, ml_f_bias, ml_norm_g, gate_bias, w_fox_proj, w_gdn_proj, w_ml_proj,
           w_out, norm_ffn_g, w_up, ffn_conv_w, w_down, norm_final_g):
    batch, seq, d_model = x.shape
    depth = w_in.shape[0]
    off = _in_projection_columns(d_model)
    cols = lambda a, b: w_in[:, :, off[a][0]:off[b][1]].astype(BF16)
    w_small = jnp.concatenate([cols("fox_f", "fox_f"), cols("gdn_b", "gdn_a"), cols("ml_i", "ml_f")],
                              axis=2)
    w = dict(
        proj=jnp.concatenate([cols("fox_q", "fox_v"), cols("ml_q", "ml_v"), cols("gdn_z", "gdn_z"),
                              cols("ml_o", "ml_o"), cols("gate", "gate")], axis=2),
        small=jnp.pad(w_small, ((0, 0), (0, 0), (0, LANES - w_small.shape[2]))),
        gdn_qkv=cols("gdn_qkv", "gdn_qkv"),
        fox_proj=w_fox_proj.astype(BF16), gdn_proj=w_gdn_proj.astype(BF16),
        ml_proj=w_ml_proj.astype(BF16), out=w_out.astype(BF16), up=w_up.astype(BF16),
        down=w_down.astype(BF16))
    small = dict(norm_mix_g=norm_mix_g, fox_f_bias=fox_f_bias, gdn_conv_w=gdn_conv_w,
                 gdn_a_log=gdn_a_log, gdn_dt_bias=gdn_dt_bias, gdn_norm_g=gdn_norm_g,
                 ml_i_bias=ml_i_bias, ml_f_bias=ml_f_bias, ml_norm_g=ml_norm_g,
                 gate_bias=gate_bias, norm_ffn_g=norm_ffn_g, ffn_conv_w=ffn_conv_w)
    h = x.reshape(batch * seq, d_model)
    for layer in range(depth):
        h = _layer(h, layer, w, {k: v[layer] for k, v in small.items()}, batch, seq)
    return _final_norm(h, norm_final_g.reshape(1, d_model)).reshape(batch, seq, d_model)
```

```python
import functools

import jax
import jax.numpy as jnp
from jax import lax
from jax.experimental import pallas as pl
from jax.experimental.pallas import tpu as pltpu

F32 = jnp.float32
BF16 = jnp.bfloat16
HIGHEST = lax.Precision.HIGHEST

EPS = 1e-6
HEADS = 8
HEAD_DIM = 128
ML_QK_DIM = 64
CHUNK = 64
GDN_CONV = 4
FFN_CONV = 3
N_BRANCH = 3
LANES = 128
MXU_WIDTH = 256
HALO = 16

SM_FOX_F, SM_GDN_B, SM_GDN_A, SM_ML_I, SM_ML_F = 0, 8, 16, 24, 32

V7X_VMEM_BYTES = 64 * 1024 * 1024
VMEM_LIMIT_BYTES = V7X_VMEM_BYTES * 3 // 4
VMEM_LIMIT_LARGE_BYTES = V7X_VMEM_BYTES * 7 // 8


def _params(*semantics, vmem_limit_bytes=VMEM_LIMIT_BYTES):
    return pltpu.CompilerParams(dimension_semantics=semantics, vmem_limit_bytes=vmem_limit_bytes)


def _rmsnorm(x, g):
    return x * lax.rsqrt(jnp.mean(x * x, axis=-1, keepdims=True) + EPS) * g


def _sigmoid(x):
    return 1.0 / (1.0 + jnp.exp(-x))


def _mm_hi(a, b):
    return jnp.dot(a, b, precision=HIGHEST, preferred_element_type=F32)


def _norm_matmul_kernel(x_ref, g_ref, w_ref, cs_ref, o_ref, xn_ref):
    @pl.when(pl.program_id(1) == 0)
    def _():
        xn_ref[...] = _rmsnorm(x_ref[...], g_ref[...]).astype(BF16)

    acc = jnp.dot(xn_ref[...], w_ref[...], preferred_element_type=F32)
    o_ref[...] = (acc * cs_ref[...]).astype(o_ref.dtype)


def _norm_matmul(x, g, w_all, layer, colscale, out_dtype, tm=1024, tn=512):
    T, D = x.shape
    N = w_all.shape[2]
    return pl.pallas_call(
        _norm_matmul_kernel,
        grid=(T // tm, N // tn),
        in_specs=[
            pl.BlockSpec((tm, D), lambda i, j: (i, 0)),
            pl.BlockSpec((1, D), lambda i, j: (0, 0)),
            pl.BlockSpec((None, D, tn), lambda i, j: (layer, 0, j)),
            pl.BlockSpec((1, tn), lambda i, j: (0, j)),
        ],
        out_specs=pl.BlockSpec((tm, tn), lambda i, j: (i, j)),
        out_shape=jax.ShapeDtypeStruct((T, N), out_dtype),
        scratch_shapes=[pltpu.VMEM((tm, D), BF16)],
        compiler_params=_params("parallel", "arbitrary"),
    )(x, g, w_all, colscale)


def _small_kernel(x_ref, g_ref, w_ref, bias_ref, alog_ref, o_ref, rows_ref, carry_ref, *,
                  blocks_per_seq):
    i = pl.program_id(0)
    tm = x_ref.shape[0]

    @pl.when(i % blocks_per_seq == 0)
    def _():
        carry_ref[...] = jnp.zeros_like(carry_ref)

    xn = _rmsnorm(x_ref[...], g_ref[...]).astype(BF16)
    t = jnp.dot(xn, w_ref[...], preferred_element_type=F32) + bias_ref[...]
    lane = lax.broadcasted_iota(jnp.int32, (tm, LANES), 1)
    e = jnp.log1p(jnp.exp(-jnp.abs(t)))
    logsig = jnp.minimum(t, 0.0) - e
    softplus = jnp.maximum(t, 0.0) + e
    g_decay = -jnp.exp(alog_ref[...]) * softplus
    is_logsig = (lane < SM_GDN_B) | ((lane >= SM_ML_F) & (lane < SM_ML_F + HEADS))
    val = jnp.where(is_logsig, logsig,
                    jnp.where(lane < SM_GDN_A, _sigmoid(t),
                              jnp.where(lane < SM_ML_I, g_decay, t)))

    r = lax.broadcasted_iota(jnp.int32, (CHUNK, CHUNK), 0)
    c = lax.broadcasted_iota(jnp.int32, (CHUNK, CHUNK), 1)
    tri = jnp.where(c <= r, 1.0, 0.0).astype(F32)
    chunk_sums = [_mm_hi(tri, val[ci * CHUNK:(ci + 1) * CHUNK, :]) for ci in range(tm // CHUNK)]
    offset = carry_ref[...]
    full_sums = []
    for cs in chunk_sums:
        full_sums.append(cs + offset)
        offset = offset + cs[CHUNK - 1:CHUNK, :]
    carry_ref[...] = offset
    cs_chunk = jnp.concatenate(chunk_sums, axis=0)
    cs_full = jnp.concatenate(full_sums, axis=0)

    is_chunk_cs = ((lane >= SM_GDN_A) & (lane < SM_ML_I)) | ((lane >= SM_ML_F) & (lane < SM_ML_F + HEADS))
    out = jnp.where(lane < SM_GDN_B, cs_full, jnp.where(is_chunk_cs, cs_chunk, val))
    o_ref[...] = out
    for ci in range(tm // CHUNK):
        rows_ref[ci] = out[ci * CHUNK:(ci + 1) * CHUNK, :].T


def _small_gates(x, g, w_all, layer, bias, alog, seq, tm=512):
    T, D = x.shape
    return pl.pallas_call(
        functools.partial(_small_kernel, blocks_per_seq=seq // tm),
        grid=(T // tm,),
        in_specs=[
            pl.BlockSpec((tm, D), lambda i: (i, 0)),
            pl.BlockSpec((1, D), lambda i: (0, 0)),
            pl.BlockSpec((None, D, LANES), lambda i: (layer, 0, 0)),
            pl.BlockSpec((1, LANES), lambda i: (0, 0)),
            pl.BlockSpec((1, LANES), lambda i: (0, 0)),
        ],
        out_specs=[pl.BlockSpec((tm, LANES), lambda i: (i, 0)),
                   pl.BlockSpec((tm // CHUNK, LANES, CHUNK), lambda i: (i, 0, 0))],
        out_shape=[jax.ShapeDtypeStruct((T, LANES), F32),
                   jax.ShapeDtypeStruct((T // CHUNK, LANES, CHUNK), F32)],
        scratch_shapes=[pltpu.VMEM((1, LANES), F32)],
        compiler_params=_params("arbitrary"),
    )(x, g, w_all, bias, alog)


def _conv_taps(h, c_ref, cols):
    taps = c_ref.shape[0]
    tm = h.shape[0] - HALO
    acc = None
    for t in range(taps):
        start = HALO - (taps - 1) + t
        term = h[start:start + tm, :] * c_ref[t:t + 1, cols]
        acc = term if acc is None else acc + term
    return acc


def _column_groups(width):
    return [slice(c * MXU_WIDTH, (c + 1) * MXU_WIDTH) for c in range(width // MXU_WIDTH)]


def _norm_with_halo(x_ref, xh_ref, g_ref, xn_ref, at_seq_start):
    xn_ref[HALO:, :] = _rmsnorm(x_ref[...], g_ref[...]).astype(BF16)
    keep = jnp.where(at_seq_start, 0.0, 1.0)
    xn_ref[:HALO, :] = (_rmsnorm(xh_ref[...], g_ref[...]) * keep).astype(BF16)


def _proj_conv_kernel(x_ref, xh_ref, g_ref, w_ref, c_ref, o_ref, xn_ref, *, blocks_per_seq):
    @pl.when(pl.program_id(1) == 0)
    def _():
        _norm_with_halo(x_ref, xh_ref, g_ref, xn_ref, pl.program_id(0) % blocks_per_seq == 0)

    xn = xn_ref[...]
    groups = _column_groups(w_ref.shape[1])
    hidden = [jnp.dot(xn, w_ref[:, cs], preferred_element_type=F32) for cs in groups]
    for cs, h in zip(groups, hidden):
        u = _conv_taps(h, c_ref, cs)
        o_ref[:, cs] = (u * _sigmoid(u)).astype(o_ref.dtype)


def _proj_conv(x, g, w_all, layer, conv_w, seq, tm=1024, tn=1024):
    T, D = x.shape
    taps = conv_w.shape[0]
    N = w_all.shape[2]
    halo_blocks = tm // HALO
    return pl.pallas_call(
        functools.partial(_proj_conv_kernel, blocks_per_seq=seq // tm),
        grid=(T // tm, N // tn),
        in_specs=[
            pl.BlockSpec((tm, D), lambda i, j: (i, 0)),
            pl.BlockSpec((HALO, D), lambda i, j: (jnp.maximum(i * halo_blocks - 1, 0), 0)),
            pl.BlockSpec((1, D), lambda i, j: (0, 0)),
            pl.BlockSpec((None, D, tn), lambda i, j: (layer, 0, j)),
            pl.BlockSpec((taps, tn), lambda i, j: (0, j)),
        ],
        out_specs=pl.BlockSpec((tm, tn), lambda i, j: (i, j)),
        out_shape=jax.ShapeDtypeStruct((T, N), BF16),
        scratch_shapes=[pltpu.VMEM((tm + HALO, D), BF16)],
        compiler_params=_params("parallel", "arbitrary"),
    )(x, x, g, w_all, conv_w)


NEG_BIG = -1e30


LOG2E = 1.4426950408889634


def _fox_kernel(q_ref, k_ref, v_ref, f_ref, o_ref, m_ref, acc_ref, *, tq, tk, hps):
    qi = pl.program_id(2)
    ones = jnp.ones((tk, HEAD_DIM), BF16)
    lane_tiles = tk // LANES
    m_ref[...] = jnp.full_like(m_ref, NEG_BIG)
    acc_ref[...] = jnp.zeros_like(acc_ref)

    def block(j, masked):
        start = pl.multiple_of(j * tk, tk)
        heads = [slice(h * HEAD_DIM, (h + 1) * HEAD_DIM) for h in range(hps)]
        scores = []
        for h, hs in enumerate(heads):
            s = lax.dot_general(q_ref[:, hs], k_ref[pl.ds(start, tk), hs],
                                (((1,), (1,)), ((), ())), preferred_element_type=F32)
            s = s - f_ref[h, pl.ds(j, 1), :] * LOG2E
            if masked:
                row = lax.broadcasted_iota(jnp.int32, (tq, tk), 0)
                col = lax.broadcasted_iota(jnp.int32, (tq, tk), 1)
                s = jnp.where(col <= row, s, NEG_BIG)
            scores.append([s[:, c * LANES:(c + 1) * LANES] for c in range(lane_tiles)])
        probs, alphas = [], []
        for h, tiles in enumerate(scores):
            m_prev = m_ref[h]
            m_tile = functools.reduce(jnp.maximum, tiles)
            m_new = jnp.maximum(m_prev, jnp.max(m_tile, axis=-1, keepdims=True))
            m_ref[h] = m_new
            alphas.append(jnp.exp2(m_prev - m_new))
            probs.append(jnp.concatenate([jnp.exp2(t - m_new).astype(BF16) for t in tiles], axis=1))
        for h, hs in enumerate(heads):
            v_ext = jnp.concatenate([v_ref[pl.ds(start, tk), hs], ones], axis=1)
            pv = jnp.dot(probs[h], v_ext, preferred_element_type=F32)
            acc_ref[h] = jnp.concatenate([alphas[h], alphas[h]], axis=1) * acc_ref[h] + pv

    def full_block(j, carry):
        block(j, masked=False)
        return carry

    lax.fori_loop(0, qi, full_block, 0)
    block(qi, masked=True)
    for h in range(hps):
        acc = acc_ref[h]
        o_ref[:, h * HEAD_DIM:(h + 1) * HEAD_DIM] = (
            acc[:, :HEAD_DIM] / acc[:, HEAD_DIM:]).astype(o_ref.dtype)


def _fox_attention(qkv, f_cum, batch, seq, *, q_col, k_col, v_col, tq=512, hps=2):
    T = qkv.shape[0]
    nq = seq // tq
    width = hps * HEAD_DIM
    f_blocks = f_cum.reshape(batch, HEADS, nq, tq)
    return pl.pallas_call(
        functools.partial(_fox_kernel, tq=tq, tk=tq, hps=hps),
        grid=(batch, HEADS // hps, nq),
        in_specs=[
            pl.BlockSpec((tq, width), lambda b, h, qi: (b * nq + qi, q_col + h)),
            pl.BlockSpec((seq, width), lambda b, h, qi: (b, k_col + h)),
            pl.BlockSpec((seq, width), lambda b, h, qi: (b, v_col + h)),
            pl.BlockSpec((None, hps, nq, tq), lambda b, h, qi: (b, h, 0, 0)),
        ],
        out_specs=pl.BlockSpec((tq, width), lambda b, h, qi: (b * nq + qi, h)),
        out_shape=jax.ShapeDtypeStruct((T, HEADS * HEAD_DIM), BF16),
        scratch_shapes=[pltpu.VMEM((hps, tq, LANES), F32),
                        pltpu.VMEM((hps, tq, 2 * HEAD_DIM), F32)],
        compiler_params=_params("parallel", "parallel", "arbitrary"),
    )(qkv, qkv, qkv, f_blocks)


def _bmm(a, b):
    return jnp.einsum("gmk,gkn->gmn", a.astype(BF16), b.astype(BF16), preferred_element_type=F32)


def _bmm_nt(a, b):
    return jnp.einsum("gmk,gnk->gmn", a.astype(BF16), b.astype(BF16), preferred_element_type=F32)


def _bmm_tn(a, b):
    return jnp.einsum("gkm,gkn->gmn", a.astype(BF16), b.astype(BF16), preferred_element_type=F32)


def _unit_lower_inverse(m, eye, r, c):
    n = m.shape[-1]
    same_pair = (r >> 1) == (c >> 1)
    x = eye - jnp.where(same_pair, m, 0.0)
    b = 2
    while b < n:
        shift = b.bit_length() - 1
        joins = ((r >> (shift + 1)) == (c >> (shift + 1))) & ((r >> shift) != (c >> shift))
        x = x - _bmm(_bmm(x, jnp.where(joins, m, 0.0)), x)
        b *= 2
    return x


STEP_CHUNKS = 4


def _chunk_head_stack(ref, width=HEAD_DIM):
    return jnp.stack([ref[ci * CHUNK:(ci + 1) * CHUNK, h * width:(h + 1) * width]
                      for ci in range(STEP_CHUNKS) for h in range(HEADS)])


def _chunk_head_cols(col, start):
    return jnp.stack([col[ci * CHUNK:(ci + 1) * CHUNK, start + h:start + h + 1]
                      for ci in range(STEP_CHUNKS) for h in range(HEADS)])


def _chunk_head_rows(row_ref):
    return jnp.concatenate([row_ref[ci] for ci in range(STEP_CHUNKS)], axis=0)[:, None, :]


def _store_heads(o_ref, y):
    for ci in range(STEP_CHUNKS):
        for h in range(HEADS):
            o_ref[ci * CHUNK:(ci + 1) * CHUNK, h * HEAD_DIM:(h + 1) * HEAD_DIM] = y[ci * HEADS + h]


def _gdn_kernel(q_ref, k_ref, v_ref, z_ref, col_ref, rowg_ref, ng_ref, o_ref, s_ref):
    @pl.when(pl.program_id(1) == 0)
    def _():
        s_ref[...] = jnp.zeros_like(s_ref)

    L = CHUNK
    r = lax.broadcasted_iota(jnp.int32, (1, L, L), 1)
    c = lax.broadcasted_iota(jnp.int32, (1, L, L), 2)
    tri = c <= r
    eye = jnp.where(c == r, 1.0, 0.0).astype(F32)
    col = col_ref[...]
    q = _chunk_head_stack(q_ref).astype(F32)
    k = _chunk_head_stack(k_ref).astype(F32)
    v = _chunk_head_stack(v_ref).astype(F32)
    qn = q * lax.rsqrt(jnp.sum(q * q, axis=-1, keepdims=True) + EPS) * (HEAD_DIM ** -0.5)
    kn = k * lax.rsqrt(jnp.sum(k * k, axis=-1, keepdims=True) + EPS)
    beta = _chunk_head_cols(col, SM_GDN_B)
    gc = _chunk_head_cols(col, SM_GDN_A)
    gr = _chunk_head_rows(rowg_ref)
    g_last = gc[:, L - 1:L, :]
    decay = jnp.where(tri, jnp.exp(jnp.where(tri, gc - gr, 0.0)), 0.0)
    eg = jnp.exp(gc)
    kb = kn * beta
    m = jnp.where(c < r, _bmm_nt(kb, kn) * decay, 0.0)
    t_inv = _unit_lower_inverse(m, eye, r, c)
    x = _bmm(t_inv, jnp.concatenate([v * beta, kb * eg], axis=-1))
    u = x[:, :, :HEAD_DIM]
    w = x[:, :, HEAD_DIM:]
    a_qk = _bmm_nt(qn, kn) * decay
    qg = qn * eg
    kg = kn * jnp.exp(g_last - gc)
    carry = jnp.exp(g_last)
    state = s_ref[...]
    outs = []
    for ci in range(STEP_CHUNKS):
        hs = slice(ci * HEADS, (ci + 1) * HEADS)
        v_new = u[hs] - _bmm(w[hs], state)
        outs.append(_bmm(qg[hs], state) + _bmm(a_qk[hs], v_new))
        state = state * carry[hs] + _bmm_tn(kg[hs], v_new)
    s_ref[...] = state
    o = jnp.concatenate(outs, axis=0)
    z = _chunk_head_stack(z_ref).astype(F32)
    _store_heads(o_ref, (_rmsnorm(o, ng_ref[...]) * (z * _sigmoid(z))).astype(o_ref.dtype))


def _gdn(qkv, z_src, col_small, row_small, norm_g, batch, seq, *, z_col):
    T = qkv.shape[0]
    rows = STEP_CHUNKS * CHUNK
    ns = seq // rows
    width = HEADS * HEAD_DIM
    row = lambda b, s: b * ns + s
    return pl.pallas_call(
        _gdn_kernel,
        grid=(batch, ns),
        in_specs=[
            pl.BlockSpec((rows, width), lambda b, s: (row(b, s), 0)),
            pl.BlockSpec((rows, width), lambda b, s: (row(b, s), 1)),
            pl.BlockSpec((rows, width), lambda b, s: (row(b, s), 2)),
            pl.BlockSpec((rows, width), lambda b, s: (row(b, s), z_col)),
            pl.BlockSpec((rows, LANES), lambda b, s: (row(b, s), 0)),
            pl.BlockSpec((STEP_CHUNKS, HEADS, CHUNK), lambda b, s: (row(b, s), SM_GDN_A // HEADS, 0)),
            pl.BlockSpec((1, HEAD_DIM), lambda b, s: (0, 0)),
        ],
        out_specs=pl.BlockSpec((rows, width), lambda b, s: (row(b, s), 0)),
        out_shape=jax.ShapeDtypeStruct((T, width), BF16),
        scratch_shapes=[pltpu.VMEM((HEADS, HEAD_DIM, HEAD_DIM), F32)],
        compiler_params=_params("parallel", "arbitrary"),
    )(qkv, qkv, qkv, z_src, col_small, row_small, norm_g)


def _mlstm_kernel(q_ref, k_ref, v_ref, og_ref, col_ref, rowi_ref, rowb_ref, ng_ref, o_ref,
                  c_ref, m_ref):
    @pl.when(pl.program_id(1) == 0)
    def _():
        c_ref[...] = jnp.zeros_like(c_ref)
        m_ref[...] = jnp.zeros_like(m_ref)

    L = CHUNK
    r = lax.broadcasted_iota(jnp.int32, (1, L, L), 1)
    c = lax.broadcasted_iota(jnp.int32, (1, L, L), 2)
    tri = c <= r
    col = col_ref[...]
    q = _chunk_head_stack(q_ref, ML_QK_DIM).astype(F32)
    k = _chunk_head_stack(k_ref, ML_QK_DIM).astype(F32)
    v = _chunk_head_stack(v_ref)
    v = jnp.concatenate([v, jnp.ones_like(v)], axis=-1)
    ic = _chunk_head_cols(col, SM_ML_I)
    bc = _chunk_head_cols(col, SM_ML_F)
    ir = _chunk_head_rows(rowi_ref)
    br = _chunk_head_rows(rowb_ref)
    dm = jnp.where(tri, bc - br + ir, -jnp.inf)
    dmax = jnp.max(dm, axis=-1, keepdims=True)
    qk = _bmm_nt(q, k)
    m_state = m_ref[...]
    m_rows, m_lasts = [], []
    for ci in range(STEP_CHUNKS):
        hs = slice(ci * HEADS, (ci + 1) * HEADS)
        m_r = jnp.maximum(bc[hs] + m_state, dmax[hs])
        m_rows.append(m_r)
        m_state = m_r[:, L - 1:L, :]
        m_lasts.append(m_state)
    m_r = jnp.concatenate(m_rows, axis=0)
    m_last = jnp.concatenate(m_lasts, axis=0)
    m_prev = jnp.concatenate([m_ref[...]] + m_lasts[:-1], axis=0)
    w_inter = jnp.exp(bc + m_prev - m_r)
    sm = jnp.exp(dm - m_r) * qk
    intra = _bmm(sm, v)
    wk = jnp.exp(bc[:, L - 1:L, :] - bc + ic - m_last) * k
    kv = _bmm_tn(wk, v)
    w_last = w_inter[:, L - 1:L, :]
    floor = jnp.exp(-m_r)
    c_state = c_ref[...]
    outs = []
    for ci in range(STEP_CHUNKS):
        hs = slice(ci * HEADS, (ci + 1) * HEADS)
        num_den = w_inter[hs] * _bmm(q[hs], c_state) + intra[hs]
        den = jnp.maximum(jnp.abs(num_den[:, :, HEAD_DIM:]), floor[hs])
        outs.append(num_den[:, :, :HEAD_DIM] / den)
        c_state = w_last[hs] * c_state + kv[hs]
    c_ref[...] = c_state
    m_ref[...] = m_state
    h_tilde = jnp.concatenate(outs, axis=0)
    og = _chunk_head_stack(og_ref).astype(F32)
    _store_heads(o_ref, _rmsnorm(_sigmoid(og) * h_tilde, ng_ref[...]).astype(o_ref.dtype))


def _mlstm(qkv, og_src, col_small, row_small, norm_g, batch, seq, *, q_col, k_col, v_col, og_col):
    T = qkv.shape[0]
    rows = STEP_CHUNKS * CHUNK
    ns = seq // rows
    qk_width = HEADS * ML_QK_DIM
    width = HEADS * HEAD_DIM
    row = lambda b, s: b * ns + s
    return pl.pallas_call(
        _mlstm_kernel,
        grid=(batch, ns),
        in_specs=[
            pl.BlockSpec((rows, qk_width), lambda b, s: (row(b, s), q_col)),
            pl.BlockSpec((rows, qk_width), lambda b, s: (row(b, s), k_col)),
            pl.BlockSpec((rows, width), lambda b, s: (row(b, s), v_col)),
            pl.BlockSpec((rows, width), lambda b, s: (row(b, s), og_col)),
            pl.BlockSpec((rows, LANES), lambda b, s: (row(b, s), 0)),
            pl.BlockSpec((STEP_CHUNKS, HEADS, CHUNK), lambda b, s: (row(b, s), SM_ML_I // HEADS, 0)),
            pl.BlockSpec((STEP_CHUNKS, HEADS, CHUNK), lambda b, s: (row(b, s), SM_ML_F // HEADS, 0)),
            pl.BlockSpec((1, HEAD_DIM), lambda b, s: (0, 0)),
        ],
        out_specs=pl.BlockSpec((rows, width), lambda b, s: (row(b, s), 0)),
        out_shape=jax.ShapeDtypeStruct((T, width), BF16),
        scratch_shapes=[pltpu.VMEM((HEADS, ML_QK_DIM, 2 * HEAD_DIM), F32),
                        pltpu.VMEM((HEADS, 1, 1), F32)],
        compiler_params=_params("parallel", "arbitrary"),
    )(qkv, qkv, qkv, og_src, col_small, row_small, row_small, norm_g)


def _merge_out_kernel(yf_ref, yg_ref, ym_ref, wf_ref, wg_ref, wm_ref, g0_ref, g1_ref, g2_ref,
                      b0_ref, b1_ref, b2_ref, wo_ref, x_ref, o_ref):
    @pl.when(pl.program_id(1) == 0)
    def _():
        o_ref[...] = x_ref[...]

    branches = ((yf_ref, wf_ref, g0_ref, b0_ref), (yg_ref, wg_ref, g1_ref, b1_ref),
                (ym_ref, wm_ref, g2_ref, b2_ref))
    prods = [jnp.dot(y_ref[...], w_ref[...], preferred_element_type=F32)
             for y_ref, w_ref, _, _ in branches]
    mixed = None
    for prod, (_, _, g_ref, b_ref) in zip(prods, branches):
        term = _sigmoid(g_ref[...] + b_ref[...]) * prod
        mixed = term if mixed is None else mixed + term
    o_ref[...] += jnp.dot(mixed.astype(BF16), wo_ref[...], preferred_element_type=F32)


def _merge_out(y_fox, y_gdn, y_ml, w_fox, w_gdn, w_ml, w_out, layer, gate_src, gate_bias, x, *,
               gate_col, tm=512, tn=512):
    T, K = y_fox.shape
    D = x.shape[1]
    nj = D // tn
    y_spec = pl.BlockSpec((tm, K), lambda i, j: (i, 0))
    w_spec = pl.BlockSpec((None, K, tn), lambda i, j: (layer, 0, j))
    gate_specs = [pl.BlockSpec((tm, tn), lambda i, j, n=n: (i, gate_col + n * nj + j))
                  for n in range(N_BRANCH)]
    bias_specs = [pl.BlockSpec((1, tn), lambda i, j, n=n: (0, n * nj + j)) for n in range(N_BRANCH)]
    return pl.pallas_call(
        _merge_out_kernel,
        grid=(T // tm, nj),
        in_specs=[y_spec, y_spec, y_spec, w_spec, w_spec, w_spec] + gate_specs + bias_specs + [
            pl.BlockSpec((None, tn, D), lambda i, j: (layer, j, 0)),
            pl.BlockSpec((tm, D), lambda i, j: (i, 0)),
        ],
        out_specs=pl.BlockSpec((tm, D), lambda i, j: (i, 0)),
        out_shape=jax.ShapeDtypeStruct((T, D), F32),
        compiler_params=_params("parallel", "arbitrary"),
    )(y_fox, y_gdn, y_ml, w_fox, w_gdn, w_ml, gate_src, gate_src, gate_src,
      gate_bias, gate_bias, gate_bias, w_out, x)


def _final_norm_kernel(x_ref, g_ref, o_ref):
    o_ref[...] = _rmsnorm(x_ref[...], g_ref[...])


def _final_norm(x, g, tm=512):
    T, D = x.shape
    return pl.pallas_call(
        _final_norm_kernel,
        grid=(T // tm,),
        in_specs=[pl.BlockSpec((tm, D), lambda i: (i, 0)), pl.BlockSpec((1, D), lambda i: (0, 0))],
        out_specs=pl.BlockSpec((tm, D), lambda i: (i, 0)),
        out_shape=jax.ShapeDtypeStruct((T, D), F32),
        compiler_params=_params("parallel"),
    )(x, g)


def _ffn_kernel(x_ref, xh_ref, g_ref, wu_ref, wg_ref, cu_ref, cg_ref, wd_ref, o_ref, xn_ref, *,
                blocks_per_seq):
    @pl.when(pl.program_id(1) == 0)
    def _():
        _norm_with_halo(x_ref, xh_ref, g_ref, xn_ref, pl.program_id(0) % blocks_per_seq == 0)
        o_ref[...] = x_ref[...]

    xn = xn_ref[...]
    groups = _column_groups(wu_ref.shape[1])
    hidden = [(jnp.dot(xn, wu_ref[:, cs], preferred_element_type=F32),
               jnp.dot(xn, wg_ref[:, cs], preferred_element_type=F32)) for cs in groups]
    acts = []
    for cs, (hu, hg) in zip(groups, hidden):
        gt = _conv_taps(hg, cg_ref, cs)
        acts.append((gt * _sigmoid(gt) * _conv_taps(hu, cu_ref, cs)).astype(BF16))
    o_ref[...] += jnp.dot(jnp.concatenate(acts, axis=1), wd_ref[...], preferred_element_type=F32)


def _ffn(x, g, w_up, conv_w, w_down, layer, seq, tm=1024, tn=512):
    T, D = x.shape
    taps = conv_w.shape[0]
    nj = w_down.shape[1] // tn
    halo_blocks = tm // HALO
    once = pl.Buffered(1)
    return pl.pallas_call(
        functools.partial(_ffn_kernel, blocks_per_seq=seq // tm),
        grid=(T // tm, nj),
        in_specs=[
            pl.BlockSpec((tm, D), lambda i, j: (i, 0)),
            pl.BlockSpec((HALO, D), lambda i, j: (jnp.maximum(i * halo_blocks - 1, 0), 0)),
            pl.BlockSpec((1, D), lambda i, j: (0, 0)),
            pl.BlockSpec((None, D, tn), lambda i, j: (layer, 0, j)),
            pl.BlockSpec((None, D, tn), lambda i, j: (layer, 0, nj + j)),
            pl.BlockSpec((taps, tn), lambda i, j: (0, j)),
            pl.BlockSpec((taps, tn), lambda i, j: (0, nj + j)),
            pl.BlockSpec((None, tn, D), lambda i, j: (layer, j, 0)),
        ],
        out_specs=pl.BlockSpec((tm, D), lambda i, j: (i, 0), pipeline_mode=once),
        out_shape=jax.ShapeDtypeStruct((T, D), F32),
        scratch_shapes=[pltpu.VMEM((tm + HALO, D), BF16)],
        compiler_params=_params("parallel", "arbitrary", vmem_limit_bytes=VMEM_LIMIT_LARGE_BYTES),
    )(x, x, g, w_up, w_up, conv_w, conv_w, w_down)


FOX_W = HEADS * HEAD_DIM
GDN_W = HEADS * HEAD_DIM
ML_QK_W = HEADS * ML_QK_DIM
ML_V_W = HEADS * HEAD_DIM
ML_BASE = 3 * FOX_W
Z_BASE = ML_BASE + 2 * ML_QK_W + ML_V_W
GATE_BASE = Z_BASE + GDN_W + ML_V_W
FOX_HEADS_PER_STEP = 4


def _in_projection_columns(d_model):
    widths = [("fox_q", FOX_W), ("fox_k", FOX_W), ("fox_v", FOX_W), ("fox_f", HEADS),
              ("gdn_qkv", 3 * GDN_W), ("gdn_z", GDN_W), ("gdn_b", HEADS), ("gdn_a", HEADS),
              ("ml_q", ML_QK_W), ("ml_k", ML_QK_W), ("ml_v", ML_V_W), ("ml_i", HEADS),
              ("ml_f", HEADS), ("ml_o", ML_V_W), ("gate", N_BRANCH * d_model)]
    off, start = {}, 0
    for name, width in widths:
        off[name] = (start, start + width)
        start += width
    return off


def _pad_lanes(v):
    return jnp.pad(v, (0, LANES - v.shape[0])).reshape(1, LANES)


def _layer(x, layer, w, p, batch, seq):
    T, D = x.shape
    g_mix = p["norm_mix_g"].reshape(1, D)
    scale_p = jnp.concatenate([
        jnp.full((FOX_W,), HEAD_DIM ** -0.5 * LOG2E, F32), jnp.ones((2 * FOX_W,), F32),
        jnp.full((ML_QK_W,), ML_QK_DIM ** -0.5, F32),
        jnp.ones((w["proj"].shape[2] - 3 * FOX_W - ML_QK_W,), F32)]).reshape(1, -1)
    proj = _norm_matmul(x, g_mix, w["proj"], layer, scale_p, BF16, tn=1024)
    zeros8 = jnp.zeros((HEADS,), F32)
    bias_s = _pad_lanes(jnp.concatenate([p["fox_f_bias"], zeros8, p["gdn_dt_bias"],
                                         p["ml_i_bias"], p["ml_f_bias"]]))
    alog_s = _pad_lanes(jnp.concatenate([zeros8, zeros8, p["gdn_a_log"]]))
    col_small, row_small = _small_gates(x, g_mix, w["small"], layer, bias_s, alog_s, seq)
    f_cum = row_small[:, SM_FOX_F:SM_FOX_F + HEADS, :].reshape(batch, seq // CHUNK, HEADS, CHUNK)
    f_cum = f_cum.transpose(0, 2, 1, 3).reshape(batch, HEADS, seq)
    gdn_qkv = _proj_conv(x, g_mix, w["gdn_qkv"], layer, p["gdn_conv_w"], seq)

    hps = FOX_HEADS_PER_STEP
    y_fox = _fox_attention(proj, f_cum, batch, seq, q_col=0, k_col=HEADS // hps,
                           v_col=2 * HEADS // hps, hps=hps)
    y_gdn = _gdn(gdn_qkv, proj, col_small, row_small, p["gdn_norm_g"].reshape(1, HEAD_DIM),
                 batch, seq, z_col=Z_BASE // GDN_W)
    y_ml = _mlstm(proj, proj, col_small, row_small, p["ml_norm_g"].reshape(1, HEAD_DIM),
                  batch, seq, q_col=ML_BASE // ML_QK_W, k_col=ML_BASE // ML_QK_W + 1,
                  v_col=(ML_BASE + 2 * ML_QK_W) // ML_V_W, og_col=(Z_BASE + GDN_W) // ML_V_W)

    tn = 512
    x = _merge_out(y_fox, y_gdn, y_ml, w["fox_proj"], w["gdn_proj"], w["ml_proj"], w["out"], layer,
                   proj, p["gate_bias"].reshape(1, -1), x, gate_col=GATE_BASE // tn, tn=tn)
    return _ffn(x, p["norm_ffn_g"].reshape(1, D), w["up"], p["ffn_conv_w"], w["down"], layer, seq)


def kernel(x, norm_mix_g, w_in, fox_f_bias, gdn_conv_w, gdn_a_log, gdn_dt_bias, gdn_norm_g,
           ml_i_bias, ml_f_bias, ml_norm_g, gate_bias, w_fox_proj, w_gdn_proj, w_ml_proj,
           w_out, norm_ffn_g, w_up, ffn_conv_w, w_down, norm_final_g):
    batch, seq, d_model = x.shape
    depth = w_in.shape[0]
    off = _in_projection_columns(d_model)
    w_in = lax.optimization_barrier(w_in.astype(BF16))
    cols = lambda a, b: w_in[:, :, off[a][0]:off[b][1]]
    w_small = jnp.concatenate([cols("fox_f", "fox_f"), cols("gdn_b", "gdn_a"), cols("ml_i", "ml_f")],
                              axis=2)
    w = dict(
        proj=jnp.concatenate([cols("fox_q", "fox_v"), cols("ml_q", "ml_v"), cols("gdn_z", "gdn_z"),
                              cols("ml_o", "ml_o"), cols("gate", "gate")], axis=2),
        small=jnp.pad(w_small, ((0, 0), (0, 0), (0, LANES - w_small.shape[2]))),
        gdn_qkv=cols("gdn_qkv", "gdn_qkv"),
        fox_proj=w_fox_proj.astype(BF16), gdn_proj=w_gdn_proj.astype(BF16),
        ml_proj=w_ml_proj.astype(BF16), out=w_out.astype(BF16), up=w_up.astype(BF16),
        down=w_down.astype(BF16))
    small = dict(norm_mix_g=norm_mix_g, fox_f_bias=fox_f_bias, gdn_conv_w=gdn_conv_w,
                 gdn_a_log=gdn_a_log, gdn_dt_bias=gdn_dt_bias, gdn_norm_g=gdn_norm_g,
                 ml_i_bias=ml_i_bias, ml_f_bias=ml_f_bias, ml_norm_g=ml_norm_g,
                 gate_bias=gate_bias, norm_ffn_g=norm_ffn_g, ffn_conv_w=ffn_conv_w)
    h = x.reshape(batch * seq, d_model)
    for layer in range(depth):
        h = _layer(h, layer, w, {k: v[layer] for k, v in small.items()}, batch, seq)
    return _final_norm(h, norm_final_g.reshape(1, d_model)).reshape(batch, seq, d_model)
```

```python
import functools

import jax
import jax.numpy as jnp
from jax import lax
from jax.experimental import pallas as pl
from jax.experimental.pallas import tpu as pltpu

F32 = jnp.float32
BF16 = jnp.bfloat16
HIGHEST = lax.Precision.HIGHEST

EPS = 1e-6
HEADS = 8
HEAD_DIM = 128
ML_QK_DIM = 64
CHUNK = 64
GDN_CONV = 4
FFN_CONV = 3
N_BRANCH = 3
LANES = 128
MXU_WIDTH = 256
HALO = 16

SM_FOX_F, SM_GDN_B, SM_GDN_A, SM_ML_I, SM_ML_F = 0, 8, 16, 24, 32

V7X_VMEM_BYTES = 64 * 1024 * 1024
VMEM_LIMIT_BYTES = V7X_VMEM_BYTES * 3 // 4
VMEM_LIMIT_LARGE_BYTES = V7X_VMEM_BYTES * 7 // 8


def _params(*semantics, vmem_limit_bytes=VMEM_LIMIT_BYTES):
    return pltpu.CompilerParams(dimension_semantics=semantics, vmem_limit_bytes=vmem_limit_bytes)


def _rmsnorm(x, g):
    return x * lax.rsqrt(jnp.mean(x * x, axis=-1, keepdims=True) + EPS) * g


def _sigmoid(x):
    return 1.0 / (1.0 + jnp.exp(-x))


def _mm_hi(a, b):
    return jnp.dot(a, b, precision=HIGHEST, preferred_element_type=F32)


def _norm_matmul_kernel(x_ref, g_ref, w_ref, cs_ref, o_ref, xn_ref):
    @pl.when(pl.program_id(1) == 0)
    def _():
        xn_ref[...] = _rmsnorm(x_ref[...], g_ref[...]).astype(BF16)

    acc = jnp.dot(xn_ref[...], w_ref[...], preferred_element_type=F32)
    o_ref[...] = (acc * cs_ref[...]).astype(o_ref.dtype)


def _norm_matmul(x, g, w_all, layer, colscale, out_dtype, tm=1024, tn=512):
    T, D = x.shape
    N = w_all.shape[2]
    return pl.pallas_call(
        _norm_matmul_kernel,
        grid=(T // tm, N // tn),
        in_specs=[
            pl.BlockSpec((tm, D), lambda i, j: (i, 0)),
            pl.BlockSpec((1, D), lambda i, j: (0, 0)),
            pl.BlockSpec((None, D, tn), lambda i, j: (layer, 0, j)),
            pl.BlockSpec((1, tn), lambda i, j: (0, j)),
        ],
        out_specs=pl.BlockSpec((tm, tn), lambda i, j: (i, j)),
        out_shape=jax.ShapeDtypeStruct((T, N), out_dtype),
        scratch_shapes=[pltpu.VMEM((tm, D), BF16)],
        compiler_params=_params("parallel", "arbitrary"),
    )(x, g, w_all, colscale)


def _small_kernel(x_ref, g_ref, w_ref, bias_ref, alog_ref, o_ref, rows_ref, carry_ref, *,
                  blocks_per_seq):
    i = pl.program_id(0)
    tm = x_ref.shape[0]

    @pl.when(i % blocks_per_seq == 0)
    def _():
        carry_ref[...] = jnp.zeros_like(carry_ref)

    xn = _rmsnorm(x_ref[...], g_ref[...]).astype(BF16)
    t = jnp.dot(xn, w_ref[...], preferred_element_type=F32) + bias_ref[...]
    lane = lax.broadcasted_iota(jnp.int32, (tm, LANES), 1)
    e = jnp.log1p(jnp.exp(-jnp.abs(t)))
    logsig = jnp.minimum(t, 0.0) - e
    softplus = jnp.maximum(t, 0.0) + e
    g_decay = -jnp.exp(alog_ref[...]) * softplus
    is_logsig = (lane < SM_GDN_B) | ((lane >= SM_ML_F) & (lane < SM_ML_F + HEADS))
    val = jnp.where(is_logsig, logsig,
                    jnp.where(lane < SM_GDN_A, _sigmoid(t),
                              jnp.where(lane < SM_ML_I, g_decay, t)))

    r = lax.broadcasted_iota(jnp.int32, (CHUNK, CHUNK), 0)
    c = lax.broadcasted_iota(jnp.int32, (CHUNK, CHUNK), 1)
    tri = jnp.where(c <= r, 1.0, 0.0).astype(F32)
    chunk_sums = [_mm_hi(tri, val[ci * CHUNK:(ci + 1) * CHUNK, :]) for ci in range(tm // CHUNK)]
    offset = carry_ref[...]
    full_sums = []
    for cs in chunk_sums:
        full_sums.append(cs + offset)
        offset = offset + cs[CHUNK - 1:CHUNK, :]
    carry_ref[...] = offset
    cs_chunk = jnp.concatenate(chunk_sums, axis=0)
    cs_full = jnp.concatenate(full_sums, axis=0)

    is_chunk_cs = ((lane >= SM_GDN_A) & (lane < SM_ML_I)) | ((lane >= SM_ML_F) & (lane < SM_ML_F + HEADS))
    out = jnp.where(lane < SM_GDN_B, cs_full, jnp.where(is_chunk_cs, cs_chunk, val))
    o_ref[...] = out
    for ci in range(tm // CHUNK):
        rows_ref[ci] = out[ci * CHUNK:(ci + 1) * CHUNK, :].T


def _small_gates(x, g, w_all, layer, bias, alog, seq, tm=512):
    T, D = x.shape
    return pl.pallas_call(
        functools.partial(_small_kernel, blocks_per_seq=seq // tm),
        grid=(T // tm,),
        in_specs=[
            pl.BlockSpec((tm, D), lambda i: (i, 0)),
            pl.BlockSpec((1, D), lambda i: (0, 0)),
            pl.BlockSpec((None, D, LANES), lambda i: (layer, 0, 0)),
            pl.BlockSpec((1, LANES), lambda i: (0, 0)),
            pl.BlockSpec((1, LANES), lambda i: (0, 0)),
        ],
        out_specs=[pl.BlockSpec((tm, LANES), lambda i: (i, 0)),
                   pl.BlockSpec((tm // CHUNK, LANES, CHUNK), lambda i: (i, 0, 0))],
        out_shape=[jax.ShapeDtypeStruct((T, LANES), F32),
                   jax.ShapeDtypeStruct((T // CHUNK, LANES, CHUNK), F32)],
        scratch_shapes=[pltpu.VMEM((1, LANES), F32)],
        compiler_params=_params("arbitrary"),
    )(x, g, w_all, bias, alog)


def _conv_taps(h, c_ref, cols):
    taps = c_ref.shape[0]
    tm = h.shape[0] - HALO
    acc = None
    for t in range(taps):
        start = HALO - (taps - 1) + t
        term = h[start:start + tm, :] * c_ref[t:t + 1, cols]
        acc = term if acc is None else acc + term
    return acc


def _column_groups(width):
    return [slice(c * MXU_WIDTH, (c + 1) * MXU_WIDTH) for c in range(width // MXU_WIDTH)]


def _norm_with_halo(x_ref, xh_ref, g_ref, xn_ref, at_seq_start):
    xn_ref[HALO:, :] = _rmsnorm(x_ref[...], g_ref[...]).astype(BF16)
    keep = jnp.where(at_seq_start, 0.0, 1.0)
    xn_ref[:HALO, :] = (_rmsnorm(xh_ref[...], g_ref[...]) * keep).astype(BF16)


def _proj_conv_kernel(x_ref, xh_ref, g_ref, w_ref, c_ref, o_ref, xn_ref, *, blocks_per_seq):
    @pl.when(pl.program_id(1) == 0)
    def _():
        _norm_with_halo(x_ref, xh_ref, g_ref, xn_ref, pl.program_id(0) % blocks_per_seq == 0)

    xn = xn_ref[...]
    groups = _column_groups(w_ref.shape[1])
    hidden = [jnp.dot(xn, w_ref[:, cs], preferred_element_type=F32) for cs in groups]
    for cs, h in zip(groups, hidden):
        u = _conv_taps(h, c_ref, cs)
        o_ref[:, cs] = (u * _sigmoid(u)).astype(o_ref.dtype)


def _proj_conv(x, g, w_all, layer, conv_w, seq, tm=1024, tn=1024):
    T, D = x.shape
    taps = conv_w.shape[0]
    N = w_all.shape[2]
    halo_blocks = tm // HALO
    return pl.pallas_call(
        functools.partial(_proj_conv_kernel, blocks_per_seq=seq // tm),
        grid=(T // tm, N // tn),
        in_specs=[
            pl.BlockSpec((tm, D), lambda i, j: (i, 0)),
            pl.BlockSpec((HALO, D), lambda i, j: (jnp.maximum(i * halo_blocks - 1, 0), 0)),
            pl.BlockSpec((1, D), lambda i, j: (0, 0)),
            pl.BlockSpec((None, D, tn), lambda i, j: (layer, 0, j)),
            pl.BlockSpec((taps, tn), lambda i, j: (0, j)),
        ],
        out_specs=pl.BlockSpec((tm, tn), lambda i, j: (i, j)),
        out_shape=jax.ShapeDtypeStruct((T, N), BF16),
        scratch_shapes=[pltpu.VMEM((tm + HALO, D), BF16)],
        compiler_params=_params("parallel", "arbitrary"),
    )(x, x, g, w_all, conv_w)


NEG_BIG = -1e30


LOG2E = 1.4426950408889634


def _fox_kernel(q_ref, k_ref, v_ref, f_ref, o_ref, m_ref, acc_ref, *, tq, tk, hps):
    qi = pl.program_id(2)
    ones = jnp.ones((tk, HEAD_DIM), BF16)
    lane_tiles = tk // LANES
    m_ref[...] = jnp.full_like(m_ref, NEG_BIG)
    acc_ref[...] = jnp.zeros_like(acc_ref)

    def block(j, masked):
        start = pl.multiple_of(j * tk, tk)
        heads = [slice(h * HEAD_DIM, (h + 1) * HEAD_DIM) for h in range(hps)]
        scores = []
        for h, hs in enumerate(heads):
            s = lax.dot_general(q_ref[:, hs], k_ref[pl.ds(start, tk), hs],
                                (((1,), (1,)), ((), ())), preferred_element_type=F32)
            s = s - f_ref[h, pl.ds(j, 1), :] * LOG2E
            if masked:
                row = lax.broadcasted_iota(jnp.int32, (tq, tk), 0)
                col = lax.broadcasted_iota(jnp.int32, (tq, tk), 1)
                s = jnp.where(col <= row, s, NEG_BIG)
            scores.append([s[:, c * LANES:(c + 1) * LANES] for c in range(lane_tiles)])
        probs, alphas = [], []
        for h, tiles in enumerate(scores):
            m_prev = m_ref[h]
            m_tile = functools.reduce(jnp.maximum, tiles)
            m_new = jnp.maximum(m_prev, jnp.max(m_tile, axis=-1, keepdims=True))
            m_ref[h] = m_new
            alphas.append(jnp.exp2(m_prev - m_new))
            probs.append(jnp.concatenate([jnp.exp2(t - m_new).astype(BF16) for t in tiles], axis=1))
        for h, hs in enumerate(heads):
            v_ext = jnp.concatenate([v_ref[pl.ds(start, tk), hs], ones], axis=1)
            pv = jnp.dot(probs[h], v_ext, preferred_element_type=F32)
            acc_ref[h] = jnp.concatenate([alphas[h], alphas[h]], axis=1) * acc_ref[h] + pv

    def full_block(j, carry):
        block(j, masked=False)
        return carry

    lax.fori_loop(0, qi, full_block, 0)
    block(qi, masked=True)
    for h in range(hps):
        acc = acc_ref[h]
        o_ref[:, h * HEAD_DIM:(h + 1) * HEAD_DIM] = (
            acc[:, :HEAD_DIM] / acc[:, HEAD_DIM:]).astype(o_ref.dtype)


def _fox_attention(qkv, f_cum, batch, seq, *, q_col, k_col, v_col, tq=512, hps=2):
    T = qkv.shape[0]
    nq = seq // tq
    width = hps * HEAD_DIM
    f_blocks = f_cum.reshape(batch, HEADS, nq, tq)
    return pl.pallas_call(
        functools.partial(_fox_kernel, tq=tq, tk=tq, hps=hps),
        grid=(batch, HEADS // hps, nq),
        in_specs=[
            pl.BlockSpec((tq, width), lambda b, h, qi: (b * nq + qi, q_col + h)),
            pl.BlockSpec((seq, width), lambda b, h, qi: (b, k_col + h)),
            pl.BlockSpec((seq, width), lambda b, h, qi: (b, v_col + h)),
            pl.BlockSpec((None, hps, nq, tq), lambda b, h, qi: (b, h, 0, 0)),
        ],
        out_specs=pl.BlockSpec((tq, width), lambda b, h, qi: (b * nq + qi, h)),
        out_shape=jax.ShapeDtypeStruct((T, HEADS * HEAD_DIM), BF16),
        scratch_shapes=[pltpu.VMEM((hps, tq, LANES), F32),
                        pltpu.VMEM((hps, tq, 2 * HEAD_DIM), F32)],
        compiler_params=_params("parallel", "parallel", "arbitrary"),
    )(qkv, qkv, qkv, f_blocks)


def _bmm(a, b):
    return jnp.einsum("gmk,gkn->gmn", a.astype(BF16), b.astype(BF16), preferred_element_type=F32)


def _bmm_nt(a, b):
    return jnp.einsum("gmk,gnk->gmn", a.astype(BF16), b.astype(BF16), preferred_element_type=F32)


def _bmm_tn(a, b):
    return jnp.einsum("gkm,gkn->gmn", a.astype(BF16), b.astype(BF16), preferred_element_type=F32)


def _unit_lower_inverse(m, eye, r, c):
    x = _inverse_pairs(m, eye, r, c)
    b = 2
    while b < m.shape[-1]:
        x = _inverse_double(x, m, r, c, b)
        b *= 2
    return x


def _inverse_pairs(m, eye, r, c):
    return eye - jnp.where((r >> 1) == (c >> 1), m, 0.0)


def _inverse_double(x, m, r, c, b):
    shift = b.bit_length() - 1
    joins = ((r >> (shift + 1)) == (c >> (shift + 1))) & ((r >> shift) != (c >> shift))
    return x - _bmm(_bmm(x, jnp.where(joins, m, 0.0)), x)


STEP_CHUNKS = 4


def _chunk_head_stack(ref, width=HEAD_DIM):
    return jnp.stack([ref[ci * CHUNK:(ci + 1) * CHUNK, h * width:(h + 1) * width]
                      for ci in range(STEP_CHUNKS) for h in range(HEADS)])


def _chunk_head_cols(col, start):
    return jnp.stack([col[ci * CHUNK:(ci + 1) * CHUNK, start + h:start + h + 1]
                      for ci in range(STEP_CHUNKS) for h in range(HEADS)])


def _chunk_head_rows(row_ref):
    return jnp.concatenate([row_ref[ci] for ci in range(STEP_CHUNKS)], axis=0)[:, None, :]


def _store_heads(o_ref, y):
    for ci in range(STEP_CHUNKS):
        for h in range(HEADS):
            o_ref[ci * CHUNK:(ci + 1) * CHUNK, h * HEAD_DIM:(h + 1) * HEAD_DIM] = y[ci * HEADS + h]


def _gdn_kernel(q_ref, k_ref, v_ref, z_ref, col_ref, rowg_ref, ng_ref, o_ref, s_ref):
    @pl.when(pl.program_id(1) == 0)
    def _():
        s_ref[...] = jnp.zeros_like(s_ref)

    L = CHUNK
    r = lax.broadcasted_iota(jnp.int32, (1, L, L), 1)
    c = lax.broadcasted_iota(jnp.int32, (1, L, L), 2)
    tri = c <= r
    eye = jnp.where(c == r, 1.0, 0.0).astype(F32)
    col = col_ref[...]
    q = _chunk_head_stack(q_ref).astype(F32)
    k = _chunk_head_stack(k_ref).astype(F32)
    v = _chunk_head_stack(v_ref).astype(F32)
    qn = q * lax.rsqrt(jnp.sum(q * q, axis=-1, keepdims=True) + EPS) * (HEAD_DIM ** -0.5)
    kn = k * lax.rsqrt(jnp.sum(k * k, axis=-1, keepdims=True) + EPS)
    beta = _chunk_head_cols(col, SM_GDN_B)
    gc = _chunk_head_cols(col, SM_GDN_A)
    gr = _chunk_head_rows(rowg_ref)
    g_last = gc[:, L - 1:L, :]
    decay = jnp.where(tri, jnp.exp(jnp.where(tri, gc - gr, 0.0)), 0.0)
    eg = jnp.exp(gc)
    kb = kn * beta
    m = jnp.where(c < r, _bmm_nt(kb, kn) * decay, 0.0)
    t_inv = _unit_lower_inverse(m, eye, r, c)
    x = _bmm(t_inv, jnp.concatenate([v * beta, kb * eg], axis=-1))
    u = x[:, :, :HEAD_DIM]
    w = x[:, :, HEAD_DIM:]
    a_qk = _bmm_nt(qn, kn) * decay
    qg = qn * eg
    kg = kn * jnp.exp(g_last - gc)
    carry = jnp.exp(g_last)
    state = s_ref[...]
    outs = []
    for ci in range(STEP_CHUNKS):
        hs = slice(ci * HEADS, (ci + 1) * HEADS)
        v_new = u[hs] - _bmm(w[hs], state)
        outs.append(_bmm(qg[hs], state) + _bmm(a_qk[hs], v_new))
        state = state * carry[hs] + _bmm_tn(kg[hs], v_new)
    s_ref[...] = state
    o = jnp.concatenate(outs, axis=0)
    z = _chunk_head_stack(z_ref).astype(F32)
    _store_heads(o_ref, (_rmsnorm(o, ng_ref[...]) * (z * _sigmoid(z))).astype(o_ref.dtype))


def _gdn(qkv, z_src, col_small, row_small, norm_g, batch, seq, *, z_col):
    T = qkv.shape[0]
    rows = STEP_CHUNKS * CHUNK
    ns = seq // rows
    width = HEADS * HEAD_DIM
    row = lambda b, s: b * ns + s
    return pl.pallas_call(
        _gdn_kernel,
        grid=(batch, ns),
        in_specs=[
            pl.BlockSpec((rows, width), lambda b, s: (row(b, s), 0)),
            pl.BlockSpec((rows, width), lambda b, s: (row(b, s), 1)),
            pl.BlockSpec((rows, width), lambda b, s: (row(b, s), 2)),
            pl.BlockSpec((rows, width), lambda b, s: (row(b, s), z_col)),
            pl.BlockSpec((rows, LANES), lambda b, s: (row(b, s), 0)),
            pl.BlockSpec((STEP_CHUNKS, HEADS, CHUNK), lambda b, s: (row(b, s), SM_GDN_A // HEADS, 0)),
            pl.BlockSpec((1, HEAD_DIM), lambda b, s: (0, 0)),
        ],
        out_specs=pl.BlockSpec((rows, width), lambda b, s: (row(b, s), 0)),
        out_shape=jax.ShapeDtypeStruct((T, width), BF16),
        scratch_shapes=[pltpu.VMEM((HEADS, HEAD_DIM, HEAD_DIM), F32)],
        compiler_params=_params("parallel", "arbitrary"),
    )(qkv, qkv, qkv, z_src, col_small, row_small, norm_g)


def _mlstm_kernel(q_ref, k_ref, v_ref, og_ref, col_ref, rowi_ref, rowb_ref, ng_ref, o_ref,
                  c_ref, m_ref):
    @pl.when(pl.program_id(1) == 0)
    def _():
        c_ref[...] = jnp.zeros_like(c_ref)
        m_ref[...] = jnp.zeros_like(m_ref)

    L = CHUNK
    r = lax.broadcasted_iota(jnp.int32, (1, L, L), 1)
    c = lax.broadcasted_iota(jnp.int32, (1, L, L), 2)
    tri = c <= r
    col = col_ref[...]
    q = _chunk_head_stack(q_ref, ML_QK_DIM).astype(F32)
    k = _chunk_head_stack(k_ref, ML_QK_DIM).astype(F32)
    v = _chunk_head_stack(v_ref)
    v = jnp.concatenate([v, jnp.ones_like(v)], axis=-1)
    ic = _chunk_head_cols(col, SM_ML_I)
    bc = _chunk_head_cols(col, SM_ML_F)
    ir = _chunk_head_rows(rowi_ref)
    br = _chunk_head_rows(rowb_ref)
    dm = jnp.where(tri, bc - br + ir, -jnp.inf)
    dmax = jnp.max(dm, axis=-1, keepdims=True)
    qk = _bmm_nt(q, k)
    m_state = m_ref[...]
    m_rows, m_lasts = [], []
    for ci in range(STEP_CHUNKS):
        hs = slice(ci * HEADS, (ci + 1) * HEADS)
        m_r = jnp.maximum(bc[hs] + m_state, dmax[hs])
        m_rows.append(m_r)
        m_state = m_r[:, L - 1:L, :]
        m_lasts.append(m_state)
    m_r = jnp.concatenate(m_rows, axis=0)
    m_last = jnp.concatenate(m_lasts, axis=0)
    m_prev = jnp.concatenate([m_ref[...]] + m_lasts[:-1], axis=0)
    w_inter = jnp.exp(bc + m_prev - m_r)
    sm = jnp.exp(dm - m_r) * qk
    intra = _bmm(sm, v)
    wk = jnp.exp(bc[:, L - 1:L, :] - bc + ic - m_last) * k
    kv = _bmm_tn(wk, v)
    w_last = w_inter[:, L - 1:L, :]
    floor = jnp.exp(-m_r)
    c_state = c_ref[...]
    outs = []
    for ci in range(STEP_CHUNKS):
        hs = slice(ci * HEADS, (ci + 1) * HEADS)
        num_den = w_inter[hs] * _bmm(q[hs], c_state) + intra[hs]
        den = jnp.maximum(jnp.abs(num_den[:, :, HEAD_DIM:]), floor[hs])
        outs.append(num_den[:, :, :HEAD_DIM] / den)
        c_state = w_last[hs] * c_state + kv[hs]
    c_ref[...] = c_state
    m_ref[...] = m_state
    h_tilde = jnp.concatenate(outs, axis=0)
    og = _chunk_head_stack(og_ref).astype(F32)
    _store_heads(o_ref, _rmsnorm(_sigmoid(og) * h_tilde, ng_ref[...]).astype(o_ref.dtype))


def _mlstm(qkv, og_src, col_small, row_small, norm_g, batch, seq, *, q_col, k_col, v_col, og_col):
    T = qkv.shape[0]
    rows = STEP_CHUNKS * CHUNK
    ns = seq // rows
    qk_width = HEADS * ML_QK_DIM
    width = HEADS * HEAD_DIM
    row = lambda b, s: b * ns + s
    return pl.pallas_call(
        _mlstm_kernel,
        grid=(batch, ns),
        in_specs=[
            pl.BlockSpec((rows, qk_width), lambda b, s: (row(b, s), q_col)),
            pl.BlockSpec((rows, qk_width), lambda b, s: (row(b, s), k_col)),
            pl.BlockSpec((rows, width), lambda b, s: (row(b, s), v_col)),
            pl.BlockSpec((rows, width), lambda b, s: (row(b, s), og_col)),
            pl.BlockSpec((rows, LANES), lambda b, s: (row(b, s), 0)),
            pl.BlockSpec((STEP_CHUNKS, HEADS, CHUNK), lambda b, s: (row(b, s), SM_ML_I // HEADS, 0)),
            pl.BlockSpec((STEP_CHUNKS, HEADS, CHUNK), lambda b, s: (row(b, s), SM_ML_F // HEADS, 0)),
            pl.BlockSpec((1, HEAD_DIM), lambda b, s: (0, 0)),
        ],
        out_specs=pl.BlockSpec((rows, width), lambda b, s: (row(b, s), 0)),
        out_shape=jax.ShapeDtypeStruct((T, width), BF16),
        scratch_shapes=[pltpu.VMEM((HEADS, ML_QK_DIM, 2 * HEAD_DIM), F32),
                        pltpu.VMEM((HEADS, 1, 1), F32)],
        compiler_params=_params("parallel", "arbitrary"),
    )(qkv, qkv, qkv, og_src, col_small, row_small, row_small, norm_g)


def _chunk_mixers_kernel(gq_ref, gk_ref, gv_ref, z_ref, mq_ref, mk_ref, mv_ref, og_ref, col_ref,
                         rowg_ref, rowi_ref, rowb_ref, gng_ref, mng_ref, yg_ref, ym_ref,
                         s_ref, c_ref, m_ref):
    @pl.when(pl.program_id(1) == 0)
    def _():
        s_ref[...] = jnp.zeros_like(s_ref)
        c_ref[...] = jnp.zeros_like(c_ref)
        m_ref[...] = jnp.zeros_like(m_ref)

    L = CHUNK
    r = lax.broadcasted_iota(jnp.int32, (1, L, L), 1)
    c = lax.broadcasted_iota(jnp.int32, (1, L, L), 2)
    tri = c <= r
    eye = jnp.where(c == r, 1.0, 0.0).astype(F32)
    col = col_ref[...]

    q = _chunk_head_stack(gq_ref).astype(F32)
    k = _chunk_head_stack(gk_ref).astype(F32)
    v = _chunk_head_stack(gv_ref).astype(F32)
    qn = q * lax.rsqrt(jnp.sum(q * q, axis=-1, keepdims=True) + EPS) * (HEAD_DIM ** -0.5)
    kn = k * lax.rsqrt(jnp.sum(k * k, axis=-1, keepdims=True) + EPS)
    beta = _chunk_head_cols(col, SM_GDN_B)
    gc = _chunk_head_cols(col, SM_GDN_A)
    gr = _chunk_head_rows(rowg_ref)
    g_last = gc[:, L - 1:L, :]
    decay = jnp.where(tri, jnp.exp(jnp.where(tri, gc - gr, 0.0)), 0.0)
    eg = jnp.exp(gc)
    kb = kn * beta
    m = jnp.where(c < r, _bmm_nt(kb, kn) * decay, 0.0)
    t_inv = _inverse_pairs(m, eye, r, c)

    mq = _chunk_head_stack(mq_ref, ML_QK_DIM).astype(F32)
    mk = _chunk_head_stack(mk_ref, ML_QK_DIM).astype(F32)
    mv = _chunk_head_stack(mv_ref)
    mv = jnp.concatenate([mv, jnp.ones_like(mv)], axis=-1)
    ic = _chunk_head_cols(col, SM_ML_I)
    bc = _chunk_head_cols(col, SM_ML_F)
    ir = _chunk_head_rows(rowi_ref)
    br = _chunk_head_rows(rowb_ref)
    dm = jnp.where(tri, bc - br + ir, -jnp.inf)
    dmax = jnp.max(dm, axis=-1, keepdims=True)
    qk = _bmm_nt(mq, mk)

    t_inv = _inverse_double(t_inv, m, r, c, 2)
    t_inv = _inverse_double(t_inv, m, r, c, 4)

    m_state = m_ref[...]
    m_rows, m_lasts = [], []
    for ci in range(STEP_CHUNKS):
        hs = slice(ci * HEADS, (ci + 1) * HEADS)
        m_r = jnp.maximum(bc[hs] + m_state, dmax[hs])
        m_rows.append(m_r)
        m_state = m_r[:, L - 1:L, :]
        m_lasts.append(m_state)
    m_r = jnp.concatenate(m_rows, axis=0)
    m_last = jnp.concatenate(m_lasts, axis=0)
    m_prev = jnp.concatenate([m_ref[...]] + m_lasts[:-1], axis=0)
    m_ref[...] = m_state
    w_inter = jnp.exp(bc + m_prev - m_r)
    sm = jnp.exp(dm - m_r) * qk
    intra = _bmm(sm, mv)

    t_inv = _inverse_double(t_inv, m, r, c, 8)

    wk = jnp.exp(bc[:, L - 1:L, :] - bc + ic - m_last) * mk
    kv = _bmm_tn(wk, mv)
    w_last = w_inter[:, L - 1:L, :]
    floor = jnp.exp(-m_r)

    t_inv = _inverse_double(t_inv, m, r, c, 16)
    t_inv = _inverse_double(t_inv, m, r, c, 32)
    x = _bmm(t_inv, jnp.concatenate([v * beta, kb * eg], axis=-1))
    u = x[:, :, :HEAD_DIM]
    w = x[:, :, HEAD_DIM:]
    a_qk = _bmm_nt(qn, kn) * decay
    qg = qn * eg
    kg = kn * jnp.exp(g_last - gc)
    carry = jnp.exp(g_last)

    state = s_ref[...]
    c_state = c_ref[...]
    g_outs, m_outs = [], []
    for ci in range(STEP_CHUNKS):
        hs = slice(ci * HEADS, (ci + 1) * HEADS)
        v_new = u[hs] - _bmm(w[hs], state)
        num_den = w_inter[hs] * _bmm(mq[hs], c_state) + intra[hs]
        g_outs.append(_bmm(qg[hs], state) + _bmm(a_qk[hs], v_new))
        state = state * carry[hs] + _bmm_tn(kg[hs], v_new)
        den = jnp.maximum(jnp.abs(num_den[:, :, HEAD_DIM:]), floor[hs])
        m_outs.append(num_den[:, :, :HEAD_DIM] / den)
        c_state = w_last[hs] * c_state + kv[hs]
    s_ref[...] = state
    c_ref[...] = c_state

    o = jnp.concatenate(g_outs, axis=0)
    z = _chunk_head_stack(z_ref).astype(F32)
    _store_heads(yg_ref, (_rmsnorm(o, gng_ref[...]) * (z * _sigmoid(z))).astype(yg_ref.dtype))
    h_tilde = jnp.concatenate(m_outs, axis=0)
    og = _chunk_head_stack(og_ref).astype(F32)
    _store_heads(ym_ref, _rmsnorm(_sigmoid(og) * h_tilde, mng_ref[...]).astype(ym_ref.dtype))


def _chunk_mixers(gdn_qkv, proj, col_small, row_small, gdn_norm_g, ml_norm_g, batch, seq, *,
                  z_col, q_col, k_col, v_col, og_col):
    T = proj.shape[0]
    rows = STEP_CHUNKS * CHUNK
    ns = seq // rows
    qk_width = HEADS * ML_QK_DIM
    width = HEADS * HEAD_DIM
    row = lambda b, s: b * ns + s
    wide = lambda col: pl.BlockSpec((rows, width), lambda b, s: (row(b, s), col))
    rows_spec = lambda seg: pl.BlockSpec((STEP_CHUNKS, HEADS, CHUNK),
                                         lambda b, s: (row(b, s), seg // HEADS, 0))
    norm_spec = pl.BlockSpec((1, HEAD_DIM), lambda b, s: (0, 0))
    out_spec = pl.BlockSpec((rows, width), lambda b, s: (row(b, s), 0))
    out_shape = jax.ShapeDtypeStruct((T, width), BF16)
    return pl.pallas_call(
        _chunk_mixers_kernel,
        grid=(batch, ns),
        in_specs=[
            wide(0), wide(1), wide(2), wide(z_col),
            pl.BlockSpec((rows, qk_width), lambda b, s: (row(b, s), q_col)),
            pl.BlockSpec((rows, qk_width), lambda b, s: (row(b, s), k_col)),
            wide(v_col), wide(og_col),
            pl.BlockSpec((rows, LANES), lambda b, s: (row(b, s), 0)),
            rows_spec(SM_GDN_A), rows_spec(SM_ML_I), rows_spec(SM_ML_F),
            norm_spec, norm_spec,
        ],
        out_specs=[out_spec, out_spec],
        out_shape=[out_shape, out_shape],
        scratch_shapes=[pltpu.VMEM((HEADS, HEAD_DIM, HEAD_DIM), F32),
                        pltpu.VMEM((HEADS, ML_QK_DIM, 2 * HEAD_DIM), F32),
                        pltpu.VMEM((HEADS, 1, 1), F32)],
        compiler_params=_params("parallel", "arbitrary"),
    )(gdn_qkv, gdn_qkv, gdn_qkv, proj, proj, proj, proj, proj, col_small,
      row_small, row_small, row_small, gdn_norm_g, ml_norm_g)


def _merge_kernel(yf_ref, yg_ref, ym_ref, wf_ref, wg_ref, wm_ref, g0_ref, g1_ref, g2_ref,
                  b0_ref, b1_ref, b2_ref, o_ref):
    def branch(y_ref, w_ref, g_ref, b_ref):
        return _sigmoid(g_ref[...] + b_ref[...]) * jnp.dot(y_ref[...], w_ref[...],
                                                          preferred_element_type=F32)

    y = (branch(yf_ref, wf_ref, g0_ref, b0_ref) + branch(yg_ref, wg_ref, g1_ref, b1_ref)
         + branch(ym_ref, wm_ref, g2_ref, b2_ref))
    o_ref[...] = y.astype(o_ref.dtype)


def _merge(y_fox, y_gdn, y_ml, w_fox, w_gdn, w_ml, layer, gate_src, gate_bias, d_model, *,
           gate_col, tm=1024, tn=512):
    T, K = y_fox.shape
    nj = d_model // tn
    y_spec = pl.BlockSpec((tm, K), lambda i, j: (i, 0))
    w_spec = pl.BlockSpec((None, K, tn), lambda i, j: (layer, 0, j))
    gate_specs = [pl.BlockSpec((tm, tn), lambda i, j, n=n: (i, gate_col + n * nj + j))
                  for n in range(N_BRANCH)]
    bias_specs = [pl.BlockSpec((1, tn), lambda i, j, n=n: (0, n * nj + j)) for n in range(N_BRANCH)]
    return pl.pallas_call(
        _merge_kernel,
        grid=(T // tm, nj),
        in_specs=[y_spec, y_spec, y_spec, w_spec, w_spec, w_spec] + gate_specs + bias_specs,
        out_specs=pl.BlockSpec((tm, tn), lambda i, j: (i, j)),
        out_shape=jax.ShapeDtypeStruct((T, d_model), BF16),
        compiler_params=_params("parallel", "parallel"),
    )(y_fox, y_gdn, y_ml, w_fox, w_gdn, w_ml, gate_src, gate_src, gate_src,
      gate_bias, gate_bias, gate_bias)


def _matmul_residual_kernel(a_ref, w_ref, x_ref, o_ref):
    o_ref[...] = x_ref[...] + jnp.dot(a_ref[...], w_ref[...], preferred_element_type=F32)


def _matmul_residual(a, w_all, layer, x, tm=1024, tn=512):
    T, K = a.shape
    N = w_all.shape[2]
    return pl.pallas_call(
        _matmul_residual_kernel,
        grid=(T // tm, N // tn),
        in_specs=[
            pl.BlockSpec((tm, K), lambda i, j: (i, 0)),
            pl.BlockSpec((None, K, tn), lambda i, j: (layer, 0, j)),
            pl.BlockSpec((tm, tn), lambda i, j: (i, j)),
        ],
        out_specs=pl.BlockSpec((tm, tn), lambda i, j: (i, j)),
        out_shape=jax.ShapeDtypeStruct((T, N), F32),
        compiler_params=_params("parallel", "parallel"),
    )(a, w_all, x)


def _final_norm_kernel(x_ref, g_ref, o_ref):
    o_ref[...] = _rmsnorm(x_ref[...], g_ref[...])


def _final_norm(x, g, tm=512):
    T, D = x.shape
    return pl.pallas_call(
        _final_norm_kernel,
        grid=(T // tm,),
        in_specs=[pl.BlockSpec((tm, D), lambda i: (i, 0)), pl.BlockSpec((1, D), lambda i: (0, 0))],
        out_specs=pl.BlockSpec((tm, D), lambda i: (i, 0)),
        out_shape=jax.ShapeDtypeStruct((T, D), F32),
        compiler_params=_params("parallel"),
    )(x, g)


def _ffn_kernel(x_ref, xh_ref, g_ref, wu_ref, wg_ref, cu_ref, cg_ref, wd_ref, o_ref, xn_ref, *,
                blocks_per_seq):
    @pl.when(pl.program_id(1) == 0)
    def _():
        _norm_with_halo(x_ref, xh_ref, g_ref, xn_ref, pl.program_id(0) % blocks_per_seq == 0)
        o_ref[...] = x_ref[...]

    xn = xn_ref[...]
    groups = _column_groups(wu_ref.shape[1])
    hidden = [(jnp.dot(xn, wu_ref[:, cs], preferred_element_type=F32),
               jnp.dot(xn, wg_ref[:, cs], preferred_element_type=F32)) for cs in groups]
    acts = []
    for cs, (hu, hg) in zip(groups, hidden):
        gt = _conv_taps(hg, cg_ref, cs)
        acts.append((gt * _sigmoid(gt) * _conv_taps(hu, cu_ref, cs)).astype(BF16))
    o_ref[...] += jnp.dot(jnp.concatenate(acts, axis=1), wd_ref[...], preferred_element_type=F32)


def _ffn(x, g, w_up, conv_w, w_down, layer, seq, tm=1024, tn=512):
    T, D = x.shape
    taps = conv_w.shape[0]
    nj = w_down.shape[1] // tn
    halo_blocks = tm // HALO
    once = pl.Buffered(1)
    return pl.pallas_call(
        functools.partial(_ffn_kernel, blocks_per_seq=seq // tm),
        grid=(T // tm, nj),
        in_specs=[
            pl.BlockSpec((tm, D), lambda i, j: (i, 0)),
            pl.BlockSpec((HALO, D), lambda i, j: (jnp.maximum(i * halo_blocks - 1, 0), 0)),
            pl.BlockSpec((1, D), lambda i, j: (0, 0)),
            pl.BlockSpec((None, D, tn), lambda i, j: (layer, 0, j)),
            pl.BlockSpec((None, D, tn), lambda i, j: (layer, 0, nj + j)),
            pl.BlockSpec((taps, tn), lambda i, j: (0, j)),
            pl.BlockSpec((taps, tn), lambda i, j: (0, nj + j)),
            pl.BlockSpec((None, tn, D), lambda i, j: (layer, j, 0)),
        ],
        out_specs=pl.BlockSpec((tm, D), lambda i, j: (i, 0), pipeline_mode=once),
        out_shape=jax.ShapeDtypeStruct((T, D), F32),
        scratch_shapes=[pltpu.VMEM((tm + HALO, D), BF16)],
        compiler_params=_params("parallel", "arbitrary", vmem_limit_bytes=VMEM_LIMIT_LARGE_BYTES),
    )(x, x, g, w_up, w_up, conv_w, conv_w, w_down)


FOX_W = HEADS * HEAD_DIM
GDN_W = HEADS * HEAD_DIM
ML_QK_W = HEADS * ML_QK_DIM
ML_V_W = HEADS * HEAD_DIM
ML_BASE = 3 * FOX_W
Z_BASE = ML_BASE + 2 * ML_QK_W + ML_V_W
GATE_BASE = Z_BASE + GDN_W + ML_V_W
FOX_HEADS_PER_STEP = 4


def _in_projection_columns(d_model):
    widths = [("fox_q", FOX_W), ("fox_k", FOX_W), ("fox_v", FOX_W), ("fox_f", HEADS),
              ("gdn_qkv", 3 * GDN_W), ("gdn_z", GDN_W), ("gdn_b", HEADS), ("gdn_a", HEADS),
              ("ml_q", ML_QK_W), ("ml_k", ML_QK_W), ("ml_v", ML_V_W), ("ml_i", HEADS),
              ("ml_f", HEADS), ("ml_o", ML_V_W), ("gate", N_BRANCH * d_model)]
    off, start = {}, 0
    for name, width in widths:
        off[name] = (start, start + width)
        start += width
    return off


def _pad_lanes(v):
    return jnp.pad(v, (0, LANES - v.shape[0])).reshape(1, LANES)


def _layer(x, layer, w, p, batch, seq):
    T, D = x.shape
    g_mix = p["norm_mix_g"].reshape(1, D)
    scale_p = jnp.concatenate([
        jnp.full((FOX_W,), HEAD_DIM ** -0.5 * LOG2E, F32), jnp.ones((2 * FOX_W,), F32),
        jnp.full((ML_QK_W,), ML_QK_DIM ** -0.5, F32),
        jnp.ones((w["proj"].shape[2] - 3 * FOX_W - ML_QK_W,), F32)]).reshape(1, -1)
    proj = _norm_matmul(x, g_mix, w["proj"], layer, scale_p, BF16, tn=1024)
    zeros8 = jnp.zeros((HEADS,), F32)
    bias_s = _pad_lanes(jnp.concatenate([p["fox_f_bias"], zeros8, p["gdn_dt_bias"],
                                         p["ml_i_bias"], p["ml_f_bias"]]))
    alog_s = _pad_lanes(jnp.concatenate([zeros8, zeros8, p["gdn_a_log"]]))
    col_small, row_small = _small_gates(x, g_mix, w["small"], layer, bias_s, alog_s, seq)
    f_cum = row_small[:, SM_FOX_F:SM_FOX_F + HEADS, :].reshape(batch, seq // CHUNK, HEADS, CHUNK)
    f_cum = f_cum.transpose(0, 2, 1, 3).reshape(batch, HEADS, seq)
    gdn_qkv = _proj_conv(x, g_mix, w["gdn_qkv"], layer, p["gdn_conv_w"], seq)

    hps = FOX_HEADS_PER_STEP
    y_fox = _fox_attention(proj, f_cum, batch, seq, q_col=0, k_col=HEADS // hps,
                           v_col=2 * HEADS // hps, hps=hps)
    y_gdn, y_ml = _chunk_mixers(
        gdn_qkv, proj, col_small, row_small, p["gdn_norm_g"].reshape(1, HEAD_DIM),
        p["ml_norm_g"].reshape(1, HEAD_DIM), batch, seq, z_col=Z_BASE // GDN_W,
        q_col=ML_BASE // ML_QK_W, k_col=ML_BASE // ML_QK_W + 1,
        v_col=(ML_BASE + 2 * ML_QK_W) // ML_V_W, og_col=(Z_BASE + GDN_W) // ML_V_W)

    tn = 512
    y = _merge(y_fox, y_gdn, y_ml, w["fox_proj"], w["gdn_proj"], w["ml_proj"], layer, proj,
               p["gate_bias"].reshape(1, -1), D, gate_col=GATE_BASE // tn, tn=tn)
    x = _matmul_residual(y, w["out"], layer, x)
    return _ffn(x, p["norm_ffn_g"].reshape(1, D), w["up"], p["ffn_conv_w"], w["down"], layer, seq)


def kernel(x, norm_mix_g, w_in, fox_f_bias, gdn_conv_w, gdn_a_log, gdn_dt_bias, gdn_norm_g,
           ml_i_bias, ml_f_bias, ml_norm_g, gate_bias, w_fox_proj, w_gdn_proj, w_ml_proj,
           w_out, norm_ffn_g, w_up, ffn_conv_w, w_down, norm_final_g):
    batch, seq, d_model = x.shape
    depth = w_in.shape[0]
    off = _in_projection_columns(d_model)
    cols = lambda a, b: w_in[:, :, off[a][0]:off[b][1]].astype(BF16)
    w_small = jnp.concatenate([cols("fox_f", "fox_f"), cols("gdn_b", "gdn_a"), cols("ml_i", "ml_f")],
                              axis=2)
    w = dict(
        proj=jnp.concatenate([cols("fox_q", "fox_v"), cols("ml_q", "ml_v"), cols("gdn_z", "gdn_z"),
                              cols("ml_o", "ml_o"), cols("gate", "gate")], axis=2),
        small=jnp.pad(w_small, ((0, 0), (0, 0), (0, LANES - w_small.shape[2]))),
        gdn_qkv=cols("gdn_qkv", "gdn_qkv"),
        fox_proj=w_fox_proj.astype(BF16), gdn_proj=w_gdn_proj.astype(BF16),
        ml_proj=w_ml_proj.astype(BF16), out=w_out.astype(BF16), up=w_up.astype(BF16),
        down=w_down.astype(BF16))
    small = dict(norm_mix_g=norm_mix_g, fox_f_bias=fox_f_bias, gdn_conv_w=gdn_conv_w,
                 gdn_a_log=gdn_a_log, gdn_dt_bias=gdn_dt_bias, gdn_norm_g=gdn_norm_g,
                 ml_i_bias=ml_i_bias, ml_f_bias=ml_f_bias, ml_norm_g=ml_norm_g,
                 gate_bias=gate_bias, norm_ffn_g=norm_ffn_g, ffn_conv_w=ffn_conv_w)
    h = x.reshape(batch * seq, d_model)
    for layer in range(depth):
        h = _layer(h, layer, w, {k: v[layer] for k, v in small.items()}, batch, seq)
    return _final_norm(h, norm_final_g.reshape(1, d_model)).reshape(batch, seq, d_model)
```

```python
import functools

import jax
import jax.numpy as jnp
from jax import lax
from jax.experimental import pallas as pl
from jax.experimental.pallas import tpu as pltpu

F32 = jnp.float32
BF16 = jnp.bfloat16
HIGHEST = lax.Precision.HIGHEST

EPS = 1e-6
HEADS = 8
HEAD_DIM = 128
ML_QK_DIM = 64
CHUNK = 64
GDN_CONV = 4
FFN_CONV = 3
N_BRANCH = 3
LANES = 128
MXU_WIDTH = 256
HALO = 16

SM_FOX_F, SM_GDN_B, SM_GDN_A, SM_ML_I, SM_ML_F = 0, 8, 16, 24, 32

V7X_VMEM_BYTES = 64 * 1024 * 1024
VMEM_LIMIT_BYTES = V7X_VMEM_BYTES * 3 // 4
VMEM_LIMIT_LARGE_BYTES = V7X_VMEM_BYTES * 7 // 8


def _params(*semantics, vmem_limit_bytes=VMEM_LIMIT_BYTES):
    return pltpu.CompilerParams(dimension_semantics=semantics, vmem_limit_bytes=vmem_limit_bytes)


def _rmsnorm(x, g):
    return x * lax.rsqrt(jnp.mean(x * x, axis=-1, keepdims=True) + EPS) * g


def _sigmoid(x):
    return 1.0 / (1.0 + jnp.exp(-x))


def _mm_hi(a, b):
    return jnp.dot(a, b, precision=HIGHEST, preferred_element_type=F32)


def _norm_matmul_kernel(x_ref, g_ref, w_ref, cs_ref, o_ref, xn_ref):
    @pl.when(pl.program_id(1) == 0)
    def _():
        xn_ref[...] = _rmsnorm(x_ref[...], g_ref[...]).astype(BF16)

    acc = jnp.dot(xn_ref[...], w_ref[...], preferred_element_type=F32)
    o_ref[...] = (acc * cs_ref[...]).astype(o_ref.dtype)


def _norm_matmul(x, g, w_all, layer, colscale, out_dtype, tm=1024, tn=512):
    T, D = x.shape
    N = w_all.shape[2]
    return pl.pallas_call(
        _norm_matmul_kernel,
        grid=(T // tm, N // tn),
        in_specs=[
            pl.BlockSpec((tm, D), lambda i, j: (i, 0)),
            pl.BlockSpec((1, D), lambda i, j: (0, 0)),
            pl.BlockSpec((None, D, tn), lambda i, j: (layer, 0, j)),
            pl.BlockSpec((1, tn), lambda i, j: (0, j)),
        ],
        out_specs=pl.BlockSpec((tm, tn), lambda i, j: (i, j)),
        out_shape=jax.ShapeDtypeStruct((T, N), out_dtype),
        scratch_shapes=[pltpu.VMEM((tm, D), BF16)],
        compiler_params=_params("parallel", "arbitrary"),
    )(x, g, w_all, colscale)


def _small_kernel(x_ref, g_ref, w_ref, bias_ref, alog_ref, o_ref, rows_ref, carry_ref, *,
                  blocks_per_seq):
    i = pl.program_id(0)
    tm = x_ref.shape[0]

    @pl.when(i % blocks_per_seq == 0)
    def _():
        carry_ref[...] = jnp.zeros_like(carry_ref)

    xn = _rmsnorm(x_ref[...], g_ref[...]).astype(BF16)
    t = jnp.dot(xn, w_ref[...], preferred_element_type=F32) + bias_ref[...]
    lane = lax.broadcasted_iota(jnp.int32, (tm, LANES), 1)
    e = jnp.log1p(jnp.exp(-jnp.abs(t)))
    logsig = jnp.minimum(t, 0.0) - e
    softplus = jnp.maximum(t, 0.0) + e
    g_decay = -jnp.exp(alog_ref[...]) * softplus
    is_logsig = (lane < SM_GDN_B) | ((lane >= SM_ML_F) & (lane < SM_ML_F + HEADS))
    val = jnp.where(is_logsig, logsig,
                    jnp.where(lane < SM_GDN_A, _sigmoid(t),
                              jnp.where(lane < SM_ML_I, g_decay, t)))

    r = lax.broadcasted_iota(jnp.int32, (CHUNK, CHUNK), 0)
    c = lax.broadcasted_iota(jnp.int32, (CHUNK, CHUNK), 1)
    tri = jnp.where(c <= r, 1.0, 0.0).astype(F32)
    chunk_sums = [_mm_hi(tri, val[ci * CHUNK:(ci + 1) * CHUNK, :]) for ci in range(tm // CHUNK)]
    offset = carry_ref[...]
    full_sums = []
    for cs in chunk_sums:
        full_sums.append(cs + offset)
        offset = offset + cs[CHUNK - 1:CHUNK, :]
    carry_ref[...] = offset
    cs_chunk = jnp.concatenate(chunk_sums, axis=0)
    cs_full = jnp.concatenate(full_sums, axis=0)

    is_chunk_cs = ((lane >= SM_GDN_A) & (lane < SM_ML_I)) | ((lane >= SM_ML_F) & (lane < SM_ML_F + HEADS))
    out = jnp.where(lane < SM_GDN_B, cs_full, jnp.where(is_chunk_cs, cs_chunk, val))
    o_ref[...] = out
    for ci in range(tm // CHUNK):
        rows_ref[ci] = out[ci * CHUNK:(ci + 1) * CHUNK, :].T


def _small_gates(x, g, w_all, layer, bias, alog, seq, tm=512):
    T, D = x.shape
    return pl.pallas_call(
        functools.partial(_small_kernel, blocks_per_seq=seq // tm),
        grid=(T // tm,),
        in_specs=[
            pl.BlockSpec((tm, D), lambda i: (i, 0)),
            pl.BlockSpec((1, D), lambda i: (0, 0)),
            pl.BlockSpec((None, D, LANES), lambda i: (layer, 0, 0)),
            pl.BlockSpec((1, LANES), lambda i: (0, 0)),
            pl.BlockSpec((1, LANES), lambda i: (0, 0)),
        ],
        out_specs=[pl.BlockSpec((tm, LANES), lambda i: (i, 0)),
                   pl.BlockSpec((tm // CHUNK, LANES, CHUNK), lambda i: (i, 0, 0))],
        out_shape=[jax.ShapeDtypeStruct((T, LANES), F32),
                   jax.ShapeDtypeStruct((T // CHUNK, LANES, CHUNK), F32)],
        scratch_shapes=[pltpu.VMEM((1, LANES), F32)],
        compiler_params=_params("arbitrary"),
    )(x, g, w_all, bias, alog)


def _conv_taps(h, c_ref, cols):
    taps = c_ref.shape[0]
    tm = h.shape[0] - HALO
    acc = None
    for t in range(taps):
        start = HALO - (taps - 1) + t
        term = h[start:start + tm, :] * c_ref[t:t + 1, cols]
        acc = term if acc is None else acc + term
    return acc


def _column_groups(width):
    return [slice(c * MXU_WIDTH, (c + 1) * MXU_WIDTH) for c in range(width // MXU_WIDTH)]


def _norm_with_halo(x_ref, xh_ref, g_ref, xn_ref, at_seq_start):
    xn_ref[HALO:, :] = _rmsnorm(x_ref[...], g_ref[...]).astype(BF16)
    keep = jnp.where(at_seq_start, 0.0, 1.0)
    xn_ref[:HALO, :] = (_rmsnorm(xh_ref[...], g_ref[...]) * keep).astype(BF16)


def _proj_conv_kernel(x_ref, xh_ref, g_ref, w_ref, c_ref, o_ref, xn_ref, *, blocks_per_seq):
    @pl.when(pl.program_id(1) == 0)
    def _():
        _norm_with_halo(x_ref, xh_ref, g_ref, xn_ref, pl.program_id(0) % blocks_per_seq == 0)

    xn = xn_ref[...]
    groups = _column_groups(w_ref.shape[1])
    hidden = [jnp.dot(xn, w_ref[:, cs], preferred_element_type=F32) for cs in groups]
    for cs, h in zip(groups, hidden):
        u = _conv_taps(h, c_ref, cs)
        o_ref[:, cs] = (u * _sigmoid(u)).astype(o_ref.dtype)


def _proj_conv(x, g, w_all, layer, conv_w, seq, tm=1024, tn=1024):
    T, D = x.shape
    taps = conv_w.shape[0]
    N = w_all.shape[2]
    halo_blocks = tm // HALO
    return pl.pallas_call(
        functools.partial(_proj_conv_kernel, blocks_per_seq=seq // tm),
        grid=(T // tm, N // tn),
        in_specs=[
            pl.BlockSpec((tm, D), lambda i, j: (i, 0)),
            pl.BlockSpec((HALO, D), lambda i, j: (jnp.maximum(i * halo_blocks - 1, 0), 0)),
            pl.BlockSpec((1, D), lambda i, j: (0, 0)),
            pl.BlockSpec((None, D, tn), lambda i, j: (layer, 0, j)),
            pl.BlockSpec((taps, tn), lambda i, j: (0, j)),
        ],
        out_specs=pl.BlockSpec((tm, tn), lambda i, j: (i, j)),
        out_shape=jax.ShapeDtypeStruct((T, N), BF16),
        scratch_shapes=[pltpu.VMEM((tm + HALO, D), BF16)],
        compiler_params=_params("parallel", "arbitrary"),
    )(x, x, g, w_all, conv_w)


NEG_BIG = -1e30


LOG2E = 1.4426950408889634


def _fox_kernel(q_ref, k_ref, v_ref, f_ref, o_ref, m_ref, acc_ref, *, tq, tk, hps):
    qi = pl.program_id(2)
    ones = jnp.ones((tk, HEAD_DIM), BF16)
    lane_tiles = tk // LANES
    m_ref[...] = jnp.full_like(m_ref, NEG_BIG)
    acc_ref[...] = jnp.zeros_like(acc_ref)

    def block(j, masked):
        start = pl.multiple_of(j * tk, tk)
        heads = [slice(h * HEAD_DIM, (h + 1) * HEAD_DIM) for h in range(hps)]
        scores = []
        for h, hs in enumerate(heads):
            s = lax.dot_general(q_ref[:, hs], k_ref[pl.ds(start, tk), hs],
                                (((1,), (1,)), ((), ())), preferred_element_type=F32)
            s = s - f_ref[h, pl.ds(j, 1), :] * LOG2E
            if masked:
                row = lax.broadcasted_iota(jnp.int32, (tq, tk), 0)
                col = lax.broadcasted_iota(jnp.int32, (tq, tk), 1)
                s = jnp.where(col <= row, s, NEG_BIG)
            scores.append([s[:, c * LANES:(c + 1) * LANES] for c in range(lane_tiles)])
        probs, alphas = [], []
        for h, tiles in enumerate(scores):
            m_prev = m_ref[h]
            m_tile = functools.reduce(jnp.maximum, tiles)
            m_new = jnp.maximum(m_prev, jnp.max(m_tile, axis=-1, keepdims=True))
            m_ref[h] = m_new
            alphas.append(jnp.exp2(m_prev - m_new))
            probs.append(jnp.concatenate([jnp.exp2(t - m_new).astype(BF16) for t in tiles], axis=1))
        for h, hs in enumerate(heads):
            v_ext = jnp.concatenate([v_ref[pl.ds(start, tk), hs], ones], axis=1)
            pv = jnp.dot(probs[h], v_ext, preferred_element_type=F32)
            acc_ref[h] = jnp.concatenate([alphas[h], alphas[h]], axis=1) * acc_ref[h] + pv

    def full_block(j, carry):
        block(j, masked=False)
        return carry

    lax.fori_loop(0, qi, full_block, 0)
    block(qi, masked=True)
    for h in range(hps):
        acc = acc_ref[h]
        o_ref[:, h * HEAD_DIM:(h + 1) * HEAD_DIM] = (
            acc[:, :HEAD_DIM] / acc[:, HEAD_DIM:]).astype(o_ref.dtype)


def _fox_attention(qkv, f_cum, batch, seq, *, q_col, k_col, v_col, tq=512, hps=2):
    T = qkv.shape[0]
    nq = seq // tq
    width = hps * HEAD_DIM
    f_blocks = f_cum.reshape(batch, HEADS, nq, tq)
    return pl.pallas_call(
        functools.partial(_fox_kernel, tq=tq, tk=tq, hps=hps),
        grid=(batch, HEADS // hps, nq),
        in_specs=[
            pl.BlockSpec((tq, width), lambda b, h, qi: (b * nq + qi, q_col + h)),
            pl.BlockSpec((seq, width), lambda b, h, qi: (b, k_col + h)),
            pl.BlockSpec((seq, width), lambda b, h, qi: (b, v_col + h)),
            pl.BlockSpec((None, hps, nq, tq), lambda b, h, qi: (b, h, 0, 0)),
        ],
        out_specs=pl.BlockSpec((tq, width), lambda b, h, qi: (b * nq + qi, h)),
        out_shape=jax.ShapeDtypeStruct((T, HEADS * HEAD_DIM), BF16),
        scratch_shapes=[pltpu.VMEM((hps, tq, LANES), F32),
                        pltpu.VMEM((hps, tq, 2 * HEAD_DIM), F32)],
        compiler_params=_params("parallel", "parallel", "arbitrary"),
    )(qkv, qkv, qkv, f_blocks)


def _bmm(a, b):
    return jnp.einsum("gmk,gkn->gmn", a.astype(BF16), b.astype(BF16), preferred_element_type=F32)


def _bmm_nt(a, b):
    return jnp.einsum("gmk,gnk->gmn", a.astype(BF16), b.astype(BF16), preferred_element_type=F32)


def _bmm_tn(a, b):
    return jnp.einsum("gkm,gkn->gmn", a.astype(BF16), b.astype(BF16), preferred_element_type=F32)


def _unit_lower_inverse(m, eye, r, c):
    x = _inverse_pairs(m, eye, r, c)
    b = 2
    while b < m.shape[-1]:
        x = _inverse_double(x, m, r, c, b)
        b *= 2
    return x


def _inverse_pairs(m, eye, r, c):
    return eye - jnp.where((r >> 1) == (c >> 1), m, 0.0)


def _inverse_double(x, m, r, c, b):
    shift = b.bit_length() - 1
    joins = ((r >> (shift + 1)) == (c >> (shift + 1))) & ((r >> shift) != (c >> shift))
    return x - _bmm(_bmm(x, jnp.where(joins, m, 0.0)), x)


STEP_CHUNKS = 4


def _chunk_head_stack(ref, width=HEAD_DIM):
    return jnp.stack([ref[ci * CHUNK:(ci + 1) * CHUNK, h * width:(h + 1) * width]
                      for ci in range(STEP_CHUNKS) for h in range(HEADS)])


def _chunk_head_cols(col, start):
    return jnp.stack([col[ci * CHUNK:(ci + 1) * CHUNK, start + h:start + h + 1]
                      for ci in range(STEP_CHUNKS) for h in range(HEADS)])


def _chunk_head_rows(row_ref):
    return jnp.concatenate([row_ref[ci] for ci in range(STEP_CHUNKS)], axis=0)[:, None, :]


def _store_heads(o_ref, y):
    for ci in range(STEP_CHUNKS):
        for h in range(HEADS):
            o_ref[ci * CHUNK:(ci + 1) * CHUNK, h * HEAD_DIM:(h + 1) * HEAD_DIM] = y[ci * HEADS + h]


def _gdn_kernel(q_ref, k_ref, v_ref, z_ref, col_ref, rowg_ref, ng_ref, o_ref, s_ref):
    @pl.when(pl.program_id(1) == 0)
    def _():
        s_ref[...] = jnp.zeros_like(s_ref)

    L = CHUNK
    r = lax.broadcasted_iota(jnp.int32, (1, L, L), 1)
    c = lax.broadcasted_iota(jnp.int32, (1, L, L), 2)
    tri = c <= r
    eye = jnp.where(c == r, 1.0, 0.0).astype(F32)
    col = col_ref[...]
    q = _chunk_head_stack(q_ref).astype(F32)
    k = _chunk_head_stack(k_ref).astype(F32)
    v = _chunk_head_stack(v_ref).astype(F32)
    qn = q * lax.rsqrt(jnp.sum(q * q, axis=-1, keepdims=True) + EPS) * (HEAD_DIM ** -0.5)
    kn = k * lax.rsqrt(jnp.sum(k * k, axis=-1, keepdims=True) + EPS)
    beta = _chunk_head_cols(col, SM_GDN_B)
    gc = _chunk_head_cols(col, SM_GDN_A)
    gr = _chunk_head_rows(rowg_ref)
    g_last = gc[:, L - 1:L, :]
    decay = jnp.where(tri, jnp.exp(jnp.where(tri, gc - gr, 0.0)), 0.0)
    eg = jnp.exp(gc)
    kb = kn * beta
    m = jnp.where(c < r, _bmm_nt(kb, kn) * decay, 0.0)
    t_inv = _unit_lower_inverse(m, eye, r, c)
    x = _bmm(t_inv, jnp.concatenate([v * beta, kb * eg], axis=-1))
    u = x[:, :, :HEAD_DIM]
    w = x[:, :, HEAD_DIM:]
    a_qk = _bmm_nt(qn, kn) * decay
    qg = qn * eg
    kg = kn * jnp.exp(g_last - gc)
    carry = jnp.exp(g_last)
    state = s_ref[...]
    outs = []
    for ci in range(STEP_CHUNKS):
        hs = slice(ci * HEADS, (ci + 1) * HEADS)
        v_new = u[hs] - _bmm(w[hs], state)
        outs.append(_bmm(qg[hs], state) + _bmm(a_qk[hs], v_new))
        state = state * carry[hs] + _bmm_tn(kg[hs], v_new)
    s_ref[...] = state
    o = jnp.concatenate(outs, axis=0)
    z = _chunk_head_stack(z_ref).astype(F32)
    _store_heads(o_ref, (_rmsnorm(o, ng_ref[...]) * (z * _sigmoid(z))).astype(o_ref.dtype))


def _gdn(qkv, z_src, col_small, row_small, norm_g, batch, seq, *, z_col):
    T = qkv.shape[0]
    rows = STEP_CHUNKS * CHUNK
    ns = seq // rows
    width = HEADS * HEAD_DIM
    row = lambda b, s: b * ns + s
    return pl.pallas_call(
        _gdn_kernel,
        grid=(batch, ns),
        in_specs=[
            pl.BlockSpec((rows, width), lambda b, s: (row(b, s), 0)),
            pl.BlockSpec((rows, width), lambda b, s: (row(b, s), 1)),
            pl.BlockSpec((rows, width), lambda b, s: (row(b, s), 2)),
            pl.BlockSpec((rows, width), lambda b, s: (row(b, s), z_col)),
            pl.BlockSpec((rows, LANES), lambda b, s: (row(b, s), 0)),
            pl.BlockSpec((STEP_CHUNKS, HEADS, CHUNK), lambda b, s: (row(b, s), SM_GDN_A // HEADS, 0)),
            pl.BlockSpec((1, HEAD_DIM), lambda b, s: (0, 0)),
        ],
        out_specs=pl.BlockSpec((rows, width), lambda b, s: (row(b, s), 0)),
        out_shape=jax.ShapeDtypeStruct((T, width), BF16),
        scratch_shapes=[pltpu.VMEM((HEADS, HEAD_DIM, HEAD_DIM), F32)],
        compiler_params=_params("parallel", "arbitrary"),
    )(qkv, qkv, qkv, z_src, col_small, row_small, norm_g)


def _mlstm_kernel(q_ref, k_ref, v_ref, og_ref, col_ref, rowi_ref, rowb_ref, ng_ref, o_ref,
                  c_ref, m_ref):
    @pl.when(pl.program_id(1) == 0)
    def _():
        c_ref[...] = jnp.zeros_like(c_ref)
        m_ref[...] = jnp.zeros_like(m_ref)

    L = CHUNK
    r = lax.broadcasted_iota(jnp.int32, (1, L, L), 1)
    c = lax.broadcasted_iota(jnp.int32, (1, L, L), 2)
    tri = c <= r
    col = col_ref[...]
    q = _chunk_head_stack(q_ref, ML_QK_DIM).astype(F32)
    k = _chunk_head_stack(k_ref, ML_QK_DIM).astype(F32)
    v = _chunk_head_stack(v_ref)
    v = jnp.concatenate([v, jnp.ones_like(v)], axis=-1)
    ic = _chunk_head_cols(col, SM_ML_I)
    bc = _chunk_head_cols(col, SM_ML_F)
    ir = _chunk_head_rows(rowi_ref)
    br = _chunk_head_rows(rowb_ref)
    dm = jnp.where(tri, bc - br + ir, -jnp.inf)
    dmax = jnp.max(dm, axis=-1, keepdims=True)
    qk = _bmm_nt(q, k)
    m_state = m_ref[...]
    m_rows, m_lasts = [], []
    for ci in range(STEP_CHUNKS):
        hs = slice(ci * HEADS, (ci + 1) * HEADS)
        m_r = jnp.maximum(bc[hs] + m_state, dmax[hs])
        m_rows.append(m_r)
        m_state = m_r[:, L - 1:L, :]
        m_lasts.append(m_state)
    m_r = jnp.concatenate(m_rows, axis=0)
    m_last = jnp.concatenate(m_lasts, axis=0)
    m_prev = jnp.concatenate([m_ref[...]] + m_lasts[:-1], axis=0)
    w_inter = jnp.exp(bc + m_prev - m_r)
    sm = jnp.exp(dm - m_r) * qk
    intra = _bmm(sm, v)
    wk = jnp.exp(bc[:, L - 1:L, :] - bc + ic - m_last) * k
    kv = _bmm_tn(wk, v)
    w_last = w_inter[:, L - 1:L, :]
    floor = jnp.exp(-m_r)
    c_state = c_ref[...]
    outs = []
    for ci in range(STEP_CHUNKS):
        hs = slice(ci * HEADS, (ci + 1) * HEADS)
        num_den = w_inter[hs] * _bmm(q[hs], c_state) + intra[hs]
        den = jnp.maximum(jnp.abs(num_den[:, :, HEAD_DIM:]), floor[hs])
        outs.append(num_den[:, :, :HEAD_DIM] / den)
        c_state = w_last[hs] * c_state + kv[hs]
    c_ref[...] = c_state
    m_ref[...] = m_state
    h_tilde = jnp.concatenate(outs, axis=0)
    og = _chunk_head_stack(og_ref).astype(F32)
    _store_heads(o_ref, _rmsnorm(_sigmoid(og) * h_tilde, ng_ref[...]).astype(o_ref.dtype))


def _mlstm(qkv, og_src, col_small, row_small, norm_g, batch, seq, *, q_col, k_col, v_col, og_col):
    T = qkv.shape[0]
    rows = STEP_CHUNKS * CHUNK
    ns = seq // rows
    qk_width = HEADS * ML_QK_DIM
    width = HEADS * HEAD_DIM
    row = lambda b, s: b * ns + s
    return pl.pallas_call(
        _mlstm_kernel,
        grid=(batch, ns),
        in_specs=[
            pl.BlockSpec((rows, qk_width), lambda b, s: (row(b, s), q_col)),
            pl.BlockSpec((rows, qk_width), lambda b, s: (row(b, s), k_col)),
            pl.BlockSpec((rows, width), lambda b, s: (row(b, s), v_col)),
            pl.BlockSpec((rows, width), lambda b, s: (row(b, s), og_col)),
            pl.BlockSpec((rows, LANES), lambda b, s: (row(b, s), 0)),
            pl.BlockSpec((STEP_CHUNKS, HEADS, CHUNK), lambda b, s: (row(b, s), SM_ML_I // HEADS, 0)),
            pl.BlockSpec((STEP_CHUNKS, HEADS, CHUNK), lambda b, s: (row(b, s), SM_ML_F // HEADS, 0)),
            pl.BlockSpec((1, HEAD_DIM), lambda b, s: (0, 0)),
        ],
        out_specs=pl.BlockSpec((rows, width), lambda b, s: (row(b, s), 0)),
        out_shape=jax.ShapeDtypeStruct((T, width), BF16),
        scratch_shapes=[pltpu.VMEM((HEADS, ML_QK_DIM, 2 * HEAD_DIM), F32),
                        pltpu.VMEM((HEADS, 1, 1), F32)],
        compiler_params=_params("parallel", "arbitrary"),
    )(qkv, qkv, qkv, og_src, col_small, row_small, row_small, norm_g)


def _chunk_mixers_kernel(gq_ref, gk_ref, gv_ref, z_ref, mq_ref, mk_ref, mv_ref, og_ref, col_ref,
                         rowg_ref, rowi_ref, rowb_ref, gng_ref, mng_ref, yg_ref, ym_ref,
                         s_ref, c_ref, m_ref):
    @pl.when(pl.program_id(1) == 0)
    def _():
        s_ref[...] = jnp.zeros_like(s_ref)
        c_ref[...] = jnp.zeros_like(c_ref)
        m_ref[...] = jnp.zeros_like(m_ref)

    L = CHUNK
    r = lax.broadcasted_iota(jnp.int32, (1, L, L), 1)
    c = lax.broadcasted_iota(jnp.int32, (1, L, L), 2)
    tri = c <= r
    eye = jnp.where(c == r, 1.0, 0.0).astype(F32)
    col = col_ref[...]

    q = _chunk_head_stack(gq_ref).astype(F32)
    k = _chunk_head_stack(gk_ref).astype(F32)
    v = _chunk_head_stack(gv_ref).astype(F32)
    qn = q * lax.rsqrt(jnp.sum(q * q, axis=-1, keepdims=True) + EPS) * (HEAD_DIM ** -0.5)
    kn = k * lax.rsqrt(jnp.sum(k * k, axis=-1, keepdims=True) + EPS)
    beta = _chunk_head_cols(col, SM_GDN_B)
    gc = _chunk_head_cols(col, SM_GDN_A)
    gr = _chunk_head_rows(rowg_ref)
    g_last = gc[:, L - 1:L, :]
    decay = jnp.where(tri, jnp.exp(jnp.where(tri, gc - gr, 0.0)), 0.0)
    eg = jnp.exp(gc)
    kb = kn * beta
    m = jnp.where(c < r, _bmm_nt(kb, kn) * decay, 0.0)
    t_inv = _inverse_pairs(m, eye, r, c)

    mq = _chunk_head_stack(mq_ref, ML_QK_DIM).astype(F32)
    mk = _chunk_head_stack(mk_ref, ML_QK_DIM).astype(F32)
    mv = _chunk_head_stack(mv_ref)
    mv = jnp.concatenate([mv, jnp.ones_like(mv)], axis=-1)
    ic = _chunk_head_cols(col, SM_ML_I)
    bc = _chunk_head_cols(col, SM_ML_F)
    ir = _chunk_head_rows(rowi_ref)
    br = _chunk_head_rows(rowb_ref)
    dm = jnp.where(tri, bc - br + ir, -jnp.inf)
    dmax = jnp.max(dm, axis=-1, keepdims=True)
    qk = _bmm_nt(mq, mk)

    t_inv = _inverse_double(t_inv, m, r, c, 2)
    t_inv = _inverse_double(t_inv, m, r, c, 4)

    m_state = m_ref[...]
    m_rows, m_lasts = [], []
    for ci in range(STEP_CHUNKS):
        hs = slice(ci * HEADS, (ci + 1) * HEADS)
        m_r = jnp.maximum(bc[hs] + m_state, dmax[hs])
        m_rows.append(m_r)
        m_state = m_r[:, L - 1:L, :]
        m_lasts.append(m_state)
    m_r = jnp.concatenate(m_rows, axis=0)
    m_last = jnp.concatenate(m_lasts, axis=0)
    m_prev = jnp.concatenate([m_ref[...]] + m_lasts[:-1], axis=0)
    m_ref[...] = m_state
    w_inter = jnp.exp(bc + m_prev - m_r)
    sm = jnp.exp(dm - m_r) * qk
    intra = _bmm(sm, mv)

    t_inv = _inverse_double(t_inv, m, r, c, 8)

    wk = jnp.exp(bc[:, L - 1:L, :] - bc + ic - m_last) * mk
    kv = _bmm_tn(wk, mv)
    w_last = w_inter[:, L - 1:L, :]
    floor = jnp.exp(-m_r)

    t_inv = _inverse_double(t_inv, m, r, c, 16)
    t_inv = _inverse_double(t_inv, m, r, c, 32)
    x = _bmm(t_inv, jnp.concatenate([v * beta, kb * eg], axis=-1))
    u = x[:, :, :HEAD_DIM]
    w = x[:, :, HEAD_DIM:]
    a_qk = _bmm_nt(qn, kn) * decay
    qg = qn * eg
    kg = kn * jnp.exp(g_last - gc)
    carry = jnp.exp(g_last)

    state = s_ref[...]
    c_state = c_ref[...]
    g_outs, m_outs = [], []
    for ci in range(STEP_CHUNKS):
        hs = slice(ci * HEADS, (ci + 1) * HEADS)
        v_new = u[hs] - _bmm(w[hs], state)
        num_den = w_inter[hs] * _bmm(mq[hs], c_state) + intra[hs]
        g_outs.append(_bmm(qg[hs], state) + _bmm(a_qk[hs], v_new))
        state = state * carry[hs] + _bmm_tn(kg[hs], v_new)
        den = jnp.maximum(jnp.abs(num_den[:, :, HEAD_DIM:]), floor[hs])
        m_outs.append(num_den[:, :, :HEAD_DIM] / den)
        c_state = w_last[hs] * c_state + kv[hs]
    s_ref[...] = state
    c_ref[...] = c_state

    o = jnp.concatenate(g_outs, axis=0)
    z = _chunk_head_stack(z_ref).astype(F32)
    _store_heads(yg_ref, (_rmsnorm(o, gng_ref[...]) * (z * _sigmoid(z))).astype(yg_ref.dtype))
    h_tilde = jnp.concatenate(m_outs, axis=0)
    og = _chunk_head_stack(og_ref).astype(F32)
    _store_heads(ym_ref, _rmsnorm(_sigmoid(og) * h_tilde, mng_ref[...]).astype(ym_ref.dtype))


def _chunk_mixers(gdn_qkv, proj, col_small, row_small, gdn_norm_g, ml_norm_g, batch, seq, *,
                  z_col, q_col, k_col, v_col, og_col):
    T = proj.shape[0]
    rows = STEP_CHUNKS * CHUNK
    ns = seq // rows
    qk_width = HEADS * ML_QK_DIM
    width = HEADS * HEAD_DIM
    row = lambda b, s: b * ns + s
    wide = lambda col: pl.BlockSpec((rows, width), lambda b, s: (row(b, s), col))
    rows_spec = lambda seg: pl.BlockSpec((STEP_CHUNKS, HEADS, CHUNK),
                                         lambda b, s: (row(b, s), seg // HEADS, 0))
    norm_spec = pl.BlockSpec((1, HEAD_DIM), lambda b, s: (0, 0))
    out_spec = pl.BlockSpec((rows, width), lambda b, s: (row(b, s), 0))
    out_shape = jax.ShapeDtypeStruct((T, width), BF16)
    return pl.pallas_call(
        _chunk_mixers_kernel,
        grid=(batch, ns),
        in_specs=[
            wide(0), wide(1), wide(2), wide(z_col),
            pl.BlockSpec((rows, qk_width), lambda b, s: (row(b, s), q_col)),
            pl.BlockSpec((rows, qk_width), lambda b, s: (row(b, s), k_col)),
            wide(v_col), wide(og_col),
            pl.BlockSpec((rows, LANES), lambda b, s: (row(b, s), 0)),
            rows_spec(SM_GDN_A), rows_spec(SM_ML_I), rows_spec(SM_ML_F),
            norm_spec, norm_spec,
        ],
        out_specs=[out_spec, out_spec],
        out_shape=[out_shape, out_shape],
        scratch_shapes=[pltpu.VMEM((HEADS, HEAD_DIM, HEAD_DIM), F32),
                        pltpu.VMEM((HEADS, ML_QK_DIM, 2 * HEAD_DIM), F32),
                        pltpu.VMEM((HEADS, 1, 1), F32)],
        compiler_params=_params("parallel", "arbitrary"),
    )(gdn_qkv, gdn_qkv, gdn_qkv, proj, proj, proj, proj, proj, col_small,
      row_small, row_small, row_small, gdn_norm_g, ml_norm_g)


def _merge_kernel(yf_ref, yg_ref, ym_ref, wf_ref, wg_ref, wm_ref, g0_ref, g1_ref, g2_ref,
                  b0_ref, b1_ref, b2_ref, o_ref):
    def branch(y_ref, w_ref, g_ref, b_ref):
        return _sigmoid(g_ref[...] + b_ref[...]) * jnp.dot(y_ref[...], w_ref[...],
                                                          preferred_element_type=F32)

    y = (branch(yf_ref, wf_ref, g0_ref, b0_ref) + branch(yg_ref, wg_ref, g1_ref, b1_ref)
         + branch(ym_ref, wm_ref, g2_ref, b2_ref))
    o_ref[...] = y.astype(o_ref.dtype)


def _merge(y_fox, y_gdn, y_ml, w_fox, w_gdn, w_ml, layer, gate_src, gate_bias, d_model, *,
           gate_col, tm=1024, tn=512):
    T, K = y_fox.shape
    nj = d_model // tn
    y_spec = pl.BlockSpec((tm, K), lambda i, j: (i, 0))
    w_spec = pl.BlockSpec((None, K, tn), lambda i, j: (layer, 0, j))
    gate_specs = [pl.BlockSpec((tm, tn), lambda i, j, n=n: (i, gate_col + n * nj + j))
                  for n in range(N_BRANCH)]
    bias_specs = [pl.BlockSpec((1, tn), lambda i, j, n=n: (0, n * nj + j)) for n in range(N_BRANCH)]
    return pl.pallas_call(
        _merge_kernel,
        grid=(T // tm, nj),
        in_specs=[y_spec, y_spec, y_spec, w_spec, w_spec, w_spec] + gate_specs + bias_specs,
        out_specs=pl.BlockSpec((tm, tn), lambda i, j: (i, j)),
        out_shape=jax.ShapeDtypeStruct((T, d_model), BF16),
        compiler_params=_params("parallel", "parallel"),
    )(y_fox, y_gdn, y_ml, w_fox, w_gdn, w_ml, gate_src, gate_src, gate_src,
      gate_bias, gate_bias, gate_bias)


def _matmul_residual_kernel(a_ref, w_ref, x_ref, o_ref):
    o_ref[...] = x_ref[...] + jnp.dot(a_ref[...], w_ref[...], preferred_element_type=F32)


def _matmul_residual(a, w_all, layer, x, tm=1024, tn=512):
    T, K = a.shape
    N = w_all.shape[2]
    return pl.pallas_call(
        _matmul_residual_kernel,
        grid=(T // tm, N // tn),
        in_specs=[
            pl.BlockSpec((tm, K), lambda i, j: (i, 0)),
            pl.BlockSpec((None, K, tn), lambda i, j: (layer, 0, j)),
            pl.BlockSpec((tm, tn), lambda i, j: (i, j)),
        ],
        out_specs=pl.BlockSpec((tm, tn), lambda i, j: (i, j)),
        out_shape=jax.ShapeDtypeStruct((T, N), F32),
        compiler_params=_params("parallel", "parallel"),
    )(a, w_all, x)


def _final_norm_kernel(x_ref, g_ref, o_ref):
    o_ref[...] = _rmsnorm(x_ref[...], g_ref[...])


def _final_norm(x, g, tm=512):
    T, D = x.shape
    return pl.pallas_call(
        _final_norm_kernel,
        grid=(T // tm,),
        in_specs=[pl.BlockSpec((tm, D), lambda i: (i, 0)), pl.BlockSpec((1, D), lambda i: (0, 0))],
        out_specs=pl.BlockSpec((tm, D), lambda i: (i, 0)),
        out_shape=jax.ShapeDtypeStruct((T, D), F32),
        compiler_params=_params("parallel"),
    )(x, g)


def _ffn_kernel(x_ref, xh_ref, g_ref, wu_ref, wg_ref, cu_ref, cg_ref, wd_ref, o_ref, xn_ref, *,
                blocks_per_seq):
    @pl.when(pl.program_id(1) == 0)
    def _():
        _norm_with_halo(x_ref, xh_ref, g_ref, xn_ref, pl.program_id(0) % blocks_per_seq == 0)
        o_ref[...] = x_ref[...]

    xn = xn_ref[...]
    groups = _column_groups(wu_ref.shape[1])
    hidden = [(jnp.dot(xn, wu_ref[:, cs], preferred_element_type=F32),
               jnp.dot(xn, wg_ref[:, cs], preferred_element_type=F32)) for cs in groups]
    acts = []
    for cs, (hu, hg) in zip(groups, hidden):
        gt = _conv_taps(hg, cg_ref, cs)
        acts.append((gt * _sigmoid(gt) * _conv_taps(hu, cu_ref, cs)).astype(BF16))
    o_ref[...] += jnp.dot(jnp.concatenate(acts, axis=1), wd_ref[...], preferred_element_type=F32)


def _ffn(x, g, w_up, conv_w, w_down, layer, seq, tm=1024, tn=512):
    T, D = x.shape
    taps = conv_w.shape[0]
    nj = w_down.shape[1] // tn
    halo_blocks = tm // HALO
    once = pl.Buffered(1)
    return pl.pallas_call(
        functools.partial(_ffn_kernel, blocks_per_seq=seq // tm),
        grid=(T // tm, nj),
        in_specs=[
            pl.BlockSpec((tm, D), lambda i, j: (i, 0)),
            pl.BlockSpec((HALO, D), lambda i, j: (jnp.maximum(i * halo_blocks - 1, 0), 0)),
            pl.BlockSpec((1, D), lambda i, j: (0, 0)),
            pl.BlockSpec((None, D, tn), lambda i, j: (layer, 0, j)),
            pl.BlockSpec((None, D, tn), lambda i, j: (layer, 0, nj + j)),
            pl.BlockSpec((taps, tn), lambda i, j: (0, j)),
            pl.BlockSpec((taps, tn), lambda i, j: (0, nj + j)),
            pl.BlockSpec((None, tn, D), lambda i, j: (layer, j, 0)),
        ],
        out_specs=pl.BlockSpec((tm, D), lambda i, j: (i, 0), pipeline_mode=once),
        out_shape=jax.ShapeDtypeStruct((T, D), F32),
        scratch_shapes=[pltpu.VMEM((tm + HALO, D), BF16)],
        compiler_params=_params("parallel", "arbitrary", vmem_limit_bytes=VMEM_LIMIT_LARGE_BYTES),
    )(x, x, g, w_up, w_up, conv_w, conv_w, w_down)


FOX_W = HEADS * HEAD_DIM
GDN_W = HEADS * HEAD_DIM
ML_QK_W = HEADS * ML_QK_DIM
ML_V_W = HEADS * HEAD_DIM
ML_BASE = 3 * FOX_W
Z_BASE = ML_BASE + 2 * ML_QK_W + ML_V_W
GATE_BASE = Z_BASE + GDN_W + ML_V_W
FOX_HEADS_PER_STEP = 4


def _in_projection_columns(d_model):
    widths = [("fox_q", FOX_W), ("fox_k", FOX_W), ("fox_v", FOX_W), ("fox_f", HEADS),
              ("gdn_qkv", 3 * GDN_W), ("gdn_z", GDN_W), ("gdn_b", HEADS), ("gdn_a", HEADS),
              ("ml_q", ML_QK_W), ("ml_k", ML_QK_W), ("ml_v", ML_V_W), ("ml_i", HEADS),
              ("ml_f", HEADS), ("ml_o", ML_V_W), ("gate", N_BRANCH * d_model)]
    off, start = {}, 0
    for name, width in widths:
        off[name] = (start, start + width)
        start += width
    return off


def _pad_lanes(v):
    return jnp.pad(v, (0, LANES - v.shape[0])).reshape(1, LANES)


def _layer(x, layer, w, p, batch, seq):
    T, D = x.shape
    g_mix = p["norm_mix_g"].reshape(1, D)
    scale_p = jnp.concatenate([
        jnp.full((FOX_W,), HEAD_DIM ** -0.5 * LOG2E, F32), jnp.ones((2 * FOX_W,), F32),
        jnp.full((ML_QK_W,), ML_QK_DIM ** -0.5, F32),
        jnp.ones((w["proj"].shape[2] - 3 * FOX_W - ML_QK_W,), F32)]).reshape(1, -1)
    proj = _norm_matmul(x, g_mix, w["proj"], layer, scale_p, BF16, tn=1024)
    zeros8 = jnp.zeros((HEADS,), F32)
    bias_s = _pad_lanes(jnp.concatenate([p["fox_f_bias"], zeros8, p["gdn_dt_bias"],
                                         p["ml_i_bias"], p["ml_f_bias"]]))
    alog_s = _pad_lanes(jnp.concatenate([zeros8, zeros8, p["gdn_a_log"]]))
    col_small, row_small = _small_gates(x, g_mix, w["small"], layer, bias_s, alog_s, seq)
    f_cum = row_small[:, SM_FOX_F:SM_FOX_F + HEADS, :].reshape(batch, seq // CHUNK, HEADS, CHUNK)
    f_cum = f_cum.transpose(0, 2, 1, 3).reshape(batch, HEADS, seq)
    gdn_qkv = _proj_conv(x, g_mix, w["gdn_qkv"], layer, p["gdn_conv_w"], seq)

    hps = FOX_HEADS_PER_STEP
    y_fox = _fox_attention(proj, f_cum, batch, seq, q_col=0, k_col=HEADS // hps,
                           v_col=2 * HEADS // hps, hps=hps)
    y_gdn, y_ml = _chunk_mixers(
        gdn_qkv, proj, col_small, row_small, p["gdn_norm_g"].reshape(1, HEAD_DIM),
        p["ml_norm_g"].reshape(1, HEAD_DIM), batch, seq, z_col=Z_BASE // GDN_W,
        q_col=ML_BASE // ML_QK_W, k_col=ML_BASE // ML_QK_W + 1,
        v_col=(ML_BASE + 2 * ML_QK_W) // ML_V_W, og_col=(Z_BASE + GDN_W) // ML_V_W)

    tn = 512
    y = _merge(y_fox, y_gdn, y_ml, w["fox_proj"], w["gdn_proj"], w["ml_proj"], layer, proj,
               p["gate_bias"].reshape(1, -1), D, gate_col=GATE_BASE // tn, tn=tn)
    x = _matmul_residual(y, w["out"], layer, x)
    return _ffn(x, p["norm_ffn_g"].reshape(1, D), w["up"], p["ffn_conv_w"], w["down"], layer, seq)


PROJ_SEGMENTS = (("fox_q", "fox_v"), ("ml_q", "ml_v"), ("gdn_z", "gdn_z"), ("ml_o", "ml_o"),
                 ("gate", "gate"))
SMALL_SEGMENTS = (("fox_f", "fox_f"), ("gdn_b", "gdn_a"), ("ml_i", "ml_f"))


def _regroup_kernel(w_ref, proj_ref, gdn_ref, small_ref, *, off):
    w = w_ref[...].astype(BF16)
    seg = lambda a, b: w[:, off[a][0]:off[b][1]]
    pos = 0
    for a, b in PROJ_SEGMENTS:
        piece = seg(a, b)
        proj_ref[:, pos:pos + piece.shape[1]] = piece
        pos += piece.shape[1]
    gdn_ref[...] = seg("gdn_qkv", "gdn_qkv")
    pieces = [seg(a, b) for a, b in SMALL_SEGMENTS]
    used = sum(p.shape[1] for p in pieces)
    small_ref[...] = jnp.concatenate(pieces + [jnp.zeros((w.shape[0], LANES - used), BF16)], axis=1)


def _regroup_in_projection(w_in, off, tr=128):
    depth, D, d_in = w_in.shape
    n_proj = sum(off[b][1] - off[a][0] for a, b in PROJ_SEGMENTS)
    n_gdn = off["gdn_qkv"][1] - off["gdn_qkv"][0]
    out_block = lambda n: pl.BlockSpec((None, tr, n), lambda l, i: (l, i, 0))
    return pl.pallas_call(
        functools.partial(_regroup_kernel, off=off),
        grid=(depth, D // tr),
        in_specs=[out_block(d_in)],
        out_specs=[out_block(n_proj), out_block(n_gdn), out_block(LANES)],
        out_shape=[jax.ShapeDtypeStruct((depth, D, n_proj), BF16),
                   jax.ShapeDtypeStruct((depth, D, n_gdn), BF16),
                   jax.ShapeDtypeStruct((depth, D, LANES), BF16)],
        compiler_params=_params("parallel", "parallel"),
    )(w_in)


def kernel(x, norm_mix_g, w_in, fox_f_bias, gdn_conv_w, gdn_a_log, gdn_dt_bias, gdn_norm_g,
           ml_i_bias, ml_f_bias, ml_norm_g, gate_bias, w_fox_proj, w_gdn_proj, w_ml_proj,
           w_out, norm_ffn_g, w_up, ffn_conv_w, w_down, norm_final_g):
    batch, seq, d_model = x.shape
    depth = w_in.shape[0]
    w_proj, w_gdn_qkv, w_small = _regroup_in_projection(w_in, _in_projection_columns(d_model))
    w = dict(
        proj=w_proj, small=w_small, gdn_qkv=w_gdn_qkv,
        fox_proj=w_fox_proj.astype(BF16), gdn_proj=w_gdn_proj.astype(BF16),
        ml_proj=w_ml_proj.astype(BF16), out=w_out.astype(BF16), up=w_up.astype(BF16),
        down=w_down.astype(BF16))
    small = dict(norm_mix_g=norm_mix_g, fox_f_bias=fox_f_bias, gdn_conv_w=gdn_conv_w,
                 gdn_a_log=gdn_a_log, gdn_dt_bias=gdn_dt_bias, gdn_norm_g=gdn_norm_g,
                 ml_i_bias=ml_i_bias, ml_f_bias=ml_f_bias, ml_norm_g=ml_norm_g,
                 gate_bias=gate_bias, norm_ffn_g=norm_ffn_g, ffn_conv_w=ffn_conv_w)
    h = x.reshape(batch * seq, d_model)
    for layer in range(depth):
        h = _layer(h, layer, w, {k: v[layer] for k, v in small.items()}, batch, seq)
    return _final_norm(h, norm_final_g.reshape(1, d_model)).reshape(batch, seq, d_model)
```

```python
import functools

import jax
import jax.numpy as jnp
from jax import lax
from jax.experimental import pallas as pl
from jax.experimental.pallas import tpu as pltpu

F32 = jnp.float32
BF16 = jnp.bfloat16
HIGHEST = lax.Precision.HIGHEST

EPS = 1e-6
HEADS = 8
HEAD_DIM = 128
ML_QK_DIM = 64
CHUNK = 64
N_BRANCH = 3
LANES = 128
MXU_WIDTH = 256
HALO = 16

SM_FOX_F, SM_GDN_B, SM_GDN_A, SM_ML_I, SM_ML_F = 0, 8, 16, 24, 32

V7X_VMEM_BYTES = 64 * 1024 * 1024
VMEM_LIMIT_BYTES = V7X_VMEM_BYTES * 3 // 4
VMEM_LIMIT_LARGE_BYTES = V7X_VMEM_BYTES * 7 // 8


def _params(*semantics, vmem_limit_bytes=VMEM_LIMIT_BYTES):
    return pltpu.CompilerParams(dimension_semantics=semantics, vmem_limit_bytes=vmem_limit_bytes)


def _rmsnorm(x, g):
    return x * lax.rsqrt(jnp.mean(x * x, axis=-1, keepdims=True) + EPS) * g


def _sigmoid(x):
    return 1.0 / (1.0 + jnp.exp(-x))


def _mm_hi(a, b):
    return jnp.dot(a, b, precision=HIGHEST, preferred_element_type=F32)


def _norm_matmul_kernel(x_ref, g_ref, w_ref, cs_ref, o_ref, xn_ref):
    @pl.when(pl.program_id(1) == 0)
    def _():
        xn_ref[...] = _rmsnorm(x_ref[...], g_ref[...]).astype(BF16)

    acc = jnp.dot(xn_ref[...], w_ref[...], preferred_element_type=F32)
    o_ref[...] = (acc * cs_ref[...]).astype(o_ref.dtype)


def _norm_matmul(x, g, w_all, layer, colscale, out_dtype, tm=1024, tn=512):
    T, D = x.shape
    N = w_all.shape[2]
    return pl.pallas_call(
        _norm_matmul_kernel,
        grid=(T // tm, N // tn),
        in_specs=[
            pl.BlockSpec((tm, D), lambda i, j: (i, 0)),
            pl.BlockSpec((1, D), lambda i, j: (0, 0)),
            pl.BlockSpec((None, D, tn), lambda i, j: (layer, 0, j)),
            pl.BlockSpec((1, tn), lambda i, j: (0, j)),
        ],
        out_specs=pl.BlockSpec((tm, tn), lambda i, j: (i, j)),
        out_shape=jax.ShapeDtypeStruct((T, N), out_dtype),
        scratch_shapes=[pltpu.VMEM((tm, D), BF16)],
        compiler_params=_params("parallel", "arbitrary"),
    )(x, g, w_all, colscale)


def _small_kernel(x_ref, g_ref, w_ref, bias_ref, alog_ref, o_ref, rows_ref, carry_ref, *,
                  blocks_per_seq):
    i = pl.program_id(0)
    tm = x_ref.shape[0]

    @pl.when(i % blocks_per_seq == 0)
    def _():
        carry_ref[...] = jnp.zeros_like(carry_ref)

    xn = _rmsnorm(x_ref[...], g_ref[...]).astype(BF16)
    t = jnp.dot(xn, w_ref[...], preferred_element_type=F32) + bias_ref[...]
    lane = lax.broadcasted_iota(jnp.int32, (tm, LANES), 1)
    e = jnp.log1p(jnp.exp(-jnp.abs(t)))
    logsig = jnp.minimum(t, 0.0) - e
    softplus = jnp.maximum(t, 0.0) + e
    g_decay = -jnp.exp(alog_ref[...]) * softplus
    is_logsig = (lane < SM_GDN_B) | ((lane >= SM_ML_F) & (lane < SM_ML_F + HEADS))
    val = jnp.where(is_logsig, logsig,
                    jnp.where(lane < SM_GDN_A, _sigmoid(t),
                              jnp.where(lane < SM_ML_I, g_decay, t)))

    r = lax.broadcasted_iota(jnp.int32, (CHUNK, CHUNK), 0)
    c = lax.broadcasted_iota(jnp.int32, (CHUNK, CHUNK), 1)
    tri = jnp.where(c <= r, 1.0, 0.0).astype(F32)
    chunk_sums = [_mm_hi(tri, val[ci * CHUNK:(ci + 1) * CHUNK, :]) for ci in range(tm // CHUNK)]
    offset = carry_ref[...]
    full_sums = []
    for cs in chunk_sums:
        full_sums.append(cs + offset)
        offset = offset + cs[CHUNK - 1:CHUNK, :]
    carry_ref[...] = offset
    cs_chunk = jnp.concatenate(chunk_sums, axis=0)
    cs_full = jnp.concatenate(full_sums, axis=0)

    is_chunk_cs = ((lane >= SM_GDN_A) & (lane < SM_ML_I)) | ((lane >= SM_ML_F) & (lane < SM_ML_F + HEADS))
    out = jnp.where(lane < SM_GDN_B, cs_full, jnp.where(is_chunk_cs, cs_chunk, val))
    o_ref[...] = out
    for ci in range(tm // CHUNK):
        rows_ref[ci] = out[ci * CHUNK:(ci + 1) * CHUNK, :].T


def _small_gates(x, g, w_all, layer, bias, alog, seq, tm=512):
    T, D = x.shape
    return pl.pallas_call(
        functools.partial(_small_kernel, blocks_per_seq=seq // tm),
        grid=(T // tm,),
        in_specs=[
            pl.BlockSpec((tm, D), lambda i: (i, 0)),
            pl.BlockSpec((1, D), lambda i: (0, 0)),
            pl.BlockSpec((None, D, LANES), lambda i: (layer, 0, 0)),
            pl.BlockSpec((1, LANES), lambda i: (0, 0)),
            pl.BlockSpec((1, LANES), lambda i: (0, 0)),
        ],
        out_specs=[pl.BlockSpec((tm, LANES), lambda i: (i, 0)),
                   pl.BlockSpec((tm // CHUNK, LANES, CHUNK), lambda i: (i, 0, 0))],
        out_shape=[jax.ShapeDtypeStruct((T, LANES), F32),
                   jax.ShapeDtypeStruct((T // CHUNK, LANES, CHUNK), F32)],
        scratch_shapes=[pltpu.VMEM((1, LANES), F32)],
        compiler_params=_params("arbitrary"),
    )(x, g, w_all, bias, alog)


def _conv_taps(h, c_ref, cols):
    taps = c_ref.shape[0]
    tm = h.shape[0] - HALO
    acc = None
    for t in range(taps):
        start = HALO - (taps - 1) + t
        term = h[start:start + tm, :] * c_ref[t:t + 1, cols]
        acc = term if acc is None else acc + term
    return acc


def _column_groups(width):
    return [slice(c * MXU_WIDTH, (c + 1) * MXU_WIDTH) for c in range(width // MXU_WIDTH)]


def _norm_with_halo(x_ref, xh_ref, g_ref, xn_ref, at_seq_start):
    xn_ref[HALO:, :] = _rmsnorm(x_ref[...], g_ref[...]).astype(BF16)
    keep = jnp.where(at_seq_start, 0.0, 1.0)
    xn_ref[:HALO, :] = (_rmsnorm(xh_ref[...], g_ref[...]) * keep).astype(BF16)


def _proj_conv_kernel(x_ref, xh_ref, g_ref, w_ref, c_ref, o_ref, xn_ref, *, blocks_per_seq):
    @pl.when(pl.program_id(1) == 0)
    def _():
        _norm_with_halo(x_ref, xh_ref, g_ref, xn_ref, pl.program_id(0) % blocks_per_seq == 0)

    xn = xn_ref[...]
    groups = _column_groups(w_ref.shape[1])
    hidden = [jnp.dot(xn, w_ref[:, cs], preferred_element_type=F32) for cs in groups]
    for cs, h in zip(groups, hidden):
        u = _conv_taps(h, c_ref, cs)
        o_ref[:, cs] = (u * _sigmoid(u)).astype(o_ref.dtype)


def _proj_conv(x, g, w_all, layer, conv_w, seq, tm=1024, tn=1024):
    T, D = x.shape
    taps = conv_w.shape[0]
    N = w_all.shape[2]
    halo_blocks = tm // HALO
    return pl.pallas_call(
        functools.partial(_proj_conv_kernel, blocks_per_seq=seq // tm),
        grid=(T // tm, N // tn),
        in_specs=[
            pl.BlockSpec((tm, D), lambda i, j: (i, 0)),
            pl.BlockSpec((HALO, D), lambda i, j: (jnp.maximum(i * halo_blocks - 1, 0), 0)),
            pl.BlockSpec((1, D), lambda i, j: (0, 0)),
            pl.BlockSpec((None, D, tn), lambda i, j: (layer, 0, j)),
            pl.BlockSpec((taps, tn), lambda i, j: (0, j)),
        ],
        out_specs=pl.BlockSpec((tm, tn), lambda i, j: (i, j)),
        out_shape=jax.ShapeDtypeStruct((T, N), BF16),
        scratch_shapes=[pltpu.VMEM((tm + HALO, D), BF16)],
        compiler_params=_params("parallel", "arbitrary"),
    )(x, x, g, w_all, conv_w)


NEG_BIG = -1e30
LOG2E = 1.4426950408889634


def _fox_kernel(q_ref, k_ref, v_ref, f_ref, o_ref, m_ref, acc_ref, *, tq, tk, hps):
    qi = pl.program_id(2)
    ones = jnp.ones((tk, HEAD_DIM), BF16)
    lane_tiles = tk // LANES
    m_ref[...] = jnp.full_like(m_ref, NEG_BIG)
    acc_ref[...] = jnp.zeros_like(acc_ref)

    def block(j, masked):
        start = pl.multiple_of(j * tk, tk)
        heads = [slice(h * HEAD_DIM, (h + 1) * HEAD_DIM) for h in range(hps)]
        scores = []
        for h, hs in enumerate(heads):
            s = lax.dot_general(q_ref[:, hs], k_ref[pl.ds(start, tk), hs],
                                (((1,), (1,)), ((), ())), preferred_element_type=F32)
            s = s - f_ref[h, pl.ds(j, 1), :] * LOG2E
            if masked:
                row = lax.broadcasted_iota(jnp.int32, (tq, tk), 0)
                col = lax.broadcasted_iota(jnp.int32, (tq, tk), 1)
                s = jnp.where(col <= row, s, NEG_BIG)
            scores.append([s[:, c * LANES:(c + 1) * LANES] for c in range(lane_tiles)])
        probs, alphas = [], []
        for h, tiles in enumerate(scores):
            m_prev = m_ref[h]
            m_tile = functools.reduce(jnp.maximum, tiles)
            m_new = jnp.maximum(m_prev, jnp.max(m_tile, axis=-1, keepdims=True))
            m_ref[h] = m_new
            alphas.append(jnp.exp2(m_prev - m_new))
            probs.append(jnp.concatenate([jnp.exp2(t - m_new).astype(BF16) for t in tiles], axis=1))
        for h, hs in enumerate(heads):
            v_ext = jnp.concatenate([v_ref[pl.ds(start, tk), hs], ones], axis=1)
            pv = jnp.dot(probs[h], v_ext, preferred_element_type=F32)
            acc_ref[h] = jnp.concatenate([alphas[h], alphas[h]], axis=1) * acc_ref[h] + pv

    def full_block(j, carry):
        block(j, masked=False)
        return carry

    lax.fori_loop(0, qi, full_block, 0)
    block(qi, masked=True)
    for h in range(hps):
        acc = acc_ref[h]
        o_ref[:, h * HEAD_DIM:(h + 1) * HEAD_DIM] = (
            acc[:, :HEAD_DIM] / acc[:, HEAD_DIM:]).astype(o_ref.dtype)


def _fox_attention(qkv, f_cum, batch, seq, *, q_col, k_col, v_col, tq=512, hps=2):
    T = qkv.shape[0]
    nq = seq // tq
    width = hps * HEAD_DIM
    f_blocks = f_cum.reshape(batch, HEADS, nq, tq)
    return pl.pallas_call(
        functools.partial(_fox_kernel, tq=tq, tk=tq, hps=hps),
        grid=(batch, HEADS // hps, nq),
        in_specs=[
            pl.BlockSpec((tq, width), lambda b, h, qi: (b * nq + qi, q_col + h)),
            pl.BlockSpec((seq, width), lambda b, h, qi: (b, k_col + h)),
            pl.BlockSpec((seq, width), lambda b, h, qi: (b, v_col + h)),
            pl.BlockSpec((None, hps, nq, tq), lambda b, h, qi: (b, h, 0, 0)),
        ],
        out_specs=pl.BlockSpec((tq, width), lambda b, h, qi: (b * nq + qi, h)),
        out_shape=jax.ShapeDtypeStruct((T, HEADS * HEAD_DIM), BF16),
        scratch_shapes=[pltpu.VMEM((hps, tq, LANES), F32),
                        pltpu.VMEM((hps, tq, 2 * HEAD_DIM), F32)],
        compiler_params=_params("parallel", "parallel", "arbitrary"),
    )(qkv, qkv, qkv, f_blocks)


def _bmm(a, b):
    return jnp.einsum("gmk,gkn->gmn", a.astype(BF16), b.astype(BF16), preferred_element_type=F32)


def _bmm_nt(a, b):
    return jnp.einsum("gmk,gnk->gmn", a.astype(BF16), b.astype(BF16), preferred_element_type=F32)


def _bmm_tn(a, b):
    return jnp.einsum("gkm,gkn->gmn", a.astype(BF16), b.astype(BF16), preferred_element_type=F32)


def _inverse_pairs(m, eye, r, c):
    return eye - jnp.where((r >> 1) == (c >> 1), m, 0.0)


def _inverse_double(x, m, r, c, b):
    shift = b.bit_length() - 1
    joins = ((r >> (shift + 1)) == (c >> (shift + 1))) & ((r >> shift) != (c >> shift))
    return x - _bmm(_bmm(x, jnp.where(joins, m, 0.0)), x)


STEP_CHUNKS = 4


def _chunk_head_stack(ref, width=HEAD_DIM):
    return jnp.stack([ref[ci * CHUNK:(ci + 1) * CHUNK, h * width:(h + 1) * width]
                      for ci in range(STEP_CHUNKS) for h in range(HEADS)])


def _chunk_head_cols(col, start):
    return jnp.stack([col[ci * CHUNK:(ci + 1) * CHUNK, start + h:start + h + 1]
                      for ci in range(STEP_CHUNKS) for h in range(HEADS)])


def _chunk_head_rows(row_ref):
    return jnp.concatenate([row_ref[ci] for ci in range(STEP_CHUNKS)], axis=0)[:, None, :]


def _store_heads(o_ref, y):
    for ci in range(STEP_CHUNKS):
        for h in range(HEADS):
            o_ref[ci * CHUNK:(ci + 1) * CHUNK, h * HEAD_DIM:(h + 1) * HEAD_DIM] = y[ci * HEADS + h]


def _chunk_mixers_kernel(gq_ref, gk_ref, gv_ref, z_ref, mq_ref, mk_ref, mv_ref, og_ref, col_ref,
                         rowg_ref, rowi_ref, rowb_ref, gng_ref, mng_ref, yg_ref, ym_ref,
                         s_ref, c_ref, m_ref):
    @pl.when(pl.program_id(1) == 0)
    def _():
        s_ref[...] = jnp.zeros_like(s_ref)
        c_ref[...] = jnp.zeros_like(c_ref)
        m_ref[...] = jnp.zeros_like(m_ref)

    L = CHUNK
    r = lax.broadcasted_iota(jnp.int32, (1, L, L), 1)
    c = lax.broadcasted_iota(jnp.int32, (1, L, L), 2)
    tri = c <= r
    eye = jnp.where(c == r, 1.0, 0.0).astype(F32)
    col = col_ref[...]

    q = _chunk_head_stack(gq_ref).astype(F32)
    k = _chunk_head_stack(gk_ref).astype(F32)
    v = _chunk_head_stack(gv_ref).astype(F32)
    qn = q * lax.rsqrt(jnp.sum(q * q, axis=-1, keepdims=True) + EPS) * (HEAD_DIM ** -0.5)
    kn = k * lax.rsqrt(jnp.sum(k * k, axis=-1, keepdims=True) + EPS)
    beta = _chunk_head_cols(col, SM_GDN_B)
    gc = _chunk_head_cols(col, SM_GDN_A)
    gr = _chunk_head_rows(rowg_ref)
    g_last = gc[:, L - 1:L, :]
    decay = jnp.where(tri, jnp.exp(jnp.where(tri, gc - gr, 0.0)), 0.0)
    eg = jnp.exp(gc)
    kb = kn * beta
    m = jnp.where(c < r, _bmm_nt(kb, kn) * decay, 0.0)
    t_inv = _inverse_pairs(m, eye, r, c)

    mq = _chunk_head_stack(mq_ref, ML_QK_DIM).astype(F32)
    mk = _chunk_head_stack(mk_ref, ML_QK_DIM).astype(F32)
    mv = _chunk_head_stack(mv_ref)
    mv = jnp.concatenate([mv, jnp.ones_like(mv)], axis=-1)
    ic = _chunk_head_cols(col, SM_ML_I)
    bc = _chunk_head_cols(col, SM_ML_F)
    ir = _chunk_head_rows(rowi_ref)
    br = _chunk_head_rows(rowb_ref)
    dm = jnp.where(tri, bc - br + ir, -jnp.inf)
    dmax = jnp.max(dm, axis=-1, keepdims=True)
    qk = _bmm_nt(mq, mk)

    t_inv = _inverse_double(t_inv, m, r, c, 2)
    t_inv = _inverse_double(t_inv, m, r, c, 4)

    m_state = m_ref[...]
    m_rows, m_lasts = [], []
    for ci in range(STEP_CHUNKS):
        hs = slice(ci * HEADS, (ci + 1) * HEADS)
        m_r = jnp.maximum(bc[hs] + m_state, dmax[hs])
        m_rows.append(m_r)
        m_state = m_r[:, L - 1:L, :]
        m_lasts.append(m_state)
    m_r = jnp.concatenate(m_rows, axis=0)
    m_last = jnp.concatenate(m_lasts, axis=0)
    m_prev = jnp.concatenate([m_ref[...]] + m_lasts[:-1], axis=0)
    m_ref[...] = m_state
    w_inter = jnp.exp(bc + m_prev - m_r)
    sm = jnp.exp(dm - m_r) * qk
    intra = _bmm(sm, mv)

    t_inv = _inverse_double(t_inv, m, r, c, 8)

    wk = jnp.exp(bc[:, L - 1:L, :] - bc + ic - m_last) * mk
    kv = _bmm_tn(wk, mv)
    w_last = w_inter[:, L - 1:L, :]
    floor = jnp.exp(-m_r)

    t_inv = _inverse_double(t_inv, m, r, c, 16)
    t_inv = _inverse_double(t_inv, m, r, c, 32)
    x = _bmm(t_inv, jnp.concatenate([v * beta, kb * eg], axis=-1))
    u = x[:, :, :HEAD_DIM]
    w = x[:, :, HEAD_DIM:]
    a_qk = _bmm_nt(qn, kn) * decay
    qg = qn * eg
    kg = kn * jnp.exp(g_last - gc)
    carry = jnp.exp(g_last)

    state = s_ref[...]
    c_state = c_ref[...]
    g_outs, m_outs = [], []
    for ci in range(STEP_CHUNKS):
        hs = slice(ci * HEADS, (ci + 1) * HEADS)
        v_new = u[hs] - _bmm(w[hs], state)
        num_den = w_inter[hs] * _bmm(mq[hs], c_state) + intra[hs]
        g_outs.append(_bmm(qg[hs], state) + _bmm(a_qk[hs], v_new))
        state = state * carry[hs] + _bmm_tn(kg[hs], v_new)
        den = jnp.maximum(jnp.abs(num_den[:, :, HEAD_DIM:]), floor[hs])
        m_outs.append(num_den[:, :, :HEAD_DIM] / den)
        c_state = w_last[hs] * c_state + kv[hs]
    s_ref[...] = state
    c_ref[...] = c_state

    o = jnp.concatenate(g_outs, axis=0)
    z = _chunk_head_stack(z_ref).astype(F32)
    _store_heads(yg_ref, (_rmsnorm(o, gng_ref[...]) * (z * _sigmoid(z))).astype(yg_ref.dtype))
    h_tilde = jnp.concatenate(m_outs, axis=0)
    og = _chunk_head_stack(og_ref).astype(F32)
    _store_heads(ym_ref, _rmsnorm(_sigmoid(og) * h_tilde, mng_ref[...]).astype(ym_ref.dtype))


def _chunk_mixers(gdn_qkv, proj, col_small, row_small, gdn_norm_g, ml_norm_g, batch, seq, *,
                  z_col, q_col, k_col, v_col, og_col):
    T = proj.shape[0]
    rows = STEP_CHUNKS * CHUNK
    ns = seq // rows
    qk_width = HEADS * ML_QK_DIM
    width = HEADS * HEAD_DIM
    row = lambda b, s: b * ns + s
    wide = lambda col: pl.BlockSpec((rows, width), lambda b, s: (row(b, s), col))
    rows_spec = lambda seg: pl.BlockSpec((STEP_CHUNKS, HEADS, CHUNK),
                                         lambda b, s: (row(b, s), seg // HEADS, 0))
    norm_spec = pl.BlockSpec((1, HEAD_DIM), lambda b, s: (0, 0))
    out_spec = pl.BlockSpec((rows, width), lambda b, s: (row(b, s), 0))
    out_shape = jax.ShapeDtypeStruct((T, width), BF16)
    return pl.pallas_call(
        _chunk_mixers_kernel,
        grid=(batch, ns),
        in_specs=[
            wide(0), wide(1), wide(2), wide(z_col),
            pl.BlockSpec((rows, qk_width), lambda b, s: (row(b, s), q_col)),
            pl.BlockSpec((rows, qk_width), lambda b, s: (row(b, s), k_col)),
            wide(v_col), wide(og_col),
            pl.BlockSpec((rows, LANES), lambda b, s: (row(b, s), 0)),
            rows_spec(SM_GDN_A), rows_spec(SM_ML_I), rows_spec(SM_ML_F),
            norm_spec, norm_spec,
        ],
        out_specs=[out_spec, out_spec],
        out_shape=[out_shape, out_shape],
        scratch_shapes=[pltpu.VMEM((HEADS, HEAD_DIM, HEAD_DIM), F32),
                        pltpu.VMEM((HEADS, ML_QK_DIM, 2 * HEAD_DIM), F32),
                        pltpu.VMEM((HEADS, 1, 1), F32)],
        compiler_params=_params("parallel", "arbitrary"),
    )(gdn_qkv, gdn_qkv, gdn_qkv, proj, proj, proj, proj, proj, col_small,
      row_small, row_small, row_small, gdn_norm_g, ml_norm_g)


def _merge_kernel(yf_ref, yg_ref, ym_ref, wf_ref, wg_ref, wm_ref, g0_ref, g1_ref, g2_ref,
                  b0_ref, b1_ref, b2_ref, o_ref):
    def branch(y_ref, w_ref, g_ref, b_ref):
        return _sigmoid(g_ref[...] + b_ref[...]) * jnp.dot(y_ref[...], w_ref[...],
                                                          preferred_element_type=F32)

    y = (branch(yf_ref, wf_ref, g0_ref, b0_ref) + branch(yg_ref, wg_ref, g1_ref, b1_ref)
         + branch(ym_ref, wm_ref, g2_ref, b2_ref))
    o_ref[...] = y.astype(o_ref.dtype)


def _merge(y_fox, y_gdn, y_ml, w_fox, w_gdn, w_ml, layer, gate_src, gate_bias, d_model, *,
           gate_col, tm=1024, tn=512):
    T, K = y_fox.shape
    nj = d_model // tn
    y_spec = pl.BlockSpec((tm, K), lambda i, j: (i, 0))
    w_spec = pl.BlockSpec((None, K, tn), lambda i, j: (layer, 0, j))
    gate_specs = [pl.BlockSpec((tm, tn), lambda i, j, n=n: (i, gate_col + n * nj + j))
                  for n in range(N_BRANCH)]
    bias_specs = [pl.BlockSpec((1, tn), lambda i, j, n=n: (0, n * nj + j)) for n in range(N_BRANCH)]
    return pl.pallas_call(
        _merge_kernel,
        grid=(T // tm, nj),
        in_specs=[y_spec, y_spec, y_spec, w_spec, w_spec, w_spec] + gate_specs + bias_specs,
        out_specs=pl.BlockSpec((tm, tn), lambda i, j: (i, j)),
        out_shape=jax.ShapeDtypeStruct((T, d_model), BF16),
        compiler_params=_params("parallel", "parallel"),
    )(y_fox, y_gdn, y_ml, w_fox, w_gdn, w_ml, gate_src, gate_src, gate_src,
      gate_bias, gate_bias, gate_bias)


def _matmul_residual_kernel(a_ref, w_ref, x_ref, o_ref):
    o_ref[...] = x_ref[...] + jnp.dot(a_ref[...], w_ref[...], preferred_element_type=F32)


def _matmul_residual(a, w_all, layer, x, tm=1024, tn=512):
    T, K = a.shape
    N = w_all.shape[2]
    return pl.pallas_call(
        _matmul_residual_kernel,
        grid=(T // tm, N // tn),
        in_specs=[
            pl.BlockSpec((tm, K), lambda i, j: (i, 0)),
            pl.BlockSpec((None, K, tn), lambda i, j: (layer, 0, j)),
            pl.BlockSpec((tm, tn), lambda i, j: (i, j)),
        ],
        out_specs=pl.BlockSpec((tm, tn), lambda i, j: (i, j)),
        out_shape=jax.ShapeDtypeStruct((T, N), F32),
        compiler_params=_params("parallel", "parallel"),
    )(a, w_all, x)


def _final_norm_kernel(x_ref, g_ref, o_ref):
    o_ref[...] = _rmsnorm(x_ref[...], g_ref[...])


def _final_norm(x, g, tm=512):
    T, D = x.shape
    return pl.pallas_call(
        _final_norm_kernel,
        grid=(T // tm,),
        in_specs=[pl.BlockSpec((tm, D), lambda i: (i, 0)), pl.BlockSpec((1, D), lambda i: (0, 0))],
        out_specs=pl.BlockSpec((tm, D), lambda i: (i, 0)),
        out_shape=jax.ShapeDtypeStruct((T, D), F32),
        compiler_params=_params("parallel"),
    )(x, g)


def _ffn_kernel(x_ref, xh_ref, g_ref, wu_ref, wg_ref, cu_ref, cg_ref, wd_ref, o_ref, xn_ref, *,
                blocks_per_seq):
    @pl.when(pl.program_id(1) == 0)
    def _():
        _norm_with_halo(x_ref, xh_ref, g_ref, xn_ref, pl.program_id(0) % blocks_per_seq == 0)
        o_ref[...] = x_ref[...]

    xn = xn_ref[...]
    groups = _column_groups(wu_ref.shape[1])
    hidden = [(jnp.dot(xn, wu_ref[:, cs], preferred_element_type=F32),
               jnp.dot(xn, wg_ref[:, cs], preferred_element_type=F32)) for cs in groups]
    acts = []
    for cs, (hu, hg) in zip(groups, hidden):
        gt = _conv_taps(hg, cg_ref, cs)
        acts.append((gt * _sigmoid(gt) * _conv_taps(hu, cu_ref, cs)).astype(BF16))
    o_ref[...] += jnp.dot(jnp.concatenate(acts, axis=1), wd_ref[...], preferred_element_type=F32)


def _ffn(x, g, w_up, conv_w, w_down, layer, seq, tm=1024, tn=512):
    T, D = x.shape
    taps = conv_w.shape[0]
    nj = w_down.shape[1] // tn
    halo_blocks = tm // HALO
    once = pl.Buffered(1)
    return pl.pallas_call(
        functools.partial(_ffn_kernel, blocks_per_seq=seq // tm),
        grid=(T // tm, nj),
        in_specs=[
            pl.BlockSpec((tm, D), lambda i, j: (i, 0)),
            pl.BlockSpec((HALO, D), lambda i, j: (jnp.maximum(i * halo_blocks - 1, 0), 0)),
            pl.BlockSpec((1, D), lambda i, j: (0, 0)),
            pl.BlockSpec((None, D, tn), lambda i, j: (layer, 0, j)),
            pl.BlockSpec((None, D, tn), lambda i, j: (layer, 0, nj + j)),
            pl.BlockSpec((taps, tn), lambda i, j: (0, j)),
            pl.BlockSpec((taps, tn), lambda i, j: (0, nj + j)),
            pl.BlockSpec((None, tn, D), lambda i, j: (layer, j, 0)),
        ],
        out_specs=pl.BlockSpec((tm, D), lambda i, j: (i, 0), pipeline_mode=once),
        out_shape=jax.ShapeDtypeStruct((T, D), F32),
        scratch_shapes=[pltpu.VMEM((tm + HALO, D), BF16)],
        compiler_params=_params("parallel", "arbitrary", vmem_limit_bytes=VMEM_LIMIT_LARGE_BYTES),
    )(x, x, g, w_up, w_up, conv_w, conv_w, w_down)


FOX_W = HEADS * HEAD_DIM
GDN_W = HEADS * HEAD_DIM
ML_QK_W = HEADS * ML_QK_DIM
ML_V_W = HEADS * HEAD_DIM
ML_BASE = 3 * FOX_W
Z_BASE = ML_BASE + 2 * ML_QK_W + ML_V_W
GATE_BASE = Z_BASE + GDN_W + ML_V_W
FOX_HEADS_PER_STEP = 4


def _in_projection_columns(d_model):
    widths = [("fox_q", FOX_W), ("fox_k", FOX_W), ("fox_v", FOX_W), ("fox_f", HEADS),
              ("gdn_qkv", 3 * GDN_W), ("gdn_z", GDN_W), ("gdn_b", HEADS), ("gdn_a", HEADS),
              ("ml_q", ML_QK_W), ("ml_k", ML_QK_W), ("ml_v", ML_V_W), ("ml_i", HEADS),
              ("ml_f", HEADS), ("ml_o", ML_V_W), ("gate", N_BRANCH * d_model)]
    off, start = {}, 0
    for name, width in widths:
        off[name] = (start, start + width)
        start += width
    return off


def _pad_lanes(v):
    return jnp.pad(v, (0, LANES - v.shape[0])).reshape(1, LANES)


def _layer(x, layer, w, p, batch, seq):
    T, D = x.shape
    g_mix = p["norm_mix_g"].reshape(1, D)
    scale_p = jnp.concatenate([
        jnp.full((FOX_W,), HEAD_DIM ** -0.5 * LOG2E, F32), jnp.ones((2 * FOX_W,), F32),
        jnp.full((ML_QK_W,), ML_QK_DIM ** -0.5, F32),
        jnp.ones((w["proj"].shape[2] - 3 * FOX_W - ML_QK_W,), F32)]).reshape(1, -1)
    proj = _norm_matmul(x, g_mix, w["proj"], layer, scale_p, BF16, tn=1024)
    zeros8 = jnp.zeros((HEADS,), F32)
    bias_s = _pad_lanes(jnp.concatenate([p["fox_f_bias"], zeros8, p["gdn_dt_bias"],
                                         p["ml_i_bias"], p["ml_f_bias"]]))
    alog_s = _pad_lanes(jnp.concatenate([zeros8, zeros8, p["gdn_a_log"]]))
    col_small, row_small = _small_gates(x, g_mix, w["small"], layer, bias_s, alog_s, seq)
    f_cum = row_small[:, SM_FOX_F:SM_FOX_F + HEADS, :].reshape(batch, seq // CHUNK, HEADS, CHUNK)
    f_cum = f_cum.transpose(0, 2, 1, 3).reshape(batch, HEADS, seq)
    gdn_qkv = _proj_conv(x, g_mix, w["gdn_qkv"], layer, p["gdn_conv_w"], seq)

    hps = FOX_HEADS_PER_STEP
    y_fox = _fox_attention(proj, f_cum, batch, seq, q_col=0, k_col=HEADS // hps,
                           v_col=2 * HEADS // hps, hps=hps)
    y_gdn, y_ml = _chunk_mixers(
        gdn_qkv, proj, col_small, row_small, p["gdn_norm_g"].reshape(1, HEAD_DIM),
        p["ml_norm_g"].reshape(1, HEAD_DIM), batch, seq, z_col=Z_BASE // GDN_W,
        q_col=ML_BASE // ML_QK_W, k_col=ML_BASE // ML_QK_W + 1,
        v_col=(ML_BASE + 2 * ML_QK_W) // ML_V_W, og_col=(Z_BASE + GDN_W) // ML_V_W)

    tn = 512
    y = _merge(y_fox, y_gdn, y_ml, w["fox_proj"], w["gdn_proj"], w["ml_proj"], layer, proj,
               p["gate_bias"].reshape(1, -1), D, gate_col=GATE_BASE // tn, tn=tn)
    x = _matmul_residual(y, w["out"], layer, x)
    return _ffn(x, p["norm_ffn_g"].reshape(1, D), w["up"], p["ffn_conv_w"], w["down"], layer, seq)


def kernel(x, norm_mix_g, w_in, fox_f_bias, gdn_conv_w, gdn_a_log, gdn_dt_bias, gdn_norm_g,
           ml_i_bias, ml_f_bias, ml_norm_g, gate_bias, w_fox_proj, w_gdn_proj, w_ml_proj,
           w_out, norm_ffn_g, w_up, ffn_conv_w, w_down, norm_final_g):
    batch, seq, d_model = x.shape
    depth = w_in.shape[0]
    off = _in_projection_columns(d_model)
    cols = lambda a, b: w_in[:, :, off[a][0]:off[b][1]].astype(BF16)
    w_small = jnp.concatenate([cols("fox_f", "fox_f"), cols("gdn_b", "gdn_a"), cols("ml_i", "ml_f")],
                              axis=2)
    w = dict(
        proj=jnp.concatenate([cols("fox_q", "fox_v"), cols("ml_q", "ml_v"), cols("gdn_z", "gdn_z"),
                              cols("ml_o", "ml_o"), cols("gate", "gate")], axis=2),
        small=jnp.pad(w_small, ((0, 0), (0, 0), (0, LANES - w_small.shape[2]))),
        gdn_qkv=cols("gdn_qkv", "gdn_qkv"),
        fox_proj=w_fox_proj.astype(BF16), gdn_proj=w_gdn_proj.astype(BF16),
        ml_proj=w_ml_proj.astype(BF16), out=w_out.astype(BF16), up=w_up.astype(BF16),
        down=w_down.astype(BF16))
    small = dict(norm_mix_g=norm_mix_g, fox_f_bias=fox_f_bias, gdn_conv_w=gdn_conv_w,
                 gdn_a_log=gdn_a_log, gdn_dt_bias=gdn_dt_bias, gdn_norm_g=gdn_norm_g,
                 ml_i_bias=ml_i_bias, ml_f_bias=ml_f_bias, ml_norm_g=ml_norm_g,
                 gate_bias=gate_bias, norm_ffn_g=norm_ffn_g, ffn_conv_w=ffn_conv_w)
    h = x.reshape(batch * seq, d_model)
    for layer in range(depth):
        h = _layer(h, layer, w, {k: v[layer] for k, v in small.items()}, batch, seq)
    return _final_norm(h, norm_final_g.reshape(1, d_model)).reshape(batch, seq, d_model)
```

```python
import functools

import jax
import jax.numpy as jnp
from jax import lax
from jax.experimental import pallas as pl
from jax.experimental.pallas import tpu as pltpu

F32 = jnp.float32
BF16 = jnp.bfloat16
HIGHEST = lax.Precision.HIGHEST

EPS = 1e-6
HEADS = 8
HEAD_DIM = 128
ML_QK_DIM = 64
CHUNK = 64
N_BRANCH = 3
LANES = 128
MXU_WIDTH = 256
HALO = 16

SM_FOX_F, SM_GDN_B, SM_GDN_A, SM_ML_I, SM_ML_F = 0, 8, 16, 24, 32

V7X_VMEM_BYTES = 64 * 1024 * 1024
VMEM_LIMIT_BYTES = V7X_VMEM_BYTES * 3 // 4
VMEM_LIMIT_LARGE_BYTES = V7X_VMEM_BYTES * 7 // 8


def _params(*semantics, vmem_limit_bytes=VMEM_LIMIT_BYTES):
    return pltpu.CompilerParams(dimension_semantics=semantics, vmem_limit_bytes=vmem_limit_bytes)


def _rmsnorm(x, g):
    return x * lax.rsqrt(jnp.mean(x * x, axis=-1, keepdims=True) + EPS) * g


def _sigmoid(x):
    return 1.0 / (1.0 + jnp.exp(-x))


def _mm_hi(a, b):
    return jnp.dot(a, b, precision=HIGHEST, preferred_element_type=F32)


def _norm_matmul_kernel(x_ref, g_ref, w_ref, cs_ref, o_ref, xn_ref):
    @pl.when(pl.program_id(1) == 0)
    def _():
        xn_ref[...] = _rmsnorm(x_ref[...], g_ref[...]).astype(BF16)

    acc = jnp.dot(xn_ref[...], w_ref[...], preferred_element_type=F32)
    o_ref[...] = (acc * cs_ref[...]).astype(o_ref.dtype)


def _norm_matmul(x, g, w_all, layer, colscale, out_dtype, tm=1024, tn=512):
    T, D = x.shape
    N = w_all.shape[2]
    return pl.pallas_call(
        _norm_matmul_kernel,
        grid=(T // tm, N // tn),
        in_specs=[
            pl.BlockSpec((tm, D), lambda i, j: (i, 0)),
            pl.BlockSpec((1, D), lambda i, j: (0, 0)),
            pl.BlockSpec((None, D, tn), lambda i, j: (layer, 0, j)),
            pl.BlockSpec((1, tn), lambda i, j: (0, j)),
        ],
        out_specs=pl.BlockSpec((tm, tn), lambda i, j: (i, j)),
        out_shape=jax.ShapeDtypeStruct((T, N), out_dtype),
        scratch_shapes=[pltpu.VMEM((tm, D), BF16)],
        compiler_params=_params("parallel", "arbitrary"),
    )(x, g, w_all, colscale)


def _small_kernel(x_ref, g_ref, w_ref, bias_ref, alog_ref, o_ref, rows_ref, carry_ref, *,
                  blocks_per_seq):
    i = pl.program_id(0)
    tm = x_ref.shape[0]

    @pl.when(i % blocks_per_seq == 0)
    def _():
        carry_ref[...] = jnp.zeros_like(carry_ref)

    xn = _rmsnorm(x_ref[...], g_ref[...]).astype(BF16)
    t = jnp.dot(xn, w_ref[...], preferred_element_type=F32) + bias_ref[...]
    lane = lax.broadcasted_iota(jnp.int32, (tm, LANES), 1)
    e = jnp.log1p(jnp.exp(-jnp.abs(t)))
    logsig = jnp.minimum(t, 0.0) - e
    softplus = jnp.maximum(t, 0.0) + e
    g_decay = -jnp.exp(alog_ref[...]) * softplus
    is_logsig = (lane < SM_GDN_B) | ((lane >= SM_ML_F) & (lane < SM_ML_F + HEADS))
    val = jnp.where(is_logsig, logsig,
                    jnp.where(lane < SM_GDN_A, _sigmoid(t),
                              jnp.where(lane < SM_ML_I, g_decay, t)))

    r = lax.broadcasted_iota(jnp.int32, (CHUNK, CHUNK), 0)
    c = lax.broadcasted_iota(jnp.int32, (CHUNK, CHUNK), 1)
    tri = jnp.where(c <= r, 1.0, 0.0).astype(F32)
    chunk_sums = [_mm_hi(tri, val[ci * CHUNK:(ci + 1) * CHUNK, :]) for ci in range(tm // CHUNK)]
    offset = carry_ref[...]
    full_sums = []
    for cs in chunk_sums:
        full_sums.append(cs + offset)
        offset = offset + cs[CHUNK - 1:CHUNK, :]
    carry_ref[...] = offset
    cs_chunk = jnp.concatenate(chunk_sums, axis=0)
    cs_full = jnp.concatenate(full_sums, axis=0)

    is_chunk_cs = ((lane >= SM_GDN_A) & (lane < SM_ML_I)) | ((lane >= SM_ML_F) & (lane < SM_ML_F + HEADS))
    out = jnp.where(lane < SM_GDN_B, cs_full, jnp.where(is_chunk_cs, cs_chunk, val))
    o_ref[...] = out
    for ci in range(tm // CHUNK):
        rows_ref[ci] = out[ci * CHUNK:(ci + 1) * CHUNK, :].T


def _small_gates(x, g, w_all, layer, bias, alog, seq, tm=512):
    T, D = x.shape
    return pl.pallas_call(
        functools.partial(_small_kernel, blocks_per_seq=seq // tm),
        grid=(T // tm,),
        in_specs=[
            pl.BlockSpec((tm, D), lambda i: (i, 0)),
            pl.BlockSpec((1, D), lambda i: (0, 0)),
            pl.BlockSpec((None, D, LANES), lambda i: (layer, 0, 0)),
            pl.BlockSpec((1, LANES), lambda i: (0, 0)),
            pl.BlockSpec((1, LANES), lambda i: (0, 0)),
        ],
        out_specs=[pl.BlockSpec((tm, LANES), lambda i: (i, 0)),
                   pl.BlockSpec((tm // CHUNK, LANES, CHUNK), lambda i: (i, 0, 0))],
        out_shape=[jax.ShapeDtypeStruct((T, LANES), F32),
                   jax.ShapeDtypeStruct((T // CHUNK, LANES, CHUNK), F32)],
        scratch_shapes=[pltpu.VMEM((1, LANES), F32)],
        compiler_params=_params("arbitrary"),
    )(x, g, w_all, bias, alog)


def _conv_taps(h, c_ref, cols):
    taps = c_ref.shape[0]
    tm = h.shape[0] - HALO
    acc = None
    for t in range(taps):
        start = HALO - (taps - 1) + t
        term = h[start:start + tm, :] * c_ref[t:t + 1, cols]
        acc = term if acc is None else acc + term
    return acc


def _column_groups(width):
    return [slice(c * MXU_WIDTH, (c + 1) * MXU_WIDTH) for c in range(width // MXU_WIDTH)]


def _norm_with_halo(x_ref, xh_ref, g_ref, xn_ref, at_seq_start):
    xn_ref[HALO:, :] = _rmsnorm(x_ref[...], g_ref[...]).astype(BF16)
    keep = jnp.where(at_seq_start, 0.0, 1.0)
    xn_ref[:HALO, :] = (_rmsnorm(xh_ref[...], g_ref[...]) * keep).astype(BF16)


def _proj_conv_kernel(x_ref, xh_ref, g_ref, w_ref, c_ref, o_ref, xn_ref, *, blocks_per_seq):
    @pl.when(pl.program_id(1) == 0)
    def _():
        _norm_with_halo(x_ref, xh_ref, g_ref, xn_ref, pl.program_id(0) % blocks_per_seq == 0)

    xn = xn_ref[...]
    groups = _column_groups(w_ref.shape[1])
    hidden = [jnp.dot(xn, w_ref[:, cs], preferred_element_type=F32) for cs in groups]
    for cs, h in zip(groups, hidden):
        u = _conv_taps(h, c_ref, cs)
        o_ref[:, cs] = (u * _sigmoid(u)).astype(o_ref.dtype)


def _proj_conv(x, g, w_all, layer, conv_w, seq, tm=1024, tn=1024):
    T, D = x.shape
    taps = conv_w.shape[0]
    N = w_all.shape[2]
    halo_blocks = tm // HALO
    return pl.pallas_call(
        functools.partial(_proj_conv_kernel, blocks_per_seq=seq // tm),
        grid=(T // tm, N // tn),
        in_specs=[
            pl.BlockSpec((tm, D), lambda i, j: (i, 0)),
            pl.BlockSpec((HALO, D), lambda i, j: (jnp.maximum(i * halo_blocks - 1, 0), 0)),
            pl.BlockSpec((1, D), lambda i, j: (0, 0)),
            pl.BlockSpec((None, D, tn), lambda i, j: (layer, 0, j)),
            pl.BlockSpec((taps, tn), lambda i, j: (0, j)),
        ],
        out_specs=pl.BlockSpec((tm, tn), lambda i, j: (i, j)),
        out_shape=jax.ShapeDtypeStruct((T, N), BF16),
        scratch_shapes=[pltpu.VMEM((tm + HALO, D), BF16)],
        compiler_params=_params("parallel", "arbitrary"),
    )(x, x, g, w_all, conv_w)


NEG_BIG = -1e30
LOG2E = 1.4426950408889634


def _fox_kernel(q_ref, k_ref, v_ref, f_ref, o_ref, m_ref, acc_ref, *, tq, tk, hps):
    qi = pl.program_id(2)
    ones = jnp.ones((tk, HEAD_DIM), BF16)
    lane_tiles = tk // LANES
    m_ref[...] = jnp.full_like(m_ref, NEG_BIG)
    acc_ref[...] = jnp.zeros_like(acc_ref)

    heads = [slice(h * HEAD_DIM, (h + 1) * HEAD_DIM) for h in range(hps)]

    def scores(j, masked):
        start = pl.multiple_of(j * tk, tk)
        out = []
        for h, hs in enumerate(heads):
            s = lax.dot_general(q_ref[:, hs], k_ref[pl.ds(start, tk), hs],
                                (((1,), (1,)), ((), ())), preferred_element_type=F32)
            s = s - f_ref[h, pl.ds(j, 1), :] * LOG2E
            if masked:
                row = lax.broadcasted_iota(jnp.int32, (tq, tk), 0)
                col = lax.broadcasted_iota(jnp.int32, (tq, tk), 1)
                s = jnp.where(col <= row, s, NEG_BIG)
            out.append([s[:, c * LANES:(c + 1) * LANES] for c in range(lane_tiles)])
        return out

    def softmax(score_tiles):
        probs, alphas = [], []
        for h, tiles in enumerate(score_tiles):
            m_prev = m_ref[h]
            m_tile = functools.reduce(jnp.maximum, tiles)
            m_new = jnp.maximum(m_prev, jnp.max(m_tile, axis=-1, keepdims=True))
            m_ref[h] = m_new
            alphas.append(jnp.exp2(m_prev - m_new))
            probs.append(jnp.concatenate([jnp.exp2(t - m_new).astype(BF16) for t in tiles], axis=1))
        return probs, alphas

    def accumulate(j, probs, alphas):
        start = pl.multiple_of(j * tk, tk)
        for h, hs in enumerate(heads):
            v_ext = jnp.concatenate([v_ref[pl.ds(start, tk), hs], ones], axis=1)
            pv = jnp.dot(probs[h], v_ext, preferred_element_type=F32)
            acc_ref[h] = jnp.concatenate([alphas[h], alphas[h]], axis=1) * acc_ref[h] + pv

    def block(j, masked):
        accumulate(j, *softmax(scores(j, masked)))

    def block_pair(j, second_masked):
        first = softmax(scores(j, False))
        second_scores = scores(j + 1, second_masked)
        accumulate(j, *first)
        accumulate(j + 1, *softmax(second_scores))

    def full_pair(t, carry):
        block_pair(2 * t, False)
        return carry

    lax.fori_loop(0, qi >> 1, full_pair, 0)

    @pl.when((qi & 1) == 1)
    def _():
        block_pair(qi - 1, True)

    @pl.when((qi & 1) == 0)
    def _():
        block(qi, True)

    for h in range(hps):
        acc = acc_ref[h]
        o_ref[:, h * HEAD_DIM:(h + 1) * HEAD_DIM] = (
            acc[:, :HEAD_DIM] / acc[:, HEAD_DIM:]).astype(o_ref.dtype)


def _fox_attention(qkv, f_cum, batch, seq, *, q_col, k_col, v_col, tq=512, hps=2):
    T = qkv.shape[0]
    nq = seq // tq
    width = hps * HEAD_DIM
    f_blocks = f_cum.reshape(batch, HEADS, nq, tq)
    return pl.pallas_call(
        functools.partial(_fox_kernel, tq=tq, tk=tq, hps=hps),
        grid=(batch, HEADS // hps, nq),
        in_specs=[
            pl.BlockSpec((tq, width), lambda b, h, qi: (b * nq + qi, q_col + h)),
            pl.BlockSpec((seq, width), lambda b, h, qi: (b, k_col + h)),
            pl.BlockSpec((seq, width), lambda b, h, qi: (b, v_col + h)),
            pl.BlockSpec((None, hps, nq, tq), lambda b, h, qi: (b, h, 0, 0)),
        ],
        out_specs=pl.BlockSpec((tq, width), lambda b, h, qi: (b * nq + qi, h)),
        out_shape=jax.ShapeDtypeStruct((T, HEADS * HEAD_DIM), BF16),
        scratch_shapes=[pltpu.VMEM((hps, tq, LANES), F32),
                        pltpu.VMEM((hps, tq, 2 * HEAD_DIM), F32)],
        compiler_params=_params("parallel", "parallel", "arbitrary",
                                vmem_limit_bytes=VMEM_LIMIT_LARGE_BYTES),
    )(qkv, qkv, qkv, f_blocks)


def _bmm(a, b):
    return jnp.einsum("gmk,gkn->gmn", a.astype(BF16), b.astype(BF16), preferred_element_type=F32)


def _bmm_nt(a, b):
    return jnp.einsum("gmk,gnk->gmn", a.astype(BF16), b.astype(BF16), preferred_element_type=F32)


def _bmm_tn(a, b):
    return jnp.einsum("gkm,gkn->gmn", a.astype(BF16), b.astype(BF16), preferred_element_type=F32)


def _inverse_pairs(m, eye, r, c):
    return eye - jnp.where((r >> 1) == (c >> 1), m, 0.0)


def _inverse_double(x, m, r, c, b):
    shift = b.bit_length() - 1
    joins = ((r >> (shift + 1)) == (c >> (shift + 1))) & ((r >> shift) != (c >> shift))
    return x - _bmm(_bmm(x, jnp.where(joins, m, 0.0)), x)


STEP_CHUNKS = 4


def _chunk_head_stack(ref, width=HEAD_DIM):
    return jnp.stack([ref[ci * CHUNK:(ci + 1) * CHUNK, h * width:(h + 1) * width]
                      for ci in range(STEP_CHUNKS) for h in range(HEADS)])


def _chunk_head_cols(col, start):
    return jnp.stack([col[ci * CHUNK:(ci + 1) * CHUNK, start + h:start + h + 1]
                      for ci in range(STEP_CHUNKS) for h in range(HEADS)])


def _chunk_head_rows(row_ref):
    return jnp.concatenate([row_ref[ci] for ci in range(STEP_CHUNKS)], axis=0)[:, None, :]


def _store_heads(o_ref, y):
    for ci in range(STEP_CHUNKS):
        for h in range(HEADS):
            o_ref[ci * CHUNK:(ci + 1) * CHUNK, h * HEAD_DIM:(h + 1) * HEAD_DIM] = y[ci * HEADS + h]


def _chunk_mixers_kernel(gq_ref, gk_ref, gv_ref, z_ref, mq_ref, mk_ref, mv_ref, og_ref, col_ref,
                         rowg_ref, rowi_ref, rowb_ref, gng_ref, mng_ref, yg_ref, ym_ref,
                         s_ref, c_ref, m_ref):
    @pl.when(pl.program_id(1) == 0)
    def _():
        s_ref[...] = jnp.zeros_like(s_ref)
        c_ref[...] = jnp.zeros_like(c_ref)
        m_ref[...] = jnp.zeros_like(m_ref)

    L = CHUNK
    r = lax.broadcasted_iota(jnp.int32, (1, L, L), 1)
    c = lax.broadcasted_iota(jnp.int32, (1, L, L), 2)
    tri = c <= r
    eye = jnp.where(c == r, 1.0, 0.0).astype(F32)
    col = col_ref[...]

    q = _chunk_head_stack(gq_ref).astype(F32)
    k = _chunk_head_stack(gk_ref).astype(F32)
    v = _chunk_head_stack(gv_ref).astype(F32)
    qn = q * lax.rsqrt(jnp.sum(q * q, axis=-1, keepdims=True) + EPS) * (HEAD_DIM ** -0.5)
    kn = k * lax.rsqrt(jnp.sum(k * k, axis=-1, keepdims=True) + EPS)
    beta = _chunk_head_cols(col, SM_GDN_B)
    gc = _chunk_head_cols(col, SM_GDN_A)
    gr = _chunk_head_rows(rowg_ref)
    g_last = gc[:, L - 1:L, :]
    decay = jnp.where(tri, jnp.exp(jnp.where(tri, gc - gr, 0.0)), 0.0)
    eg = jnp.exp(gc)
    kb = kn * beta
    m = jnp.where(c < r, _bmm_nt(kb, kn) * decay, 0.0)
    t_inv = _inverse_pairs(m, eye, r, c)

    mq = _chunk_head_stack(mq_ref, ML_QK_DIM).astype(F32)
    mk = _chunk_head_stack(mk_ref, ML_QK_DIM).astype(F32)
    mv = _chunk_head_stack(mv_ref)
    mv = jnp.concatenate([mv, jnp.ones_like(mv)], axis=-1)
    ic = _chunk_head_cols(col, SM_ML_I)
    bc = _chunk_head_cols(col, SM_ML_F)
    ir = _chunk_head_rows(rowi_ref)
    br = _chunk_head_rows(rowb_ref)
    dm = jnp.where(tri, bc - br + ir, -jnp.inf)
    dmax = jnp.max(dm, axis=-1, keepdims=True)
    qk = _bmm_nt(mq, mk)

    t_inv = _inverse_double(t_inv, m, r, c, 2)
    t_inv = _inverse_double(t_inv, m, r, c, 4)

    m_state = m_ref[...]
    m_rows, m_lasts = [], []
    for ci in range(STEP_CHUNKS):
        hs = slice(ci * HEADS, (ci + 1) * HEADS)
        m_r = jnp.maximum(bc[hs] + m_state, dmax[hs])
        m_rows.append(m_r)
        m_state = m_r[:, L - 1:L, :]
        m_lasts.append(m_state)
    m_r = jnp.concatenate(m_rows, axis=0)
    m_last = jnp.concatenate(m_lasts, axis=0)
    m_prev = jnp.concatenate([m_ref[...]] + m_lasts[:-1], axis=0)
    m_ref[...] = m_state
    w_inter = jnp.exp(bc + m_prev - m_r)
    sm = jnp.exp(dm - m_r) * qk
    intra = _bmm(sm, mv)

    t_inv = _inverse_double(t_inv, m, r, c, 8)

    wk = jnp.exp(bc[:, L - 1:L, :] - bc + ic - m_last) * mk
    kv = _bmm_tn(wk, mv)
    w_last = w_inter[:, L - 1:L, :]
    floor = jnp.exp(-m_r)

    t_inv = _inverse_double(t_inv, m, r, c, 16)
    t_inv = _inverse_double(t_inv, m, r, c, 32)
    x = _bmm(t_inv, jnp.concatenate([v * beta, kb * eg], axis=-1))
    u = x[:, :, :HEAD_DIM]
    w = x[:, :, HEAD_DIM:]
    a_qk = _bmm_nt(qn, kn) * decay
    qg = qn * eg
    kg = kn * jnp.exp(g_last - gc)
    carry = jnp.exp(g_last)

    state = s_ref[...]
    c_state = c_ref[...]
    g_outs, m_outs = [], []
    for ci in range(STEP_CHUNKS):
        hs = slice(ci * HEADS, (ci + 1) * HEADS)
        v_new = u[hs] - _bmm(w[hs], state)
        num_den = w_inter[hs] * _bmm(mq[hs], c_state) + intra[hs]
        g_outs.append(_bmm(qg[hs], state) + _bmm(a_qk[hs], v_new))
        state = state * carry[hs] + _bmm_tn(kg[hs], v_new)
        den = jnp.maximum(jnp.abs(num_den[:, :, HEAD_DIM:]), floor[hs])
        m_outs.append(num_den[:, :, :HEAD_DIM] / den)
        c_state = w_last[hs] * c_state + kv[hs]
    s_ref[...] = state
    c_ref[...] = c_state

    o = jnp.concatenate(g_outs, axis=0)
    z = _chunk_head_stack(z_ref).astype(F32)
    _store_heads(yg_ref, (_rmsnorm(o, gng_ref[...]) * (z * _sigmoid(z))).astype(yg_ref.dtype))
    h_tilde = jnp.concatenate(m_outs, axis=0)
    og = _chunk_head_stack(og_ref).astype(F32)
    _store_heads(ym_ref, _rmsnorm(_sigmoid(og) * h_tilde, mng_ref[...]).astype(ym_ref.dtype))


def _chunk_mixers(gdn_qkv, proj, col_small, row_small, gdn_norm_g, ml_norm_g, batch, seq, *,
                  z_col, q_col, k_col, v_col, og_col):
    T = proj.shape[0]
    rows = STEP_CHUNKS * CHUNK
    ns = seq // rows
    qk_width = HEADS * ML_QK_DIM
    width = HEADS * HEAD_DIM
    row = lambda b, s: b * ns + s
    wide = lambda col: pl.BlockSpec((rows, width), lambda b, s: (row(b, s), col))
    rows_spec = lambda seg: pl.BlockSpec((STEP_CHUNKS, HEADS, CHUNK),
                                         lambda b, s: (row(b, s), seg // HEADS, 0))
    norm_spec = pl.BlockSpec((1, HEAD_DIM), lambda b, s: (0, 0))
    out_spec = pl.BlockSpec((rows, width), lambda b, s: (row(b, s), 0))
    out_shape = jax.ShapeDtypeStruct((T, width), BF16)
    return pl.pallas_call(
        _chunk_mixers_kernel,
        grid=(batch, ns),
        in_specs=[
            wide(0), wide(1), wide(2), wide(z_col),
            pl.BlockSpec((rows, qk_width), lambda b, s: (row(b, s), q_col)),
            pl.BlockSpec((rows, qk_width), lambda b, s: (row(b, s), k_col)),
            wide(v_col), wide(og_col),
            pl.BlockSpec((rows, LANES), lambda b, s: (row(b, s), 0)),
            rows_spec(SM_GDN_A), rows_spec(SM_ML_I), rows_spec(SM_ML_F),
            norm_spec, norm_spec,
        ],
        out_specs=[out_spec, out_spec],
        out_shape=[out_shape, out_shape],
        scratch_shapes=[pltpu.VMEM((HEADS, HEAD_DIM, HEAD_DIM), F32),
                        pltpu.VMEM((HEADS, ML_QK_DIM, 2 * HEAD_DIM), F32),
                        pltpu.VMEM((HEADS, 1, 1), F32)],
        compiler_params=_params("parallel", "arbitrary"),
    )(gdn_qkv, gdn_qkv, gdn_qkv, proj, proj, proj, proj, proj, col_small,
      row_small, row_small, row_small, gdn_norm_g, ml_norm_g)


def _merge_kernel(yf_ref, yg_ref, ym_ref, wf_ref, wg_ref, wm_ref, g0_ref, g1_ref, g2_ref,
                  b0_ref, b1_ref, b2_ref, o_ref):
    def branch(y_ref, w_ref, g_ref, b_ref):
        return _sigmoid(g_ref[...] + b_ref[...]) * jnp.dot(y_ref[...], w_ref[...],
                                                          preferred_element_type=F32)

    y = (branch(yf_ref, wf_ref, g0_ref, b0_ref) + branch(yg_ref, wg_ref, g1_ref, b1_ref)
         + branch(ym_ref, wm_ref, g2_ref, b2_ref))
    o_ref[...] = y.astype(o_ref.dtype)


def _merge(y_fox, y_gdn, y_ml, w_fox, w_gdn, w_ml, layer, gate_src, gate_bias, d_model, *,
           gate_col, tm=1024, tn=512):
    T, K = y_fox.shape
    nj = d_model // tn
    y_spec = pl.BlockSpec((tm, K), lambda i, j: (i, 0))
    w_spec = pl.BlockSpec((None, K, tn), lambda i, j: (layer, 0, j))
    gate_specs = [pl.BlockSpec((tm, tn), lambda i, j, n=n: (i, gate_col + n * nj + j))
                  for n in range(N_BRANCH)]
    bias_specs = [pl.BlockSpec((1, tn), lambda i, j, n=n: (0, n * nj + j)) for n in range(N_BRANCH)]
    return pl.pallas_call(
        _merge_kernel,
        grid=(T // tm, nj),
        in_specs=[y_spec, y_spec, y_spec, w_spec, w_spec, w_spec] + gate_specs + bias_specs,
        out_specs=pl.BlockSpec((tm, tn), lambda i, j: (i, j)),
        out_shape=jax.ShapeDtypeStruct((T, d_model), BF16),
        compiler_params=_params("parallel", "parallel"),
    )(y_fox, y_gdn, y_ml, w_fox, w_gdn, w_ml, gate_src, gate_src, gate_src,
      gate_bias, gate_bias, gate_bias)


def _matmul_residual_kernel(a_ref, w_ref, x_ref, o_ref):
    o_ref[...] = x_ref[...] + jnp.dot(a_ref[...], w_ref[...], preferred_element_type=F32)


def _matmul_residual(a, w_all, layer, x, tm=1024, tn=512):
    T, K = a.shape
    N = w_all.shape[2]
    return pl.pallas_call(
        _matmul_residual_kernel,
        grid=(T // tm, N // tn),
        in_specs=[
            pl.BlockSpec((tm, K), lambda i, j: (i, 0)),
            pl.BlockSpec((None, K, tn), lambda i, j: (layer, 0, j)),
            pl.BlockSpec((tm, tn), lambda i, j: (i, j)),
        ],
        out_specs=pl.BlockSpec((tm, tn), lambda i, j: (i, j)),
        out_shape=jax.ShapeDtypeStruct((T, N), F32),
        compiler_params=_params("parallel", "parallel"),
    )(a, w_all, x)


def _final_norm_kernel(x_ref, g_ref, o_ref):
    o_ref[...] = _rmsnorm(x_ref[...], g_ref[...])


def _final_norm(x, g, tm=512):
    T, D = x.shape
    return pl.pallas_call(
        _final_norm_kernel,
        grid=(T // tm,),
        in_specs=[pl.BlockSpec((tm, D), lambda i: (i, 0)), pl.BlockSpec((1, D), lambda i: (0, 0))],
        out_specs=pl.BlockSpec((tm, D), lambda i: (i, 0)),
        out_shape=jax.ShapeDtypeStruct((T, D), F32),
        compiler_params=_params("parallel"),
    )(x, g)


def _ffn_kernel(x_ref, xh_ref, g_ref, wu_ref, wg_ref, cu_ref, cg_ref, wd_ref, o_ref, xn_ref, *,
                blocks_per_seq):
    @pl.when(pl.program_id(1) == 0)
    def _():
        _norm_with_halo(x_ref, xh_ref, g_ref, xn_ref, pl.program_id(0) % blocks_per_seq == 0)
        o_ref[...] = x_ref[...]

    xn = xn_ref[...]
    groups = _column_groups(wu_ref.shape[1])
    hidden = [(jnp.dot(xn, wu_ref[:, cs], preferred_element_type=F32),
               jnp.dot(xn, wg_ref[:, cs], preferred_element_type=F32)) for cs in groups]
    acts = []
    for cs, (hu, hg) in zip(groups, hidden):
        gt = _conv_taps(hg, cg_ref, cs)
        acts.append((gt * _sigmoid(gt) * _conv_taps(hu, cu_ref, cs)).astype(BF16))
    o_ref[...] += jnp.dot(jnp.concatenate(acts, axis=1), wd_ref[...], preferred_element_type=F32)


def _ffn(x, g, w_up, conv_w, w_down, layer, seq, tm=1024, tn=512):
    T, D = x.shape
    taps = conv_w.shape[0]
    nj = w_down.shape[1] // tn
    halo_blocks = tm // HALO
    once = pl.Buffered(1)
    return pl.pallas_call(
        functools.partial(_ffn_kernel, blocks_per_seq=seq // tm),
        grid=(T // tm, nj),
        in_specs=[
            pl.BlockSpec((tm, D), lambda i, j: (i, 0)),
            pl.BlockSpec((HALO, D), lambda i, j: (jnp.maximum(i * halo_blocks - 1, 0), 0)),
            pl.BlockSpec((1, D), lambda i, j: (0, 0)),
            pl.BlockSpec((None, D, tn), lambda i, j: (layer, 0, j)),
            pl.BlockSpec((None, D, tn), lambda i, j: (layer, 0, nj + j)),
            pl.BlockSpec((taps, tn), lambda i, j: (0, j)),
            pl.BlockSpec((taps, tn), lambda i, j: (0, nj + j)),
            pl.BlockSpec((None, tn, D), lambda i, j: (layer, j, 0)),
        ],
        out_specs=pl.BlockSpec((tm, D), lambda i, j: (i, 0), pipeline_mode=once),
        out_shape=jax.ShapeDtypeStruct((T, D), F32),
        scratch_shapes=[pltpu.VMEM((tm + HALO, D), BF16)],
        compiler_params=_params("parallel", "arbitrary", vmem_limit_bytes=VMEM_LIMIT_LARGE_BYTES),
    )(x, x, g, w_up, w_up, conv_w, conv_w, w_down)


FOX_W = HEADS * HEAD_DIM
GDN_W = HEADS * HEAD_DIM
ML_QK_W = HEADS * ML_QK_DIM
ML_V_W = HEADS * HEAD_DIM
ML_BASE = 3 * FOX_W
Z_BASE = ML_BASE + 2 * ML_QK_W + ML_V_W
GATE_BASE = Z_BASE + GDN_W + ML_V_W
FOX_HEADS_PER_STEP = 4


def _in_projection_columns(d_model):
    widths = [("fox_q", FOX_W), ("fox_k", FOX_W), ("fox_v", FOX_W), ("fox_f", HEADS),
              ("gdn_qkv", 3 * GDN_W), ("gdn_z", GDN_W), ("gdn_b", HEADS), ("gdn_a", HEADS),
              ("ml_q", ML_QK_W), ("ml_k", ML_QK_W), ("ml_v", ML_V_W), ("ml_i", HEADS),
              ("ml_f", HEADS), ("ml_o", ML_V_W), ("gate", N_BRANCH * d_model)]
    off, start = {}, 0
    for name, width in widths:
        off[name] = (start, start + width)
        start += width
    return off


def _pad_lanes(v):
    return jnp.pad(v, (0, LANES - v.shape[0])).reshape(1, LANES)


def _layer(x, layer, w, p, batch, seq):
    T, D = x.shape
    g_mix = p["norm_mix_g"].reshape(1, D)
    scale_p = jnp.concatenate([
        jnp.full((FOX_W,), HEAD_DIM ** -0.5 * LOG2E, F32), jnp.ones((2 * FOX_W,), F32),
        jnp.full((ML_QK_W,), ML_QK_DIM ** -0.5, F32),
        jnp.ones((w["proj"].shape[2] - 3 * FOX_W - ML_QK_W,), F32)]).reshape(1, -1)
    proj = _norm_matmul(x, g_mix, w["proj"], layer, scale_p, BF16, tn=1024)
    zeros8 = jnp.zeros((HEADS,), F32)
    bias_s = _pad_lanes(jnp.concatenate([p["fox_f_bias"], zeros8, p["gdn_dt_bias"],
                                         p["ml_i_bias"], p["ml_f_bias"]]))
    alog_s = _pad_lanes(jnp.concatenate([zeros8, zeros8, p["gdn_a_log"]]))
    col_small, row_small = _small_gates(x, g_mix, w["small"], layer, bias_s, alog_s, seq)
    f_cum = row_small[:, SM_FOX_F:SM_FOX_F + HEADS, :].reshape(batch, seq // CHUNK, HEADS, CHUNK)
    f_cum = f_cum.transpose(0, 2, 1, 3).reshape(batch, HEADS, seq)
    gdn_qkv = _proj_conv(x, g_mix, w["gdn_qkv"], layer, p["gdn_conv_w"], seq)

    hps = FOX_HEADS_PER_STEP
    y_fox = _fox_attention(proj, f_cum, batch, seq, q_col=0, k_col=HEADS // hps,
                           v_col=2 * HEADS // hps, hps=hps)
    y_gdn, y_ml = _chunk_mixers(
        gdn_qkv, proj, col_small, row_small, p["gdn_norm_g"].reshape(1, HEAD_DIM),
        p["ml_norm_g"].reshape(1, HEAD_DIM), batch, seq, z_col=Z_BASE // GDN_W,
        q_col=ML_BASE // ML_QK_W, k_col=ML_BASE // ML_QK_W + 1,
        v_col=(ML_BASE + 2 * ML_QK_W) // ML_V_W, og_col=(Z_BASE + GDN_W) // ML_V_W)

    tn = 512
    y = _merge(y_fox, y_gdn, y_ml, w["fox_proj"], w["gdn_proj"], w["ml_proj"], layer, proj,
               p["gate_bias"].reshape(1, -1), D, gate_col=GATE_BASE // tn, tn=tn)
    x = _matmul_residual(y, w["out"], layer, x)
    return _ffn(x, p["norm_ffn_g"].reshape(1, D), w["up"], p["ffn_conv_w"], w["down"], layer, seq)


def kernel(x, norm_mix_g, w_in, fox_f_bias, gdn_conv_w, gdn_a_log, gdn_dt_bias, gdn_norm_g,
           ml_i_bias, ml_f_bias, ml_norm_g, gate_bias, w_fox_proj, w_gdn_proj, w_ml_proj,
           w_out, norm_ffn_g, w_up, ffn_conv_w, w_down, norm_final_g):
    batch, seq, d_model = x.shape
    depth = w_in.shape[0]
    off = _in_projection_columns(d_model)
    cols = lambda a, b: w_in[:, :, off[a][0]:off[b][1]].astype(BF16)
    w_small = jnp.concatenate([cols("fox_f", "fox_f"), cols("gdn_b", "gdn_a"), cols("ml_i", "ml_f")],
                              axis=2)
    w = dict(
        proj=jnp.concatenate([cols("fox_q", "fox_v"), cols("ml_q", "ml_v"), cols("gdn_z", "gdn_z"),
                              cols("ml_o", "ml_o"), cols("gate", "gate")], axis=2),
        small=jnp.pad(w_small, ((0, 0), (0, 0), (0, LANES - w_small.shape[2]))),
        gdn_qkv=cols("gdn_qkv", "gdn_qkv"),
        fox_proj=w_fox_proj.astype(BF16), gdn_proj=w_gdn_proj.astype(BF16),
        ml_proj=w_ml_proj.astype(BF16), out=w_out.astype(BF16), up=w_up.astype(BF16),
        down=w_down.astype(BF16))
    small = dict(norm_mix_g=norm_mix_g, fox_f_bias=fox_f_bias, gdn_conv_w=gdn_conv_w,
                 gdn_a_log=gdn_a_log, gdn_dt_bias=gdn_dt_bias, gdn_norm_g=gdn_norm_g,
                 ml_i_bias=ml_i_bias, ml_f_bias=ml_f_bias, ml_norm_g=ml_norm_g,
                 gate_bias=gate_bias, norm_ffn_g=norm_ffn_g, ffn_conv_w=ffn_conv_w)
    h = x.reshape(batch * seq, d_model)
    for layer in range(depth):
        h = _layer(h, layer, w, {k: v[layer] for k, v in small.items()}, batch, seq)
    return _final_norm(h, norm_final_g.reshape(1, d_model)).reshape(batch, seq, d_model)
```

```python
import functools

import jax
import jax.numpy as jnp
from jax import lax
from jax.experimental import pallas as pl
from jax.experimental.pallas import tpu as pltpu

F32 = jnp.float32
BF16 = jnp.bfloat16
HIGHEST = lax.Precision.HIGHEST

EPS = 1e-6
HEADS = 8
HEAD_DIM = 128
ML_QK_DIM = 64
CHUNK = 64
N_BRANCH = 3
LANES = 128
MXU_WIDTH = 256
HALO = 16

SM_FOX_F, SM_GDN_B, SM_GDN_A, SM_ML_I, SM_ML_F = 0, 8, 16, 24, 32

V7X_VMEM_BYTES = 64 * 1024 * 1024
VMEM_LIMIT_BYTES = V7X_VMEM_BYTES * 3 // 4
VMEM_LIMIT_LARGE_BYTES = V7X_VMEM_BYTES * 7 // 8


def _params(*semantics, vmem_limit_bytes=VMEM_LIMIT_BYTES):
    return pltpu.CompilerParams(dimension_semantics=semantics, vmem_limit_bytes=vmem_limit_bytes)


def _rmsnorm(x, g):
    return x * lax.rsqrt(jnp.mean(x * x, axis=-1, keepdims=True) + EPS) * g


def _sigmoid(x):
    return 1.0 / (1.0 + jnp.exp(-x))


def _mm_hi(a, b):
    return jnp.dot(a, b, precision=HIGHEST, preferred_element_type=F32)


def _norm_matmul_kernel(x_ref, g_ref, w_ref, cs_ref, o_ref, xn_ref):
    @pl.when(pl.program_id(1) == 0)
    def _():
        xn_ref[...] = _rmsnorm(x_ref[...], g_ref[...]).astype(BF16)

    acc = jnp.dot(xn_ref[...], w_ref[...], preferred_element_type=F32)
    o_ref[...] = (acc * cs_ref[...]).astype(o_ref.dtype)


def _norm_matmul(x, g, w_all, layer, colscale, out_dtype, tm=1024, tn=512):
    T, D = x.shape
    N = w_all.shape[2]
    return pl.pallas_call(
        _norm_matmul_kernel,
        grid=(T // tm, N // tn),
        in_specs=[
            pl.BlockSpec((tm, D), lambda i, j: (i, 0)),
            pl.BlockSpec((1, D), lambda i, j: (0, 0)),
            pl.BlockSpec((None, D, tn), lambda i, j: (layer, 0, j)),
            pl.BlockSpec((1, tn), lambda i, j: (0, j)),
        ],
        out_specs=pl.BlockSpec((tm, tn), lambda i, j: (i, j)),
        out_shape=jax.ShapeDtypeStruct((T, N), out_dtype),
        scratch_shapes=[pltpu.VMEM((tm, D), BF16)],
        compiler_params=_params("parallel", "arbitrary"),
    )(x, g, w_all, colscale)


def _small_kernel(x_ref, g_ref, w_ref, bias_ref, alog_ref, o_ref, rows_ref, carry_ref, *,
                  blocks_per_seq):
    i = pl.program_id(0)
    tm = x_ref.shape[0]

    @pl.when(i % blocks_per_seq == 0)
    def _():
        carry_ref[...] = jnp.zeros_like(carry_ref)

    xn = _rmsnorm(x_ref[...], g_ref[...]).astype(BF16)
    t = jnp.dot(xn, w_ref[...], preferred_element_type=F32) + bias_ref[...]
    lane = lax.broadcasted_iota(jnp.int32, (tm, LANES), 1)
    e = jnp.log1p(jnp.exp(-jnp.abs(t)))
    logsig = jnp.minimum(t, 0.0) - e
    softplus = jnp.maximum(t, 0.0) + e
    g_decay = -jnp.exp(alog_ref[...]) * softplus
    is_logsig = (lane < SM_GDN_B) | ((lane >= SM_ML_F) & (lane < SM_ML_F + HEADS))
    val = jnp.where(is_logsig, logsig,
                    jnp.where(lane < SM_GDN_A, _sigmoid(t),
                              jnp.where(lane < SM_ML_I, g_decay, t)))

    r = lax.broadcasted_iota(jnp.int32, (CHUNK, CHUNK), 0)
    c = lax.broadcasted_iota(jnp.int32, (CHUNK, CHUNK), 1)
    tri = jnp.where(c <= r, 1.0, 0.0).astype(F32)
    chunk_sums = [_mm_hi(tri, val[ci * CHUNK:(ci + 1) * CHUNK, :]) for ci in range(tm // CHUNK)]
    offset = carry_ref[...]
    full_sums = []
    for cs in chunk_sums:
        full_sums.append(cs + offset)
        offset = offset + cs[CHUNK - 1:CHUNK, :]
    carry_ref[...] = offset
    cs_chunk = jnp.concatenate(chunk_sums, axis=0)
    cs_full = jnp.concatenate(full_sums, axis=0)

    is_chunk_cs = ((lane >= SM_GDN_A) & (lane < SM_ML_I)) | ((lane >= SM_ML_F) & (lane < SM_ML_F + HEADS))
    out = jnp.where(lane < SM_GDN_B, cs_full, jnp.where(is_chunk_cs, cs_chunk, val))
    o_ref[...] = out
    for ci in range(tm // CHUNK):
        rows_ref[ci] = out[ci * CHUNK:(ci + 1) * CHUNK, :].T


def _small_gates(x, g, w_all, layer, bias, alog, seq, tm=512):
    T, D = x.shape
    return pl.pallas_call(
        functools.partial(_small_kernel, blocks_per_seq=seq // tm),
        grid=(T // tm,),
        in_specs=[
            pl.BlockSpec((tm, D), lambda i: (i, 0)),
            pl.BlockSpec((1, D), lambda i: (0, 0)),
            pl.BlockSpec((None, D, LANES), lambda i: (layer, 0, 0)),
            pl.BlockSpec((1, LANES), lambda i: (0, 0)),
            pl.BlockSpec((1, LANES), lambda i: (0, 0)),
        ],
        out_specs=[pl.BlockSpec((tm, LANES), lambda i: (i, 0)),
                   pl.BlockSpec((tm // CHUNK, LANES, CHUNK), lambda i: (i, 0, 0))],
        out_shape=[jax.ShapeDtypeStruct((T, LANES), F32),
                   jax.ShapeDtypeStruct((T // CHUNK, LANES, CHUNK), F32)],
        scratch_shapes=[pltpu.VMEM((1, LANES), F32)],
        compiler_params=_params("arbitrary"),
    )(x, g, w_all, bias, alog)


def _conv_taps(h, c_ref, cols):
    taps = c_ref.shape[0]
    tm = h.shape[0] - HALO
    acc = None
    for t in range(taps):
        start = HALO - (taps - 1) + t
        term = h[start:start + tm, :] * c_ref[t:t + 1, cols]
        acc = term if acc is None else acc + term
    return acc


def _column_groups(width):
    return [slice(c * MXU_WIDTH, (c + 1) * MXU_WIDTH) for c in range(width // MXU_WIDTH)]


def _norm_with_halo(x_ref, xh_ref, g_ref, xn_ref, at_seq_start):
    xn_ref[HALO:, :] = _rmsnorm(x_ref[...], g_ref[...]).astype(BF16)
    keep = jnp.where(at_seq_start, 0.0, 1.0)
    xn_ref[:HALO, :] = (_rmsnorm(xh_ref[...], g_ref[...]) * keep).astype(BF16)


def _proj_conv_kernel(x_ref, xh_ref, g_ref, w_ref, c_ref, o_ref, xn_ref, *, blocks_per_seq):
    @pl.when(pl.program_id(1) == 0)
    def _():
        _norm_with_halo(x_ref, xh_ref, g_ref, xn_ref, pl.program_id(0) % blocks_per_seq == 0)

    xn = xn_ref[...]
    groups = _column_groups(w_ref.shape[1])
    hidden = [jnp.dot(xn, w_ref[:, cs], preferred_element_type=F32) for cs in groups]
    for cs, h in zip(groups, hidden):
        u = _conv_taps(h, c_ref, cs)
        o_ref[:, cs] = (u * _sigmoid(u)).astype(o_ref.dtype)


def _proj_conv(x, g, w_all, layer, conv_w, seq, tm=1024, tn=1024):
    T, D = x.shape
    taps = conv_w.shape[0]
    N = w_all.shape[2]
    halo_blocks = tm // HALO
    return pl.pallas_call(
        functools.partial(_proj_conv_kernel, blocks_per_seq=seq // tm),
        grid=(T // tm, N // tn),
        in_specs=[
            pl.BlockSpec((tm, D), lambda i, j: (i, 0)),
            pl.BlockSpec((HALO, D), lambda i, j: (jnp.maximum(i * halo_blocks - 1, 0), 0)),
            pl.BlockSpec((1, D), lambda i, j: (0, 0)),
            pl.BlockSpec((None, D, tn), lambda i, j: (layer, 0, j)),
            pl.BlockSpec((taps, tn), lambda i, j: (0, j)),
        ],
        out_specs=pl.BlockSpec((tm, tn), lambda i, j: (i, j)),
        out_shape=jax.ShapeDtypeStruct((T, N), BF16),
        scratch_shapes=[pltpu.VMEM((tm + HALO, D), BF16)],
        compiler_params=_params("parallel", "arbitrary"),
    )(x, x, g, w_all, conv_w)


NEG_BIG = -1e30
LOG2E = 1.4426950408889634


def _fox_kernel(q_ref, k_ref, v_ref, f_ref, o_ref, m_ref, acc_ref, *, tq, tk, hps):
    qi = pl.program_id(2)
    ones = jnp.ones((tk, HEAD_DIM), BF16)
    lane_tiles = tk // LANES
    m_ref[...] = jnp.full_like(m_ref, NEG_BIG)
    acc_ref[...] = jnp.zeros_like(acc_ref)

    heads = [slice(h * HEAD_DIM, (h + 1) * HEAD_DIM) for h in range(hps)]

    def scores(j, masked):
        start = pl.multiple_of(j * tk, tk)
        out = []
        for h, hs in enumerate(heads):
            s = lax.dot_general(q_ref[:, hs], k_ref[pl.ds(start, tk), hs],
                                (((1,), (1,)), ((), ())), preferred_element_type=F32)
            s = s - f_ref[h, pl.ds(j, 1), :] * LOG2E
            if masked:
                row = lax.broadcasted_iota(jnp.int32, (tq, tk), 0)
                col = lax.broadcasted_iota(jnp.int32, (tq, tk), 1)
                s = jnp.where(col <= row, s, NEG_BIG)
            out.append([s[:, c * LANES:(c + 1) * LANES] for c in range(lane_tiles)])
        return out

    def softmax(score_tiles):
        probs, alphas = [], []
        for h, tiles in enumerate(score_tiles):
            m_prev = m_ref[h]
            m_tile = functools.reduce(jnp.maximum, tiles)
            m_new = jnp.maximum(m_prev, jnp.max(m_tile, axis=-1, keepdims=True))
            m_ref[h] = m_new
            alphas.append(jnp.exp2(m_prev - m_new))
            probs.append(jnp.concatenate([jnp.exp2(t - m_new).astype(BF16) for t in tiles], axis=1))
        return probs, alphas

    def accumulate(j, probs, alphas):
        start = pl.multiple_of(j * tk, tk)
        for h, hs in enumerate(heads):
            v_ext = jnp.concatenate([v_ref[pl.ds(start, tk), hs], ones], axis=1)
            pv = jnp.dot(probs[h], v_ext, preferred_element_type=F32)
            acc_ref[h] = jnp.concatenate([alphas[h], alphas[h]], axis=1) * acc_ref[h] + pv

    def block(j, masked):
        accumulate(j, *softmax(scores(j, masked)))

    def block_pair(j, second_masked):
        first = softmax(scores(j, False))
        second_scores = scores(j + 1, second_masked)
        accumulate(j, *first)
        accumulate(j + 1, *softmax(second_scores))

    def full_pair(t, carry):
        block_pair(2 * t, False)
        return carry

    lax.fori_loop(0, qi >> 1, full_pair, 0)

    @pl.when((qi & 1) == 1)
    def _():
        block_pair(qi - 1, True)

    @pl.when((qi & 1) == 0)
    def _():
        block(qi, True)

    for h in range(hps):
        acc = acc_ref[h]
        o_ref[:, h * HEAD_DIM:(h + 1) * HEAD_DIM] = (
            acc[:, :HEAD_DIM] / acc[:, HEAD_DIM:]).astype(o_ref.dtype)


def _fox_attention(qkv, f_cum, batch, seq, *, q_col, k_col, v_col, tq=512, hps=2):
    T = qkv.shape[0]
    nq = seq // tq
    width = hps * HEAD_DIM
    f_blocks = f_cum.reshape(batch, HEADS, nq, tq)
    return pl.pallas_call(
        functools.partial(_fox_kernel, tq=tq, tk=tq, hps=hps),
        grid=(batch, HEADS // hps, nq),
        in_specs=[
            pl.BlockSpec((tq, width), lambda b, h, qi: (b * nq + qi, q_col + h)),
            pl.BlockSpec((seq, width), lambda b, h, qi: (b, k_col + h)),
            pl.BlockSpec((seq, width), lambda b, h, qi: (b, v_col + h)),
            pl.BlockSpec((None, hps, nq, tq), lambda b, h, qi: (b, h, 0, 0)),
        ],
        out_specs=pl.BlockSpec((tq, width), lambda b, h, qi: (b * nq + qi, h)),
        out_shape=jax.ShapeDtypeStruct((T, HEADS * HEAD_DIM), BF16),
        scratch_shapes=[pltpu.VMEM((hps, tq, LANES), F32),
                        pltpu.VMEM((hps, tq, 2 * HEAD_DIM), F32)],
        compiler_params=_params("parallel", "parallel", "arbitrary",
                                vmem_limit_bytes=VMEM_LIMIT_LARGE_BYTES),
    )(qkv, qkv, qkv, f_blocks)


def _bmm(a, b):
    return jnp.einsum("gmk,gkn->gmn", a.astype(BF16), b.astype(BF16), preferred_element_type=F32)


def _bmm_nt(a, b):
    return jnp.einsum("gmk,gnk->gmn", a.astype(BF16), b.astype(BF16), preferred_element_type=F32)


def _bmm_tn(a, b):
    return jnp.einsum("gkm,gkn->gmn", a.astype(BF16), b.astype(BF16), preferred_element_type=F32)


def _inverse_pairs(m, eye, r, c):
    return eye - jnp.where((r >> 1) == (c >> 1), m, 0.0)


def _inverse_double(x, m, r, c, b):
    shift = b.bit_length() - 1
    joins = ((r >> (shift + 1)) == (c >> (shift + 1))) & ((r >> shift) != (c >> shift))
    return x - _bmm(_bmm(x, jnp.where(joins, m, 0.0)), x)


STEP_CHUNKS = 4


def _chunk_head_stack(ref, width=HEAD_DIM):
    return jnp.stack([ref[ci * CHUNK:(ci + 1) * CHUNK, h * width:(h + 1) * width]
                      for ci in range(STEP_CHUNKS) for h in range(HEADS)])


def _chunk_head_cols(col, start):
    return jnp.stack([col[ci * CHUNK:(ci + 1) * CHUNK, start + h:start + h + 1]
                      for ci in range(STEP_CHUNKS) for h in range(HEADS)])


def _chunk_head_rows(row_ref):
    return jnp.concatenate([row_ref[ci] for ci in range(STEP_CHUNKS)], axis=0)[:, None, :]


def _store_heads(o_ref, y):
    for ci in range(STEP_CHUNKS):
        for h in range(HEADS):
            o_ref[ci * CHUNK:(ci + 1) * CHUNK, h * HEAD_DIM:(h + 1) * HEAD_DIM] = y[ci * HEADS + h]


def _chunk_mixers_kernel(gq_ref, gk_ref, gv_ref, z_ref, mq_ref, mk_ref, mv_ref, og_ref, col_ref,
                         rowg_ref, rowi_ref, rowb_ref, gng_ref, mng_ref, yg_ref, ym_ref,
                         s_ref, c_ref, m_ref):
    @pl.when(pl.program_id(1) == 0)
    def _():
        s_ref[...] = jnp.zeros_like(s_ref)
        c_ref[...] = jnp.zeros_like(c_ref)
        m_ref[...] = jnp.zeros_like(m_ref)

    L = CHUNK
    r = lax.broadcasted_iota(jnp.int32, (1, L, L), 1)
    c = lax.broadcasted_iota(jnp.int32, (1, L, L), 2)
    tri = c <= r
    eye = jnp.where(c == r, 1.0, 0.0).astype(F32)
    col = col_ref[...]

    q = _chunk_head_stack(gq_ref).astype(F32)
    k = _chunk_head_stack(gk_ref).astype(F32)
    v = _chunk_head_stack(gv_ref).astype(F32)
    qn = q * lax.rsqrt(jnp.sum(q * q, axis=-1, keepdims=True) + EPS) * (HEAD_DIM ** -0.5)
    kn = k * lax.rsqrt(jnp.sum(k * k, axis=-1, keepdims=True) + EPS)
    beta = _chunk_head_cols(col, SM_GDN_B)
    gc = _chunk_head_cols(col, SM_GDN_A)
    gr = _chunk_head_rows(rowg_ref)
    g_last = gc[:, L - 1:L, :]
    decay = jnp.where(tri, jnp.exp(jnp.where(tri, gc - gr, 0.0)), 0.0)
    eg = jnp.exp(gc)
    kb = kn * beta
    m = jnp.where(c < r, _bmm_nt(kb, kn) * decay, 0.0)
    t_inv = _inverse_pairs(m, eye, r, c)

    mq = _chunk_head_stack(mq_ref, ML_QK_DIM).astype(F32)
    mk = _chunk_head_stack(mk_ref, ML_QK_DIM).astype(F32)
    mv = _chunk_head_stack(mv_ref)
    mv = jnp.concatenate([mv, jnp.ones_like(mv)], axis=-1)
    ic = _chunk_head_cols(col, SM_ML_I)
    bc = _chunk_head_cols(col, SM_ML_F)
    ir = _chunk_head_rows(rowi_ref)
    br = _chunk_head_rows(rowb_ref)
    dm = jnp.where(tri, bc - br + ir, -jnp.inf)
    dmax = jnp.max(dm, axis=-1, keepdims=True)
    qk = _bmm_nt(mq, mk)

    t_inv = _inverse_double(t_inv, m, r, c, 2)
    t_inv = _inverse_double(t_inv, m, r, c, 4)

    m_state = m_ref[...]
    m_rows, m_lasts = [], []
    for ci in range(STEP_CHUNKS):
        hs = slice(ci * HEADS, (ci + 1) * HEADS)
        m_r = jnp.maximum(bc[hs] + m_state, dmax[hs])
        m_rows.append(m_r)
        m_state = m_r[:, L - 1:L, :]
        m_lasts.append(m_state)
    m_r = jnp.concatenate(m_rows, axis=0)
    m_last = jnp.concatenate(m_lasts, axis=0)
    m_prev = jnp.concatenate([m_ref[...]] + m_lasts[:-1], axis=0)
    m_ref[...] = m_state
    w_inter = jnp.exp(bc + m_prev - m_r)
    sm = jnp.exp(dm - m_r) * qk
    intra = _bmm(sm, mv)

    t_inv = _inverse_double(t_inv, m, r, c, 8)

    wk = jnp.exp(bc[:, L - 1:L, :] - bc + ic - m_last) * mk
    kv = _bmm_tn(wk, mv)
    w_last = w_inter[:, L - 1:L, :]
    floor = jnp.exp(-m_r)

    t_inv = _inverse_double(t_inv, m, r, c, 16)
    t_inv = _inverse_double(t_inv, m, r, c, 32)
    x = _bmm(t_inv, jnp.concatenate([v * beta, kb * eg], axis=-1))
    u = x[:, :, :HEAD_DIM]
    w = x[:, :, HEAD_DIM:]
    a_qk = _bmm_nt(qn, kn) * decay
    qg = qn * eg
    kg = kn * jnp.exp(g_last - gc)
    carry = jnp.exp(g_last)

    state = s_ref[...]
    c_state = c_ref[...]
    g_outs, m_outs = [], []
    for ci in range(STEP_CHUNKS):
        hs = slice(ci * HEADS, (ci + 1) * HEADS)
        v_new = u[hs] - _bmm(w[hs], state)
        num_den = w_inter[hs] * _bmm(mq[hs], c_state) + intra[hs]
        g_outs.append(_bmm(qg[hs], state) + _bmm(a_qk[hs], v_new))
        state = state * carry[hs] + _bmm_tn(kg[hs], v_new)
        den = jnp.maximum(jnp.abs(num_den[:, :, HEAD_DIM:]), floor[hs])
        m_outs.append(num_den[:, :, :HEAD_DIM] / den)
        c_state = w_last[hs] * c_state + kv[hs]
    s_ref[...] = state
    c_ref[...] = c_state

    o = jnp.concatenate(g_outs, axis=0)
    z = _chunk_head_stack(z_ref).astype(F32)
    _store_heads(yg_ref, (_rmsnorm(o, gng_ref[...]) * (z * _sigmoid(z))).astype(yg_ref.dtype))
    h_tilde = jnp.concatenate(m_outs, axis=0)
    og = _chunk_head_stack(og_ref).astype(F32)
    _store_heads(ym_ref, _rmsnorm(_sigmoid(og) * h_tilde, mng_ref[...]).astype(ym_ref.dtype))


def _chunk_mixers(gdn_qkv, proj, col_small, row_small, gdn_norm_g, ml_norm_g, batch, seq, *,
                  z_col, q_col, k_col, v_col, og_col):
    T = proj.shape[0]
    rows = STEP_CHUNKS * CHUNK
    ns = seq // rows
    qk_width = HEADS * ML_QK_DIM
    width = HEADS * HEAD_DIM
    row = lambda b, s: b * ns + s
    wide = lambda col: pl.BlockSpec((rows, width), lambda b, s: (row(b, s), col))
    rows_spec = lambda seg: pl.BlockSpec((STEP_CHUNKS, HEADS, CHUNK),
                                         lambda b, s: (row(b, s), seg // HEADS, 0))
    norm_spec = pl.BlockSpec((1, HEAD_DIM), lambda b, s: (0, 0))
    out_spec = pl.BlockSpec((rows, width), lambda b, s: (row(b, s), 0))
    out_shape = jax.ShapeDtypeStruct((T, width), BF16)
    return pl.pallas_call(
        _chunk_mixers_kernel,
        grid=(batch, ns),
        in_specs=[
            wide(0), wide(1), wide(2), wide(z_col),
            pl.BlockSpec((rows, qk_width), lambda b, s: (row(b, s), q_col)),
            pl.BlockSpec((rows, qk_width), lambda b, s: (row(b, s), k_col)),
            wide(v_col), wide(og_col),
            pl.BlockSpec((rows, LANES), lambda b, s: (row(b, s), 0)),
            rows_spec(SM_GDN_A), rows_spec(SM_ML_I), rows_spec(SM_ML_F),
            norm_spec, norm_spec,
        ],
        out_specs=[out_spec, out_spec],
        out_shape=[out_shape, out_shape],
        scratch_shapes=[pltpu.VMEM((HEADS, HEAD_DIM, HEAD_DIM), F32),
                        pltpu.VMEM((HEADS, ML_QK_DIM, 2 * HEAD_DIM), F32),
                        pltpu.VMEM((HEADS, 1, 1), F32)],
        compiler_params=_params("parallel", "arbitrary"),
    )(gdn_qkv, gdn_qkv, gdn_qkv, proj, proj, proj, proj, proj, col_small,
      row_small, row_small, row_small, gdn_norm_g, ml_norm_g)


def _merge_kernel(yf_ref, yg_ref, ym_ref, wf_ref, wg_ref, wm_ref, g0_ref, g1_ref, g2_ref,
                  b0_ref, b1_ref, b2_ref, o_ref):
    def branch(y_ref, w_ref, g_ref, b_ref):
        return _sigmoid(g_ref[...] + b_ref[...]) * jnp.dot(y_ref[...], w_ref[...],
                                                          preferred_element_type=F32)

    y = (branch(yf_ref, wf_ref, g0_ref, b0_ref) + branch(yg_ref, wg_ref, g1_ref, b1_ref)
         + branch(ym_ref, wm_ref, g2_ref, b2_ref))
    o_ref[...] = y.astype(o_ref.dtype)


def _merge(y_fox, y_gdn, y_ml, w_fox, w_gdn, w_ml, layer, gate_src, gate_bias, d_model, *,
           gate_col, tm=1024, tn=512):
    T, K = y_fox.shape
    nj = d_model // tn
    y_spec = pl.BlockSpec((tm, K), lambda i, j: (i, 0))
    w_spec = pl.BlockSpec((None, K, tn), lambda i, j: (layer, 0, j))
    gate_specs = [pl.BlockSpec((tm, tn), lambda i, j, n=n: (i, gate_col + n * nj + j))
                  for n in range(N_BRANCH)]
    bias_specs = [pl.BlockSpec((1, tn), lambda i, j, n=n: (0, n * nj + j)) for n in range(N_BRANCH)]
    return pl.pallas_call(
        _merge_kernel,
        grid=(T // tm, nj),
        in_specs=[y_spec, y_spec, y_spec, w_spec, w_spec, w_spec] + gate_specs + bias_specs,
        out_specs=pl.BlockSpec((tm, tn), lambda i, j: (i, j)),
        out_shape=jax.ShapeDtypeStruct((T, d_model), BF16),
        compiler_params=_params("parallel", "parallel"),
    )(y_fox, y_gdn, y_ml, w_fox, w_gdn, w_ml, gate_src, gate_src, gate_src,
      gate_bias, gate_bias, gate_bias)


def _matmul_residual_kernel(a_ref, w_ref, x_ref, o_ref):
    o_ref[...] = x_ref[...] + jnp.dot(a_ref[...], w_ref[...], preferred_element_type=F32)


def _matmul_residual(a, w_all, layer, x, tm=1024, tn=512):
    T, K = a.shape
    N = w_all.shape[2]
    return pl.pallas_call(
        _matmul_residual_kernel,
        grid=(T // tm, N // tn),
        in_specs=[
            pl.BlockSpec((tm, K), lambda i, j: (i, 0)),
            pl.BlockSpec((None, K, tn), lambda i, j: (layer, 0, j)),
            pl.BlockSpec((tm, tn), lambda i, j: (i, j)),
        ],
        out_specs=pl.BlockSpec((tm, tn), lambda i, j: (i, j)),
        out_shape=jax.ShapeDtypeStruct((T, N), F32),
        compiler_params=_params("parallel", "parallel"),
    )(a, w_all, x)


def _final_norm_kernel(x_ref, g_ref, o_ref):
    o_ref[...] = _rmsnorm(x_ref[...], g_ref[...])


def _final_norm(x, g, tm=512):
    T, D = x.shape
    return pl.pallas_call(
        _final_norm_kernel,
        grid=(T // tm,),
        in_specs=[pl.BlockSpec((tm, D), lambda i: (i, 0)), pl.BlockSpec((1, D), lambda i: (0, 0))],
        out_specs=pl.BlockSpec((tm, D), lambda i: (i, 0)),
        out_shape=jax.ShapeDtypeStruct((T, D), F32),
        compiler_params=_params("parallel"),
    )(x, g)


def _ffn_kernel(x_ref, xh_ref, g_ref, wu_ref, wg_ref, cu_ref, cg_ref, wd_ref, o_ref, xn_ref, *,
                blocks_per_seq):
    @pl.when(pl.program_id(1) == 0)
    def _():
        _norm_with_halo(x_ref, xh_ref, g_ref, xn_ref, pl.program_id(0) % blocks_per_seq == 0)
        o_ref[...] = x_ref[...]

    groups = _column_groups(wu_ref.shape[1])
    half = x_ref.shape[0] // 2

    def up(rows):
        xn = xn_ref[rows, :]
        return [(jnp.dot(xn, wu_ref[:, cs], preferred_element_type=F32),
                 jnp.dot(xn, wg_ref[:, cs], preferred_element_type=F32)) for cs in groups]

    def gate(hidden):
        acts = []
        for cs, (hu, hg) in zip(groups, hidden):
            gt = _conv_taps(hg, cg_ref, cs)
            acts.append((gt * _sigmoid(gt) * _conv_taps(hu, cu_ref, cs)).astype(BF16))
        return jnp.concatenate(acts, axis=1)

    def down(rows, act):
        o_ref[rows, :] += jnp.dot(act, wd_ref[...], preferred_element_type=F32)

    act_a = gate(up(slice(0, half + HALO)))
    hidden_b = up(slice(half, 2 * half + HALO))
    down(slice(0, half), act_a)
    down(slice(half, 2 * half), gate(hidden_b))


def _ffn(x, g, w_up, conv_w, w_down, layer, seq, tm=1024, tn=512):
    T, D = x.shape
    taps = conv_w.shape[0]
    nj = w_down.shape[1] // tn
    halo_blocks = tm // HALO
    once = pl.Buffered(1)
    return pl.pallas_call(
        functools.partial(_ffn_kernel, blocks_per_seq=seq // tm),
        grid=(T // tm, nj),
        in_specs=[
            pl.BlockSpec((tm, D), lambda i, j: (i, 0)),
            pl.BlockSpec((HALO, D), lambda i, j: (jnp.maximum(i * halo_blocks - 1, 0), 0)),
            pl.BlockSpec((1, D), lambda i, j: (0, 0)),
            pl.BlockSpec((None, D, tn), lambda i, j: (layer, 0, j)),
            pl.BlockSpec((None, D, tn), lambda i, j: (layer, 0, nj + j)),
            pl.BlockSpec((taps, tn), lambda i, j: (0, j)),
            pl.BlockSpec((taps, tn), lambda i, j: (0, nj + j)),
            pl.BlockSpec((None, tn, D), lambda i, j: (layer, j, 0)),
        ],
        out_specs=pl.BlockSpec((tm, D), lambda i, j: (i, 0), pipeline_mode=once),
        out_shape=jax.ShapeDtypeStruct((T, D), F32),
        scratch_shapes=[pltpu.VMEM((tm + HALO, D), BF16)],
        compiler_params=_params("parallel", "arbitrary", vmem_limit_bytes=VMEM_LIMIT_LARGE_BYTES),
    )(x, x, g, w_up, w_up, conv_w, conv_w, w_down)


FOX_W = HEADS * HEAD_DIM
GDN_W = HEADS * HEAD_DIM
ML_QK_W = HEADS * ML_QK_DIM
ML_V_W = HEADS * HEAD_DIM
ML_BASE = 3 * FOX_W
Z_BASE = ML_BASE + 2 * ML_QK_W + ML_V_W
GATE_BASE = Z_BASE + GDN_W + ML_V_W
FOX_HEADS_PER_STEP = 4


def _in_projection_columns(d_model):
    widths = [("fox_q", FOX_W), ("fox_k", FOX_W), ("fox_v", FOX_W), ("fox_f", HEADS),
              ("gdn_qkv", 3 * GDN_W), ("gdn_z", GDN_W), ("gdn_b", HEADS), ("gdn_a", HEADS),
              ("ml_q", ML_QK_W), ("ml_k", ML_QK_W), ("ml_v", ML_V_W), ("ml_i", HEADS),
              ("ml_f", HEADS), ("ml_o", ML_V_W), ("gate", N_BRANCH * d_model)]
    off, start = {}, 0
    for name, width in widths:
        off[name] = (start, start + width)
        start += width
    return off


def _pad_lanes(v):
    return jnp.pad(v, (0, LANES - v.shape[0])).reshape(1, LANES)


def _layer(x, layer, w, p, batch, seq):
    T, D = x.shape
    g_mix = p["norm_mix_g"].reshape(1, D)
    scale_p = jnp.concatenate([
        jnp.full((FOX_W,), HEAD_DIM ** -0.5 * LOG2E, F32), jnp.ones((2 * FOX_W,), F32),
        jnp.full((ML_QK_W,), ML_QK_DIM ** -0.5, F32),
        jnp.ones((w["proj"].shape[2] - 3 * FOX_W - ML_QK_W,), F32)]).reshape(1, -1)
    proj = _norm_matmul(x, g_mix, w["proj"], layer, scale_p, BF16, tn=1024)
    zeros8 = jnp.zeros((HEADS,), F32)
    bias_s = _pad_lanes(jnp.concatenate([p["fox_f_bias"], zeros8, p["gdn_dt_bias"],
                                         p["ml_i_bias"], p["ml_f_bias"]]))
    alog_s = _pad_lanes(jnp.concatenate([zeros8, zeros8, p["gdn_a_log"]]))
    col_small, row_small = _small_gates(x, g_mix, w["small"], layer, bias_s, alog_s, seq)
    f_cum = row_small[:, SM_FOX_F:SM_FOX_F + HEADS, :].reshape(batch, seq // CHUNK, HEADS, CHUNK)
    f_cum = f_cum.transpose(0, 2, 1, 3).reshape(batch, HEADS, seq)
    gdn_qkv = _proj_conv(x, g_mix, w["gdn_qkv"], layer, p["gdn_conv_w"], seq)

    hps = FOX_HEADS_PER_STEP
    y_fox = _fox_attention(proj, f_cum, batch, seq, q_col=0, k_col=HEADS // hps,
                           v_col=2 * HEADS // hps, hps=hps)
    y_gdn, y_ml = _chunk_mixers(
        gdn_qkv, proj, col_small, row_small, p["gdn_norm_g"].reshape(1, HEAD_DIM),
        p["ml_norm_g"].reshape(1, HEAD_DIM), batch, seq, z_col=Z_BASE // GDN_W,
        q_col=ML_BASE // ML_QK_W, k_col=ML_BASE // ML_QK_W + 1,
        v_col=(ML_BASE + 2 * ML_QK_W) // ML_V_W, og_col=(Z_BASE + GDN_W) // ML_V_W)

    tn = 512
    y = _merge(y_fox, y_gdn, y_ml, w["fox_proj"], w["gdn_proj"], w["ml_proj"], layer, proj,
               p["gate_bias"].reshape(1, -1), D, gate_col=GATE_BASE // tn, tn=tn)
    x = _matmul_residual(y, w["out"], layer, x)
    return _ffn(x, p["norm_ffn_g"].reshape(1, D), w["up"], p["ffn_conv_w"], w["down"], layer, seq)


def kernel(x, norm_mix_g, w_in, fox_f_bias, gdn_conv_w, gdn_a_log, gdn_dt_bias, gdn_norm_g,
           ml_i_bias, ml_f_bias, ml_norm_g, gate_bias, w_fox_proj, w_gdn_proj, w_ml_proj,
           w_out, norm_ffn_g, w_up, ffn_conv_w, w_down, norm_final_g):
    batch, seq, d_model = x.shape
    depth = w_in.shape[0]
    off = _in_projection_columns(d_model)
    cols = lambda a, b: w_in[:, :, off[a][0]:off[b][1]].astype(BF16)
    w_small = jnp.concatenate([cols("fox_f", "fox_f"), cols("gdn_b", "gdn_a"), cols("ml_i", "ml_f")],
                              axis=2)
    w = dict(
        proj=jnp.concatenate([cols("fox_q", "fox_v"), cols("ml_q", "ml_v"), cols("gdn_z", "gdn_z"),
                              cols("ml_o", "ml_o"), cols("gate", "gate")], axis=2),
        small=jnp.pad(w_small, ((0, 0), (0, 0), (0, LANES - w_small.shape[2]))),
        gdn_qkv=cols("gdn_qkv", "gdn_qkv"),
        fox_proj=w_fox_proj.astype(BF16), gdn_proj=w_gdn_proj.astype(BF16),
        ml_proj=w_ml_proj.astype(BF16), out=w_out.astype(BF16), up=w_up.astype(BF16),
        down=w_down.astype(BF16))
    small = dict(norm_mix_g=norm_mix_g, fox_f_bias=fox_f_bias, gdn_conv_w=gdn_conv_w,
                 gdn_a_log=gdn_a_log, gdn_dt_bias=gdn_dt_bias, gdn_norm_g=gdn_norm_g,
                 ml_i_bias=ml_i_bias, ml_f_bias=ml_f_bias, ml_norm_g=ml_norm_g,
                 gate_bias=gate_bias, norm_ffn_g=norm_ffn_g, ffn_conv_w=ffn_conv_w)
    h = x.reshape(batch * seq, d_model)
    for layer in range(depth):
        h = _layer(h, layer, w, {k: v[layer] for k, v in small.items()}, batch, seq)
    return _final_norm(h, norm_final_g.reshape(1, d_model)).reshape(batch, seq, d_model)
```

```python
import functools

import jax
import jax.numpy as jnp
from jax import lax
from jax.experimental import pallas as pl
from jax.experimental.pallas import tpu as pltpu

F32 = jnp.float32
BF16 = jnp.bfloat16
HIGHEST = lax.Precision.HIGHEST

EPS = 1e-6
HEADS = 8
HEAD_DIM = 128
ML_QK_DIM = 64
CHUNK = 64
N_BRANCH = 3
LANES = 128
MXU_WIDTH = 256
HALO = 16

SM_FOX_F, SM_GDN_B, SM_GDN_A, SM_ML_I, SM_ML_F = 0, 8, 16, 24, 32

V7X_VMEM_BYTES = 64 * 1024 * 1024
VMEM_LIMIT_BYTES = V7X_VMEM_BYTES * 3 // 4
VMEM_LIMIT_LARGE_BYTES = V7X_VMEM_BYTES * 7 // 8


def _params(*semantics, vmem_limit_bytes=VMEM_LIMIT_BYTES):
    return pltpu.CompilerParams(dimension_semantics=semantics, vmem_limit_bytes=vmem_limit_bytes)


def _rmsnorm(x, g):
    return x * lax.rsqrt(jnp.mean(x * x, axis=-1, keepdims=True) + EPS) * g


def _sigmoid(x):
    return 1.0 / (1.0 + jnp.exp(-x))


def _mm_hi(a, b):
    return jnp.dot(a, b, precision=HIGHEST, preferred_element_type=F32)


def _norm_matmul_kernel(x_ref, g_ref, w_ref, cs_ref, o_ref, xn_ref):
    def tile(xn):
        acc = jnp.dot(xn, w_ref[...], preferred_element_type=F32)
        o_ref[...] = (acc * cs_ref[...]).astype(o_ref.dtype)

    @pl.when(pl.program_id(1) == 0)
    def _():
        xn = _rmsnorm(x_ref[...], g_ref[...]).astype(BF16)
        xn_ref[...] = xn
        tile(xn)

    @pl.when(pl.program_id(1) != 0)
    def _():
        tile(xn_ref[...])


def _norm_matmul(x, g, w_all, layer, colscale, out_dtype, tm=1024, tn=512):
    T, D = x.shape
    N = w_all.shape[2]
    return pl.pallas_call(
        _norm_matmul_kernel,
        grid=(T // tm, N // tn),
        in_specs=[
            pl.BlockSpec((tm, D), lambda i, j: (i, 0)),
            pl.BlockSpec((1, D), lambda i, j: (0, 0)),
            pl.BlockSpec((None, D, tn), lambda i, j: (layer, 0, j)),
            pl.BlockSpec((1, tn), lambda i, j: (0, j)),
        ],
        out_specs=pl.BlockSpec((tm, tn), lambda i, j: (i, j)),
        out_shape=jax.ShapeDtypeStruct((T, N), out_dtype),
        scratch_shapes=[pltpu.VMEM((tm, D), BF16)],
        compiler_params=_params("parallel", "arbitrary"),
    )(x, g, w_all, colscale)


def _small_kernel(x_ref, g_ref, w_ref, bias_ref, alog_ref, o_ref, rows_ref, carry_ref, *,
                  blocks_per_seq):
    i = pl.program_id(0)
    tm = x_ref.shape[0]

    @pl.when(i % blocks_per_seq == 0)
    def _():
        carry_ref[...] = jnp.zeros_like(carry_ref)

    xn = _rmsnorm(x_ref[...], g_ref[...]).astype(BF16)
    t = jnp.dot(xn, w_ref[...], preferred_element_type=F32) + bias_ref[...]
    lane = lax.broadcasted_iota(jnp.int32, (tm, LANES), 1)
    e = jnp.log1p(jnp.exp(-jnp.abs(t)))
    logsig = jnp.minimum(t, 0.0) - e
    softplus = jnp.maximum(t, 0.0) + e
    g_decay = -jnp.exp(alog_ref[...]) * softplus
    is_logsig = (lane < SM_GDN_B) | ((lane >= SM_ML_F) & (lane < SM_ML_F + HEADS))
    val = jnp.where(is_logsig, logsig,
                    jnp.where(lane < SM_GDN_A, _sigmoid(t),
                              jnp.where(lane < SM_ML_I, g_decay, t)))

    r = lax.broadcasted_iota(jnp.int32, (CHUNK, CHUNK), 0)
    c = lax.broadcasted_iota(jnp.int32, (CHUNK, CHUNK), 1)
    tri = jnp.where(c <= r, 1.0, 0.0).astype(F32)
    chunk_sums = [_mm_hi(tri, val[ci * CHUNK:(ci + 1) * CHUNK, :]) for ci in range(tm // CHUNK)]
    offset = carry_ref[...]
    full_sums = []
    for cs in chunk_sums:
        full_sums.append(cs + offset)
        offset = offset + cs[CHUNK - 1:CHUNK, :]
    carry_ref[...] = offset
    cs_chunk = jnp.concatenate(chunk_sums, axis=0)
    cs_full = jnp.concatenate(full_sums, axis=0)

    is_chunk_cs = ((lane >= SM_GDN_A) & (lane < SM_ML_I)) | ((lane >= SM_ML_F) & (lane < SM_ML_F + HEADS))
    out = jnp.where(lane < SM_GDN_B, cs_full, jnp.where(is_chunk_cs, cs_chunk, val))
    o_ref[...] = out
    for ci in range(tm // CHUNK):
        rows_ref[ci] = out[ci * CHUNK:(ci + 1) * CHUNK, :].T


def _small_gates(x, g, w_all, layer, bias, alog, seq, tm=512):
    T, D = x.shape
    return pl.pallas_call(
        functools.partial(_small_kernel, blocks_per_seq=seq // tm),
        grid=(T // tm,),
        in_specs=[
            pl.BlockSpec((tm, D), lambda i: (i, 0)),
            pl.BlockSpec((1, D), lambda i: (0, 0)),
            pl.BlockSpec((None, D, LANES), lambda i: (layer, 0, 0)),
            pl.BlockSpec((1, LANES), lambda i: (0, 0)),
            pl.BlockSpec((1, LANES), lambda i: (0, 0)),
        ],
        out_specs=[pl.BlockSpec((tm, LANES), lambda i: (i, 0)),
                   pl.BlockSpec((tm // CHUNK, LANES, CHUNK), lambda i: (i, 0, 0))],
        out_shape=[jax.ShapeDtypeStruct((T, LANES), F32),
                   jax.ShapeDtypeStruct((T // CHUNK, LANES, CHUNK), F32)],
        scratch_shapes=[pltpu.VMEM((1, LANES), F32)],
        compiler_params=_params("arbitrary"),
    )(x, g, w_all, bias, alog)


def _conv_taps(h, c_ref, cols):
    taps = c_ref.shape[0]
    tm = h.shape[0] - HALO
    acc = None
    for t in range(taps):
        start = HALO - (taps - 1) + t
        term = h[start:start + tm, :] * c_ref[t:t + 1, cols]
        acc = term if acc is None else acc + term
    return acc


def _column_groups(width):
    return [slice(c * MXU_WIDTH, (c + 1) * MXU_WIDTH) for c in range(width // MXU_WIDTH)]


def _norm_with_halo(x_ref, xh_ref, g_ref, xn_ref, at_seq_start):
    keep = jnp.where(at_seq_start, 0.0, 1.0)
    xn = jnp.concatenate([(_rmsnorm(xh_ref[...], g_ref[...]) * keep).astype(BF16),
                          _rmsnorm(x_ref[...], g_ref[...]).astype(BF16)], axis=0)
    xn_ref[...] = xn
    return xn


def _proj_conv_kernel(x_ref, xh_ref, g_ref, w_ref, c_ref, o_ref, xn_ref, *, blocks_per_seq):
    groups = _column_groups(w_ref.shape[1])

    def tile(xn):
        hidden = [jnp.dot(xn, w_ref[:, cs], preferred_element_type=F32) for cs in groups]
        for cs, h in zip(groups, hidden):
            u = _conv_taps(h, c_ref, cs)
            o_ref[:, cs] = (u * _sigmoid(u)).astype(o_ref.dtype)

    @pl.when(pl.program_id(1) == 0)
    def _():
        tile(_norm_with_halo(x_ref, xh_ref, g_ref, xn_ref,
                             pl.program_id(0) % blocks_per_seq == 0))

    @pl.when(pl.program_id(1) != 0)
    def _():
        tile(xn_ref[...])


def _proj_conv(x, g, w_all, layer, conv_w, seq, tm=1024, tn=1024):
    T, D = x.shape
    taps = conv_w.shape[0]
    N = w_all.shape[2]
    halo_blocks = tm // HALO
    return pl.pallas_call(
        functools.partial(_proj_conv_kernel, blocks_per_seq=seq // tm),
        grid=(T // tm, N // tn),
        in_specs=[
            pl.BlockSpec((tm, D), lambda i, j: (i, 0)),
            pl.BlockSpec((HALO, D), lambda i, j: (jnp.maximum(i * halo_blocks - 1, 0), 0)),
            pl.BlockSpec((1, D), lambda i, j: (0, 0)),
            pl.BlockSpec((None, D, tn), lambda i, j: (layer, 0, j)),
            pl.BlockSpec((taps, tn), lambda i, j: (0, j)),
        ],
        out_specs=pl.BlockSpec((tm, tn), lambda i, j: (i, j)),
        out_shape=jax.ShapeDtypeStruct((T, N), BF16),
        scratch_shapes=[pltpu.VMEM((tm + HALO, D), BF16)],
        compiler_params=_params("parallel", "arbitrary"),
    )(x, x, g, w_all, conv_w)


NEG_BIG = -1e30
LOG2E = 1.4426950408889634


def _fox_kernel(q_ref, k_ref, v_ref, f_ref, o_ref, m_ref, acc_ref, *, tq, tk, hps):
    qi = pl.program_id(2)
    ones = jnp.ones((tk, HEAD_DIM), BF16)
    lane_tiles = tk // LANES
    m_ref[...] = jnp.full_like(m_ref, NEG_BIG)
    acc_ref[...] = jnp.zeros_like(acc_ref)

    heads = [slice(h * HEAD_DIM, (h + 1) * HEAD_DIM) for h in range(hps)]

    def scores(j, masked):
        start = pl.multiple_of(j * tk, tk)
        out = []
        for h, hs in enumerate(heads):
            s = lax.dot_general(q_ref[:, hs], k_ref[pl.ds(start, tk), hs],
                                (((1,), (1,)), ((), ())), preferred_element_type=F32)
            s = s - f_ref[h, pl.ds(j, 1), :] * LOG2E
            if masked:
                row = lax.broadcasted_iota(jnp.int32, (tq, tk), 0)
                col = lax.broadcasted_iota(jnp.int32, (tq, tk), 1)
                s = jnp.where(col <= row, s, NEG_BIG)
            out.append([s[:, c * LANES:(c + 1) * LANES] for c in range(lane_tiles)])
        return out

    def softmax(score_tiles):
        probs, alphas = [], []
        for h, tiles in enumerate(score_tiles):
            m_prev = m_ref[h]
            m_tile = functools.reduce(jnp.maximum, tiles)
            m_new = jnp.maximum(m_prev, jnp.max(m_tile, axis=-1, keepdims=True))
            m_ref[h] = m_new
            alphas.append(jnp.exp2(m_prev - m_new))
            probs.append(jnp.concatenate([jnp.exp2(t - m_new).astype(BF16) for t in tiles], axis=1))
        return probs, alphas

    def accumulate(j, probs, alphas):
        start = pl.multiple_of(j * tk, tk)
        for h, hs in enumerate(heads):
            v_ext = jnp.concatenate([v_ref[pl.ds(start, tk), hs], ones], axis=1)
            pv = jnp.dot(probs[h], v_ext, preferred_element_type=F32)
            acc_ref[h] = jnp.concatenate([alphas[h], alphas[h]], axis=1) * acc_ref[h] + pv

    def block(j, masked):
        accumulate(j, *softmax(scores(j, masked)))

    def block_pair(j, second_masked):
        first = softmax(scores(j, False))
        second_scores = scores(j + 1, second_masked)
        accumulate(j, *first)
        accumulate(j + 1, *softmax(second_scores))

    def full_pair(t, carry):
        block_pair(2 * t, False)
        return carry

    lax.fori_loop(0, qi >> 1, full_pair, 0)

    @pl.when((qi & 1) == 1)
    def _():
        block_pair(qi - 1, True)

    @pl.when((qi & 1) == 0)
    def _():
        block(qi, True)

    for h in range(hps):
        acc = acc_ref[h]
        o_ref[:, h * HEAD_DIM:(h + 1) * HEAD_DIM] = (
            acc[:, :HEAD_DIM] / acc[:, HEAD_DIM:]).astype(o_ref.dtype)


def _fox_attention(qkv, f_cum, batch, seq, *, q_col, k_col, v_col, tq=512, hps=2):
    T = qkv.shape[0]
    nq = seq // tq
    width = hps * HEAD_DIM
    f_blocks = f_cum.reshape(batch, HEADS, nq, tq)
    return pl.pallas_call(
        functools.partial(_fox_kernel, tq=tq, tk=tq, hps=hps),
        grid=(batch, HEADS // hps, nq),
        in_specs=[
            pl.BlockSpec((tq, width), lambda b, h, qi: (b * nq + qi, q_col + h)),
            pl.BlockSpec((seq, width), lambda b, h, qi: (b, k_col + h)),
            pl.BlockSpec((seq, width), lambda b, h, qi: (b, v_col + h)),
            pl.BlockSpec((None, hps, nq, tq), lambda b, h, qi: (b, h, 0, 0)),
        ],
        out_specs=pl.BlockSpec((tq, width), lambda b, h, qi: (b * nq + qi, h)),
        out_shape=jax.ShapeDtypeStruct((T, HEADS * HEAD_DIM), BF16),
        scratch_shapes=[pltpu.VMEM((hps, tq, LANES), F32),
                        pltpu.VMEM((hps, tq, 2 * HEAD_DIM), F32)],
        compiler_params=_params("parallel", "parallel", "arbitrary",
                                vmem_limit_bytes=VMEM_LIMIT_LARGE_BYTES),
    )(qkv, qkv, qkv, f_blocks)


def _bmm(a, b):
    return jnp.einsum("gmk,gkn->gmn", a.astype(BF16), b.astype(BF16), preferred_element_type=F32)


def _bmm_nt(a, b):
    return jnp.einsum("gmk,gnk->gmn", a.astype(BF16), b.astype(BF16), preferred_element_type=F32)


def _bmm_tn(a, b):
    return jnp.einsum("gkm,gkn->gmn", a.astype(BF16), b.astype(BF16), preferred_element_type=F32)


def _inverse_pairs(m, eye, r, c):
    return eye - jnp.where((r >> 1) == (c >> 1), m, 0.0)


def _inverse_double(x, m, r, c, b):
    shift = b.bit_length() - 1
    joins = ((r >> (shift + 1)) == (c >> (shift + 1))) & ((r >> shift) != (c >> shift))
    return x - _bmm(_bmm(x, jnp.where(joins, m, 0.0)), x)


STEP_CHUNKS = 4


def _chunk_head_stack(ref, width=HEAD_DIM):
    return jnp.stack([ref[ci * CHUNK:(ci + 1) * CHUNK, h * width:(h + 1) * width]
                      for ci in range(STEP_CHUNKS) for h in range(HEADS)])


def _chunk_head_cols(col, start):
    return jnp.stack([col[ci * CHUNK:(ci + 1) * CHUNK, start + h:start + h + 1]
                      for ci in range(STEP_CHUNKS) for h in range(HEADS)])


def _chunk_head_rows(row_ref):
    return jnp.concatenate([row_ref[ci] for ci in range(STEP_CHUNKS)], axis=0)[:, None, :]


def _store_heads(o_ref, y):
    for ci in range(STEP_CHUNKS):
        for h in range(HEADS):
            o_ref[ci * CHUNK:(ci + 1) * CHUNK, h * HEAD_DIM:(h + 1) * HEAD_DIM] = y[ci * HEADS + h]


def _chunk_mixers_kernel(gq_ref, gk_ref, gv_ref, z_ref, mq_ref, mk_ref, mv_ref, og_ref, col_ref,
                         rowg_ref, rowi_ref, rowb_ref, gng_ref, mng_ref, yg_ref, ym_ref,
                         s_ref, c_ref, m_ref):
    @pl.when(pl.program_id(1) == 0)
    def _():
        s_ref[...] = jnp.zeros_like(s_ref)
        c_ref[...] = jnp.zeros_like(c_ref)
        m_ref[...] = jnp.zeros_like(m_ref)

    L = CHUNK
    r = lax.broadcasted_iota(jnp.int32, (1, L, L), 1)
    c = lax.broadcasted_iota(jnp.int32, (1, L, L), 2)
    tri = c <= r
    eye = jnp.where(c == r, 1.0, 0.0).astype(F32)
    col = col_ref[...]

    q = _chunk_head_stack(gq_ref).astype(F32)
    k = _chunk_head_stack(gk_ref).astype(F32)
    v = _chunk_head_stack(gv_ref).astype(F32)
    qn = q * lax.rsqrt(jnp.sum(q * q, axis=-1, keepdims=True) + EPS) * (HEAD_DIM ** -0.5)
    kn = k * lax.rsqrt(jnp.sum(k * k, axis=-1, keepdims=True) + EPS)
    beta = _chunk_head_cols(col, SM_GDN_B)
    gc = _chunk_head_cols(col, SM_GDN_A)
    gr = _chunk_head_rows(rowg_ref)
    g_last = gc[:, L - 1:L, :]
    decay = jnp.where(tri, jnp.exp(jnp.where(tri, gc - gr, 0.0)), 0.0)
    eg = jnp.exp(gc)
    kb = kn * beta
    m = jnp.where(c < r, _bmm_nt(kb, kn) * decay, 0.0)
    t_inv = _inverse_pairs(m, eye, r, c)

    mq = _chunk_head_stack(mq_ref, ML_QK_DIM).astype(F32)
    mk = _chunk_head_stack(mk_ref, ML_QK_DIM).astype(F32)
    mv = _chunk_head_stack(mv_ref)
    mv = jnp.concatenate([mv, jnp.ones_like(mv)], axis=-1)
    ic = _chunk_head_cols(col, SM_ML_I)
    bc = _chunk_head_cols(col, SM_ML_F)
    ir = _chunk_head_rows(rowi_ref)
    br = _chunk_head_rows(rowb_ref)
    dm = jnp.where(tri, bc - br + ir, -jnp.inf)
    dmax = jnp.max(dm, axis=-1, keepdims=True)
    qk = _bmm_nt(mq, mk)

    t_inv = _inverse_double(t_inv, m, r, c, 2)
    t_inv = _inverse_double(t_inv, m, r, c, 4)

    m_state = m_ref[...]
    m_rows, m_lasts = [], []
    for ci in range(STEP_CHUNKS):
        hs = slice(ci * HEADS, (ci + 1) * HEADS)
        m_r = jnp.maximum(bc[hs] + m_state, dmax[hs])
        m_rows.append(m_r)
        m_state = m_r[:, L - 1:L, :]
        m_lasts.append(m_state)
    m_r = jnp.concatenate(m_rows, axis=0)
    m_last = jnp.concatenate(m_lasts, axis=0)
    m_prev = jnp.concatenate([m_ref[...]] + m_lasts[:-1], axis=0)
    m_ref[...] = m_state
    w_inter = jnp.exp(bc + m_prev - m_r)
    sm = jnp.exp(dm - m_r) * qk
    intra = _bmm(sm, mv)

    t_inv = _inverse_double(t_inv, m, r, c, 8)

    wk = jnp.exp(bc[:, L - 1:L, :] - bc + ic - m_last) * mk
    kv = _bmm_tn(wk, mv)
    w_last = w_inter[:, L - 1:L, :]
    floor = jnp.exp(-m_r)

    t_inv = _inverse_double(t_inv, m, r, c, 16)
    t_inv = _inverse_double(t_inv, m, r, c, 32)
    x = _bmm(t_inv, jnp.concatenate([v * beta, kb * eg], axis=-1))
    u = x[:, :, :HEAD_DIM]
    w = x[:, :, HEAD_DIM:]
    a_qk = _bmm_nt(qn, kn) * decay
    qg = qn * eg
    kg = kn * jnp.exp(g_last - gc)
    carry = jnp.exp(g_last)

    state = s_ref[...]
    c_state = c_ref[...]
    g_outs, m_outs = [], []
    for ci in range(STEP_CHUNKS):
        hs = slice(ci * HEADS, (ci + 1) * HEADS)
        v_new = u[hs] - _bmm(w[hs], state)
        num_den = w_inter[hs] * _bmm(mq[hs], c_state) + intra[hs]
        g_outs.append(_bmm(qg[hs], state) + _bmm(a_qk[hs], v_new))
        state = state * carry[hs] + _bmm_tn(kg[hs], v_new)
        den = jnp.maximum(jnp.abs(num_den[:, :, HEAD_DIM:]), floor[hs])
        m_outs.append(num_den[:, :, :HEAD_DIM] / den)
        c_state = w_last[hs] * c_state + kv[hs]
    s_ref[...] = state
    c_ref[...] = c_state

    o = jnp.concatenate(g_outs, axis=0)
    z = _chunk_head_stack(z_ref).astype(F32)
    _store_heads(yg_ref, (_rmsnorm(o, gng_ref[...]) * (z * _sigmoid(z))).astype(yg_ref.dtype))
    h_tilde = jnp.concatenate(m_outs, axis=0)
    og = _chunk_head_stack(og_ref).astype(F32)
    _store_heads(ym_ref, _rmsnorm(_sigmoid(og) * h_tilde, mng_ref[...]).astype(ym_ref.dtype))


def _chunk_mixers(gdn_qkv, proj, col_small, row_small, gdn_norm_g, ml_norm_g, batch, seq, *,
                  z_col, q_col, k_col, v_col, og_col):
    T = proj.shape[0]
    rows = STEP_CHUNKS * CHUNK
    ns = seq // rows
    qk_width = HEADS * ML_QK_DIM
    width = HEADS * HEAD_DIM
    row = lambda b, s: b * ns + s
    wide = lambda col: pl.BlockSpec((rows, width), lambda b, s: (row(b, s), col))
    rows_spec = lambda seg: pl.BlockSpec((STEP_CHUNKS, HEADS, CHUNK),
                                         lambda b, s: (row(b, s), seg // HEADS, 0))
    norm_spec = pl.BlockSpec((1, HEAD_DIM), lambda b, s: (0, 0))
    out_spec = pl.BlockSpec((rows, width), lambda b, s: (row(b, s), 0))
    out_shape = jax.ShapeDtypeStruct((T, width), BF16)
    return pl.pallas_call(
        _chunk_mixers_kernel,
        grid=(batch, ns),
        in_specs=[
            wide(0), wide(1), wide(2), wide(z_col),
            pl.BlockSpec((rows, qk_width), lambda b, s: (row(b, s), q_col)),
            pl.BlockSpec((rows, qk_width), lambda b, s: (row(b, s), k_col)),
            wide(v_col), wide(og_col),
            pl.BlockSpec((rows, LANES), lambda b, s: (row(b, s), 0)),
            rows_spec(SM_GDN_A), rows_spec(SM_ML_I), rows_spec(SM_ML_F),
            norm_spec, norm_spec,
        ],
        out_specs=[out_spec, out_spec],
        out_shape=[out_shape, out_shape],
        scratch_shapes=[pltpu.VMEM((HEADS, HEAD_DIM, HEAD_DIM), F32),
                        pltpu.VMEM((HEADS, ML_QK_DIM, 2 * HEAD_DIM), F32),
                        pltpu.VMEM((HEADS, 1, 1), F32)],
        compiler_params=_params("parallel", "arbitrary"),
    )(gdn_qkv, gdn_qkv, gdn_qkv, proj, proj, proj, proj, proj, col_small,
      row_small, row_small, row_small, gdn_norm_g, ml_norm_g)


def _merge_kernel(yf_ref, yg_ref, ym_ref, wf_ref, wg_ref, wm_ref, g0_ref, g1_ref, g2_ref,
                  b0_ref, b1_ref, b2_ref, o_ref):
    def branch(y_ref, w_ref, g_ref, b_ref):
        return _sigmoid(g_ref[...] + b_ref[...]) * jnp.dot(y_ref[...], w_ref[...],
                                                          preferred_element_type=F32)

    y = (branch(yf_ref, wf_ref, g0_ref, b0_ref) + branch(yg_ref, wg_ref, g1_ref, b1_ref)
         + branch(ym_ref, wm_ref, g2_ref, b2_ref))
    o_ref[...] = y.astype(o_ref.dtype)


def _merge(y_fox, y_gdn, y_ml, w_fox, w_gdn, w_ml, layer, gate_src, gate_bias, d_model, *,
           gate_col, tm=1024, tn=512):
    T, K = y_fox.shape
    nj = d_model // tn
    y_spec = pl.BlockSpec((tm, K), lambda i, j: (i, 0))
    w_spec = pl.BlockSpec((None, K, tn), lambda i, j: (layer, 0, j))
    gate_specs = [pl.BlockSpec((tm, tn), lambda i, j, n=n: (i, gate_col + n * nj + j))
                  for n in range(N_BRANCH)]
    bias_specs = [pl.BlockSpec((1, tn), lambda i, j, n=n: (0, n * nj + j)) for n in range(N_BRANCH)]
    return pl.pallas_call(
        _merge_kernel,
        grid=(T // tm, nj),
        in_specs=[y_spec, y_spec, y_spec, w_spec, w_spec, w_spec] + gate_specs + bias_specs,
        out_specs=pl.BlockSpec((tm, tn), lambda i, j: (i, j)),
        out_shape=jax.ShapeDtypeStruct((T, d_model), BF16),
        compiler_params=_params("parallel", "parallel"),
    )(y_fox, y_gdn, y_ml, w_fox, w_gdn, w_ml, gate_src, gate_src, gate_src,
      gate_bias, gate_bias, gate_bias)


def _matmul_residual_kernel(a_ref, w_ref, x_ref, o_ref):
    o_ref[...] = x_ref[...] + jnp.dot(a_ref[...], w_ref[...], preferred_element_type=F32)


def _matmul_residual(a, w_all, layer, x, tm=1024, tn=512):
    T, K = a.shape
    N = w_all.shape[2]
    return pl.pallas_call(
        _matmul_residual_kernel,
        grid=(T // tm, N // tn),
        in_specs=[
            pl.BlockSpec((tm, K), lambda i, j: (i, 0)),
            pl.BlockSpec((None, K, tn), lambda i, j: (layer, 0, j)),
            pl.BlockSpec((tm, tn), lambda i, j: (i, j)),
        ],
        out_specs=pl.BlockSpec((tm, tn), lambda i, j: (i, j)),
        out_shape=jax.ShapeDtypeStruct((T, N), F32),
        compiler_params=_params("parallel", "parallel"),
    )(a, w_all, x)


def _final_norm_kernel(x_ref, g_ref, o_ref):
    o_ref[...] = _rmsnorm(x_ref[...], g_ref[...])


def _final_norm(x, g, tm=512):
    T, D = x.shape
    return pl.pallas_call(
        _final_norm_kernel,
        grid=(T // tm,),
        in_specs=[pl.BlockSpec((tm, D), lambda i: (i, 0)), pl.BlockSpec((1, D), lambda i: (0, 0))],
        out_specs=pl.BlockSpec((tm, D), lambda i: (i, 0)),
        out_shape=jax.ShapeDtypeStruct((T, D), F32),
        compiler_params=_params("parallel"),
    )(x, g)


def _ffn_kernel(x_ref, xh_ref, g_ref, wu_ref, wg_ref, cu_ref, cg_ref, wd_ref, o_ref, xn_ref, *,
                blocks_per_seq):
    groups = _column_groups(wu_ref.shape[1])
    half = x_ref.shape[0] // 2

    def up(xn):
        return [(jnp.dot(xn, wu_ref[:, cs], preferred_element_type=F32),
                 jnp.dot(xn, wg_ref[:, cs], preferred_element_type=F32)) for cs in groups]

    def gate(hidden):
        acts = []
        for cs, (hu, hg) in zip(groups, hidden):
            gt = _conv_taps(hg, cg_ref, cs)
            acts.append((gt * _sigmoid(gt) * _conv_taps(hu, cu_ref, cs)).astype(BF16))
        return jnp.concatenate(acts, axis=1)

    def down(rows, act):
        o_ref[rows, :] += jnp.dot(act, wd_ref[...], preferred_element_type=F32)

    def tile(xn):
        act_a = gate(up(xn[:half + HALO, :]))
        hidden_b = up(xn[half:, :])
        down(slice(0, half), act_a)
        down(slice(half, 2 * half), gate(hidden_b))

    @pl.when(pl.program_id(1) == 0)
    def _():
        o_ref[...] = x_ref[...]
        tile(_norm_with_halo(x_ref, xh_ref, g_ref, xn_ref,
                             pl.program_id(0) % blocks_per_seq == 0))

    @pl.when(pl.program_id(1) != 0)
    def _():
        tile(xn_ref[...])


def _ffn(x, g, w_up, conv_w, w_down, layer, seq, tm=1024, tn=512):
    T, D = x.shape
    taps = conv_w.shape[0]
    nj = w_down.shape[1] // tn
    halo_blocks = tm // HALO
    once = pl.Buffered(1)
    return pl.pallas_call(
        functools.partial(_ffn_kernel, blocks_per_seq=seq // tm),
        grid=(T // tm, nj),
        in_specs=[
            pl.BlockSpec((tm, D), lambda i, j: (i, 0)),
            pl.BlockSpec((HALO, D), lambda i, j: (jnp.maximum(i * halo_blocks - 1, 0), 0)),
            pl.BlockSpec((1, D), lambda i, j: (0, 0)),
            pl.BlockSpec((None, D, tn), lambda i, j: (layer, 0, j)),
            pl.BlockSpec((None, D, tn), lambda i, j: (layer, 0, nj + j)),
            pl.BlockSpec((taps, tn), lambda i, j: (0, j)),
            pl.BlockSpec((taps, tn), lambda i, j: (0, nj + j)),
            pl.BlockSpec((None, tn, D), lambda i, j: (layer, j, 0)),
        ],
        out_specs=pl.BlockSpec((tm, D), lambda i, j: (i, 0), pipeline_mode=once),
        out_shape=jax.ShapeDtypeStruct((T, D), F32),
        scratch_shapes=[pltpu.VMEM((tm + HALO, D), BF16)],
        compiler_params=_params("parallel", "arbitrary", vmem_limit_bytes=VMEM_LIMIT_LARGE_BYTES),
    )(x, x, g, w_up, w_up, conv_w, conv_w, w_down)


FOX_W = HEADS * HEAD_DIM
GDN_W = HEADS * HEAD_DIM
ML_QK_W = HEADS * ML_QK_DIM
ML_V_W = HEADS * HEAD_DIM
ML_BASE = 3 * FOX_W
Z_BASE = ML_BASE + 2 * ML_QK_W + ML_V_W
GATE_BASE = Z_BASE + GDN_W + ML_V_W
FOX_HEADS_PER_STEP = 4


def _in_projection_columns(d_model):
    widths = [("fox_q", FOX_W), ("fox_k", FOX_W), ("fox_v", FOX_W), ("fox_f", HEADS),
              ("gdn_qkv", 3 * GDN_W), ("gdn_z", GDN_W), ("gdn_b", HEADS), ("gdn_a", HEADS),
              ("ml_q", ML_QK_W), ("ml_k", ML_QK_W), ("ml_v", ML_V_W), ("ml_i", HEADS),
              ("ml_f", HEADS), ("ml_o", ML_V_W), ("gate", N_BRANCH * d_model)]
    off, start = {}, 0
    for name, width in widths:
        off[name] = (start, start + width)
        start += width
    return off


def _pad_lanes(v):
    return jnp.pad(v, (0, LANES - v.shape[0])).reshape(1, LANES)


def _layer(x, layer, w, p, batch, seq):
    T, D = x.shape
    g_mix = p["norm_mix_g"].reshape(1, D)
    scale_p = jnp.concatenate([
        jnp.full((FOX_W,), HEAD_DIM ** -0.5 * LOG2E, F32), jnp.ones((2 * FOX_W,), F32),
        jnp.full((ML_QK_W,), ML_QK_DIM ** -0.5, F32),
        jnp.ones((w["proj"].shape[2] - 3 * FOX_W - ML_QK_W,), F32)]).reshape(1, -1)
    proj = _norm_matmul(x, g_mix, w["proj"], layer, scale_p, BF16, tn=1024)
    zeros8 = jnp.zeros((HEADS,), F32)
    bias_s = _pad_lanes(jnp.concatenate([p["fox_f_bias"], zeros8, p["gdn_dt_bias"],
                                         p["ml_i_bias"], p["ml_f_bias"]]))
    alog_s = _pad_lanes(jnp.concatenate([zeros8, zeros8, p["gdn_a_log"]]))
    col_small, row_small = _small_gates(x, g_mix, w["small"], layer, bias_s, alog_s, seq)
    f_cum = row_small[:, SM_FOX_F:SM_FOX_F + HEADS, :].reshape(batch, seq // CHUNK, HEADS, CHUNK)
    f_cum = f_cum.transpose(0, 2, 1, 3).reshape(batch, HEADS, seq)
    gdn_qkv = _proj_conv(x, g_mix, w["gdn_qkv"], layer, p["gdn_conv_w"], seq)

    hps = FOX_HEADS_PER_STEP
    y_fox = _fox_attention(proj, f_cum, batch, seq, q_col=0, k_col=HEADS // hps,
                           v_col=2 * HEADS // hps, hps=hps)
    y_gdn, y_ml = _chunk_mixers(
        gdn_qkv, proj, col_small, row_small, p["gdn_norm_g"].reshape(1, HEAD_DIM),
        p["ml_norm_g"].reshape(1, HEAD_DIM), batch, seq, z_col=Z_BASE // GDN_W,
        q_col=ML_BASE // ML_QK_W, k_col=ML_BASE // ML_QK_W + 1,
        v_col=(ML_BASE + 2 * ML_QK_W) // ML_V_W, og_col=(Z_BASE + GDN_W) // ML_V_W)

    tn = 512
    y = _merge(y_fox, y_gdn, y_ml, w["fox_proj"], w["gdn_proj"], w["ml_proj"], layer, proj,
               p["gate_bias"].reshape(1, -1), D, gate_col=GATE_BASE // tn, tn=tn)
    x = _matmul_residual(y, w["out"], layer, x)
    return _ffn(x, p["norm_ffn_g"].reshape(1, D), w["up"], p["ffn_conv_w"], w["down"], layer, seq)


def kernel(x, norm_mix_g, w_in, fox_f_bias, gdn_conv_w, gdn_a_log, gdn_dt_bias, gdn_norm_g,
           ml_i_bias, ml_f_bias, ml_norm_g, gate_bias, w_fox_proj, w_gdn_proj, w_ml_proj,
           w_out, norm_ffn_g, w_up, ffn_conv_w, w_down, norm_final_g):
    batch, seq, d_model = x.shape
    depth = w_in.shape[0]
    off = _in_projection_columns(d_model)
    cols = lambda a, b: w_in[:, :, off[a][0]:off[b][1]].astype(BF16)
    w_small = jnp.concatenate([cols("fox_f", "fox_f"), cols("gdn_b", "gdn_a"), cols("ml_i", "ml_f")],
                              axis=2)
    w = dict(
        proj=jnp.concatenate([cols("fox_q", "fox_v"), cols("ml_q", "ml_v"), cols("gdn_z", "gdn_z"),
                              cols("ml_o", "ml_o"), cols("gate", "gate")], axis=2),
        small=jnp.pad(w_small, ((0, 0), (0, 0), (0, LANES - w_small.shape[2]))),
        gdn_qkv=cols("gdn_qkv", "gdn_qkv"),
        fox_proj=w_fox_proj.astype(BF16), gdn_proj=w_gdn_proj.astype(BF16),
        ml_proj=w_ml_proj.astype(BF16), out=w_out.astype(BF16), up=w_up.astype(BF16),
        down=w_down.astype(BF16))
    small = dict(norm_mix_g=norm_mix_g, fox_f_bias=fox_f_bias, gdn_conv_w=gdn_conv_w,
                 gdn_a_log=gdn_a_log, gdn_dt_bias=gdn_dt_bias, gdn_norm_g=gdn_norm_g,
                 ml_i_bias=ml_i_bias, ml_f_bias=ml_f_bias, ml_norm_g=ml_norm_g,
                 gate_bias=gate_bias, norm_ffn_g=norm_ffn_g, ffn_conv_w=ffn_conv_w)
    h = x.reshape(batch * seq, d_model)
    for layer in range(depth):
        h = _layer(h, layer, w, {k: v[layer] for k, v in small.items()}, batch, seq)
    return _final_norm(h, norm_final_g.reshape(1, d_model)).reshape(batch, seq, d_model)
```

```python
import functools

import jax
import jax.numpy as jnp
from jax import lax
from jax.experimental import pallas as pl
from jax.experimental.pallas import tpu as pltpu

F32 = jnp.float32
BF16 = jnp.bfloat16
HIGHEST = lax.Precision.HIGHEST

EPS = 1e-6
HEADS = 8
HEAD_DIM = 128
ML_QK_DIM = 64
CHUNK = 64
N_BRANCH = 3
LANES = 128
MXU_WIDTH = 256
HALO = 16

SM_FOX_F, SM_GDN_B, SM_GDN_A, SM_ML_I, SM_ML_F = 0, 8, 16, 24, 32

V7X_VMEM_BYTES = 64 * 1024 * 1024
VMEM_LIMIT_BYTES = V7X_VMEM_BYTES * 3 // 4
VMEM_LIMIT_LARGE_BYTES = V7X_VMEM_BYTES * 7 // 8


def _params(*semantics, vmem_limit_bytes=VMEM_LIMIT_BYTES):
    return pltpu.CompilerParams(dimension_semantics=semantics, vmem_limit_bytes=vmem_limit_bytes)


def _rmsnorm(x, g):
    return x * lax.rsqrt(jnp.mean(x * x, axis=-1, keepdims=True) + EPS) * g


def _sigmoid(x):
    return 1.0 / (1.0 + jnp.exp(-x))


def _mm_hi(a, b):
    return jnp.dot(a, b, precision=HIGHEST, preferred_element_type=F32)


def _norm_matmul_kernel(x_ref, g_ref, w_ref, cs_ref, o_ref, xn_ref):
    def tile(xn):
        acc = jnp.dot(xn, w_ref[...], preferred_element_type=F32)
        o_ref[...] = (acc * cs_ref[...]).astype(o_ref.dtype)

    @pl.when(pl.program_id(1) == 0)
    def _():
        xn = _rmsnorm(x_ref[...], g_ref[...]).astype(BF16)
        xn_ref[...] = xn
        tile(xn)

    @pl.when(pl.program_id(1) != 0)
    def _():
        tile(xn_ref[...])


def _norm_matmul(x, g, w_all, layer, colscale, out_dtype, tm=1024, tn=512):
    T, D = x.shape
    N = w_all.shape[2]
    return pl.pallas_call(
        _norm_matmul_kernel,
        grid=(T // tm, N // tn),
        in_specs=[
            pl.BlockSpec((tm, D), lambda i, j: (i, 0)),
            pl.BlockSpec((1, D), lambda i, j: (0, 0)),
            pl.BlockSpec((None, D, tn), lambda i, j: (layer, 0, j)),
            pl.BlockSpec((1, tn), lambda i, j: (0, j)),
        ],
        out_specs=pl.BlockSpec((tm, tn), lambda i, j: (i, j)),
        out_shape=jax.ShapeDtypeStruct((T, N), out_dtype),
        scratch_shapes=[pltpu.VMEM((tm, D), BF16)],
        compiler_params=_params("parallel", "arbitrary"),
    )(x, g, w_all, colscale)


def _small_kernel(x_ref, g_ref, w_ref, bias_ref, alog_ref, o_ref, rows_ref, carry_ref, *,
                  blocks_per_seq):
    i = pl.program_id(0)
    tm = x_ref.shape[0]

    @pl.when(i % blocks_per_seq == 0)
    def _():
        carry_ref[...] = jnp.zeros_like(carry_ref)

    xn = _rmsnorm(x_ref[...], g_ref[...]).astype(BF16)
    t = jnp.dot(xn, w_ref[...], preferred_element_type=F32) + bias_ref[...]
    lane = lax.broadcasted_iota(jnp.int32, (tm, LANES), 1)
    e = jnp.log1p(jnp.exp(-jnp.abs(t)))
    logsig = jnp.minimum(t, 0.0) - e
    softplus = jnp.maximum(t, 0.0) + e
    g_decay = -jnp.exp(alog_ref[...]) * softplus
    is_logsig = (lane < SM_GDN_B) | ((lane >= SM_ML_F) & (lane < SM_ML_F + HEADS))
    val = jnp.where(is_logsig, logsig,
                    jnp.where(lane < SM_GDN_A, _sigmoid(t),
                              jnp.where(lane < SM_ML_I, g_decay, t)))

    r = lax.broadcasted_iota(jnp.int32, (CHUNK, CHUNK), 0)
    c = lax.broadcasted_iota(jnp.int32, (CHUNK, CHUNK), 1)
    tri = jnp.where(c <= r, 1.0, 0.0).astype(F32)
    chunk_sums = [_mm_hi(tri, val[ci * CHUNK:(ci + 1) * CHUNK, :]) for ci in range(tm // CHUNK)]
    offset = carry_ref[...]
    full_sums = []
    for cs in chunk_sums:
        full_sums.append(cs + offset)
        offset = offset + cs[CHUNK - 1:CHUNK, :]
    carry_ref[...] = offset
    cs_chunk = jnp.concatenate(chunk_sums, axis=0)
    cs_full = jnp.concatenate(full_sums, axis=0)

    is_chunk_cs = ((lane >= SM_GDN_A) & (lane < SM_ML_I)) | ((lane >= SM_ML_F) & (lane < SM_ML_F + HEADS))
    out = jnp.where(lane < SM_GDN_B, cs_full, jnp.where(is_chunk_cs, cs_chunk, val))
    o_ref[...] = out
    for ci in range(tm // CHUNK):
        rows_ref[ci] = out[ci * CHUNK:(ci + 1) * CHUNK, :].T


def _small_gates(x, g, w_all, layer, bias, alog, seq, tm=512):
    T, D = x.shape
    return pl.pallas_call(
        functools.partial(_small_kernel, blocks_per_seq=seq // tm),
        grid=(T // tm,),
        in_specs=[
            pl.BlockSpec((tm, D), lambda i: (i, 0)),
            pl.BlockSpec((1, D), lambda i: (0, 0)),
            pl.BlockSpec((None, D, LANES), lambda i: (layer, 0, 0)),
            pl.BlockSpec((1, LANES), lambda i: (0, 0)),
            pl.BlockSpec((1, LANES), lambda i: (0, 0)),
        ],
        out_specs=[pl.BlockSpec((tm, LANES), lambda i: (i, 0)),
                   pl.BlockSpec((tm // CHUNK, LANES, CHUNK), lambda i: (i, 0, 0))],
        out_shape=[jax.ShapeDtypeStruct((T, LANES), F32),
                   jax.ShapeDtypeStruct((T // CHUNK, LANES, CHUNK), F32)],
        scratch_shapes=[pltpu.VMEM((1, LANES), F32)],
        compiler_params=_params("arbitrary"),
    )(x, g, w_all, bias, alog)


def _conv_taps(h, c_ref, cols):
    taps = c_ref.shape[0]
    tm = h.shape[0] - HALO
    acc = None
    for t in range(taps):
        start = HALO - (taps - 1) + t
        term = h[start:start + tm, :] * c_ref[t:t + 1, cols]
        acc = term if acc is None else acc + term
    return acc


def _column_groups(width):
    return [slice(c * MXU_WIDTH, (c + 1) * MXU_WIDTH) for c in range(width // MXU_WIDTH)]


def _norm_with_halo(x_ref, xh_ref, g_ref, xn_ref, at_seq_start):
    keep = jnp.where(at_seq_start, 0.0, 1.0)
    xn = jnp.concatenate([(_rmsnorm(xh_ref[...], g_ref[...]) * keep).astype(BF16),
                          _rmsnorm(x_ref[...], g_ref[...]).astype(BF16)], axis=0)
    xn_ref[...] = xn
    return xn


def _proj_conv_kernel(x_ref, xh_ref, g_ref, w_ref, c_ref, o_ref, xn_ref, *, blocks_per_seq):
    groups = _column_groups(w_ref.shape[1])

    def tile(xn):
        hidden = [jnp.dot(xn, w_ref[:, cs], preferred_element_type=F32) for cs in groups]
        for cs, h in zip(groups, hidden):
            u = _conv_taps(h, c_ref, cs)
            o_ref[:, cs] = (u * _sigmoid(u)).astype(o_ref.dtype)

    @pl.when(pl.program_id(1) == 0)
    def _():
        tile(_norm_with_halo(x_ref, xh_ref, g_ref, xn_ref,
                             pl.program_id(0) % blocks_per_seq == 0))

    @pl.when(pl.program_id(1) != 0)
    def _():
        tile(xn_ref[...])


def _proj_conv(x, g, w_all, layer, conv_w, seq, tm=1024, tn=1024):
    T, D = x.shape
    taps = conv_w.shape[0]
    N = w_all.shape[2]
    halo_blocks = tm // HALO
    return pl.pallas_call(
        functools.partial(_proj_conv_kernel, blocks_per_seq=seq // tm),
        grid=(T // tm, N // tn),
        in_specs=[
            pl.BlockSpec((tm, D), lambda i, j: (i, 0)),
            pl.BlockSpec((HALO, D), lambda i, j: (jnp.maximum(i * halo_blocks - 1, 0), 0)),
            pl.BlockSpec((1, D), lambda i, j: (0, 0)),
            pl.BlockSpec((None, D, tn), lambda i, j: (layer, 0, j)),
            pl.BlockSpec((taps, tn), lambda i, j: (0, j)),
        ],
        out_specs=pl.BlockSpec((tm, tn), lambda i, j: (i, j)),
        out_shape=jax.ShapeDtypeStruct((T, N), BF16),
        scratch_shapes=[pltpu.VMEM((tm + HALO, D), BF16)],
        compiler_params=_params("parallel", "arbitrary"),
    )(x, x, g, w_all, conv_w)


NEG_BIG = -1e30
LOG2E = 1.4426950408889634


def _fox_kernel(q_ref, k_ref, v_ref, f_ref, o_ref, m_ref, acc_ref, *, tq, tk, hps):
    qi = pl.program_id(2)
    ones = jnp.ones((tk, HEAD_DIM), BF16)
    lane_tiles = tk // LANES
    m_ref[...] = jnp.full_like(m_ref, NEG_BIG)
    acc_ref[...] = jnp.zeros_like(acc_ref)

    heads = [slice(h * HEAD_DIM, (h + 1) * HEAD_DIM) for h in range(hps)]

    def scores(j, masked):
        start = pl.multiple_of(j * tk, tk)
        out = []
        for h, hs in enumerate(heads):
            s = lax.dot_general(q_ref[:, hs], k_ref[pl.ds(start, tk), hs],
                                (((1,), (1,)), ((), ())), preferred_element_type=F32)
            s = s - f_ref[h, pl.ds(j, 1), :] * LOG2E
            if masked:
                row = lax.broadcasted_iota(jnp.int32, (tq, tk), 0)
                col = lax.broadcasted_iota(jnp.int32, (tq, tk), 1)
                s = jnp.where(col <= row, s, NEG_BIG)
            out.append([s[:, c * LANES:(c + 1) * LANES] for c in range(lane_tiles)])
        return out

    def softmax(score_tiles):
        probs, alphas = [], []
        for h, tiles in enumerate(score_tiles):
            m_prev = m_ref[h]
            m_tile = functools.reduce(jnp.maximum, tiles)
            m_new = jnp.maximum(m_prev, jnp.max(m_tile, axis=-1, keepdims=True))
            m_ref[h] = m_new
            alphas.append(jnp.exp2(m_prev - m_new))
            probs.append(jnp.concatenate([jnp.exp2(t - m_new).astype(BF16) for t in tiles], axis=1))
        return probs, alphas

    def accumulate(j, probs, alphas):
        start = pl.multiple_of(j * tk, tk)
        for h, hs in enumerate(heads):
            v_ext = jnp.concatenate([v_ref[pl.ds(start, tk), hs], ones], axis=1)
            pv = jnp.dot(probs[h], v_ext, preferred_element_type=F32)
            acc_ref[h] = jnp.concatenate([alphas[h], alphas[h]], axis=1) * acc_ref[h] + pv

    def block(j, masked):
        accumulate(j, *softmax(scores(j, masked)))

    def block_pair(j, second_masked):
        first = softmax(scores(j, False))
        second_scores = scores(j + 1, second_masked)
        accumulate(j, *first)
        accumulate(j + 1, *softmax(second_scores))

    def full_pair(t, carry):
        block_pair(2 * t, False)
        return carry

    lax.fori_loop(0, qi >> 1, full_pair, 0)

    @pl.when((qi & 1) == 1)
    def _():
        block_pair(qi - 1, True)

    @pl.when((qi & 1) == 0)
    def _():
        block(qi, True)

    for h in range(hps):
        acc = acc_ref[h]
        o_ref[:, h * HEAD_DIM:(h + 1) * HEAD_DIM] = (
            acc[:, :HEAD_DIM] / acc[:, HEAD_DIM:]).astype(o_ref.dtype)


def _fox_attention(qkv, f_cum, batch, seq, *, q_col, k_col, v_col, tq=512, hps=2):
    T = qkv.shape[0]
    nq = seq // tq
    width = hps * HEAD_DIM
    f_blocks = f_cum.reshape(batch, HEADS, nq, tq)
    return pl.pallas_call(
        functools.partial(_fox_kernel, tq=tq, tk=tq, hps=hps),
        grid=(batch, HEADS // hps, nq),
        in_specs=[
            pl.BlockSpec((tq, width), lambda b, h, qi: (b * nq + qi, q_col + h)),
            pl.BlockSpec((seq, width), lambda b, h, qi: (b, k_col + h)),
            pl.BlockSpec((seq, width), lambda b, h, qi: (b, v_col + h)),
            pl.BlockSpec((None, hps, nq, tq), lambda b, h, qi: (b, h, 0, 0)),
        ],
        out_specs=pl.BlockSpec((tq, width), lambda b, h, qi: (b * nq + qi, h)),
        out_shape=jax.ShapeDtypeStruct((T, HEADS * HEAD_DIM), BF16),
        scratch_shapes=[pltpu.VMEM((hps, tq, LANES), F32),
                        pltpu.VMEM((hps, tq, 2 * HEAD_DIM), F32)],
        compiler_params=_params("parallel", "parallel", "arbitrary",
                                vmem_limit_bytes=VMEM_LIMIT_LARGE_BYTES),
    )(qkv, qkv, qkv, f_blocks)


def _bmm(a, b):
    return jnp.einsum("gmk,gkn->gmn", a.astype(BF16), b.astype(BF16), preferred_element_type=F32)


def _bmm_nt(a, b):
    return jnp.einsum("gmk,gnk->gmn", a.astype(BF16), b.astype(BF16), preferred_element_type=F32)


def _bmm_tn(a, b):
    return jnp.einsum("gkm,gkn->gmn", a.astype(BF16), b.astype(BF16), preferred_element_type=F32)


def _inverse_pairs(m, eye, r, c):
    return eye - jnp.where((r >> 1) == (c >> 1), m, 0.0)


def _inverse_double(x, m, r, c, b):
    shift = b.bit_length() - 1
    joins = ((r >> (shift + 1)) == (c >> (shift + 1))) & ((r >> shift) != (c >> shift))
    return x - _bmm(_bmm(x, jnp.where(joins, m, 0.0)), x)


STEP_CHUNKS = 4


def _chunk_head_stack(ref, width=HEAD_DIM):
    return jnp.stack([ref[ci * CHUNK:(ci + 1) * CHUNK, h * width:(h + 1) * width]
                      for ci in range(STEP_CHUNKS) for h in range(HEADS)])


def _chunk_head_cols(col, start):
    return jnp.stack([col[ci * CHUNK:(ci + 1) * CHUNK, start + h:start + h + 1]
                      for ci in range(STEP_CHUNKS) for h in range(HEADS)])


def _chunk_head_rows(row_ref):
    return jnp.concatenate([row_ref[ci] for ci in range(STEP_CHUNKS)], axis=0)[:, None, :]


def _store_heads(o_ref, y):
    for ci in range(STEP_CHUNKS):
        for h in range(HEADS):
            o_ref[ci * CHUNK:(ci + 1) * CHUNK, h * HEAD_DIM:(h + 1) * HEAD_DIM] = y[ci * HEADS + h]


def _chunk_mixers_kernel(gq_ref, gk_ref, gv_ref, z_ref, mq_ref, mk_ref, mv_ref, og_ref, col_ref,
                         rowg_ref, rowi_ref, rowb_ref, gng_ref, mng_ref, yg_ref, ym_ref,
                         s_ref, c_ref, m_ref):
    @pl.when(pl.program_id(1) == 0)
    def _():
        s_ref[...] = jnp.zeros_like(s_ref)
        c_ref[...] = jnp.zeros_like(c_ref)
        m_ref[...] = jnp.zeros_like(m_ref)

    L = CHUNK
    r = lax.broadcasted_iota(jnp.int32, (1, L, L), 1)
    c = lax.broadcasted_iota(jnp.int32, (1, L, L), 2)
    tri = c <= r
    eye = jnp.where(c == r, 1.0, 0.0).astype(F32)
    col = col_ref[...]

    q = _chunk_head_stack(gq_ref).astype(F32)
    k = _chunk_head_stack(gk_ref).astype(F32)
    v = _chunk_head_stack(gv_ref).astype(F32)
    qn = q * lax.rsqrt(jnp.sum(q * q, axis=-1, keepdims=True) + EPS) * (HEAD_DIM ** -0.5)
    kn = k * lax.rsqrt(jnp.sum(k * k, axis=-1, keepdims=True) + EPS)
    beta = _chunk_head_cols(col, SM_GDN_B)
    gc = _chunk_head_cols(col, SM_GDN_A)
    gr = _chunk_head_rows(rowg_ref)
    g_last = gc[:, L - 1:L, :]
    decay = jnp.where(tri, jnp.exp(jnp.where(tri, gc - gr, 0.0)), 0.0)
    eg = jnp.exp(gc)
    kb = kn * beta
    m = jnp.where(c < r, _bmm_nt(kb, kn) * decay, 0.0)
    t_inv = _inverse_pairs(m, eye, r, c)

    mq = _chunk_head_stack(mq_ref, ML_QK_DIM).astype(F32)
    mk = _chunk_head_stack(mk_ref, ML_QK_DIM).astype(F32)
    mv = _chunk_head_stack(mv_ref)
    mv = jnp.concatenate([mv, jnp.ones_like(mv)], axis=-1)
    ic = _chunk_head_cols(col, SM_ML_I)
    bc = _chunk_head_cols(col, SM_ML_F)
    ir = _chunk_head_rows(rowi_ref)
    br = _chunk_head_rows(rowb_ref)
    dm = jnp.where(tri, bc - br + ir, -jnp.inf)
    dmax = jnp.max(dm, axis=-1, keepdims=True)
    qk = _bmm_nt(mq, mk)

    t_inv = _inverse_double(t_inv, m, r, c, 2)
    t_inv = _inverse_double(t_inv, m, r, c, 4)

    m_state = m_ref[...]
    m_rows, m_lasts = [], []
    for ci in range(STEP_CHUNKS):
        hs = slice(ci * HEADS, (ci + 1) * HEADS)
        m_r = jnp.maximum(bc[hs] + m_state, dmax[hs])
        m_rows.append(m_r)
        m_state = m_r[:, L - 1:L, :]
        m_lasts.append(m_state)
    m_r = jnp.concatenate(m_rows, axis=0)
    m_last = jnp.concatenate(m_lasts, axis=0)
    m_prev = jnp.concatenate([m_ref[...]] + m_lasts[:-1], axis=0)
    m_ref[...] = m_state
    w_inter = jnp.exp(bc + m_prev - m_r)
    sm = jnp.exp(dm - m_r) * qk
    intra = _bmm(sm, mv)

    t_inv = _inverse_double(t_inv, m, r, c, 8)

    wk = jnp.exp(bc[:, L - 1:L, :] - bc + ic - m_last) * mk
    kv = _bmm_tn(wk, mv)
    w_last = w_inter[:, L - 1:L, :]
    floor = jnp.exp(-m_r)

    t_inv = _inverse_double(t_inv, m, r, c, 16)
    t_inv = _inverse_double(t_inv, m, r, c, 32)
    x = _bmm(t_inv, jnp.concatenate([v * beta, kb * eg], axis=-1))
    u = x[:, :, :HEAD_DIM]
    w = x[:, :, HEAD_DIM:]
    a_qk = _bmm_nt(qn, kn) * decay
    qg = qn * eg
    kg = kn * jnp.exp(g_last - gc)
    carry = jnp.exp(g_last)

    state = s_ref[...]
    c_state = c_ref[...]
    g_outs, m_outs = [], []
    for ci in range(STEP_CHUNKS):
        hs = slice(ci * HEADS, (ci + 1) * HEADS)
        v_new = u[hs] - _bmm(w[hs], state)
        num_den = w_inter[hs] * _bmm(mq[hs], c_state) + intra[hs]
        g_outs.append(_bmm(qg[hs], state) + _bmm(a_qk[hs], v_new))
        state = state * carry[hs] + _bmm_tn(kg[hs], v_new)
        den = jnp.maximum(jnp.abs(num_den[:, :, HEAD_DIM:]), floor[hs])
        m_outs.append(num_den[:, :, :HEAD_DIM] / den)
        c_state = w_last[hs] * c_state + kv[hs]
    s_ref[...] = state
    c_ref[...] = c_state

    o = jnp.concatenate(g_outs, axis=0)
    z = _chunk_head_stack(z_ref).astype(F32)
    _store_heads(yg_ref, (_rmsnorm(o, gng_ref[...]) * (z * _sigmoid(z))).astype(yg_ref.dtype))
    h_tilde = jnp.concatenate(m_outs, axis=0)
    og = _chunk_head_stack(og_ref).astype(F32)
    _store_heads(ym_ref, _rmsnorm(_sigmoid(og) * h_tilde, mng_ref[...]).astype(ym_ref.dtype))


def _chunk_mixers(gdn_qkv, proj, col_small, row_small, gdn_norm_g, ml_norm_g, batch, seq, *,
                  z_col, q_col, k_col, v_col, og_col):
    T = proj.shape[0]
    rows = STEP_CHUNKS * CHUNK
    ns = seq // rows
    qk_width = HEADS * ML_QK_DIM
    width = HEADS * HEAD_DIM
    row = lambda b, s: b * ns + s
    wide = lambda col: pl.BlockSpec((rows, width), lambda b, s: (row(b, s), col))
    rows_spec = lambda seg: pl.BlockSpec((STEP_CHUNKS, HEADS, CHUNK),
                                         lambda b, s: (row(b, s), seg // HEADS, 0))
    norm_spec = pl.BlockSpec((1, HEAD_DIM), lambda b, s: (0, 0))
    out_spec = pl.BlockSpec((rows, width), lambda b, s: (row(b, s), 0))
    out_shape = jax.ShapeDtypeStruct((T, width), BF16)
    return pl.pallas_call(
        _chunk_mixers_kernel,
        grid=(batch, ns),
        in_specs=[
            wide(0), wide(1), wide(2), wide(z_col),
            pl.BlockSpec((rows, qk_width), lambda b, s: (row(b, s), q_col)),
            pl.BlockSpec((rows, qk_width), lambda b, s: (row(b, s), k_col)),
            wide(v_col), wide(og_col),
            pl.BlockSpec((rows, LANES), lambda b, s: (row(b, s), 0)),
            rows_spec(SM_GDN_A), rows_spec(SM_ML_I), rows_spec(SM_ML_F),
            norm_spec, norm_spec,
        ],
        out_specs=[out_spec, out_spec],
        out_shape=[out_shape, out_shape],
        scratch_shapes=[pltpu.VMEM((HEADS, HEAD_DIM, HEAD_DIM), F32),
                        pltpu.VMEM((HEADS, ML_QK_DIM, 2 * HEAD_DIM), F32),
                        pltpu.VMEM((HEADS, 1, 1), F32)],
        compiler_params=_params("parallel", "arbitrary"),
    )(gdn_qkv, gdn_qkv, gdn_qkv, proj, proj, proj, proj, proj, col_small,
      row_small, row_small, row_small, gdn_norm_g, ml_norm_g)


def _merge_kernel(yf_ref, yg_ref, ym_ref, wf_ref, wg_ref, wm_ref, g0_ref, g1_ref, g2_ref,
                  b0_ref, b1_ref, b2_ref, o_ref):
    def branch(y_ref, w_ref, g_ref, b_ref):
        return _sigmoid(g_ref[...] + b_ref[...]) * jnp.dot(y_ref[...], w_ref[...],
                                                          preferred_element_type=F32)

    y = (branch(yf_ref, wf_ref, g0_ref, b0_ref) + branch(yg_ref, wg_ref, g1_ref, b1_ref)
         + branch(ym_ref, wm_ref, g2_ref, b2_ref))
    o_ref[...] = y.astype(o_ref.dtype)


def _merge(y_fox, y_gdn, y_ml, w_fox, w_gdn, w_ml, layer, gate_src, gate_bias, d_model, *,
           gate_col, tm=1024, tn=512):
    T, K = y_fox.shape
    nj = d_model // tn
    y_spec = pl.BlockSpec((tm, K), lambda i, j: (i, 0))
    w_spec = pl.BlockSpec((None, K, tn), lambda i, j: (layer, 0, j))
    gate_specs = [pl.BlockSpec((tm, tn), lambda i, j, n=n: (i, gate_col + n * nj + j))
                  for n in range(N_BRANCH)]
    bias_specs = [pl.BlockSpec((1, tn), lambda i, j, n=n: (0, n * nj + j)) for n in range(N_BRANCH)]
    return pl.pallas_call(
        _merge_kernel,
        grid=(T // tm, nj),
        in_specs=[y_spec, y_spec, y_spec, w_spec, w_spec, w_spec] + gate_specs + bias_specs,
        out_specs=pl.BlockSpec((tm, tn), lambda i, j: (i, j)),
        out_shape=jax.ShapeDtypeStruct((T, d_model), BF16),
        compiler_params=_params("parallel", "parallel"),
    )(y_fox, y_gdn, y_ml, w_fox, w_gdn, w_ml, gate_src, gate_src, gate_src,
      gate_bias, gate_bias, gate_bias)


def _matmul_residual_kernel(a_ref, w_ref, x_ref, o_ref):
    o_ref[...] = x_ref[...] + jnp.dot(a_ref[...], w_ref[...], preferred_element_type=F32)


def _matmul_residual(a, w_all, layer, x, tm=1024, tn=512):
    T, K = a.shape
    N = w_all.shape[2]
    return pl.pallas_call(
        _matmul_residual_kernel,
        grid=(T // tm, N // tn),
        in_specs=[
            pl.BlockSpec((tm, K), lambda i, j: (i, 0)),
            pl.BlockSpec((None, K, tn), lambda i, j: (layer, 0, j)),
            pl.BlockSpec((tm, tn), lambda i, j: (i, j)),
        ],
        out_specs=pl.BlockSpec((tm, tn), lambda i, j: (i, j)),
        out_shape=jax.ShapeDtypeStruct((T, N), F32),
        compiler_params=_params("parallel", "parallel"),
    )(a, w_all, x)


def _final_norm_kernel(x_ref, g_ref, o_ref):
    o_ref[...] = _rmsnorm(x_ref[...], g_ref[...])


def _final_norm(x, g, tm=512):
    T, D = x.shape
    return pl.pallas_call(
        _final_norm_kernel,
        grid=(T // tm,),
        in_specs=[pl.BlockSpec((tm, D), lambda i: (i, 0)), pl.BlockSpec((1, D), lambda i: (0, 0))],
        out_specs=pl.BlockSpec((tm, D), lambda i: (i, 0)),
        out_shape=jax.ShapeDtypeStruct((T, D), F32),
        compiler_params=_params("parallel"),
    )(x, g)


def _ffn_kernel(x_ref, xh_ref, g_ref, wu_ref, wg_ref, cu_ref, cg_ref, wd_ref, o_ref, xn_ref, *,
                blocks_per_seq):
    groups = _column_groups(wu_ref.shape[1])
    half = x_ref.shape[0] // 2

    def up(xn):
        return [(jnp.dot(xn, wu_ref[:, cs], preferred_element_type=F32),
                 jnp.dot(xn, wg_ref[:, cs], preferred_element_type=F32)) for cs in groups]

    def gate(hidden):
        acts = []
        for cs, (hu, hg) in zip(groups, hidden):
            gt = _conv_taps(hg, cg_ref, cs)
            acts.append((gt * _sigmoid(gt) * _conv_taps(hu, cu_ref, cs)).astype(BF16))
        return jnp.concatenate(acts, axis=1)

    def down(rows, act):
        o_ref[rows, :] += jnp.dot(act, wd_ref[...], preferred_element_type=F32)

    def tile(xn):
        act_a = gate(up(xn[:half + HALO, :]))
        hidden_b = up(xn[half:, :])
        down(slice(0, half), act_a)
        down(slice(half, 2 * half), gate(hidden_b))

    @pl.when(pl.program_id(1) == 0)
    def _():
        o_ref[...] = x_ref[...]
        tile(_norm_with_halo(x_ref, xh_ref, g_ref, xn_ref,
                             pl.program_id(0) % blocks_per_seq == 0))

    @pl.when(pl.program_id(1) != 0)
    def _():
        tile(xn_ref[...])


def _ffn(x, g, w_up, conv_w, w_down, layer, seq, tm=1024, tn=512):
    T, D = x.shape
    taps = conv_w.shape[0]
    nj = w_down.shape[1] // tn
    halo_blocks = tm // HALO
    return pl.pallas_call(
        functools.partial(_ffn_kernel, blocks_per_seq=seq // tm),
        grid=(T // tm, nj),
        in_specs=[
            pl.BlockSpec((tm, D), lambda i, j: (i, 0)),
            pl.BlockSpec((HALO, D), lambda i, j: (jnp.maximum(i * halo_blocks - 1, 0), 0)),
            pl.BlockSpec((1, D), lambda i, j: (0, 0)),
            pl.BlockSpec((None, D, tn), lambda i, j: (layer, 0, j)),
            pl.BlockSpec((None, D, tn), lambda i, j: (layer, 0, nj + j)),
            pl.BlockSpec((taps, tn), lambda i, j: (0, j)),
            pl.BlockSpec((taps, tn), lambda i, j: (0, nj + j)),
            pl.BlockSpec((None, tn, D), lambda i, j: (layer, j, 0)),
        ],
        out_specs=pl.BlockSpec((tm, D), lambda i, j: (i, 0)),
        out_shape=jax.ShapeDtypeStruct((T, D), F32),
        scratch_shapes=[pltpu.VMEM((tm + HALO, D), BF16)],
        compiler_params=_params("parallel", "arbitrary", vmem_limit_bytes=VMEM_LIMIT_LARGE_BYTES),
    )(x, x, g, w_up, w_up, conv_w, conv_w, w_down)


FOX_W = HEADS * HEAD_DIM
GDN_W = HEADS * HEAD_DIM
ML_QK_W = HEADS * ML_QK_DIM
ML_V_W = HEADS * HEAD_DIM
ML_BASE = 3 * FOX_W
Z_BASE = ML_BASE + 2 * ML_QK_W + ML_V_W
GATE_BASE = Z_BASE + GDN_W + ML_V_W
FOX_HEADS_PER_STEP = 4


def _in_projection_columns(d_model):
    widths = [("fox_q", FOX_W), ("fox_k", FOX_W), ("fox_v", FOX_W), ("fox_f", HEADS),
              ("gdn_qkv", 3 * GDN_W), ("gdn_z", GDN_W), ("gdn_b", HEADS), ("gdn_a", HEADS),
              ("ml_q", ML_QK_W), ("ml_k", ML_QK_W), ("ml_v", ML_V_W), ("ml_i", HEADS),
              ("ml_f", HEADS), ("ml_o", ML_V_W), ("gate", N_BRANCH * d_model)]
    off, start = {}, 0
    for name, width in widths:
        off[name] = (start, start + width)
        start += width
    return off


def _pad_lanes(v):
    return jnp.pad(v, (0, LANES - v.shape[0])).reshape(1, LANES)


def _layer(x, layer, w, p, batch, seq):
    T, D = x.shape
    g_mix = p["norm_mix_g"].reshape(1, D)
    scale_p = jnp.concatenate([
        jnp.full((FOX_W,), HEAD_DIM ** -0.5 * LOG2E, F32), jnp.ones((2 * FOX_W,), F32),
        jnp.full((ML_QK_W,), ML_QK_DIM ** -0.5, F32),
        jnp.ones((w["proj"].shape[2] - 3 * FOX_W - ML_QK_W,), F32)]).reshape(1, -1)
    proj = _norm_matmul(x, g_mix, w["proj"], layer, scale_p, BF16, tn=1024)
    zeros8 = jnp.zeros((HEADS,), F32)
    bias_s = _pad_lanes(jnp.concatenate([p["fox_f_bias"], zeros8, p["gdn_dt_bias"],
                                         p["ml_i_bias"], p["ml_f_bias"]]))
    alog_s = _pad_lanes(jnp.concatenate([zeros8, zeros8, p["gdn_a_log"]]))
    col_small, row_small = _small_gates(x, g_mix, w["small"], layer, bias_s, alog_s, seq)
    f_cum = row_small[:, SM_FOX_F:SM_FOX_F + HEADS, :].reshape(batch, seq // CHUNK, HEADS, CHUNK)
    f_cum = f_cum.transpose(0, 2, 1, 3).reshape(batch, HEADS, seq)
    gdn_qkv = _proj_conv(x, g_mix, w["gdn_qkv"], layer, p["gdn_conv_w"], seq)

    hps = FOX_HEADS_PER_STEP
    y_fox = _fox_attention(proj, f_cum, batch, seq, q_col=0, k_col=HEADS // hps,
                           v_col=2 * HEADS // hps, hps=hps)
    y_gdn, y_ml = _chunk_mixers(
        gdn_qkv, proj, col_small, row_small, p["gdn_norm_g"].reshape(1, HEAD_DIM),
        p["ml_norm_g"].reshape(1, HEAD_DIM), batch, seq, z_col=Z_BASE // GDN_W,
        q_col=ML_BASE // ML_QK_W, k_col=ML_BASE // ML_QK_W + 1,
        v_col=(ML_BASE + 2 * ML_QK_W) // ML_V_W, og_col=(Z_BASE + GDN_W) // ML_V_W)

    tn = 512
    y = _merge(y_fox, y_gdn, y_ml, w["fox_proj"], w["gdn_proj"], w["ml_proj"], layer, proj,
               p["gate_bias"].reshape(1, -1), D, gate_col=GATE_BASE // tn, tn=tn)
    x = _matmul_residual(y, w["out"], layer, x)
    return _ffn(x, p["norm_ffn_g"].reshape(1, D), w["up"], p["ffn_conv_w"], w["down"], layer, seq)


def kernel(x, norm_mix_g, w_in, fox_f_bias, gdn_conv_w, gdn_a_log, gdn_dt_bias, gdn_norm_g,
           ml_i_bias, ml_f_bias, ml_norm_g, gate_bias, w_fox_proj, w_gdn_proj, w_ml_proj,
           w_out, norm_ffn_g, w_up, ffn_conv_w, w_down, norm_final_g):
    batch, seq, d_model = x.shape
    depth = w_in.shape[0]
    off = _in_projection_columns(d_model)
    cols = lambda a, b: w_in[:, :, off[a][0]:off[b][1]].astype(BF16)
    w_small = jnp.concatenate([cols("fox_f", "fox_f"), cols("gdn_b", "gdn_a"), cols("ml_i", "ml_f")],
                              axis=2)
    w = dict(
        proj=jnp.concatenate([cols("fox_q", "fox_v"), cols("ml_q", "ml_v"), cols("gdn_z", "gdn_z"),
                              cols("ml_o", "ml_o"), cols("gate", "gate")], axis=2),
        small=jnp.pad(w_small, ((0, 0), (0, 0), (0, LANES - w_small.shape[2]))),
        gdn_qkv=cols("gdn_qkv", "gdn_qkv"),
        fox_proj=w_fox_proj.astype(BF16), gdn_proj=w_gdn_proj.astype(BF16),
        ml_proj=w_ml_proj.astype(BF16), out=w_out.astype(BF16), up=w_up.astype(BF16),
        down=w_down.astype(BF16))
    small = dict(norm_mix_g=norm_mix_g, fox_f_bias=fox_f_bias, gdn_conv_w=gdn_conv_w,
                 gdn_a_log=gdn_a_log, gdn_dt_bias=gdn_dt_bias, gdn_norm_g=gdn_norm_g,
                 ml_i_bias=ml_i_bias, ml_f_bias=ml_f_bias, ml_norm_g=ml_norm_g,
                 gate_bias=gate_bias, norm_ffn_g=norm_ffn_g, ffn_conv_w=ffn_conv_w)
    h = x.reshape(batch * seq, d_model)
    for layer in range(depth):
        h = _layer(h, layer, w, {k: v[layer] for k, v in small.items()}, batch, seq)
    return _final_norm(h, norm_final_g.reshape(1, d_model)).reshape(batch, seq, d_model)
```

```python
import functools

import jax
import jax.numpy as jnp
from jax import lax
from jax.experimental import pallas as pl
from jax.experimental.pallas import tpu as pltpu

F32 = jnp.float32
BF16 = jnp.bfloat16
HIGHEST = lax.Precision.HIGHEST

EPS = 1e-6
HEADS = 8
HEAD_DIM = 128
ML_QK_DIM = 64
CHUNK = 64
N_BRANCH = 3
LANES = 128
MXU_WIDTH = 256
HALO = 16

SM_FOX_F, SM_GDN_B, SM_GDN_A, SM_ML_I, SM_ML_F = 0, 8, 16, 24, 32

V7X_VMEM_BYTES = 64 * 1024 * 1024
VMEM_LIMIT_BYTES = V7X_VMEM_BYTES * 3 // 4
VMEM_LIMIT_LARGE_BYTES = V7X_VMEM_BYTES * 7 // 8


def _params(*semantics, vmem_limit_bytes=VMEM_LIMIT_BYTES):
    return pltpu.CompilerParams(dimension_semantics=semantics, vmem_limit_bytes=vmem_limit_bytes)


def _rmsnorm(x, g):
    return x * lax.rsqrt(jnp.mean(x * x, axis=-1, keepdims=True) + EPS) * g


def _sigmoid(x):
    return 1.0 / (1.0 + jnp.exp(-x))


def _mm_hi(a, b):
    return jnp.dot(a, b, precision=HIGHEST, preferred_element_type=F32)


def _norm_matmul_kernel(x_ref, g_ref, w_ref, cs_ref, o_ref, xn_ref):
    def tile(xn):
        acc = jnp.dot(xn, w_ref[...], preferred_element_type=F32)
        o_ref[...] = (acc * cs_ref[...]).astype(o_ref.dtype)

    @pl.when(pl.program_id(1) == 0)
    def _():
        xn = _rmsnorm(x_ref[...], g_ref[...]).astype(BF16)
        xn_ref[...] = xn
        tile(xn)

    @pl.when(pl.program_id(1) != 0)
    def _():
        tile(xn_ref[...])


def _norm_matmul(x, g, w_all, layer, colscale, out_dtype, tm=1024, tn=512):
    T, D = x.shape
    N = w_all.shape[2]
    return pl.pallas_call(
        _norm_matmul_kernel,
        grid=(T // tm, N // tn),
        in_specs=[
            pl.BlockSpec((tm, D), lambda i, j: (i, 0)),
            pl.BlockSpec((1, D), lambda i, j: (0, 0)),
            pl.BlockSpec((None, D, tn), lambda i, j: (layer, 0, j)),
            pl.BlockSpec((1, tn), lambda i, j: (0, j)),
        ],
        out_specs=pl.BlockSpec((tm, tn), lambda i, j: (i, j)),
        out_shape=jax.ShapeDtypeStruct((T, N), out_dtype),
        scratch_shapes=[pltpu.VMEM((tm, D), BF16)],
        compiler_params=_params("parallel", "arbitrary"),
    )(x, g, w_all, colscale)


def _small_kernel(x_ref, g_ref, w_ref, bias_ref, alog_ref, o_ref, rows_ref, carry_ref, *,
                  blocks_per_seq):
    i = pl.program_id(0)
    tm = x_ref.shape[0]

    @pl.when(i % blocks_per_seq == 0)
    def _():
        carry_ref[...] = jnp.zeros_like(carry_ref)

    xn = _rmsnorm(x_ref[...], g_ref[...]).astype(BF16)
    t = jnp.dot(xn, w_ref[...], preferred_element_type=F32) + bias_ref[...]
    lane = lax.broadcasted_iota(jnp.int32, (tm, LANES), 1)
    e = jnp.log1p(jnp.exp(-jnp.abs(t)))
    logsig = jnp.minimum(t, 0.0) - e
    softplus = jnp.maximum(t, 0.0) + e
    g_decay = -jnp.exp(alog_ref[...]) * softplus
    is_logsig = (lane < SM_GDN_B) | ((lane >= SM_ML_F) & (lane < SM_ML_F + HEADS))
    val = jnp.where(is_logsig, logsig,
                    jnp.where(lane < SM_GDN_A, _sigmoid(t),
                              jnp.where(lane < SM_ML_I, g_decay, t)))

    r = lax.broadcasted_iota(jnp.int32, (CHUNK, CHUNK), 0)
    c = lax.broadcasted_iota(jnp.int32, (CHUNK, CHUNK), 1)
    tri = jnp.where(c <= r, 1.0, 0.0).astype(F32)
    chunk_sums = [_mm_hi(tri, val[ci * CHUNK:(ci + 1) * CHUNK, :]) for ci in range(tm // CHUNK)]
    offset = carry_ref[...]
    full_sums = []
    for cs in chunk_sums:
        full_sums.append(cs + offset)
        offset = offset + cs[CHUNK - 1:CHUNK, :]
    carry_ref[...] = offset
    cs_chunk = jnp.concatenate(chunk_sums, axis=0)
    cs_full = jnp.concatenate(full_sums, axis=0)

    is_chunk_cs = ((lane >= SM_GDN_A) & (lane < SM_ML_I)) | ((lane >= SM_ML_F) & (lane < SM_ML_F + HEADS))
    out = jnp.where(lane < SM_GDN_B, cs_full, jnp.where(is_chunk_cs, cs_chunk, val))
    o_ref[...] = out
    for ci in range(tm // CHUNK):
        rows_ref[ci] = out[ci * CHUNK:(ci + 1) * CHUNK, :].T


def _small_gates(x, g, w_all, layer, bias, alog, seq, tm=512):
    T, D = x.shape
    return pl.pallas_call(
        functools.partial(_small_kernel, blocks_per_seq=seq // tm),
        grid=(T // tm,),
        in_specs=[
            pl.BlockSpec((tm, D), lambda i: (i, 0)),
            pl.BlockSpec((1, D), lambda i: (0, 0)),
            pl.BlockSpec((None, D, LANES), lambda i: (layer, 0, 0)),
            pl.BlockSpec((1, LANES), lambda i: (0, 0)),
            pl.BlockSpec((1, LANES), lambda i: (0, 0)),
        ],
        out_specs=[pl.BlockSpec((tm, LANES), lambda i: (i, 0)),
                   pl.BlockSpec((tm // CHUNK, LANES, CHUNK), lambda i: (i, 0, 0))],
        out_shape=[jax.ShapeDtypeStruct((T, LANES), F32),
                   jax.ShapeDtypeStruct((T // CHUNK, LANES, CHUNK), F32)],
        scratch_shapes=[pltpu.VMEM((1, LANES), F32)],
        compiler_params=_params("arbitrary"),
    )(x, g, w_all, bias, alog)


def _conv_taps(h, c_ref, cols):
    taps = c_ref.shape[0]
    tm = h.shape[0] - HALO
    acc = None
    for t in range(taps):
        start = HALO - (taps - 1) + t
        term = h[start:start + tm, :] * c_ref[t:t + 1, cols]
        acc = term if acc is None else acc + term
    return acc


def _column_groups(width):
    return [slice(c * MXU_WIDTH, (c + 1) * MXU_WIDTH) for c in range(width // MXU_WIDTH)]


def _norm_with_halo(x_ref, xh_ref, g_ref, xn_ref, at_seq_start):
    keep = jnp.where(at_seq_start, 0.0, 1.0)
    xn = jnp.concatenate([(_rmsnorm(xh_ref[...], g_ref[...]) * keep).astype(BF16),
                          _rmsnorm(x_ref[...], g_ref[...]).astype(BF16)], axis=0)
    xn_ref[...] = xn
    return xn


def _proj_conv_kernel(x_ref, xh_ref, g_ref, w_ref, c_ref, o_ref, xn_ref, *, blocks_per_seq):
    groups = _column_groups(w_ref.shape[1])

    def tile(xn):
        hidden = [jnp.dot(xn, w_ref[:, cs], preferred_element_type=F32) for cs in groups]
        for cs, h in zip(groups, hidden):
            u = _conv_taps(h, c_ref, cs)
            o_ref[:, cs] = (u * _sigmoid(u)).astype(o_ref.dtype)

    @pl.when(pl.program_id(1) == 0)
    def _():
        tile(_norm_with_halo(x_ref, xh_ref, g_ref, xn_ref,
                             pl.program_id(0) % blocks_per_seq == 0))

    @pl.when(pl.program_id(1) != 0)
    def _():
        tile(xn_ref[...])


def _proj_conv(x, g, w_all, layer, conv_w, seq, tm=1024, tn=1024):
    T, D = x.shape
    taps = conv_w.shape[0]
    N = w_all.shape[2]
    halo_blocks = tm // HALO
    return pl.pallas_call(
        functools.partial(_proj_conv_kernel, blocks_per_seq=seq // tm),
        grid=(T // tm, N // tn),
        in_specs=[
            pl.BlockSpec((tm, D), lambda i, j: (i, 0)),
            pl.BlockSpec((HALO, D), lambda i, j: (jnp.maximum(i * halo_blocks - 1, 0), 0)),
            pl.BlockSpec((1, D), lambda i, j: (0, 0)),
            pl.BlockSpec((None, D, tn), lambda i, j: (layer, 0, j)),
            pl.BlockSpec((taps, tn), lambda i, j: (0, j)),
        ],
        out_specs=pl.BlockSpec((tm, tn), lambda i, j: (i, j)),
        out_shape=jax.ShapeDtypeStruct((T, N), BF16),
        scratch_shapes=[pltpu.VMEM((tm + HALO, D), BF16)],
        compiler_params=_params("parallel", "arbitrary"),
    )(x, x, g, w_all, conv_w)


NEG_BIG = -1e30
LOG2E = 1.4426950408889634


def _fox_kernel(q_ref, k_ref, v_ref, f_ref, o_ref, m_ref, acc_ref, *, tq, tk, hps):
    qi = pl.program_id(2)
    ones = jnp.ones((tk, HEAD_DIM), BF16)
    lane_tiles = tk // LANES
    m_ref[...] = jnp.full_like(m_ref, NEG_BIG)
    acc_ref[...] = jnp.zeros_like(acc_ref)

    heads = [slice(h * HEAD_DIM, (h + 1) * HEAD_DIM) for h in range(hps)]

    def scores(j, masked):
        start = pl.multiple_of(j * tk, tk)
        out = []
        for h, hs in enumerate(heads):
            s = lax.dot_general(q_ref[:, hs], k_ref[pl.ds(start, tk), hs],
                                (((1,), (1,)), ((), ())), preferred_element_type=F32)
            s = s - f_ref[h, pl.ds(j, 1), :] * LOG2E
            if masked:
                row = lax.broadcasted_iota(jnp.int32, (tq, tk), 0)
                col = lax.broadcasted_iota(jnp.int32, (tq, tk), 1)
                s = jnp.where(col <= row, s, NEG_BIG)
            out.append([s[:, c * LANES:(c + 1) * LANES] for c in range(lane_tiles)])
        return out

    def softmax(score_tiles):
        probs, alphas = [], []
        for h, tiles in enumerate(score_tiles):
            m_prev = m_ref[h]
            m_tile = functools.reduce(jnp.maximum, tiles)
            m_new = jnp.maximum(m_prev, jnp.max(m_tile, axis=-1, keepdims=True))
            m_ref[h] = m_new
            alphas.append(jnp.exp2(m_prev - m_new))
            probs.append(jnp.concatenate([jnp.exp2(t - m_new).astype(BF16) for t in tiles], axis=1))
        return probs, alphas

    def accumulate(j, probs, alphas):
        start = pl.multiple_of(j * tk, tk)
        for h, hs in enumerate(heads):
            v_ext = jnp.concatenate([v_ref[pl.ds(start, tk), hs], ones], axis=1)
            pv = jnp.dot(probs[h], v_ext, preferred_element_type=F32)
            acc_ref[h] = jnp.concatenate([alphas[h], alphas[h]], axis=1) * acc_ref[h] + pv

    def block(j, masked):
        accumulate(j, *softmax(scores(j, masked)))

    def block_pair(j, second_masked):
        first = softmax(scores(j, False))
        second_scores = scores(j + 1, second_masked)
        accumulate(j, *first)
        accumulate(j + 1, *softmax(second_scores))

    def full_pair(t, carry):
        block_pair(2 * t, False)
        return carry

    lax.fori_loop(0, qi >> 1, full_pair, 0)

    @pl.when((qi & 1) == 1)
    def _():
        block_pair(qi - 1, True)

    @pl.when((qi & 1) == 0)
    def _():
        block(qi, True)

    for h in range(hps):
        acc = acc_ref[h]
        o_ref[:, h * HEAD_DIM:(h + 1) * HEAD_DIM] = (
            acc[:, :HEAD_DIM] / acc[:, HEAD_DIM:]).astype(o_ref.dtype)


def _fox_attention(qkv, f_cum, batch, seq, *, q_col, k_col, v_col, tq=512, hps=2):
    T = qkv.shape[0]
    nq = seq // tq
    width = hps * HEAD_DIM
    f_blocks = f_cum.reshape(batch, HEADS, nq, tq)
    return pl.pallas_call(
        functools.partial(_fox_kernel, tq=tq, tk=tq, hps=hps),
        grid=(batch, HEADS // hps, nq),
        in_specs=[
            pl.BlockSpec((tq, width), lambda b, h, qi: (b * nq + qi, q_col + h)),
            pl.BlockSpec((seq, width), lambda b, h, qi: (b, k_col + h)),
            pl.BlockSpec((seq, width), lambda b, h, qi: (b, v_col + h)),
            pl.BlockSpec((None, hps, nq, tq), lambda b, h, qi: (b, h, 0, 0)),
        ],
        out_specs=pl.BlockSpec((tq, width), lambda b, h, qi: (b * nq + qi, h)),
        out_shape=jax.ShapeDtypeStruct((T, HEADS * HEAD_DIM), BF16),
        scratch_shapes=[pltpu.VMEM((hps, tq, LANES), F32),
                        pltpu.VMEM((hps, tq, 2 * HEAD_DIM), F32)],
        compiler_params=_params("parallel", "parallel", "arbitrary",
                                vmem_limit_bytes=VMEM_LIMIT_LARGE_BYTES),
    )(qkv, qkv, qkv, f_blocks)


def _bmm(a, b):
    return jnp.einsum("gmk,gkn->gmn", a.astype(BF16), b.astype(BF16), preferred_element_type=F32)


def _bmm_nt(a, b):
    return jnp.einsum("gmk,gnk->gmn", a.astype(BF16), b.astype(BF16), preferred_element_type=F32)


def _bmm_tn(a, b):
    return jnp.einsum("gkm,gkn->gmn", a.astype(BF16), b.astype(BF16), preferred_element_type=F32)


def _inverse_pairs(m, eye, r, c):
    return eye - jnp.where((r >> 1) == (c >> 1), m, 0.0)


def _inverse_double(x, m, r, c, b):
    shift = b.bit_length() - 1
    joins = ((r >> (shift + 1)) == (c >> (shift + 1))) & ((r >> shift) != (c >> shift))
    return x - _bmm(_bmm(x, jnp.where(joins, m, 0.0)), x)


STEP_CHUNKS = 4


def _chunk_head_stack(ref, width=HEAD_DIM):
    return jnp.stack([ref[ci * CHUNK:(ci + 1) * CHUNK, h * width:(h + 1) * width]
                      for ci in range(STEP_CHUNKS) for h in range(HEADS)])


def _chunk_head_cols(col, start):
    return jnp.stack([col[ci * CHUNK:(ci + 1) * CHUNK, start + h:start + h + 1]
                      for ci in range(STEP_CHUNKS) for h in range(HEADS)])


def _chunk_head_rows(row_ref):
    return jnp.concatenate([row_ref[ci] for ci in range(STEP_CHUNKS)], axis=0)[:, None, :]


def _store_heads(o_ref, y):
    for ci in range(STEP_CHUNKS):
        for h in range(HEADS):
            o_ref[ci * CHUNK:(ci + 1) * CHUNK, h * HEAD_DIM:(h + 1) * HEAD_DIM] = y[ci * HEADS + h]


def _chunk_mixers_kernel(gq_ref, gk_ref, gv_ref, z_ref, mq_ref, mk_ref, mv_ref, og_ref, col_ref,
                         rowg_ref, rowi_ref, rowb_ref, gng_ref, mng_ref, yg_ref, ym_ref,
                         s_ref, c_ref, m_ref):
    @pl.when(pl.program_id(1) == 0)
    def _():
        s_ref[...] = jnp.zeros_like(s_ref)
        c_ref[...] = jnp.zeros_like(c_ref)
        m_ref[...] = jnp.zeros_like(m_ref)

    L = CHUNK
    r = lax.broadcasted_iota(jnp.int32, (1, L, L), 1)
    c = lax.broadcasted_iota(jnp.int32, (1, L, L), 2)
    tri = c <= r
    eye = jnp.where(c == r, 1.0, 0.0).astype(F32)
    col = col_ref[...]

    q = _chunk_head_stack(gq_ref).astype(F32)
    k = _chunk_head_stack(gk_ref).astype(F32)
    v = _chunk_head_stack(gv_ref).astype(F32)
    qn = q * lax.rsqrt(jnp.sum(q * q, axis=-1, keepdims=True) + EPS) * (HEAD_DIM ** -0.5)
    kn = k * lax.rsqrt(jnp.sum(k * k, axis=-1, keepdims=True) + EPS)
    beta = _chunk_head_cols(col, SM_GDN_B)
    gc = _chunk_head_cols(col, SM_GDN_A)
    gr = _chunk_head_rows(rowg_ref)
    g_last = gc[:, L - 1:L, :]
    decay = jnp.where(tri, jnp.exp(jnp.where(tri, gc - gr, 0.0)), 0.0)
    eg = jnp.exp(gc)
    kb = kn * beta
    m = jnp.where(c < r, _bmm_nt(kb, kn) * decay, 0.0)
    t_inv = _inverse_pairs(m, eye, r, c)

    mq = _chunk_head_stack(mq_ref, ML_QK_DIM).astype(F32)
    mk = _chunk_head_stack(mk_ref, ML_QK_DIM).astype(F32)
    mv = _chunk_head_stack(mv_ref)
    mv = jnp.concatenate([mv, jnp.ones_like(mv)], axis=-1)
    ic = _chunk_head_cols(col, SM_ML_I)
    bc = _chunk_head_cols(col, SM_ML_F)
    ir = _chunk_head_rows(rowi_ref)
    br = _chunk_head_rows(rowb_ref)
    dm = jnp.where(tri, bc - br + ir, -jnp.inf)
    dmax = jnp.max(dm, axis=-1, keepdims=True)
    qk = _bmm_nt(mq, mk)

    t_inv = _inverse_double(t_inv, m, r, c, 2)
    t_inv = _inverse_double(t_inv, m, r, c, 4)

    m_state = m_ref[...]
    m_rows, m_lasts = [], []
    for ci in range(STEP_CHUNKS):
        hs = slice(ci * HEADS, (ci + 1) * HEADS)
        m_r = jnp.maximum(bc[hs] + m_state, dmax[hs])
        m_rows.append(m_r)
        m_state = m_r[:, L - 1:L, :]
        m_lasts.append(m_state)
    m_r = jnp.concatenate(m_rows, axis=0)
    m_last = jnp.concatenate(m_lasts, axis=0)
    m_prev = jnp.concatenate([m_ref[...]] + m_lasts[:-1], axis=0)
    m_ref[...] = m_state
    w_inter = jnp.exp(bc + m_prev - m_r)
    sm = jnp.exp(dm - m_r) * qk
    intra = _bmm(sm, mv)

    t_inv = _inverse_double(t_inv, m, r, c, 8)

    wk = jnp.exp(bc[:, L - 1:L, :] - bc + ic - m_last) * mk
    kv = _bmm_tn(wk, mv)
    w_last = w_inter[:, L - 1:L, :]
    floor = jnp.exp(-m_r)

    t_inv = _inverse_double(t_inv, m, r, c, 16)
    t_inv = _inverse_double(t_inv, m, r, c, 32)
    x = _bmm(t_inv, jnp.concatenate([v * beta, kb * eg], axis=-1))
    u = x[:, :, :HEAD_DIM]
    w = x[:, :, HEAD_DIM:]
    a_qk = _bmm_nt(qn, kn) * decay
    qg = qn * eg
    kg = kn * jnp.exp(g_last - gc)
    carry = jnp.exp(g_last)

    state = s_ref[...]
    c_state = c_ref[...]
    g_outs, m_outs = [], []
    for ci in range(STEP_CHUNKS):
        hs = slice(ci * HEADS, (ci + 1) * HEADS)
        v_new = u[hs] - _bmm(w[hs], state)
        num_den = w_inter[hs] * _bmm(mq[hs], c_state) + intra[hs]
        g_outs.append(_bmm(qg[hs], state) + _bmm(a_qk[hs], v_new))
        state = state * carry[hs] + _bmm_tn(kg[hs], v_new)
        den = jnp.maximum(jnp.abs(num_den[:, :, HEAD_DIM:]), floor[hs])
        m_outs.append(num_den[:, :, :HEAD_DIM] / den)
        c_state = w_last[hs] * c_state + kv[hs]
    s_ref[...] = state
    c_ref[...] = c_state

    o = jnp.concatenate(g_outs, axis=0)
    z = _chunk_head_stack(z_ref).astype(F32)
    _store_heads(yg_ref, (_rmsnorm(o, gng_ref[...]) * (z * _sigmoid(z))).astype(yg_ref.dtype))
    h_tilde = jnp.concatenate(m_outs, axis=0)
    og = _chunk_head_stack(og_ref).astype(F32)
    _store_heads(ym_ref, _rmsnorm(_sigmoid(og) * h_tilde, mng_ref[...]).astype(ym_ref.dtype))


def _chunk_mixers(gdn_qkv, proj, col_small, row_small, gdn_norm_g, ml_norm_g, batch, seq, *,
                  z_col, q_col, k_col, v_col, og_col):
    T = proj.shape[0]
    rows = STEP_CHUNKS * CHUNK
    ns = seq // rows
    qk_width = HEADS * ML_QK_DIM
    width = HEADS * HEAD_DIM
    row = lambda b, s: b * ns + s
    wide = lambda col: pl.BlockSpec((rows, width), lambda b, s: (row(b, s), col))
    rows_spec = lambda seg: pl.BlockSpec((STEP_CHUNKS, HEADS, CHUNK),
                                         lambda b, s: (row(b, s), seg // HEADS, 0))
    norm_spec = pl.BlockSpec((1, HEAD_DIM), lambda b, s: (0, 0))
    out_spec = pl.BlockSpec((rows, width), lambda b, s: (row(b, s), 0))
    out_shape = jax.ShapeDtypeStruct((T, width), BF16)
    return pl.pallas_call(
        _chunk_mixers_kernel,
        grid=(batch, ns),
        in_specs=[
            wide(0), wide(1), wide(2), wide(z_col),
            pl.BlockSpec((rows, qk_width), lambda b, s: (row(b, s), q_col)),
            pl.BlockSpec((rows, qk_width), lambda b, s: (row(b, s), k_col)),
            wide(v_col), wide(og_col),
            pl.BlockSpec((rows, LANES), lambda b, s: (row(b, s), 0)),
            rows_spec(SM_GDN_A), rows_spec(SM_ML_I), rows_spec(SM_ML_F),
            norm_spec, norm_spec,
        ],
        out_specs=[out_spec, out_spec],
        out_shape=[out_shape, out_shape],
        scratch_shapes=[pltpu.VMEM((HEADS, HEAD_DIM, HEAD_DIM), F32),
                        pltpu.VMEM((HEADS, ML_QK_DIM, 2 * HEAD_DIM), F32),
                        pltpu.VMEM((HEADS, 1, 1), F32)],
        compiler_params=_params("parallel", "arbitrary"),
    )(gdn_qkv, gdn_qkv, gdn_qkv, proj, proj, proj, proj, proj, col_small,
      row_small, row_small, row_small, gdn_norm_g, ml_norm_g)


def _merge_kernel(yf_ref, yg_ref, ym_ref, wf_ref, wg_ref, wm_ref, g0_ref, g1_ref, g2_ref,
                  b0_ref, b1_ref, b2_ref, o_ref):
    def branch(y_ref, w_ref, g_ref, b_ref):
        return _sigmoid(g_ref[...] + b_ref[...]) * jnp.dot(y_ref[...], w_ref[...],
                                                          preferred_element_type=F32)

    y = (branch(yf_ref, wf_ref, g0_ref, b0_ref) + branch(yg_ref, wg_ref, g1_ref, b1_ref)
         + branch(ym_ref, wm_ref, g2_ref, b2_ref))
    o_ref[...] = y.astype(o_ref.dtype)


def _merge(y_fox, y_gdn, y_ml, w_fox, w_gdn, w_ml, layer, gate_src, gate_bias, d_model, *,
           gate_col, tm=1024, tn=512):
    T, K = y_fox.shape
    nj = d_model // tn
    y_spec = pl.BlockSpec((tm, K), lambda i, j: (i, 0))
    w_spec = pl.BlockSpec((None, K, tn), lambda i, j: (layer, 0, j))
    gate_specs = [pl.BlockSpec((tm, tn), lambda i, j, n=n: (i, gate_col + n * nj + j))
                  for n in range(N_BRANCH)]
    bias_specs = [pl.BlockSpec((1, tn), lambda i, j, n=n: (0, n * nj + j)) for n in range(N_BRANCH)]
    return pl.pallas_call(
        _merge_kernel,
        grid=(T // tm, nj),
        in_specs=[y_spec, y_spec, y_spec, w_spec, w_spec, w_spec] + gate_specs + bias_specs,
        out_specs=pl.BlockSpec((tm, tn), lambda i, j: (i, j)),
        out_shape=jax.ShapeDtypeStruct((T, d_model), BF16),
        compiler_params=_params("parallel", "parallel", vmem_limit_bytes=VMEM_LIMIT_LARGE_BYTES),
    )(y_fox, y_gdn, y_ml, w_fox, w_gdn, w_ml, gate_src, gate_src, gate_src,
      gate_bias, gate_bias, gate_bias)


def _matmul_residual_kernel(a_ref, w_ref, x_ref, o_ref):
    o_ref[...] = x_ref[...] + jnp.dot(a_ref[...], w_ref[...], preferred_element_type=F32)


def _matmul_residual(a, w_all, layer, x, tm=1024, tn=512):
    T, K = a.shape
    N = w_all.shape[2]
    return pl.pallas_call(
        _matmul_residual_kernel,
        grid=(T // tm, N // tn),
        in_specs=[
            pl.BlockSpec((tm, K), lambda i, j: (i, 0)),
            pl.BlockSpec((None, K, tn), lambda i, j: (layer, 0, j)),
            pl.BlockSpec((tm, tn), lambda i, j: (i, j)),
        ],
        out_specs=pl.BlockSpec((tm, tn), lambda i, j: (i, j)),
        out_shape=jax.ShapeDtypeStruct((T, N), F32),
        compiler_params=_params("parallel", "parallel"),
    )(a, w_all, x)


def _final_norm_kernel(x_ref, g_ref, o_ref):
    o_ref[...] = _rmsnorm(x_ref[...], g_ref[...])


def _final_norm(x, g, tm=512):
    T, D = x.shape
    return pl.pallas_call(
        _final_norm_kernel,
        grid=(T // tm,),
        in_specs=[pl.BlockSpec((tm, D), lambda i: (i, 0)), pl.BlockSpec((1, D), lambda i: (0, 0))],
        out_specs=pl.BlockSpec((tm, D), lambda i: (i, 0)),
        out_shape=jax.ShapeDtypeStruct((T, D), F32),
        compiler_params=_params("parallel"),
    )(x, g)


def _ffn_kernel(x_ref, xh_ref, g_ref, wu_ref, wg_ref, cu_ref, cg_ref, wd_ref, o_ref, xn_ref, *,
                blocks_per_seq):
    groups = _column_groups(wu_ref.shape[1])
    half = x_ref.shape[0] // 2

    def up(xn):
        return [(jnp.dot(xn, wu_ref[:, cs], preferred_element_type=F32),
                 jnp.dot(xn, wg_ref[:, cs], preferred_element_type=F32)) for cs in groups]

    def gate(hidden):
        acts = []
        for cs, (hu, hg) in zip(groups, hidden):
            gt = _conv_taps(hg, cg_ref, cs)
            acts.append((gt * _sigmoid(gt) * _conv_taps(hu, cu_ref, cs)).astype(BF16))
        return jnp.concatenate(acts, axis=1)

    def down(rows, act):
        o_ref[rows, :] += jnp.dot(act, wd_ref[...], preferred_element_type=F32)

    def tile(xn):
        act_a = gate(up(xn[:half + HALO, :]))
        hidden_b = up(xn[half:, :])
        down(slice(0, half), act_a)
        down(slice(half, 2 * half), gate(hidden_b))

    @pl.when(pl.program_id(1) == 0)
    def _():
        o_ref[...] = x_ref[...]
        tile(_norm_with_halo(x_ref, xh_ref, g_ref, xn_ref,
                             pl.program_id(0) % blocks_per_seq == 0))

    @pl.when(pl.program_id(1) != 0)
    def _():
        tile(xn_ref[...])


def _ffn(x, g, w_up, conv_w, w_down, layer, seq, tm=1024, tn=512):
    T, D = x.shape
    taps = conv_w.shape[0]
    nj = w_down.shape[1] // tn
    halo_blocks = tm // HALO
    return pl.pallas_call(
        functools.partial(_ffn_kernel, blocks_per_seq=seq // tm),
        grid=(T // tm, nj),
        in_specs=[
            pl.BlockSpec((tm, D), lambda i, j: (i, 0)),
            pl.BlockSpec((HALO, D), lambda i, j: (jnp.maximum(i * halo_blocks - 1, 0), 0)),
            pl.BlockSpec((1, D), lambda i, j: (0, 0)),
            pl.BlockSpec((None, D, tn), lambda i, j: (layer, 0, j)),
            pl.BlockSpec((None, D, tn), lambda i, j: (layer, 0, nj + j)),
            pl.BlockSpec((taps, tn), lambda i, j: (0, j)),
            pl.BlockSpec((taps, tn), lambda i, j: (0, nj + j)),
            pl.BlockSpec((None, tn, D), lambda i, j: (layer, j, 0)),
        ],
        out_specs=pl.BlockSpec((tm, D), lambda i, j: (i, 0)),
        out_shape=jax.ShapeDtypeStruct((T, D), F32),
        scratch_shapes=[pltpu.VMEM((tm + HALO, D), BF16)],
        compiler_params=_params("parallel", "arbitrary", vmem_limit_bytes=VMEM_LIMIT_LARGE_BYTES),
    )(x, x, g, w_up, w_up, conv_w, conv_w, w_down)


FOX_W = HEADS * HEAD_DIM
GDN_W = HEADS * HEAD_DIM
ML_QK_W = HEADS * ML_QK_DIM
ML_V_W = HEADS * HEAD_DIM
ML_BASE = 3 * FOX_W
Z_BASE = ML_BASE + 2 * ML_QK_W + ML_V_W
GATE_BASE = Z_BASE + GDN_W + ML_V_W
FOX_HEADS_PER_STEP = 4


def _in_projection_columns(d_model):
    widths = [("fox_q", FOX_W), ("fox_k", FOX_W), ("fox_v", FOX_W), ("fox_f", HEADS),
              ("gdn_qkv", 3 * GDN_W), ("gdn_z", GDN_W), ("gdn_b", HEADS), ("gdn_a", HEADS),
              ("ml_q", ML_QK_W), ("ml_k", ML_QK_W), ("ml_v", ML_V_W), ("ml_i", HEADS),
              ("ml_f", HEADS), ("ml_o", ML_V_W), ("gate", N_BRANCH * d_model)]
    off, start = {}, 0
    for name, width in widths:
        off[name] = (start, start + width)
        start += width
    return off


def _pad_lanes(v):
    return jnp.pad(v, (0, LANES - v.shape[0])).reshape(1, LANES)


def _layer(x, layer, w, p, batch, seq):
    T, D = x.shape
    g_mix = p["norm_mix_g"].reshape(1, D)
    scale_p = jnp.concatenate([
        jnp.full((FOX_W,), HEAD_DIM ** -0.5 * LOG2E, F32), jnp.ones((2 * FOX_W,), F32),
        jnp.full((ML_QK_W,), ML_QK_DIM ** -0.5, F32),
        jnp.ones((w["proj"].shape[2] - 3 * FOX_W - ML_QK_W,), F32)]).reshape(1, -1)
    proj = _norm_matmul(x, g_mix, w["proj"], layer, scale_p, BF16, tn=1024)
    zeros8 = jnp.zeros((HEADS,), F32)
    bias_s = _pad_lanes(jnp.concatenate([p["fox_f_bias"], zeros8, p["gdn_dt_bias"],
                                         p["ml_i_bias"], p["ml_f_bias"]]))
    alog_s = _pad_lanes(jnp.concatenate([zeros8, zeros8, p["gdn_a_log"]]))
    col_small, row_small = _small_gates(x, g_mix, w["small"], layer, bias_s, alog_s, seq)
    f_cum = row_small[:, SM_FOX_F:SM_FOX_F + HEADS, :].reshape(batch, seq // CHUNK, HEADS, CHUNK)
    f_cum = f_cum.transpose(0, 2, 1, 3).reshape(batch, HEADS, seq)
    gdn_qkv = _proj_conv(x, g_mix, w["gdn_qkv"], layer, p["gdn_conv_w"], seq)

    hps = FOX_HEADS_PER_STEP
    y_fox = _fox_attention(proj, f_cum, batch, seq, q_col=0, k_col=HEADS // hps,
                           v_col=2 * HEADS // hps, hps=hps)
    y_gdn, y_ml = _chunk_mixers(
        gdn_qkv, proj, col_small, row_small, p["gdn_norm_g"].reshape(1, HEAD_DIM),
        p["ml_norm_g"].reshape(1, HEAD_DIM), batch, seq, z_col=Z_BASE // GDN_W,
        q_col=ML_BASE // ML_QK_W, k_col=ML_BASE // ML_QK_W + 1,
        v_col=(ML_BASE + 2 * ML_QK_W) // ML_V_W, og_col=(Z_BASE + GDN_W) // ML_V_W)

    tn = 1024
    y = _merge(y_fox, y_gdn, y_ml, w["fox_proj"], w["gdn_proj"], w["ml_proj"], layer, proj,
               p["gate_bias"].reshape(1, -1), D, gate_col=GATE_BASE // tn, tn=tn)
    x = _matmul_residual(y, w["out"], layer, x, tn=tn)
    return _ffn(x, p["norm_ffn_g"].reshape(1, D), w["up"], p["ffn_conv_w"], w["down"], layer, seq)


def kernel(x, norm_mix_g, w_in, fox_f_bias, gdn_conv_w, gdn_a_log, gdn_dt_bias, gdn_norm_g,
           ml_i_bias, ml_f_bias, ml_norm_g, gate_bias, w_fox_proj, w_gdn_proj, w_ml_proj,
           w_out, norm_ffn_g, w_up, ffn_conv_w, w_down, norm_final_g):
    batch, seq, d_model = x.shape
    depth = w_in.shape[0]
    off = _in_projection_columns(d_model)
    cols = lambda a, b: w_in[:, :, off[a][0]:off[b][1]].astype(BF16)
    w_small = jnp.concatenate([cols("fox_f", "fox_f"), cols("gdn_b", "gdn_a"), cols("ml_i", "ml_f")],
                              axis=2)
    w = dict(
        proj=jnp.concatenate([cols("fox_q", "fox_v"), cols("ml_q", "ml_v"), cols("gdn_z", "gdn_z"),
                              cols("ml_o", "ml_o"), cols("gate", "gate")], axis=2),
        small=jnp.pad(w_small, ((0, 0), (0, 0), (0, LANES - w_small.shape[2]))),
        gdn_qkv=cols("gdn_qkv", "gdn_qkv"),
        fox_proj=w_fox_proj.astype(BF16), gdn_proj=w_gdn_proj.astype(BF16),
        ml_proj=w_ml_proj.astype(BF16), out=w_out.astype(BF16), up=w_up.astype(BF16),
        down=w_down.astype(BF16))
    small = dict(norm_mix_g=norm_mix_g, fox_f_bias=fox_f_bias, gdn_conv_w=gdn_conv_w,
                 gdn_a_log=gdn_a_log, gdn_dt_bias=gdn_dt_bias, gdn_norm_g=gdn_norm_g,
                 ml_i_bias=ml_i_bias, ml_f_bias=ml_f_bias, ml_norm_g=ml_norm_g,
                 gate_bias=gate_bias, norm_ffn_g=norm_ffn_g, ffn_conv_w=ffn_conv_w)
    h = x.reshape(batch * seq, d_model)
    for layer in range(depth):
        h = _layer(h, layer, w, {k: v[layer] for k, v in small.items()}, batch, seq)
    return _final_norm(h, norm_final_g.reshape(1, d_model)).reshape(batch, seq, d_model)
```
